```python
import jax, jax.numpy as jnp
from jax import lax
import numpy as np

D_MODEL = 1024
BATCH = 2
SEQ = 16384
DEPTH = 2

N_MEM = 256
EPS = 1e-6
NEG = -1e30

NSA_HEAD_DIM = 64
NSA_HEADS = (D_MODEL // 2) // NSA_HEAD_DIM
NSA_KV_GROUPS = 2
NSA_Q_WIDTH = NSA_HEADS * NSA_HEAD_DIM
NSA_KV_WIDTH = NSA_KV_GROUPS * NSA_HEAD_DIM
L_CMP = 32
D_CMP = 16
CMP_HIDDEN = 2 * NSA_HEAD_DIM
L_SLC = 64
N_SLC_SEL = 16
WINDOW = 512
Q_BLOCK = 128
FORCE_BONUS = 1.0e3
SGU_WIDTH = D_MODEL // 4
SGU_GROUPS = 4
SGU_CHUNK = 128
POOL_WIDTH = D_MODEL // 4
POOL_WINDOWS = (2, 4, 8, 16)
N_BRANCH = 3
IN_SPLITS = (NSA_Q_WIDTH, 6 * NSA_KV_WIDTH, 3 * NSA_HEADS, 2 * SGU_WIDTH, POOL_WIDTH, N_BRANCH * D_MODEL)
IN_WIDTH = NSA_Q_WIDTH + 6 * NSA_KV_WIDTH + 3 * NSA_HEADS + 2 * SGU_WIDTH + POOL_WIDTH + N_BRANCH * D_MODEL
XA_HEADS = 4
XA_HEAD_DIM = 128
XA_WIDTH = XA_HEADS * XA_HEAD_DIM
PEER_HEADS = 8
PEER_KEYS = 128
PEER_EXPERTS = PEER_KEYS * PEER_KEYS
PEER_QDIM = 128
PEER_TOPK = 16
PEER_BLOCK = 128

kernel_name = "hybrid_nsa_sgu_pool_peer_block"


def rms_norm(x, g):
    xf = x.astype(jnp.float32)
    y = xf * lax.rsqrt(jnp.mean(xf * xf, axis=-1, keepdims=True) + EPS)
    return (y * g).astype(x.dtype)


def layer_norm(x, g, b):
    xf = x.astype(jnp.float32)
    mu = jnp.mean(xf, axis=-1, keepdims=True)
    var = jnp.mean(jnp.square(xf - mu), axis=-1, keepdims=True)
    return ((xf - mu) * lax.rsqrt(var + EPS) * g + b).astype(x.dtype)


def masked_softmax(s, mask):
    s = jnp.where(mask, s.astype(jnp.float32), NEG)
    e = jnp.exp(s - jnp.max(s, axis=-1, keepdims=True)) * mask
    return e / jnp.maximum(jnp.sum(e, axis=-1, keepdims=True), 1e-30)


def nsa_attention(q, kv, gate_logits, q_norm, k_norm, cmp_pe, cmp_w1, cmp_w2):
    B, S, H, dk = q.shape
    G = NSA_KV_GROUPS
    R = H // G
    scale = dk ** -0.5
    q = rms_norm(q, q_norm)

    n_cmp = (S - L_CMP) // D_CMP + 1
    cmp_idx = np.arange(n_cmp)[:, None] * D_CMP + np.arange(L_CMP)[None, :]
    kv_raw = jnp.moveaxis(kv[:, :, 0:2], 2, 0)
    blocks = kv_raw[:, :, cmp_idx] + cmp_pe[:, None, None, :, None, :]
    hid = jax.nn.gelu(jnp.einsum('cbnlgd,clde->cbnge', blocks, cmp_w1))
    comp = jnp.einsum('cbnge,ced->cbngd', hid, cmp_w2)
    k_c = rms_norm(comp[0], k_norm[0])
    v_c = comp[1]
    cmp_end = jnp.asarray(np.arange(n_cmp) * D_CMP + L_CMP - 1)

    n_slc = S // L_SLC
    n_sel = min(N_SLC_SEL, n_slc)
    c0 = np.arange(n_cmp)[:, None] * D_CMP
    s0 = np.arange(n_slc)[None, :] * L_SLC
    overlap = np.minimum(c0 + L_CMP, s0 + L_SLC) - np.maximum(c0, s0)
    cmp_to_slc = jnp.asarray(np.maximum(overlap, 0) / D_CMP, dtype=jnp.float32)
    k_s = rms_norm(kv[:, :, 2], k_norm[1]).reshape(B, n_slc, L_SLC, G, dk).transpose(0, 3, 1, 2, 4)
    v_s = kv[:, :, 3].reshape(B, n_slc, L_SLC, G, dk).transpose(0, 3, 1, 2, 4)

    pad = ((0, 0), (WINDOW, 0), (0, 0), (0, 0))
    k_w = jnp.pad(rms_norm(kv[:, :, 4], k_norm[2]), pad)
    v_w = jnp.pad(kv[:, :, 5], pad)

    n_qb = S // Q_BLOCK
    q_blocks = q.reshape(B, n_qb, Q_BLOCK, G, R, dk).transpose(1, 0, 2, 3, 4, 5)
    g_blocks = jax.nn.sigmoid(gate_logits.astype(jnp.float32)).reshape(B, n_qb, Q_BLOCK, G, R, 3)
    g_blocks = g_blocks.transpose(1, 0, 2, 3, 4, 5)
    gather = jax.vmap(jax.vmap(lambda kb, i: kb[i]))

    def block(args):
        qb, qblk, gblk = args
        t = qb * Q_BLOCK + jnp.arange(Q_BLOCK)
        s = jnp.einsum('bqgrd,bngd->bgrqn', qblk, k_c) * scale
        p_c = masked_softmax(s, cmp_end[None, :] <= t[:, None])
        o_c = jnp.einsum('bgrqn,bngd->bqgrd', p_c.astype(v_c.dtype), v_c)
        imp = jnp.einsum('bgqn,nj->bgqj', jnp.sum(p_c, axis=2), cmp_to_slc)
        cur = t // L_SLC
        j = jnp.arange(n_slc)[None, :]
        forced = (j == 0) | (j == cur[:, None]) | (j == cur[:, None] - 1)
        sel_score = jnp.where(j <= cur[:, None], imp + FORCE_BONUS * forced, NEG)
        top_val, top_idx = lax.top_k(sel_score, n_sel)
        k_sel = gather(k_s, top_idx)
        v_sel = gather(v_s, top_idx)
        pos = top_idx[..., None] * L_SLC + jnp.arange(L_SLC)
        mask_s = (top_val[..., None] > 0.5 * NEG) & (pos <= t[:, None, None])
        s = jnp.einsum('bqgrd,bgqnld->bgrqnl', qblk, k_sel) * scale
        p_s = masked_softmax(s.reshape(B, G, R, Q_BLOCK, n_sel * L_SLC),
                             mask_s.reshape(B, G, 1, Q_BLOCK, n_sel * L_SLC)).reshape(s.shape)
        o_s = jnp.einsum('bgrqnl,bgqnld->bqgrd', p_s.astype(v_sel.dtype), v_sel)
        k_win = lax.dynamic_slice_in_dim(k_w, qb * Q_BLOCK, WINDOW + Q_BLOCK, axis=1)
        v_win = lax.dynamic_slice_in_dim(v_w, qb * Q_BLOCK, WINDOW + Q_BLOCK, axis=1)
        kpos = qb * Q_BLOCK - WINDOW + jnp.arange(WINDOW + Q_BLOCK)
        delta = t[:, None] - kpos[None, :]
        mask_w = (delta >= 0) & (delta < WINDOW) & (kpos[None, :] >= 0)
        s = jnp.einsum('bqgrd,bkgd->bgrqk', qblk, k_win) * scale
        p_w = masked_softmax(s, mask_w)
        o_w = jnp.einsum('bgrqk,bkgd->bqgrd', p_w.astype(v_win.dtype), v_win)
        o = (gblk[..., 0:1] * o_c + gblk[..., 1:2] * o_s + gblk[..., 2:3] * o_w)
        return o.astype(qblk.dtype)

    out = lax.map(block, (jnp.arange(n_qb), q_blocks, g_blocks))
    return out.transpose(1, 0, 2, 3, 4, 5).reshape(B, S, H * dk)


def spatial_gating(zg, ln_g, ln_b, w_s, b_s):
    B, S, _ = zg.shape
    zg = jax.nn.gelu(zg)
    u, v = jnp.split(zg, 2, axis=-1)
    v = layer_norm(v, ln_g, ln_b)
    n_ch = S // SGU_CHUNK
    v = v.reshape(B, n_ch, SGU_CHUNK, SGU_GROUPS, SGU_WIDTH // SGU_GROUPS)
    w = w_s * jnp.tril(jnp.ones((SGU_CHUNK, SGU_CHUNK), w_s.dtype))
    mixed = jnp.einsum('gts,bnsgc->bntgc', w, v) + b_s.T[:, :, None]
    return u * mixed.reshape(B, S, SGU_WIDTH)


def multiscale_pool(p, w_pool, scale):
    B, S, C = p.shape
    cg = C // len(POOL_WINDOWS)
    pf = p.astype(jnp.float32)
    csum = jnp.cumsum(pf, axis=1)
    csum_pad = jnp.pad(csum, ((0, 0), (1, 0), (0, 0)))
    pos = jnp.arange(S)
    outs = []
    for gi, w in enumerate(POOL_WINDOWS):
        sl = slice(gi * cg, (gi + 1) * cg)
        lower = jnp.pad(csum_pad[:, :S - w + 1, sl], ((0, 0), (w - 1, 0), (0, 0)))
        count = jnp.minimum(pos + 1, w).astype(jnp.float32)[None, :, None]
        outs.append((csum[:, :, sl] - lower) / count - pf[:, :, sl])
    d = jnp.stack(outs, axis=2)
    y = jnp.einsum('bsgc,gcd->bsgd', d, w_pool.astype(jnp.float32)).reshape(B, S, C) * scale
    return y.astype(p.dtype)


def memory_xattn(h, mem_h, wq, wk, wv, q_norm, k_norm, wo):
    B, S, _ = h.shape
    M = mem_h.shape[1]
    q = rms_norm((h @ wq).reshape(B, S, XA_HEADS, XA_HEAD_DIM), q_norm)
    k = rms_norm((mem_h @ wk).reshape(B, M, XA_HEADS, XA_HEAD_DIM), k_norm)
    v = (mem_h @ wv).reshape(B, M, XA_HEADS, XA_HEAD_DIM)
    s = jnp.einsum('bshd,bmhd->bhsm', q, k).astype(jnp.float32) * (XA_HEAD_DIM ** -0.5)
    p = jax.nn.softmax(s, axis=-1)
    o = jnp.einsum('bhsm,bmhd->bshd', p.astype(v.dtype), v).reshape(B, S, XA_WIDTH)
    return o @ wo


def peer_ffn(h, wq, keys1, keys2, u_tab, v_tab):
    B, S, D = h.shape
    half = PEER_QDIM // 2
    tokens = h.reshape(-1, PEER_BLOCK, D)

    def step(ht):
        T = ht.shape[0]
        q = (ht @ wq).reshape(T, PEER_HEADS, 2, half)
        s1 = jnp.einsum('thd,kd->thk', q[:, :, 0], keys1).astype(jnp.float32)
        s2 = jnp.einsum('thd,kd->thk', q[:, :, 1], keys2).astype(jnp.float32)
        v1, i1 = lax.top_k(s1, PEER_TOPK)
        v2, i2 = lax.top_k(s2, PEER_TOPK)
        cand = (v1[..., :, None] + v2[..., None, :]).reshape(T, PEER_HEADS, PEER_TOPK * PEER_TOPK)
        cv, ci = lax.top_k(cand, PEER_TOPK)
        a = jnp.take_along_axis(i1, ci // PEER_TOPK, axis=-1)
        b = jnp.take_along_axis(i2, ci % PEER_TOPK, axis=-1)
        e = a * PEER_KEYS + b
        g = jax.nn.softmax(cv, axis=-1)
        act = jax.nn.gelu(jnp.einsum('thkd,td->thk', u_tab[e], ht))
        return jnp.einsum('thk,thkd->td', (g * act).astype(ht.dtype), v_tab[e])

    return lax.map(step, tokens).reshape(B, S, D)


def setup_inputs(seed: int = 0) -> dict:
    key = jax.random.key(seed)
    ks = iter(jax.random.split(key, 40))
    f32 = jnp.float32
    L, D, dk = DEPTH, D_MODEL, NSA_HEAD_DIM
    cg = POOL_WIDTH // len(POOL_WINDOWS)

    def nrm(shape, scale):
        return jax.random.normal(next(ks), shape, f32) * scale

    def gain(shape):
        return 1.0 + 0.02 * jax.random.normal(next(ks), shape, f32)

    return {
        "x": nrm((BATCH, SEQ, D), 1.0),
        "mem": nrm((BATCH, N_MEM, D), 1.0),
        "mix_norm": gain((L, D)),
        "w_in": nrm((L, D, IN_WIDTH), D ** -0.5),
        "nsa_q_norm": gain((L, dk)),
        "nsa_k_norm": gain((L, 3, dk)),
        "cmp_pe": nrm((L, 2, L_CMP, dk), 0.5),
        "cmp_w1": nrm((L, 2, L_CMP, dk, CMP_HIDDEN), (L_CMP * dk) ** -0.5),
        "cmp_w2": nrm((L, 2, CMP_HIDDEN, dk), CMP_HIDDEN ** -0.5),
        "sgu_ln_g": gain((L, SGU_WIDTH)),
        "sgu_ln_b": nrm((L, SGU_WIDTH), 0.02),
        "sgu_w": nrm((L, SGU_GROUPS, SGU_CHUNK, SGU_CHUNK), SGU_CHUNK ** -0.5),
        "sgu_b": gain((L, SGU_GROUPS, SGU_CHUNK)),
        "pool_w": nrm((L, len(POOL_WINDOWS), cg, cg), cg ** -0.5),
        "pool_scale": gain((L, POOL_WIDTH)),
        "lift_a": nrm((L, NSA_Q_WIDTH, D), NSA_Q_WIDTH ** -0.5),
        "lift_b": nrm((L, SGU_WIDTH, D), SGU_WIDTH ** -0.5),
        "lift_c": nrm((L, POOL_WIDTH, D), POOL_WIDTH ** -0.5),
        "w_out": nrm((L, D, D), D ** -0.5),
        "xa_norm": gain((L, D)),
        "mem_norm": gain((L, D)),
        "xa_wq": nrm((L, D, XA_WIDTH), D ** -0.5),
        "xa_wk": nrm((L, D, XA_WIDTH), D ** -0.5),
        "xa_wv": nrm((L, D, XA_WIDTH), D ** -0.5),
        "xa_q_norm": gain((L, XA_HEAD_DIM)),
        "xa_k_norm": gain((L, XA_HEAD_DIM)),
        "xa_wo": nrm((L, XA_WIDTH, D), XA_WIDTH ** -0.5),
        "ffn_norm": gain((L, D)),
        "peer_wq": nrm((L, D, PEER_HEADS * PEER_QDIM), D ** -0.5),
        "peer_keys1": nrm((L, PEER_KEYS, PEER_QDIM // 2), (PEER_QDIM // 2) ** -0.5),
        "peer_keys2": nrm((L, PEER_KEYS, PEER_QDIM // 2), (PEER_QDIM // 2) ** -0.5),
        "peer_u": nrm((L, PEER_EXPERTS, D), D ** -0.5),
        "peer_v": nrm((L, PEER_EXPERTS, D), 0.25),
    }


def reference(x, mem, mix_norm, w_in, nsa_q_norm, nsa_k_norm, cmp_pe, cmp_w1, cmp_w2,
              sgu_ln_g, sgu_ln_b, sgu_w, sgu_b, pool_w, pool_scale, lift_a, lift_b, lift_c,
              w_out, xa_norm, mem_norm, xa_wq, xa_wk, xa_wv, xa_q_norm, xa_k_norm, xa_wo,
              ffn_norm, peer_wq, peer_keys1, peer_keys2, peer_u, peer_v):
    B, S, D = x.shape
    split_points = [int(c) for c in np.cumsum(IN_SPLITS)[:-1]]
    for l in range(DEPTH):
        h = rms_norm(x, mix_norm[l])
        z = h @ w_in[l]
        z_q, z_kv, z_g, z_sgu, z_pool, z_merge = jnp.split(z, split_points, axis=-1)
        o_a = nsa_attention(z_q.reshape(B, S, NSA_HEADS, NSA_HEAD_DIM),
                            z_kv.reshape(B, S, 6, NSA_KV_GROUPS, NSA_HEAD_DIM),
                            z_g.reshape(B, S, NSA_HEADS, 3),
                            nsa_q_norm[l], nsa_k_norm[l], cmp_pe[l], cmp_w1[l], cmp_w2[l])
        o_b = spatial_gating(z_sgu, sgu_ln_g[l], sgu_ln_b[l], sgu_w[l], sgu_b[l])
        o_c = multiscale_pool(z_pool, pool_w[l], pool_scale[l])
        gate = jax.nn.sigmoid(z_merge.astype(jnp.float32)).reshape(B, S, N_BRANCH, D).astype(x.dtype)
        merged = (gate[:, :, 0] * (o_a @ lift_a[l]) + gate[:, :, 1] * (o_b @ lift_b[l])
                  + gate[:, :, 2] * (o_c @ lift_c[l]))
        x = x + merged @ w_out[l]
        x = x + memory_xattn(rms_norm(x, xa_norm[l]), rms_norm(mem, mem_norm[l]),
                             xa_wq[l], xa_wk[l], xa_wv[l], xa_q_norm[l], xa_k_norm[l], xa_wo[l])
        x = x + peer_ffn(rms_norm(x, ffn_norm[l]), peer_wq[l], peer_keys1[l], peer_keys2[l],
                         peer_u[l], peer_v[l])
    return x
```

```python
import functools

import numpy as np
import jax
import jax.numpy as jnp
from jax import lax
from jax.experimental import pallas as pl
from jax.experimental.pallas import tpu as pltpu

F32 = jnp.float32
BF16 = jnp.bfloat16

EPS = 1e-6
NEG = -1e30
REMOVED = -3e38

D_MODEL = 1024
DK = 64
NSA_HEADS = 8
NSA_GROUPS = 2
NSA_REP = NSA_HEADS // NSA_GROUPS
L_CMP, D_CMP = 32, 16
CMP_HIDDEN = 128
L_SLC = 64
N_SEL = 16
WINDOW = 512
QB = 128
FORCE_BONUS = 1.0e3
SGU_WIDTH = 256
SGU_GROUPS = 4
SGU_CHUNK = 128
POOL_WIDTH = 256
POOL_WINDOWS = (2, 4, 8, 16)
XA_HEADS, XA_HD = 4, 128
XA_WIDTH = XA_HEADS * XA_HD
PEER_HEADS = 8
PEER_KEYS = 128
PEER_TOPK = 16
PEER_HALF = 64

LANES = 128
SEL_SUPER_BLOCKS = 64
SEL_SUPER = SEL_SUPER_BLOCKS * L_SLC
SEL_SUB = 512
PEER_TT = 512
PEER_CH = 1024
VMEM_LIMIT = 56 * 1024 * 1024


def _cp(sem):
    return pltpu.CompilerParams(dimension_semantics=sem, vmem_limit_bytes=VMEM_LIMIT)


def _gelu(x):
    return 0.5 * x * (1.0 + jnp.tanh(0.7978845608028654 * (x + 0.044715 * (x * x * x))))


def _rms(x, g):
    return x * lax.rsqrt(jnp.mean(x * x, axis=-1, keepdims=True) + EPS) * g


def _dot(a, b):
    return jnp.dot(a, b, preferred_element_type=F32)


def _dot_nt(a, b):
    return lax.dot_general(a, b, (((1,), (1,)), ((), ())), preferred_element_type=F32)


def _iota(shape, dim):
    return lax.broadcasted_iota(jnp.int32, shape, dim)


def _div_pow2(x, n):
    assert n & (n - 1) == 0
    return lax.shift_right_logical(x, jnp.int32(n.bit_length() - 1))


def _mod_pow2(x, n):
    assert n & (n - 1) == 0
    return x & (n - 1)


def _norm_matmul_kernel(x_ref, g_ref, w_ref, *o_refs, splits):
    hb = _rms(x_ref[...], g_ref[...]).astype(BF16)
    for o_ref, (a, b) in zip(o_refs, splits):
        o_ref[...] = _dot(hb, w_ref[:, a:b]).astype(o_ref.dtype)


def norm_matmul(x, g, w, splits, tm, name):
    t, d = x.shape
    n = w.shape[1]
    return pl.pallas_call(
        functools.partial(_norm_matmul_kernel, splits=splits),
        grid=(t // tm,),
        in_specs=[pl.BlockSpec((tm, d), lambda i: (i, 0)),
                  pl.BlockSpec((1, d), lambda i: (0, 0)),
                  pl.BlockSpec((d, n), lambda i: (0, 0))],
        out_specs=[pl.BlockSpec((tm, b - a), lambda i: (i, 0)) for a, b in splits],
        out_shape=[jax.ShapeDtypeStruct((t, b - a), F32) for a, b in splits],
        compiler_params=_cp(("parallel",)),
        name=name,
    )(x, g.reshape(1, d), w)


def _nsa_prep_kernel(zq_ref, zkv_ref, qn_ref, kn_ref,
                     q_out, kvc_out, ks_out, vs_out, kw_out, vw_out, *, ts):
    st = pl.program_id(1)
    zq = zq_ref[0]
    scale = DK ** -0.5
    for h in range(NSA_HEADS):
        q_out[0, h] = (_rms(zq[:, h * DK:(h + 1) * DK], qn_ref[...]) * scale).astype(BF16)
    zkv = zkv_ref[0]

    def piece(i, g):
        o = i * NSA_GROUPS * DK + g * DK
        return zkv[:, o:o + DK]

    pos = st * ts + _iota((ts, SEL_SUPER_BLOCKS), 0)
    blk = _mod_pow2(_div_pow2(pos, L_SLC), SEL_SUPER_BLOCKS)
    onehot = jnp.where(_iota((ts, SEL_SUPER_BLOCKS), 1) == blk, 1.0, 0.0).astype(BF16)
    for g in range(NSA_GROUPS):
        kvc_out[0, 0, g] = piece(0, g)
        kvc_out[1, 0, g] = piece(1, g)
        ks = _rms(piece(2, g), kn_ref[1:2, :]).astype(BF16)
        ks_out[0, g] = jnp.concatenate([ks, onehot], axis=-1)
        vs_out[0, g] = piece(3, g).astype(BF16)
        kw_out[0, g] = _rms(piece(4, g), kn_ref[2:3, :]).astype(BF16)
        vw_out[0, g] = piece(5, g).astype(BF16)


def nsa_prep(zq, zkv, q_norm, k_norm, ts=512):
    b, s, _ = zq.shape
    g = NSA_GROUPS
    hm = lambda bi, si: (bi, 0, si, 0)
    return pl.pallas_call(
        functools.partial(_nsa_prep_kernel, ts=ts),
        grid=(b, s // ts),
        in_specs=[pl.BlockSpec((1, ts, NSA_HEADS * DK), lambda bi, si: (bi, si, 0)),
                  pl.BlockSpec((1, ts, 6 * g * DK), lambda bi, si: (bi, si, 0)),
                  pl.BlockSpec((1, DK), lambda bi, si: (0, 0)),
                  pl.BlockSpec((3, DK), lambda bi, si: (0, 0))],
        out_specs=[pl.BlockSpec((1, NSA_HEADS, ts, DK), hm),
                   pl.BlockSpec((2, 1, g, ts, DK), lambda bi, si: (0, bi, 0, si, 0)),
                   pl.BlockSpec((1, g, ts, 2 * DK), hm),
                   pl.BlockSpec((1, g, ts, DK), hm),
                   pl.BlockSpec((1, g, ts, DK), hm),
                   pl.BlockSpec((1, g, ts, DK), hm)],
        out_shape=[jax.ShapeDtypeStruct((b, NSA_HEADS, s, DK), BF16),
                   jax.ShapeDtypeStruct((2, b, g, s, DK), F32),
                   jax.ShapeDtypeStruct((b, g, s, 2 * DK), BF16),
                   jax.ShapeDtypeStruct((b, g, s, DK), BF16),
                   jax.ShapeDtypeStruct((b, g, s, DK), BF16),
                   jax.ShapeDtypeStruct((b, g, s, DK), BF16)],
        compiler_params=_cp(("parallel", "parallel")),
        name="nsa_prep",
    )(zq, zkv, q_norm.reshape(1, DK), k_norm)


def _cmp_mlp_kernel(x_ref, pe_ref, w1_ref, w2_ref, kn_ref, o_ref, *, nc):
    c = pl.program_id(0)
    x = x_ref[0, 0, 0]
    a = _dot((x + pe_ref[0, 0:1, :]).astype(BF16), w1_ref[0, 0])
    b = _dot((x + pe_ref[0, 1:2, :]).astype(BF16), w1_ref[0, 1])
    pre = a + pltpu.roll(b, nc - 1, 0)
    comp = _dot(_gelu(pre).astype(BF16), w2_ref[0])
    out = jnp.where(c == 0, _rms(comp, kn_ref[...]), comp)
    o_ref[0, 0, 0] = out.astype(BF16)


def cmp_mlp(xc, pe, w1, w2, k_norm0):
    _, b, g, nc, width = xc.shape
    return pl.pallas_call(
        functools.partial(_cmp_mlp_kernel, nc=nc),
        grid=(2, b, g),
        in_specs=[pl.BlockSpec((1, 1, 1, nc, width), lambda c, bi, gi: (c, bi, gi, 0, 0)),
                  pl.BlockSpec((1, 2, width), lambda c, bi, gi: (c, 0, 0)),
                  pl.BlockSpec((1, 2, width, CMP_HIDDEN), lambda c, bi, gi: (c, 0, 0, 0)),
                  pl.BlockSpec((1, CMP_HIDDEN, DK), lambda c, bi, gi: (c, 0, 0)),
                  pl.BlockSpec((1, DK), lambda c, bi, gi: (0, 0))],
        out_specs=pl.BlockSpec((1, 1, 1, nc, DK), lambda c, bi, gi: (c, bi, gi, 0, 0)),
        out_shape=jax.ShapeDtypeStruct((2, b, g, nc, DK), BF16),
        compiler_params=_cp(("parallel", "parallel", "parallel")),
        name="cmp_mlp",
    )(xc, pe, w1, w2, k_norm0.reshape(1, DK))


def _cmp_attn_kernel(q_ref, k_ref, v_ref, c_ref, oc_ref, sb_ref, *, nc, n_slc):
    qb = pl.program_id(2)
    rows = NSA_REP * QB
    q = q_ref[0].reshape(rows, DK)
    s = _dot_nt(q, k_ref[0, 0, 0])
    t = qb * QB + _mod_pow2(_iota((rows, nc), 0), QB)
    valid = _iota((rows, nc), 1) * D_CMP + (L_CMP - 1) <= t
    s = jnp.where(valid, s, NEG)
    e = jnp.where(valid, jnp.exp(s - jnp.max(s, axis=-1, keepdims=True)), 0.0)
    p = e / jnp.maximum(jnp.sum(e, axis=-1, keepdims=True), 1e-30)
    oc_ref[0] = _dot(p.astype(BF16), v_ref[0, 0, 0]).reshape(NSA_REP, QB, DK)

    ps = p[0:QB]
    for r in range(1, NSA_REP):
        ps = ps + p[r * QB:(r + 1) * QB]
    hi = ps.astype(BF16)
    lo = (ps - hi.astype(F32)).astype(BF16)
    imp = _dot(hi, c_ref[...]) + _dot(lo, c_ref[...])

    cur = _div_pow2(qb * QB + _iota((QB, n_slc), 0), L_SLC)
    j = _iota((QB, n_slc), 1)
    forced = (j == 0) | (j == cur) | (j == cur - 1)
    score = jnp.where(j <= cur, imp + jnp.where(forced, FORCE_BONUS, 0.0), NEG)
    work = score
    tau = None
    for _ in range(N_SEL):
        tau = jnp.max(work, axis=-1, keepdims=True)
        work = jnp.where(work >= tau, REMOVED, work)
    selected = (score > 0.5 * NEG) & (score >= tau)
    bias = jnp.where(selected, 0.0, NEG).astype(BF16)
    for si in range(n_slc // SEL_SUPER_BLOCKS):
        sb_ref[0, 0, 0, si] = bias[:, si * SEL_SUPER_BLOCKS:(si + 1) * SEL_SUPER_BLOCKS]


def cmp_attn_select(qn, kc, cmat):
    b, h, s, _ = qn.shape
    g = NSA_GROUPS
    nc = kc.shape[3]
    n_slc = s // L_SLC
    nsup = n_slc // SEL_SUPER_BLOCKS
    nqb = s // QB
    return pl.pallas_call(
        functools.partial(_cmp_attn_kernel, nc=nc, n_slc=n_slc),
        grid=(b, g, nqb),
        in_specs=[pl.BlockSpec((1, NSA_REP, QB, DK), lambda bi, gi, qi: (bi, gi, qi, 0)),
                  pl.BlockSpec((1, 1, 1, nc, DK), lambda bi, gi, qi: (0, bi, gi, 0, 0)),
                  pl.BlockSpec((1, 1, 1, nc, DK), lambda bi, gi, qi: (1, bi, gi, 0, 0)),
                  pl.BlockSpec((nc, n_slc), lambda bi, gi, qi: (0, 0))],
        out_specs=[pl.BlockSpec((1, NSA_REP, QB, DK), lambda bi, gi, qi: (bi, gi, qi, 0)),
                   pl.BlockSpec((1, 1, 1, nsup, QB, SEL_SUPER_BLOCKS),
                                lambda bi, gi, qi: (bi, gi, qi, 0, 0, 0))],
        out_shape=[jax.ShapeDtypeStruct((b, h, s, DK), F32),
                   jax.ShapeDtypeStruct((b, g, nqb, nsup, QB, SEL_SUPER_BLOCKS), BF16)],
        compiler_params=_cp(("parallel", "parallel", "parallel")),
        name="cmp_attn_select",
    )(qn, kc, kc, cmat)


def _win_attn_kernel(q_ref, *refs):
    nkb = WINDOW // QB + 1
    k_refs, v_refs, o_ref = refs[:nkb], refs[nkb:2 * nkb], refs[2 * nkb]
    qb = pl.program_id(2)
    rows = NSA_REP * QB
    nk = nkb * QB
    q = q_ref[0].reshape(rows, DK)
    k = jnp.concatenate([r[0, 0] for r in k_refs], axis=0)
    v = jnp.concatenate([r[0, 0] for r in v_refs], axis=0)
    s = _dot_nt(q, k)
    t = qb * QB + _mod_pow2(_iota((rows, nk), 0), QB)
    kpos = (qb - (nkb - 1)) * QB + _iota((rows, nk), 1)
    delta = t - kpos
    valid = (delta >= 0) & (delta < WINDOW) & (kpos >= 0)
    s = jnp.where(valid, s, NEG)
    e = jnp.where(valid, jnp.exp(s - jnp.max(s, axis=-1, keepdims=True)), 0.0)
    p = e / jnp.maximum(jnp.sum(e, axis=-1, keepdims=True), 1e-30)
    o_ref[0] = _dot(p.astype(BF16), v).reshape(NSA_REP, QB, DK)


def win_attn(qn, kw, vw):
    b, h, s, _ = qn.shape
    g = NSA_GROUPS
    nkb = WINDOW // QB + 1
    nqb = s // QB

    def kv_spec(i):
        return pl.BlockSpec((1, 1, QB, DK),
                            lambda bi, gi, qi: (bi, gi, jnp.maximum(qi - (nkb - 1) + i, 0), 0))

    return pl.pallas_call(
        _win_attn_kernel,
        grid=(b, g, nqb),
        in_specs=[pl.BlockSpec((1, NSA_REP, QB, DK), lambda bi, gi, qi: (bi, gi, qi, 0))]
        + [kv_spec(i) for i in range(nkb)] + [kv_spec(i) for i in range(nkb)],
        out_specs=pl.BlockSpec((1, NSA_REP, QB, DK), lambda bi, gi, qi: (bi, gi, qi, 0)),
        out_shape=jax.ShapeDtypeStruct((b, h, s, DK), F32),
        compiler_params=_cp(("parallel", "parallel", "parallel")),
        name="win_attn",
    )(qn, *([kw] * nkb), *([vw] * nkb))


def _sel_attn_kernel(q_ref, sb_ref, k_ref, v_ref, oc_ref, ow_ref, gl_ref, o_ref,
                     qp_s, m_s, l_s, acc_s):
    qb = pl.program_id(2)
    si = pl.program_id(3)
    rows = NSA_REP * QB
    last = (qb * QB) // SEL_SUPER

    @pl.when(si == 0)
    def _():
        qp_s[:, 0:DK] = q_ref[0].reshape(rows, DK)
        m_s[...] = jnp.full_like(m_s, NEG)
        l_s[...] = jnp.zeros_like(l_s)
        acc_s[...] = jnp.zeros_like(acc_s)

    @pl.when(si <= last)
    def _():
        sb = sb_ref[0, 0, 0, 0]
        for r in range(NSA_REP):
            qp_s[r * QB:(r + 1) * QB, DK:2 * DK] = sb
        qv = qp_s[...]
        t = qb * QB + _mod_pow2(_iota((rows, SEL_SUB), 0), QB)
        lane = _iota((rows, SEL_SUB), 1)
        nsub = jnp.minimum(SEL_SUPER // SEL_SUB,
                           (qb * QB + QB - si * SEL_SUPER + SEL_SUB - 1) // SEL_SUB)

        def body(i, carry):
            off = pl.multiple_of(i * SEL_SUB, SEL_SUB)
            s = _dot_nt(qv, k_ref[0, 0, pl.ds(off, SEL_SUB), :])
            s = jnp.where(si * SEL_SUPER + off + lane <= t, s, NEG)
            m_old = m_s[...]
            m_new = jnp.maximum(m_old, jnp.max(s, axis=-1, keepdims=True))
            alpha = jnp.exp(m_old - m_new)
            p = jnp.exp(s - m_new)
            l_s[...] = alpha * l_s[...] + jnp.sum(p, axis=-1, keepdims=True)
            acc_s[...] = alpha * acc_s[...] + _dot(p.astype(BF16), v_ref[0, 0, pl.ds(off, SEL_SUB), :])
            m_s[...] = m_new
            return carry

        lax.fori_loop(0, nsub, body, 0)

    @pl.when(si == last)
    def _():
        o_s = acc_s[...] / jnp.maximum(l_s[...], 1e-30)
        gate = jax.nn.sigmoid(gl_ref[0].reshape(rows, 3))
        out = (gate[:, 0:1] * oc_ref[0].reshape(rows, DK) + gate[:, 1:2] * o_s
               + gate[:, 2:3] * ow_ref[0].reshape(rows, DK))
        o_ref[0] = out.reshape(NSA_REP, QB, DK)


def sel_attn(qn, selb, ks, vs, oc, ow, gl):
    b, h, s, _ = qn.shape
    g = NSA_GROUPS
    nqb = s // QB
    nsup = s // SEL_SUPER
    qmap = lambda bi, gi, qi, si: (bi, gi, qi, 0)

    def eff(qi, si):
        return jnp.minimum(si, (qi * QB) // SEL_SUPER)

    return pl.pallas_call(
        _sel_attn_kernel,
        grid=(b, g, nqb, nsup),
        in_specs=[pl.BlockSpec((1, NSA_REP, QB, DK), qmap),
                  pl.BlockSpec((1, 1, 1, 1, QB, SEL_SUPER_BLOCKS),
                               lambda bi, gi, qi, si: (bi, gi, qi, eff(qi, si), 0, 0)),
                  pl.BlockSpec((1, 1, SEL_SUPER, 2 * DK), lambda bi, gi, qi, si: (bi, gi, eff(qi, si), 0)),
                  pl.BlockSpec((1, 1, SEL_SUPER, DK), lambda bi, gi, qi, si: (bi, gi, eff(qi, si), 0)),
                  pl.BlockSpec((1, NSA_REP, QB, DK), qmap),
                  pl.BlockSpec((1, NSA_REP, QB, DK), qmap),
                  pl.BlockSpec((1, NSA_REP, QB, 3), qmap)],
        out_specs=pl.BlockSpec((1, NSA_REP, QB, DK), qmap),
        out_shape=jax.ShapeDtypeStruct((b, h, s, DK), F32),
        scratch_shapes=[pltpu.VMEM((NSA_REP * QB, 2 * DK), BF16),
                        pltpu.VMEM((NSA_REP * QB, 1), F32),
                        pltpu.VMEM((NSA_REP * QB, 1), F32),
                        pltpu.VMEM((NSA_REP * QB, DK), F32)],
        compiler_params=_cp(("parallel", "parallel", "parallel", "arbitrary")),
        name="sel_attn",
    )(qn, selb, ks, vs, oc, ow, gl)


def _sgu_kernel(z_ref, g_ref, b_ref, w_ref, bias_ref, o_ref, *, tm):
    z = _gelu(z_ref[...])
    u = z[:, :SGU_WIDTH]
    v = z[:, SGU_WIDTH:]
    mu = jnp.mean(v, axis=-1, keepdims=True)
    var = jnp.mean(jnp.square(v - mu), axis=-1, keepdims=True)
    vn = ((v - mu) * lax.rsqrt(var + EPS) * g_ref[...] + b_ref[...]).astype(BF16)
    tril = _iota((SGU_CHUNK, SGU_CHUNK), 1) <= _iota((SGU_CHUNK, SGU_CHUNK), 0)
    wm = [jnp.where(tril, w_ref[gi], 0.0).astype(BF16) for gi in range(SGU_GROUPS)]
    grp = _div_pow2(_iota((SGU_CHUNK, SGU_WIDTH), 1), SGU_WIDTH // SGU_GROUPS)
    for ch in range(tm // SGU_CHUNK):
        rs = slice(ch * SGU_CHUNK, (ch + 1) * SGU_CHUNK)
        mixed = bias_ref[...]
        for gi in range(SGU_GROUPS):
            mixed = mixed + jnp.where(grp == gi, _dot(wm[gi], vn[rs]), 0.0)
        o_ref[rs, :] = u[rs] * mixed


def sgu(z, ln_g, ln_b, w, bias_full, tm=512):
    t = z.shape[0]
    return pl.pallas_call(
        functools.partial(_sgu_kernel, tm=tm),
        grid=(t // tm,),
        in_specs=[pl.BlockSpec((tm, 2 * SGU_WIDTH), lambda i: (i, 0)),
                  pl.BlockSpec((1, SGU_WIDTH), lambda i: (0, 0)),
                  pl.BlockSpec((1, SGU_WIDTH), lambda i: (0, 0)),
                  pl.BlockSpec((SGU_GROUPS, SGU_CHUNK, SGU_CHUNK), lambda i: (0, 0, 0)),
                  pl.BlockSpec((SGU_CHUNK, SGU_WIDTH), lambda i: (0, 0))],
        out_specs=pl.BlockSpec((tm, SGU_WIDTH), lambda i: (i, 0)),
        out_shape=jax.ShapeDtypeStruct((t, SGU_WIDTH), F32),
        compiler_params=_cp(("parallel",)),
        name="sgu",
    )(z, ln_g.reshape(1, -1), ln_b.reshape(1, -1), w, bias_full)


def _pool_kernel(p_ref, h_ref, w_ref, sc_ref, o_ref, *, tp):
    i = pl.program_id(1)
    halo_rows = POOL_WINDOWS[-1]
    p = p_ref[0]
    halo = jnp.where(i > 0, h_ref[0], 0.0)
    ext = jnp.concatenate([halo, p], axis=0)
    sums = [ext]
    shift = 1
    for _ in POOL_WINDOWS:
        prev = sums[-1]
        sums.append(prev + pltpu.roll(prev, shift, 0))
        shift *= 2
    pos = i * tp + _iota((tp, POOL_WIDTH), 0)
    grp = _div_pow2(_iota((tp, POOL_WIDTH), 1), POOL_WIDTH // len(POOL_WINDOWS))
    d = jnp.zeros((tp, POOL_WIDTH), F32)
    for gi, w in enumerate(POOL_WINDOWS):
        mean = sums[gi + 1][halo_rows:halo_rows + tp] / jnp.minimum(pos + 1, w).astype(F32)
        d = jnp.where(grp == gi, mean, d)
    d = d - p
    o_ref[0] = _dot(d.astype(BF16), w_ref[...]) * sc_ref[...]


def pool(p, w_bd, scale, tp=512):
    b, s, c = p.shape
    halo_rows = POOL_WINDOWS[-1]
    return pl.pallas_call(
        functools.partial(_pool_kernel, tp=tp),
        grid=(b, s // tp),
        in_specs=[pl.BlockSpec((1, tp, c), lambda bi, i: (bi, i, 0)),
                  pl.BlockSpec((1, halo_rows, c),
                               lambda bi, i: (bi, jnp.maximum(i * (tp // halo_rows) - 1, 0), 0)),
                  pl.BlockSpec((c, c), lambda bi, i: (0, 0)),
                  pl.BlockSpec((1, c), lambda bi, i: (0, 0))],
        out_specs=pl.BlockSpec((1, tp, c), lambda bi, i: (bi, i, 0)),
        out_shape=jax.ShapeDtypeStruct((b, s, c), F32),
        compiler_params=_cp(("parallel", "parallel")),
        name="pool",
    )(p, p, w_bd, scale.reshape(1, c))


def _merge_kernel(x_ref, oa_ref, ob_ref, oc_ref, zm_ref, la_ref, lb_ref, lc_ref, wo_ref, o_ref):
    d = D_MODEL
    oa = jnp.concatenate([oa_ref[0, h].astype(BF16) for h in range(NSA_HEADS)], axis=-1)
    zm = zm_ref[0]
    merged = (jax.nn.sigmoid(zm[:, 0:d]) * _dot(oa, la_ref[...])
              + jax.nn.sigmoid(zm[:, d:2 * d]) * _dot(ob_ref[0].astype(BF16), lb_ref[...])
              + jax.nn.sigmoid(zm[:, 2 * d:3 * d]) * _dot(oc_ref[0].astype(BF16), lc_ref[...]))
    o_ref[0] = x_ref[0] + _dot(merged.astype(BF16), wo_ref[...])


def merge(x, oa, ob, oc, zm, la, lb, lc, wo, tm=512):
    b, s, d = x.shape
    row = lambda bi, i: (bi, i, 0)
    full = lambda bi, i: (0, 0)
    return pl.pallas_call(
        _merge_kernel,
        grid=(b, s // tm),
        in_specs=[pl.BlockSpec((1, tm, d), row),
                  pl.BlockSpec((1, NSA_HEADS, tm, DK), lambda bi, i: (bi, 0, i, 0)),
                  pl.BlockSpec((1, tm, SGU_WIDTH), row),
                  pl.BlockSpec((1, tm, POOL_WIDTH), row),
                  pl.BlockSpec((1, tm, 3 * d), row),
                  pl.BlockSpec(la.shape, full), pl.BlockSpec(lb.shape, full),
                  pl.BlockSpec(lc.shape, full), pl.BlockSpec(wo.shape, full)],
        out_specs=pl.BlockSpec((1, tm, d), row),
        out_shape=jax.ShapeDtypeStruct((b, s, d), F32),
        compiler_params=_cp(("parallel", "parallel")),
        name="merge",
    )(x, oa, ob, oc, zm, la, lb, lc, wo)


def _mem_kv_kernel(m_ref, g_ref, wk_ref, wv_ref, kn_ref, k_out, v_out):
    mh = _rms(m_ref[0], g_ref[...]).astype(BF16)
    k = _dot(mh, wk_ref[...])
    for h in range(XA_HEADS):
        hs = slice(h * XA_HD, (h + 1) * XA_HD)
        k_out[0, :, hs] = _rms(k[:, hs], kn_ref[...]).astype(BF16)
    v_out[0] = _dot(mh, wv_ref[...]).astype(BF16)


def mem_kv(mem, g, wk, wv, k_norm):
    b, m, d = mem.shape
    full = lambda bi: (0, 0)
    return pl.pallas_call(
        _mem_kv_kernel,
        grid=(b,),
        in_specs=[pl.BlockSpec((1, m, d), lambda bi: (bi, 0, 0)),
                  pl.BlockSpec((1, d), full),
                  pl.BlockSpec(wk.shape, full), pl.BlockSpec(wv.shape, full),
                  pl.BlockSpec((1, XA_HD), full)],
        out_specs=[pl.BlockSpec((1, m, XA_WIDTH), lambda bi: (bi, 0, 0))] * 2,
        out_shape=[jax.ShapeDtypeStruct((b, m, XA_WIDTH), BF16)] * 2,
        compiler_params=_cp(("parallel",)),
        name="mem_kv",
    )(mem, g.reshape(1, d), wk, wv, k_norm.reshape(1, XA_HD))


def _xattn_kernel(x_ref, g_ref, wq_ref, qn_ref, k_ref, v_ref, wo_ref, o_ref):
    x = x_ref[0]
    q = _dot(_rms(x, g_ref[...]).astype(BF16), wq_ref[...])
    k = k_ref[0]
    v = v_ref[0]
    outs = []
    for h in range(XA_HEADS):
        hs = slice(h * XA_HD, (h + 1) * XA_HD)
        qh = (_rms(q[:, hs], qn_ref[...])).astype(BF16)
        s = _dot_nt(qh, k[:, hs]) * (XA_HD ** -0.5)
        e = jnp.exp(s - jnp.max(s, axis=-1, keepdims=True))
        p = e / jnp.sum(e, axis=-1, keepdims=True)
        outs.append(_dot(p.astype(BF16), v[:, hs]).astype(BF16))
    o = jnp.concatenate(outs, axis=-1)
    o_ref[0] = x + _dot(o, wo_ref[...])


def xattn(x, g, wq, q_norm, k, v, wo, tm=512):
    b, s, d = x.shape
    m = k.shape[1]
    row = lambda bi, i: (bi, i, 0)
    full = lambda bi, i: (0, 0)
    return pl.pallas_call(
        _xattn_kernel,
        grid=(b, s // tm),
        in_specs=[pl.BlockSpec((1, tm, d), row),
                  pl.BlockSpec((1, d), full),
                  pl.BlockSpec(wq.shape, full),
                  pl.BlockSpec((1, XA_HD), full),
                  pl.BlockSpec((1, m, XA_WIDTH), lambda bi, i: (bi, 0, 0)),
                  pl.BlockSpec((1, m, XA_WIDTH), lambda bi, i: (bi, 0, 0)),
                  pl.BlockSpec(wo.shape, full)],
        out_specs=pl.BlockSpec((1, tm, d), row),
        out_shape=jax.ShapeDtypeStruct((b, s, d), F32),
        compiler_params=_cp(("parallel", "parallel")),
        name="xattn",
    )(x, g.reshape(1, d), wq, q_norm.reshape(1, XA_HD), k, v, wo)


def _top_rows(s, n):
    out = []
    work = s
    for _ in range(n):
        m = jnp.max(work, axis=0, keepdims=True)
        out.append(m)
        work = jnp.where(work >= m, REMOVED, work)
    return out


def _stack_rows(rows, pad_rows):
    tt = rows[0].shape[1]
    rowi = _iota((pad_rows, tt), 0)
    out = jnp.full((pad_rows, tt), REMOVED, F32)
    for i, r in enumerate(rows[:pad_rows]):
        out = jnp.where(rowi == i, r, out)
    return out


def _peer_kernel(x_ref, g_ref, wq_ref, k1_ref, k2_ref, u_ref, vt_ref, o_ref,
                 hb_s, p2_s, n1_s, e1_s, e2_s, acc_s, *, tt, ch):
    c = pl.program_id(1)
    ntop = PEER_TOPK + 1
    pad = 24

    @pl.when(c == 0)
    def _():
        hb = _rms(x_ref[...], g_ref[...]).astype(BF16)
        hb_s[...] = hb
        q = _dot(hb, wq_ref[...])
        acc_s[...] = jnp.zeros_like(acc_s)
        row_full = _iota((pad, tt), 0)
        row_8 = _iota((8, tt), 0)
        for hd in range(PEER_HEADS):
            qh = q[:, hd * 2 * PEER_HALF:(hd + 1) * 2 * PEER_HALF].astype(BF16)
            s1 = _dot_nt(k1_ref[...], qh)
            s2 = _dot_nt(k2_ref[...], qh)
            v1 = _top_rows(s1, ntop)
            v2 = _top_rows(s2, ntop)
            v2_full = _stack_rows(v2, pad)
            v2_8 = _stack_rows(v2, 8)
            pieces = [jnp.where(row_full < ntop, v1[0] + v2_full, REMOVED)]
            for i in range(1, ntop):
                pieces.append(jnp.where(row_8 < ntop // (i + 1), v1[i] + v2_8, REMOVED))
            cand = jnp.concatenate(pieces, axis=0)
            top = v1[0] + v2[0]
            z = jnp.zeros((1, tt), F32)
            c_prev = top
            c_last = top
            work = cand
            for r in range(ntop):
                m = jnp.max(work, axis=0, keepdims=True)
                work = jnp.where(work >= m, REMOVED, work)
                if r < PEER_TOPK:
                    z = z + jnp.exp(m - top)
                c_prev, c_last = c_last, m
            thr = 0.5 * (c_prev + c_last)
            p2_s[hd] = s2 - thr
            n1_s[hd] = -s1
            e1_s[hd] = jnp.exp(s1 - v1[0])
            e2_s[hd] = jnp.exp(s2 - v2[0]) / z

    act = _gelu(_dot_nt(u_ref[...], hb_s[...]))
    ys = []
    for al in range(ch // PEER_KEYS):
        a = c * (ch // PEER_KEYS) + al
        wc = jnp.zeros((PEER_KEYS, tt), F32)
        for hd in range(PEER_HEADS):
            n1 = n1_s[hd, pl.ds(a, 1), :]
            e1 = e1_s[hd, pl.ds(a, 1), :]
            wc = wc + jnp.where(p2_s[hd] >= n1, e2_s[hd] * e1, 0.0)
        ys.append((wc * act[al * PEER_KEYS:(al + 1) * PEER_KEYS]).astype(BF16))
    y = jnp.concatenate(ys, axis=0)
    acc_s[...] += _dot(vt_ref[...], y)

    @pl.when(c == pl.num_programs(1) - 1)
    def _():
        o_ref[...] = x_ref[...] + acc_s[...].T


def peer(x, g, wq, k1p, k2p, u, vt, tt=PEER_TT, ch=PEER_CH):
    t, d = x.shape
    ne = u.shape[0]
    return pl.pallas_call(
        functools.partial(_peer_kernel, tt=tt, ch=ch),
        grid=(t // tt, ne // ch),
        in_specs=[pl.BlockSpec((tt, d), lambda i, c: (i, 0)),
                  pl.BlockSpec((1, d), lambda i, c: (0, 0)),
                  pl.BlockSpec(wq.shape, lambda i, c: (0, 0)),
                  pl.BlockSpec(k1p.shape, lambda i, c: (0, 0)),
                  pl.BlockSpec(k2p.shape, lambda i, c: (0, 0)),
                  pl.BlockSpec((ch, d), lambda i, c: (c, 0)),
                  pl.BlockSpec((d, ch), lambda i, c: (0, c))],
        out_specs=pl.BlockSpec((tt, d), lambda i, c: (i, 0)),
        out_shape=jax.ShapeDtypeStruct((t, d), F32),
        scratch_shapes=[pltpu.VMEM((tt, d), BF16),
                        pltpu.VMEM((PEER_HEADS, PEER_KEYS, tt), F32),
                        pltpu.VMEM((PEER_HEADS, PEER_KEYS, tt), F32),
                        pltpu.VMEM((PEER_HEADS, PEER_KEYS, tt), F32),
                        pltpu.VMEM((PEER_HEADS, PEER_KEYS, tt), F32),
                        pltpu.VMEM((d, tt), F32)],
        compiler_params=_cp(("parallel", "arbitrary")),
        name="peer",
    )(x, g.reshape(1, d), wq, k1p, k2p, u, vt)


_IN_WIDTHS = (NSA_HEADS * DK, 6 * NSA_GROUPS * DK, 3 * NSA_HEADS, 2 * SGU_WIDTH, POOL_WIDTH, 3 * D_MODEL)


def _in_proj_layout():
    splits, src = [], []
    o_src = o_dst = 0
    for w in _IN_WIDTHS:
        wp = -(-w // LANES) * LANES
        splits.append((o_dst, o_dst + wp))
        src.append((o_src, o_src + w))
        o_src += w
        o_dst += wp
    return splits, src, o_dst


def _cmp_to_slc(s):
    n_cmp_pad = s // D_CMP
    n_slc = s // L_SLC
    c0 = np.arange(n_cmp_pad)[:, None] * D_CMP
    s0 = np.arange(n_slc)[None, :] * L_SLC
    overlap = np.minimum(c0 + L_CMP, s0 + L_SLC) - np.maximum(c0, s0)
    m = np.maximum(overlap, 0) / D_CMP
    m[n_cmp_pad - 1] = 0.0
    return jnp.asarray(m, dtype=BF16)


def nsa_layer(zq, zkv, zg, q_norm, k_norm, cmp_pe, cmp_w1, cmp_w2, b, s):
    qn, kvc, ks, vs, kw, vw = nsa_prep(zq.reshape(b, s, -1), zkv.reshape(b, s, -1), q_norm, k_norm)
    nc = s // D_CMP
    xc = kvc.reshape(2, b, NSA_GROUPS, nc, D_CMP * DK)
    pe = cmp_pe.reshape(2, 2, D_CMP * DK)
    w1 = cmp_w1.reshape(2, 2, D_CMP * DK, CMP_HIDDEN).astype(BF16)
    kc = cmp_mlp(xc, pe, w1, cmp_w2.astype(BF16), k_norm[0])
    oc, selb = cmp_attn_select(qn, kc, _cmp_to_slc(s))
    ow = win_attn(qn, kw, vw)
    gl = zg[:, :3 * NSA_HEADS].reshape(b, s, NSA_HEADS, 3).transpose(0, 2, 1, 3)
    return sel_attn(qn, selb, ks, vs, oc, ow, gl)


def kernel(x, mem, mix_norm, w_in, nsa_q_norm, nsa_k_norm, cmp_pe, cmp_w1, cmp_w2, sgu_ln_g, sgu_ln_b, sgu_w, sgu_b, pool_w, pool_scale, lift_a, lift_b, lift_c, w_out, xa_norm, mem_norm, xa_wq, xa_wk, xa_wv, xa_q_norm, xa_k_norm, xa_wo, ffn_norm, peer_wq, peer_keys1, peer_keys2, peer_u, peer_v):
    b, s, d = x.shape
    t = b * s
    depth = w_in.shape[0]
    splits, src, n_pad = _in_proj_layout()
    zeros_half = jnp.zeros((PEER_KEYS, PEER_HALF), BF16)
    for l in range(depth):
        w_parts = []
        for (a0, a1), (d0, d1) in zip(src, splits):
            w_parts.append(jnp.pad(w_in[l][:, a0:a1], ((0, 0), (0, (d1 - d0) - (a1 - a0)))))
        w_pad = jnp.concatenate(w_parts, axis=1).astype(BF16)
        zq, zkv, zg, zs, zp, zm = norm_matmul(x.reshape(t, d), mix_norm[l], w_pad, splits, 256, "in_proj")

        oa = nsa_layer(zq, zkv, zg, nsa_q_norm[l], nsa_k_norm[l], cmp_pe[l], cmp_w1[l], cmp_w2[l], b, s)
        bias_full = jnp.repeat(sgu_b[l].T, SGU_WIDTH // SGU_GROUPS, axis=1)
        ob = sgu(zs, sgu_ln_g[l], sgu_ln_b[l], sgu_w[l], bias_full)
        cg = POOL_WIDTH // len(POOL_WINDOWS)
        w_bd = jnp.zeros((POOL_WIDTH, POOL_WIDTH), F32)
        for gi in range(len(POOL_WINDOWS)):
            w_bd = w_bd.at[gi * cg:(gi + 1) * cg, gi * cg:(gi + 1) * cg].set(pool_w[l, gi])
        oc = pool(zp.reshape(b, s, POOL_WIDTH), w_bd.astype(BF16), pool_scale[l])
        x = merge(x, oa, ob.reshape(b, s, SGU_WIDTH), oc, zm.reshape(b, s, 3 * d),
                  lift_a[l].astype(BF16), lift_b[l].astype(BF16), lift_c[l].astype(BF16),
                  w_out[l].astype(BF16))

        mk, mv = mem_kv(mem, mem_norm[l], xa_wk[l].astype(BF16), xa_wv[l].astype(BF16), xa_k_norm[l])
        x = xattn(x, xa_norm[l], xa_wq[l].astype(BF16), xa_q_norm[l], mk, mv, xa_wo[l].astype(BF16))

        k1p = jnp.concatenate([peer_keys1[l].astype(BF16), zeros_half], axis=1)
        k2p = jnp.concatenate([zeros_half, peer_keys2[l].astype(BF16)], axis=1)
        x = peer(x.reshape(t, d), ffn_norm[l], peer_wq[l].astype(BF16), k1p, k2p,
                 peer_u[l].astype(BF16), peer_v[l].T.astype(BF16)).reshape(b, s, d)
    return x
```

```python
import functools

import numpy as np
import jax
import jax.numpy as jnp
from jax import lax
from jax.experimental import pallas as pl
from jax.experimental.pallas import tpu as pltpu

F32 = jnp.float32
BF16 = jnp.bfloat16

EPS = 1e-6
LOG2E = 1.4426950408889634
NEG = -1e30
REMOVED = -3e38

D_MODEL = 1024
DK = 64
NSA_HEADS = 8
NSA_GROUPS = 2
NSA_REP = NSA_HEADS // NSA_GROUPS
L_CMP, D_CMP = 32, 16
CMP_HIDDEN = 128
L_SLC = 64
N_SEL = 16
WINDOW = 512
QB = 128
FORCE_BONUS = 1.0e3
SGU_WIDTH = 256
SGU_GROUPS = 4
SGU_CHUNK = 128
POOL_WIDTH = 256
POOL_WINDOWS = (2, 4, 8, 16)
XA_HEADS, XA_HD = 4, 128
XA_WIDTH = XA_HEADS * XA_HD
PEER_HEADS = 8
PEER_KEYS = 128
PEER_TOPK = 16
PEER_HALF = 64

LANES = 128
SEL_SUPER_BLOCKS = 64
SEL_SUPER = SEL_SUPER_BLOCKS * L_SLC
SEL_SUB = 512
PEER_TT = 512
PEER_CH = 1024
VMEM_LIMIT = 56 * 1024 * 1024


def _cp(sem):
    return pltpu.CompilerParams(dimension_semantics=sem, vmem_limit_bytes=VMEM_LIMIT)


def _gelu(x):
    return 0.5 * x * (1.0 + jnp.tanh(0.7978845608028654 * (x + 0.044715 * (x * x * x))))


def _rms(x, g):
    return x * lax.rsqrt(jnp.mean(x * x, axis=-1, keepdims=True) + EPS) * g


def _dot(a, b):
    return jnp.dot(a, b, preferred_element_type=F32)


def _dot_nt(a, b):
    return lax.dot_general(a, b, (((1,), (1,)), ((), ())), preferred_element_type=F32)


def _iota(shape, dim):
    return lax.broadcasted_iota(jnp.int32, shape, dim)


def _div_pow2(x, n):
    assert n & (n - 1) == 0
    return lax.shift_right_logical(x, jnp.int32(n.bit_length() - 1))


def _mod_pow2(x, n):
    assert n & (n - 1) == 0
    return x & (n - 1)


def _norm_matmul_kernel(x_ref, g_ref, w_ref, *o_refs, splits):
    hb = _rms(x_ref[...], g_ref[...]).astype(BF16)
    for o_ref, (a, b) in zip(o_refs, splits):
        o_ref[...] = _dot(hb, w_ref[:, a:b]).astype(o_ref.dtype)


def norm_matmul(x, g, w, splits, tm, name):
    t, d = x.shape
    n = w.shape[1]
    return pl.pallas_call(
        functools.partial(_norm_matmul_kernel, splits=splits),
        grid=(t // tm,),
        in_specs=[pl.BlockSpec((tm, d), lambda i: (i, 0)),
                  pl.BlockSpec((1, d), lambda i: (0, 0)),
                  pl.BlockSpec((d, n), lambda i: (0, 0))],
        out_specs=[pl.BlockSpec((tm, b - a), lambda i: (i, 0)) for a, b in splits],
        out_shape=[jax.ShapeDtypeStruct((t, b - a), F32) for a, b in splits],
        compiler_params=_cp(("parallel",)),
        name=name,
    )(x, g.reshape(1, d), w)


def _nsa_prep_kernel(zq_ref, zkv_ref, qn_ref, kn_ref,
                     q_out, kvc_out, ks_out, vs_out, kw_out, vw_out, *, ts):
    st = pl.program_id(1)
    zq = zq_ref[0]
    scale = DK ** -0.5 * LOG2E
    for h in range(NSA_HEADS):
        q_out[0, h] = (_rms(zq[:, h * DK:(h + 1) * DK], qn_ref[...]) * scale).astype(BF16)
    zkv = zkv_ref[0]

    def piece(i, g):
        o = i * NSA_GROUPS * DK + g * DK
        return zkv[:, o:o + DK]

    pos = st * ts + _iota((ts, SEL_SUPER_BLOCKS), 0)
    blk = _mod_pow2(_div_pow2(pos, L_SLC), SEL_SUPER_BLOCKS)
    onehot = jnp.where(_iota((ts, SEL_SUPER_BLOCKS), 1) == blk, 1.0, 0.0).astype(BF16)
    ones_col = jnp.where(_iota((ts, DK), 1) == 0, 1.0, 0.0)
    for g in range(NSA_GROUPS):
        kvc_out[0, 0, g] = piece(0, g)
        kvc_out[1, 0, g] = piece(1, g)
        ks = _rms(piece(2, g), kn_ref[1:2, :]).astype(BF16)
        ks_out[0, g] = jnp.concatenate([ks, onehot], axis=-1)
        vs_out[0, g] = jnp.concatenate([piece(3, g), ones_col], axis=-1).T.astype(BF16)
        kw_out[0, g] = _rms(piece(4, g), kn_ref[2:3, :]).astype(BF16)
        vw_out[0, g] = piece(5, g).astype(BF16)


def nsa_prep(zq, zkv, q_norm, k_norm, ts=512):
    b, s, _ = zq.shape
    g = NSA_GROUPS
    hm = lambda bi, si: (bi, 0, si, 0)
    return pl.pallas_call(
        functools.partial(_nsa_prep_kernel, ts=ts),
        grid=(b, s // ts),
        in_specs=[pl.BlockSpec((1, ts, NSA_HEADS * DK), lambda bi, si: (bi, si, 0)),
                  pl.BlockSpec((1, ts, 6 * g * DK), lambda bi, si: (bi, si, 0)),
                  pl.BlockSpec((1, DK), lambda bi, si: (0, 0)),
                  pl.BlockSpec((3, DK), lambda bi, si: (0, 0))],
        out_specs=[pl.BlockSpec((1, NSA_HEADS, ts, DK), hm),
                   pl.BlockSpec((2, 1, g, ts, DK), lambda bi, si: (0, bi, 0, si, 0)),
                   pl.BlockSpec((1, g, ts, 2 * DK), hm),
                   pl.BlockSpec((1, g, 2 * DK, ts), lambda bi, si: (bi, 0, 0, si)),
                   pl.BlockSpec((1, g, ts, DK), hm),
                   pl.BlockSpec((1, g, ts, DK), hm)],
        out_shape=[jax.ShapeDtypeStruct((b, NSA_HEADS, s, DK), BF16),
                   jax.ShapeDtypeStruct((2, b, g, s, DK), F32),
                   jax.ShapeDtypeStruct((b, g, s, 2 * DK), BF16),
                   jax.ShapeDtypeStruct((b, g, 2 * DK, s), BF16),
                   jax.ShapeDtypeStruct((b, g, s, DK), BF16),
                   jax.ShapeDtypeStruct((b, g, s, DK), BF16)],
        compiler_params=_cp(("parallel", "parallel")),
        name="nsa_prep",
    )(zq, zkv, q_norm.reshape(1, DK), k_norm)


def _cmp_mlp_kernel(x_ref, pe_ref, w1_ref, w2_ref, kn_ref, o_ref, *, nc):
    c = pl.program_id(0)
    x = x_ref[0, 0, 0]
    a = _dot((x + pe_ref[0, 0:1, :]).astype(BF16), w1_ref[0, 0])
    b = _dot((x + pe_ref[0, 1:2, :]).astype(BF16), w1_ref[0, 1])
    pre = a + pltpu.roll(b, nc - 1, 0)
    comp = _dot(_gelu(pre).astype(BF16), w2_ref[0])
    out = jnp.where(c == 0, _rms(comp, kn_ref[...]), comp)
    o_ref[0, 0, 0] = out.astype(BF16)


def cmp_mlp(xc, pe, w1, w2, k_norm0):
    _, b, g, nc, width = xc.shape
    return pl.pallas_call(
        functools.partial(_cmp_mlp_kernel, nc=nc),
        grid=(2, b, g),
        in_specs=[pl.BlockSpec((1, 1, 1, nc, width), lambda c, bi, gi: (c, bi, gi, 0, 0)),
                  pl.BlockSpec((1, 2, width), lambda c, bi, gi: (c, 0, 0)),
                  pl.BlockSpec((1, 2, width, CMP_HIDDEN), lambda c, bi, gi: (c, 0, 0, 0)),
                  pl.BlockSpec((1, CMP_HIDDEN, DK), lambda c, bi, gi: (c, 0, 0)),
                  pl.BlockSpec((1, DK), lambda c, bi, gi: (0, 0))],
        out_specs=pl.BlockSpec((1, 1, 1, nc, DK), lambda c, bi, gi: (c, bi, gi, 0, 0)),
        out_shape=jax.ShapeDtypeStruct((2, b, g, nc, DK), BF16),
        compiler_params=_cp(("parallel", "parallel", "parallel")),
        name="cmp_mlp",
    )(xc, pe, w1, w2, k_norm0.reshape(1, DK))


def _cmp_attn_kernel(q_ref, k_ref, v_ref, c_ref, oc_ref, sb_ref, *, nc, n_slc):
    qb = pl.program_id(2)
    rows = NSA_REP * QB
    q = q_ref[0].reshape(rows, DK)
    s = _dot_nt(q, k_ref[0, 0, 0])
    t = qb * QB + _mod_pow2(_iota((rows, nc), 0), QB)
    valid = _iota((rows, nc), 1) * D_CMP + (L_CMP - 1) <= t
    s = jnp.where(valid, s, NEG)
    e = jnp.where(valid, jnp.exp2(s - jnp.max(s, axis=-1, keepdims=True)), 0.0)
    p = e / jnp.maximum(jnp.sum(e, axis=-1, keepdims=True), 1e-30)
    oc_ref[0] = _dot(p.astype(BF16), v_ref[0, 0, 0]).reshape(NSA_REP, QB, DK)

    ps = p[0:QB]
    for r in range(1, NSA_REP):
        ps = ps + p[r * QB:(r + 1) * QB]
    hi = ps.astype(BF16)
    lo = (ps - hi.astype(F32)).astype(BF16)
    imp = _dot(hi, c_ref[...]) + _dot(lo, c_ref[...])

    cur = _div_pow2(qb * QB + _iota((QB, n_slc), 0), L_SLC)
    j = _iota((QB, n_slc), 1)
    forced = (j == 0) | (j == cur) | (j == cur - 1)
    score = jnp.where(j <= cur, imp + jnp.where(forced, FORCE_BONUS, 0.0), NEG)
    work = score
    tau = None
    for _ in range(N_SEL):
        tau = jnp.max(work, axis=-1, keepdims=True)
        work = jnp.where(work >= tau, REMOVED, work)
    selected = (score > 0.5 * NEG) & (score >= tau)
    bias = jnp.where(selected, 0.0, NEG).astype(BF16)
    for si in range(n_slc // SEL_SUPER_BLOCKS):
        sb_ref[0, 0, 0, si] = bias[:, si * SEL_SUPER_BLOCKS:(si + 1) * SEL_SUPER_BLOCKS]


def cmp_attn_select(qn, kc, cmat):
    b, h, s, _ = qn.shape
    g = NSA_GROUPS
    nc = kc.shape[3]
    n_slc = s // L_SLC
    nsup = n_slc // SEL_SUPER_BLOCKS
    nqb = s // QB
    return pl.pallas_call(
        functools.partial(_cmp_attn_kernel, nc=nc, n_slc=n_slc),
        grid=(b, g, nqb),
        in_specs=[pl.BlockSpec((1, NSA_REP, QB, DK), lambda bi, gi, qi: (bi, gi, qi, 0)),
                  pl.BlockSpec((1, 1, 1, nc, DK), lambda bi, gi, qi: (0, bi, gi, 0, 0)),
                  pl.BlockSpec((1, 1, 1, nc, DK), lambda bi, gi, qi: (1, bi, gi, 0, 0)),
                  pl.BlockSpec((nc, n_slc), lambda bi, gi, qi: (0, 0))],
        out_specs=[pl.BlockSpec((1, NSA_REP, QB, DK), lambda bi, gi, qi: (bi, gi, qi, 0)),
                   pl.BlockSpec((1, 1, 1, nsup, QB, SEL_SUPER_BLOCKS),
                                lambda bi, gi, qi: (bi, gi, qi, 0, 0, 0))],
        out_shape=[jax.ShapeDtypeStruct((b, h, s, DK), F32),
                   jax.ShapeDtypeStruct((b, g, nqb, nsup, QB, SEL_SUPER_BLOCKS), BF16)],
        compiler_params=_cp(("parallel", "parallel", "parallel")),
        name="cmp_attn_select",
    )(qn, kc, kc, cmat)


def _win_attn_kernel(q_ref, *refs):
    nkb = WINDOW // QB + 1
    k_refs, v_refs, o_ref = refs[:nkb], refs[nkb:2 * nkb], refs[2 * nkb]
    qb = pl.program_id(2)
    rows = NSA_REP * QB
    nk = nkb * QB
    q = q_ref[0].reshape(rows, DK)
    k = jnp.concatenate([r[0, 0] for r in k_refs], axis=0)
    v = jnp.concatenate([r[0, 0] for r in v_refs], axis=0)
    s = _dot_nt(q, k)
    t = qb * QB + _mod_pow2(_iota((rows, nk), 0), QB)
    kpos = (qb - (nkb - 1)) * QB + _iota((rows, nk), 1)
    delta = t - kpos
    valid = (delta >= 0) & (delta < WINDOW) & (kpos >= 0)
    s = jnp.where(valid, s, NEG)
    e = jnp.where(valid, jnp.exp2(s - jnp.max(s, axis=-1, keepdims=True)), 0.0)
    p = e / jnp.maximum(jnp.sum(e, axis=-1, keepdims=True), 1e-30)
    o_ref[0] = _dot(p.astype(BF16), v).reshape(NSA_REP, QB, DK)


def win_attn(qn, kw, vw):
    b, h, s, _ = qn.shape
    g = NSA_GROUPS
    nkb = WINDOW // QB + 1
    nqb = s // QB

    def kv_spec(i):
        return pl.BlockSpec((1, 1, QB, DK),
                            lambda bi, gi, qi: (bi, gi, jnp.maximum(qi - (nkb - 1) + i, 0), 0))

    return pl.pallas_call(
        _win_attn_kernel,
        grid=(b, g, nqb),
        in_specs=[pl.BlockSpec((1, NSA_REP, QB, DK), lambda bi, gi, qi: (bi, gi, qi, 0))]
        + [kv_spec(i) for i in range(nkb)] + [kv_spec(i) for i in range(nkb)],
        out_specs=pl.BlockSpec((1, NSA_REP, QB, DK), lambda bi, gi, qi: (bi, gi, qi, 0)),
        out_shape=jax.ShapeDtypeStruct((b, h, s, DK), F32),
        compiler_params=_cp(("parallel", "parallel", "parallel")),
        name="win_attn",
    )(qn, *([kw] * nkb), *([vw] * nkb))


def _sel_attn_kernel(q_ref, sb_ref, k_ref, vt_ref, oc_ref, ow_ref, gl_ref, o_ref,
                     qp_s, m_s, acc_s):
    qb = pl.program_id(2)
    si = pl.program_id(3)
    rows = NSA_REP * QB
    last = (qb * QB) // SEL_SUPER

    @pl.when(si == 0)
    def _():
        qp_s[:, 0:DK] = q_ref[0].reshape(rows, DK)
        m_s[...] = jnp.full_like(m_s, NEG)
        acc_s[...] = jnp.zeros_like(acc_s)

    @pl.when(si <= last)
    def _():
        sb = sb_ref[0, 0, 0, 0]
        for r in range(NSA_REP):
            qp_s[r * QB:(r + 1) * QB, DK:2 * DK] = sb
        qv = qp_s[...]
        nsub = jnp.minimum(SEL_SUPER // SEL_SUB,
                           (qb * QB + QB - si * SEL_SUPER + SEL_SUB - 1) // SEL_SUB)

        def step(i, causal):
            off = pl.multiple_of(i * SEL_SUB, SEL_SUB)
            st = _dot_nt(k_ref[0, 0, pl.ds(off, SEL_SUB), :], qv)
            if causal:
                kpos = si * SEL_SUPER + off + _iota((SEL_SUB, rows), 0)
                t = qb * QB + _mod_pow2(_iota((SEL_SUB, rows), 1), QB)
                st = jnp.where(kpos <= t, st, NEG)
            m_old = m_s[...]
            m_new = jnp.maximum(m_old, jnp.max(st, axis=0, keepdims=True))
            p = jnp.exp2(st - m_new).astype(BF16)
            acc_s[...] = (jnp.exp2(m_old - m_new) * acc_s[...]
                          + _dot(vt_ref[0, 0, :, pl.ds(off, SEL_SUB)], p))
            m_s[...] = m_new

        n_plain = jnp.where(si == last, nsub - 1, nsub)

        def body(i, carry):
            step(i, False)
            return carry

        lax.fori_loop(0, n_plain, body, 0)

        @pl.when(si == last)
        def _():
            step(nsub - 1, True)

    @pl.when(si == last)
    def _():
        o_s = (acc_s[...] / jnp.maximum(acc_s[DK:DK + 1, :], 1e-30)).T[:, 0:DK]
        gate = jax.nn.sigmoid(gl_ref[0].reshape(rows, 3))
        out = (gate[:, 0:1] * oc_ref[0].reshape(rows, DK) + gate[:, 1:2] * o_s
               + gate[:, 2:3] * ow_ref[0].reshape(rows, DK))
        o_ref[0] = out.reshape(NSA_REP, QB, DK)


def sel_attn(qn, selb, ks, vs, oc, ow, gl):
    b, h, s, _ = qn.shape
    g = NSA_GROUPS
    nqb = s // QB
    nsup = s // SEL_SUPER
    qmap = lambda bi, gi, qi, si: (bi, gi, qi, 0)

    def eff(qi, si):
        return jnp.minimum(si, (qi * QB) // SEL_SUPER)

    return pl.pallas_call(
        _sel_attn_kernel,
        grid=(b, g, nqb, nsup),
        in_specs=[pl.BlockSpec((1, NSA_REP, QB, DK), qmap),
                  pl.BlockSpec((1, 1, 1, 1, QB, SEL_SUPER_BLOCKS),
                               lambda bi, gi, qi, si: (bi, gi, qi, eff(qi, si), 0, 0)),
                  pl.BlockSpec((1, 1, SEL_SUPER, 2 * DK), lambda bi, gi, qi, si: (bi, gi, eff(qi, si), 0)),
                  pl.BlockSpec((1, 1, 2 * DK, SEL_SUPER), lambda bi, gi, qi, si: (bi, gi, 0, eff(qi, si))),
                  pl.BlockSpec((1, NSA_REP, QB, DK), qmap),
                  pl.BlockSpec((1, NSA_REP, QB, DK), qmap),
                  pl.BlockSpec((1, NSA_REP, QB, 3), qmap)],
        out_specs=pl.BlockSpec((1, NSA_REP, QB, DK), qmap),
        out_shape=jax.ShapeDtypeStruct((b, h, s, DK), F32),
        scratch_shapes=[pltpu.VMEM((NSA_REP * QB, 2 * DK), BF16),
                        pltpu.VMEM((1, NSA_REP * QB), F32),
                        pltpu.VMEM((2 * DK, NSA_REP * QB), F32)],
        compiler_params=_cp(("parallel", "parallel", "parallel", "arbitrary")),
        name="sel_attn",
    )(qn, selb, ks, vs, oc, ow, gl)


def _sgu_kernel(z_ref, g_ref, b_ref, w_ref, bias_ref, o_ref, *, tm):
    z = _gelu(z_ref[...])
    u = z[:, :SGU_WIDTH]
    v = z[:, SGU_WIDTH:]
    mu = jnp.mean(v, axis=-1, keepdims=True)
    var = jnp.mean(jnp.square(v - mu), axis=-1, keepdims=True)
    vn = ((v - mu) * lax.rsqrt(var + EPS) * g_ref[...] + b_ref[...]).astype(BF16)
    tril = _iota((SGU_CHUNK, SGU_CHUNK), 1) <= _iota((SGU_CHUNK, SGU_CHUNK), 0)
    wm = [jnp.where(tril, w_ref[gi], 0.0).astype(BF16) for gi in range(SGU_GROUPS)]
    grp = _div_pow2(_iota((SGU_CHUNK, SGU_WIDTH), 1), SGU_WIDTH // SGU_GROUPS)
    for ch in range(tm // SGU_CHUNK):
        rs = slice(ch * SGU_CHUNK, (ch + 1) * SGU_CHUNK)
        mixed = bias_ref[...]
        for gi in range(SGU_GROUPS):
            mixed = mixed + jnp.where(grp == gi, _dot(wm[gi], vn[rs]), 0.0)
        o_ref[rs, :] = u[rs] * mixed


def sgu(z, ln_g, ln_b, w, bias_full, tm=512):
    t = z.shape[0]
    return pl.pallas_call(
        functools.partial(_sgu_kernel, tm=tm),
        grid=(t // tm,),
        in_specs=[pl.BlockSpec((tm, 2 * SGU_WIDTH), lambda i: (i, 0)),
                  pl.BlockSpec((1, SGU_WIDTH), lambda i: (0, 0)),
                  pl.BlockSpec((1, SGU_WIDTH), lambda i: (0, 0)),
                  pl.BlockSpec((SGU_GROUPS, SGU_CHUNK, SGU_CHUNK), lambda i: (0, 0, 0)),
                  pl.BlockSpec((SGU_CHUNK, SGU_WIDTH), lambda i: (0, 0))],
        out_specs=pl.BlockSpec((tm, SGU_WIDTH), lambda i: (i, 0)),
        out_shape=jax.ShapeDtypeStruct((t, SGU_WIDTH), F32),
        compiler_params=_cp(("parallel",)),
        name="sgu",
    )(z, ln_g.reshape(1, -1), ln_b.reshape(1, -1), w, bias_full)


def _pool_kernel(p_ref, h_ref, w_ref, sc_ref, o_ref, *, tp):
    i = pl.program_id(1)
    halo_rows = POOL_WINDOWS[-1]
    p = p_ref[0]
    halo = jnp.where(i > 0, h_ref[0], 0.0)
    ext = jnp.concatenate([halo, p], axis=0)
    sums = [ext]
    shift = 1
    for _ in POOL_WINDOWS:
        prev = sums[-1]
        sums.append(prev + pltpu.roll(prev, shift, 0))
        shift *= 2
    pos = i * tp + _iota((tp, POOL_WIDTH), 0)
    grp = _div_pow2(_iota((tp, POOL_WIDTH), 1), POOL_WIDTH // len(POOL_WINDOWS))
    d = jnp.zeros((tp, POOL_WIDTH), F32)
    for gi, w in enumerate(POOL_WINDOWS):
        mean = sums[gi + 1][halo_rows:halo_rows + tp] / jnp.minimum(pos + 1, w).astype(F32)
        d = jnp.where(grp == gi, mean, d)
    d = d - p
    o_ref[0] = _dot(d.astype(BF16), w_ref[...]) * sc_ref[...]


def pool(p, w_bd, scale, tp=512):
    b, s, c = p.shape
    halo_rows = POOL_WINDOWS[-1]
    return pl.pallas_call(
        functools.partial(_pool_kernel, tp=tp),
        grid=(b, s // tp),
        in_specs=[pl.BlockSpec((1, tp, c), lambda bi, i: (bi, i, 0)),
                  pl.BlockSpec((1, halo_rows, c),
                               lambda bi, i: (bi, jnp.maximum(i * (tp // halo_rows) - 1, 0), 0)),
                  pl.BlockSpec((c, c), lambda bi, i: (0, 0)),
                  pl.BlockSpec((1, c), lambda bi, i: (0, 0))],
        out_specs=pl.BlockSpec((1, tp, c), lambda bi, i: (bi, i, 0)),
        out_shape=jax.ShapeDtypeStruct((b, s, c), F32),
        compiler_params=_cp(("parallel", "parallel")),
        name="pool",
    )(p, p, w_bd, scale.reshape(1, c))


def _merge_kernel(x_ref, oa_ref, ob_ref, oc_ref, zm_ref, la_ref, lb_ref, lc_ref, wo_ref, o_ref):
    d = D_MODEL
    oa = jnp.concatenate([oa_ref[0, h].astype(BF16) for h in range(NSA_HEADS)], axis=-1)
    zm = zm_ref[0]
    merged = (jax.nn.sigmoid(zm[:, 0:d]) * _dot(oa, la_ref[...])
              + jax.nn.sigmoid(zm[:, d:2 * d]) * _dot(ob_ref[0].astype(BF16), lb_ref[...])
              + jax.nn.sigmoid(zm[:, 2 * d:3 * d]) * _dot(oc_ref[0].astype(BF16), lc_ref[...]))
    o_ref[0] = x_ref[0] + _dot(merged.astype(BF16), wo_ref[...])


def merge(x, oa, ob, oc, zm, la, lb, lc, wo, tm=512):
    b, s, d = x.shape
    row = lambda bi, i: (bi, i, 0)
    full = lambda bi, i: (0, 0)
    return pl.pallas_call(
        _merge_kernel,
        grid=(b, s // tm),
        in_specs=[pl.BlockSpec((1, tm, d), row),
                  pl.BlockSpec((1, NSA_HEADS, tm, DK), lambda bi, i: (bi, 0, i, 0)),
                  pl.BlockSpec((1, tm, SGU_WIDTH), row),
                  pl.BlockSpec((1, tm, POOL_WIDTH), row),
                  pl.BlockSpec((1, tm, 3 * d), row),
                  pl.BlockSpec(la.shape, full), pl.BlockSpec(lb.shape, full),
                  pl.BlockSpec(lc.shape, full), pl.BlockSpec(wo.shape, full)],
        out_specs=pl.BlockSpec((1, tm, d), row),
        out_shape=jax.ShapeDtypeStruct((b, s, d), F32),
        compiler_params=_cp(("parallel", "parallel")),
        name="merge",
    )(x, oa, ob, oc, zm, la, lb, lc, wo)


def _mem_kv_kernel(m_ref, g_ref, wk_ref, wv_ref, kn_ref, k_out, v_out):
    mh = _rms(m_ref[0], g_ref[...]).astype(BF16)
    k = _dot(mh, wk_ref[...])
    for h in range(XA_HEADS):
        hs = slice(h * XA_HD, (h + 1) * XA_HD)
        k_out[0, :, hs] = _rms(k[:, hs], kn_ref[...]).astype(BF16)
    v_out[0] = _dot(mh, wv_ref[...]).astype(BF16)


def mem_kv(mem, g, wk, wv, k_norm):
    b, m, d = mem.shape
    full = lambda bi: (0, 0)
    return pl.pallas_call(
        _mem_kv_kernel,
        grid=(b,),
        in_specs=[pl.BlockSpec((1, m, d), lambda bi: (bi, 0, 0)),
                  pl.BlockSpec((1, d), full),
                  pl.BlockSpec(wk.shape, full), pl.BlockSpec(wv.shape, full),
                  pl.BlockSpec((1, XA_HD), full)],
        out_specs=[pl.BlockSpec((1, m, XA_WIDTH), lambda bi: (bi, 0, 0))] * 2,
        out_shape=[jax.ShapeDtypeStruct((b, m, XA_WIDTH), BF16)] * 2,
        compiler_params=_cp(("parallel",)),
        name="mem_kv",
    )(mem, g.reshape(1, d), wk, wv, k_norm.reshape(1, XA_HD))


def _xattn_kernel(x_ref, g_ref, wq_ref, qn_ref, k_ref, v_ref, wo_ref, o_ref):
    x = x_ref[0]
    q = _dot(_rms(x, g_ref[...]).astype(BF16), wq_ref[...])
    k = k_ref[0]
    v = v_ref[0]
    outs = []
    for h in range(XA_HEADS):
        hs = slice(h * XA_HD, (h + 1) * XA_HD)
        qh = (_rms(q[:, hs], qn_ref[...])).astype(BF16)
        s = _dot_nt(qh, k[:, hs]) * (XA_HD ** -0.5)
        e = jnp.exp(s - jnp.max(s, axis=-1, keepdims=True))
        p = e / jnp.sum(e, axis=-1, keepdims=True)
        outs.append(_dot(p.astype(BF16), v[:, hs]).astype(BF16))
    o = jnp.concatenate(outs, axis=-1)
    o_ref[0] = x + _dot(o, wo_ref[...])


def xattn(x, g, wq, q_norm, k, v, wo, tm=512):
    b, s, d = x.shape
    m = k.shape[1]
    row = lambda bi, i: (bi, i, 0)
    full = lambda bi, i: (0, 0)
    return pl.pallas_call(
        _xattn_kernel,
        grid=(b, s // tm),
        in_specs=[pl.BlockSpec((1, tm, d), row),
                  pl.BlockSpec((1, d), full),
                  pl.BlockSpec(wq.shape, full),
                  pl.BlockSpec((1, XA_HD), full),
                  pl.BlockSpec((1, m, XA_WIDTH), lambda bi, i: (bi, 0, 0)),
                  pl.BlockSpec((1, m, XA_WIDTH), lambda bi, i: (bi, 0, 0)),
                  pl.BlockSpec(wo.shape, full)],
        out_specs=pl.BlockSpec((1, tm, d), row),
        out_shape=jax.ShapeDtypeStruct((b, s, d), F32),
        compiler_params=_cp(("parallel", "parallel")),
        name="xattn",
    )(x, g.reshape(1, d), wq, q_norm.reshape(1, XA_HD), k, v, wo)


def _top_rows(s, n, want_rank):
    out = []
    work = s
    rank = jnp.full(s.shape, float(s.shape[0] - 1), F32) if want_rank else None
    for i in range(n):
        m = jnp.max(work, axis=0, keepdims=True)
        out.append(m)
        hit = work >= m
        if want_rank:
            rank = jnp.where(hit, float(i), rank)
        work = jnp.where(hit, REMOVED, work)
    return out, rank


def _stack_rows(rows, pad_rows):
    tt = rows[0].shape[1]
    rowi = _iota((pad_rows, tt), 0)
    out = jnp.full((pad_rows, tt), REMOVED, F32)
    for i, r in enumerate(rows[:pad_rows]):
        out = jnp.where(rowi == i, r, out)
    return out


def _peer_kernel(x_ref, g_ref, wq_ref, k1_ref, k2_ref, u_ref, vt_ref, o_ref,
                 hb_s, r2_s, na_s, e1_s, e2_s, acc_s, *, tt, ch):
    c = pl.program_id(1)
    ntop = PEER_TOPK + 1
    pad = 24

    @pl.when(c == 0)
    def _():
        hb = _rms(x_ref[...], g_ref[...]).astype(BF16)
        hb_s[...] = hb
        q = _dot(hb, wq_ref[...])
        acc_s[...] = jnp.zeros_like(acc_s)
        row_full = _iota((pad, tt), 0)
        row_8 = _iota((8, tt), 0)
        for hd in range(PEER_HEADS):
            qh = q[:, hd * 2 * PEER_HALF:(hd + 1) * 2 * PEER_HALF].astype(BF16)
            s1 = _dot_nt(k1_ref[...], qh)
            s2 = _dot_nt(k2_ref[...], qh)
            v1, _ = _top_rows(s1, ntop, False)
            v2, rank2 = _top_rows(s2, ntop, True)
            v2_full = _stack_rows(v2, pad)
            v2_8 = _stack_rows(v2, 8)
            pieces = [jnp.where(row_full < ntop, v1[0] + v2_full, REMOVED)]
            for i in range(1, ntop):
                pieces.append(jnp.where(row_8 < ntop // (i + 1), v1[i] + v2_8, REMOVED))
            cand = jnp.concatenate(pieces, axis=0)
            top = v1[0] + v2[0]
            z = jnp.zeros((1, tt), F32)
            c_prev = top
            c_last = top
            work = cand
            for r in range(ntop):
                m = jnp.max(work, axis=0, keepdims=True)
                work = jnp.where(work >= m, REMOVED, work)
                if r < PEER_TOPK:
                    z = z + jnp.exp(m - top)
                c_prev, c_last = c_last, m
            thr = 0.5 * (c_prev + c_last)
            na = jnp.zeros((PEER_KEYS, tt), F32)
            for j in range(ntop):
                na = na + jnp.where(s1 + v2[j] >= thr, 1.0, 0.0)
            r2_s[hd] = rank2.astype(BF16)
            na_s[hd] = na
            e1_s[hd] = jnp.exp(s1 - v1[0])
            e2_s[hd] = (jnp.exp(s2 - v2[0]) / z).astype(BF16)

    gl = _gelu(_dot_nt(u_ref[...], hb_s[...]).astype(BF16))
    ys = []
    for al in range(ch // PEER_KEYS):
        a = c * (ch // PEER_KEYS) + al
        wc = jnp.zeros((PEER_KEYS, tt), BF16)
        for hd in range(PEER_HEADS):
            na = na_s[hd, pl.ds(a, 1), :].astype(BF16)
            e1 = e1_s[hd, pl.ds(a, 1), :].astype(BF16)
            wc = wc + jnp.where(r2_s[hd] < na, e2_s[hd] * e1, jnp.zeros((), BF16))
        ys.append(wc * gl[al * PEER_KEYS:(al + 1) * PEER_KEYS])
    y = jnp.concatenate(ys, axis=0)
    acc_s[...] += _dot(vt_ref[...], y)

    @pl.when(c == pl.num_programs(1) - 1)
    def _():
        o_ref[...] = x_ref[...] + acc_s[...].T


def peer(x, g, wq, k1p, k2p, u, vt, tt=PEER_TT, ch=PEER_CH):
    t, d = x.shape
    ne = u.shape[0]
    return pl.pallas_call(
        functools.partial(_peer_kernel, tt=tt, ch=ch),
        grid=(t // tt, ne // ch),
        in_specs=[pl.BlockSpec((tt, d), lambda i, c: (i, 0)),
                  pl.BlockSpec((1, d), lambda i, c: (0, 0)),
                  pl.BlockSpec(wq.shape, lambda i, c: (0, 0)),
                  pl.BlockSpec(k1p.shape, lambda i, c: (0, 0)),
                  pl.BlockSpec(k2p.shape, lambda i, c: (0, 0)),
                  pl.BlockSpec((ch, d), lambda i, c: (c, 0)),
                  pl.BlockSpec((d, ch), lambda i, c: (0, c))],
        out_specs=pl.BlockSpec((tt, d), lambda i, c: (i, 0)),
        out_shape=jax.ShapeDtypeStruct((t, d), F32),
        scratch_shapes=[pltpu.VMEM((tt, d), BF16),
                        pltpu.VMEM((PEER_HEADS, PEER_KEYS, tt), BF16),
                        pltpu.VMEM((PEER_HEADS, PEER_KEYS, tt), F32),
                        pltpu.VMEM((PEER_HEADS, PEER_KEYS, tt), F32),
                        pltpu.VMEM((PEER_HEADS, PEER_KEYS, tt), BF16),
                        pltpu.VMEM((d, tt), F32)],
        compiler_params=_cp(("parallel", "arbitrary")),
        name="peer",
    )(x, g.reshape(1, d), wq, k1p, k2p, u, vt)


_IN_WIDTHS = (NSA_HEADS * DK, 6 * NSA_GROUPS * DK, 3 * NSA_HEADS, 2 * SGU_WIDTH, POOL_WIDTH, 3 * D_MODEL)


def _in_proj_layout():
    splits, src = [], []
    o_src = o_dst = 0
    for w in _IN_WIDTHS:
        wp = -(-w // LANES) * LANES
        splits.append((o_dst, o_dst + wp))
        src.append((o_src, o_src + w))
        o_src += w
        o_dst += wp
    return splits, src, o_dst


def _cmp_to_slc(s):
    n_cmp_pad = s // D_CMP
    n_slc = s // L_SLC
    c0 = np.arange(n_cmp_pad)[:, None] * D_CMP
    s0 = np.arange(n_slc)[None, :] * L_SLC
    overlap = np.minimum(c0 + L_CMP, s0 + L_SLC) - np.maximum(c0, s0)
    m = np.maximum(overlap, 0) / D_CMP
    m[n_cmp_pad - 1] = 0.0
    return jnp.asarray(m, dtype=BF16)


def nsa_layer(zq, zkv, zg, q_norm, k_norm, cmp_pe, cmp_w1, cmp_w2, b, s):
    qn, kvc, ks, vs, kw, vw = nsa_prep(zq.reshape(b, s, -1), zkv.reshape(b, s, -1), q_norm, k_norm)
    nc = s // D_CMP
    xc = kvc.reshape(2, b, NSA_GROUPS, nc, D_CMP * DK)
    pe = cmp_pe.reshape(2, 2, D_CMP * DK)
    w1 = cmp_w1.reshape(2, 2, D_CMP * DK, CMP_HIDDEN).astype(BF16)
    kc = cmp_mlp(xc, pe, w1, cmp_w2.astype(BF16), k_norm[0])
    oc, selb = cmp_attn_select(qn, kc, _cmp_to_slc(s))
    ow = win_attn(qn, kw, vw)
    gl = zg[:, :3 * NSA_HEADS].reshape(b, s, NSA_HEADS, 3).transpose(0, 2, 1, 3)
    return sel_attn(qn, selb, ks, vs, oc, ow, gl)


def kernel(x, mem, mix_norm, w_in, nsa_q_norm, nsa_k_norm, cmp_pe, cmp_w1, cmp_w2, sgu_ln_g, sgu_ln_b, sgu_w, sgu_b, pool_w, pool_scale, lift_a, lift_b, lift_c, w_out, xa_norm, mem_norm, xa_wq, xa_wk, xa_wv, xa_q_norm, xa_k_norm, xa_wo, ffn_norm, peer_wq, peer_keys1, peer_keys2, peer_u, peer_v):
    b, s, d = x.shape
    t = b * s
    depth = w_in.shape[0]
    splits, src, n_pad = _in_proj_layout()
    zeros_half = jnp.zeros((PEER_KEYS, PEER_HALF), BF16)
    for l in range(depth):
        w_parts = []
        for (a0, a1), (d0, d1) in zip(src, splits):
            w_parts.append(jnp.pad(w_in[l][:, a0:a1], ((0, 0), (0, (d1 - d0) - (a1 - a0)))))
        w_pad = jnp.concatenate(w_parts, axis=1).astype(BF16)
        zq, zkv, zg, zs, zp, zm = norm_matmul(x.reshape(t, d), mix_norm[l], w_pad, splits, 256, "in_proj")

        oa = nsa_layer(zq, zkv, zg, nsa_q_norm[l], nsa_k_norm[l], cmp_pe[l], cmp_w1[l], cmp_w2[l], b, s)
        bias_full = jnp.repeat(sgu_b[l].T, SGU_WIDTH // SGU_GROUPS, axis=1)
        ob = sgu(zs, sgu_ln_g[l], sgu_ln_b[l], sgu_w[l], bias_full)
        cg = POOL_WIDTH // len(POOL_WINDOWS)
        w_bd = jnp.zeros((POOL_WIDTH, POOL_WIDTH), F32)
        for gi in range(len(POOL_WINDOWS)):
            w_bd = w_bd.at[gi * cg:(gi + 1) * cg, gi * cg:(gi + 1) * cg].set(pool_w[l, gi])
        oc = pool(zp.reshape(b, s, POOL_WIDTH), w_bd.astype(BF16), pool_scale[l])
        x = merge(x, oa, ob.reshape(b, s, SGU_WIDTH), oc, zm.reshape(b, s, 3 * d),
                  lift_a[l].astype(BF16), lift_b[l].astype(BF16), lift_c[l].astype(BF16),
                  w_out[l].astype(BF16))

        mk, mv = mem_kv(mem, mem_norm[l], xa_wk[l].astype(BF16), xa_wv[l].astype(BF16), xa_k_norm[l])
        x = xattn(x, xa_norm[l], xa_wq[l].astype(BF16), xa_q_norm[l], mk, mv, xa_wo[l].astype(BF16))

        k1p = jnp.concatenate([peer_keys1[l].astype(BF16), zeros_half], axis=1)
        k2p = jnp.concatenate([zeros_half, peer_keys2[l].astype(BF16)], axis=1)
        x = peer(x.reshape(t, d), ffn_norm[l], peer_wq[l].astype(BF16), k1p, k2p,
                 peer_u[l].astype(BF16), peer_v[l].T.astype(BF16)).reshape(b, s, d)
    return x
```

```python
import functools

import numpy as np
import jax
import jax.numpy as jnp
from jax import lax
from jax.experimental import pallas as pl
from jax.experimental.pallas import tpu as pltpu

F32 = jnp.float32
BF16 = jnp.bfloat16

EPS = 1e-6
LOG2E = 1.4426950408889634
NEG = -1e30
REMOVED = -3e38

D_MODEL = 1024
DK = 64
NSA_HEADS = 8
NSA_GROUPS = 2
NSA_REP = NSA_HEADS // NSA_GROUPS
L_CMP, D_CMP = 32, 16
CMP_HIDDEN = 128
L_SLC = 64
N_SEL = 16
WINDOW = 512
QB = 128
FORCE_BONUS = 1.0e3
SGU_WIDTH = 256
SGU_GROUPS = 4
SGU_CHUNK = 128
POOL_WIDTH = 256
POOL_WINDOWS = (2, 4, 8, 16)
XA_HEADS, XA_HD = 4, 128
XA_WIDTH = XA_HEADS * XA_HD
PEER_HEADS = 8
PEER_KEYS = 128
PEER_TOPK = 16
PEER_HALF = 64

LANES = 128
SEL_SUPER_BLOCKS = 64
SEL_SUPER = SEL_SUPER_BLOCKS * L_SLC
SEL_SUB = 512
PEER_TT = 512
PEER_CH = 1024
VMEM_LIMIT = 56 * 1024 * 1024


def _cp(sem, flags=None):
    return pltpu.CompilerParams(dimension_semantics=sem, vmem_limit_bytes=VMEM_LIMIT, flags=flags)


_PEER_FLAGS = None


def _gelu(x):
    return 0.5 * x * (1.0 + jnp.tanh(0.7978845608028654 * (x + 0.044715 * (x * x * x))))


def _rms(x, g):
    return x * lax.rsqrt(jnp.mean(x * x, axis=-1, keepdims=True) + EPS) * g


def _dot(a, b):
    return jnp.dot(a, b, preferred_element_type=F32)


def _dot_nt(a, b):
    return lax.dot_general(a, b, (((1,), (1,)), ((), ())), preferred_element_type=F32)


def _iota(shape, dim):
    return lax.broadcasted_iota(jnp.int32, shape, dim)


def _div_pow2(x, n):
    assert n & (n - 1) == 0
    return lax.shift_right_logical(x, jnp.int32(n.bit_length() - 1))


def _mod_pow2(x, n):
    assert n & (n - 1) == 0
    return x & (n - 1)


def _norm_matmul_kernel(x_ref, g_ref, w_ref, *o_refs, splits):
    hb = _rms(x_ref[...], g_ref[...]).astype(BF16)
    for o_ref, (a, b) in zip(o_refs, splits):
        o_ref[...] = _dot(hb, w_ref[:, a:b]).astype(o_ref.dtype)


def norm_matmul(x, g, w, splits, tm, name):
    t, d = x.shape
    n = w.shape[1]
    return pl.pallas_call(
        functools.partial(_norm_matmul_kernel, splits=splits),
        grid=(t // tm,),
        in_specs=[pl.BlockSpec((tm, d), lambda i: (i, 0)),
                  pl.BlockSpec((1, d), lambda i: (0, 0)),
                  pl.BlockSpec((d, n), lambda i: (0, 0))],
        out_specs=[pl.BlockSpec((tm, b - a), lambda i: (i, 0)) for a, b in splits],
        out_shape=[jax.ShapeDtypeStruct((t, b - a), F32) for a, b in splits],
        compiler_params=_cp(("parallel",)),
        name=name,
    )(x, g.reshape(1, d), w)


def _nsa_prep_kernel(zq_ref, zkv_ref, qn_ref, kn_ref,
                     q_out, kvc_out, ks_out, vs_out, kw_out, vw_out, *, ts):
    st = pl.program_id(1)
    zq = zq_ref[0]
    scale = DK ** -0.5 * LOG2E
    for h in range(NSA_HEADS):
        q_out[0, h] = (_rms(zq[:, h * DK:(h + 1) * DK], qn_ref[...]) * scale).astype(BF16)
    zkv = zkv_ref[0]

    def piece(i, g):
        o = i * NSA_GROUPS * DK + g * DK
        return zkv[:, o:o + DK]

    pos = st * ts + _iota((ts, SEL_SUPER_BLOCKS), 0)
    blk = _mod_pow2(_div_pow2(pos, L_SLC), SEL_SUPER_BLOCKS)
    onehot = jnp.where(_iota((ts, SEL_SUPER_BLOCKS), 1) == blk, 1.0, 0.0).astype(BF16)
    ones_col = jnp.where(_iota((ts, DK), 1) == 0, 1.0, 0.0)
    for g in range(NSA_GROUPS):
        kvc_out[0, 0, g] = piece(0, g)
        kvc_out[1, 0, g] = piece(1, g)
        ks = _rms(piece(2, g), kn_ref[1:2, :]).astype(BF16)
        ks_out[0, g] = jnp.concatenate([ks, onehot], axis=-1)
        vs_out[0, g] = jnp.concatenate([piece(3, g), ones_col], axis=-1).T.astype(BF16)
        kw_out[0, g] = _rms(piece(4, g), kn_ref[2:3, :]).astype(BF16)
        vw_out[0, g] = piece(5, g).astype(BF16)


def nsa_prep(zq, zkv, q_norm, k_norm, ts=512):
    b, s, _ = zq.shape
    g = NSA_GROUPS
    hm = lambda bi, si: (bi, 0, si, 0)
    return pl.pallas_call(
        functools.partial(_nsa_prep_kernel, ts=ts),
        grid=(b, s // ts),
        in_specs=[pl.BlockSpec((1, ts, NSA_HEADS * DK), lambda bi, si: (bi, si, 0)),
                  pl.BlockSpec((1, ts, 6 * g * DK), lambda bi, si: (bi, si, 0)),
                  pl.BlockSpec((1, DK), lambda bi, si: (0, 0)),
                  pl.BlockSpec((3, DK), lambda bi, si: (0, 0))],
        out_specs=[pl.BlockSpec((1, NSA_HEADS, ts, DK), hm),
                   pl.BlockSpec((2, 1, g, ts, DK), lambda bi, si: (0, bi, 0, si, 0)),
                   pl.BlockSpec((1, g, ts, 2 * DK), hm),
                   pl.BlockSpec((1, g, 2 * DK, ts), lambda bi, si: (bi, 0, 0, si)),
                   pl.BlockSpec((1, g, ts, DK), hm),
                   pl.BlockSpec((1, g, ts, DK), hm)],
        out_shape=[jax.ShapeDtypeStruct((b, NSA_HEADS, s, DK), BF16),
                   jax.ShapeDtypeStruct((2, b, g, s, DK), F32),
                   jax.ShapeDtypeStruct((b, g, s, 2 * DK), BF16),
                   jax.ShapeDtypeStruct((b, g, 2 * DK, s), BF16),
                   jax.ShapeDtypeStruct((b, g, s, DK), BF16),
                   jax.ShapeDtypeStruct((b, g, s, DK), BF16)],
        compiler_params=_cp(("parallel", "parallel")),
        name="nsa_prep",
    )(zq, zkv, q_norm.reshape(1, DK), k_norm)


def _cmp_mlp_kernel(x_ref, pe_ref, w1_ref, w2_ref, kn_ref, o_ref, *, nc):
    c = pl.program_id(0)
    x = x_ref[0, 0, 0]
    a = _dot((x + pe_ref[0, 0:1, :]).astype(BF16), w1_ref[0, 0])
    b = _dot((x + pe_ref[0, 1:2, :]).astype(BF16), w1_ref[0, 1])
    pre = a + pltpu.roll(b, nc - 1, 0)
    comp = _dot(_gelu(pre).astype(BF16), w2_ref[0])
    out = jnp.where(c == 0, _rms(comp, kn_ref[...]), comp)
    o_ref[0, 0, 0] = out.astype(BF16)


def cmp_mlp(xc, pe, w1, w2, k_norm0):
    _, b, g, nc, width = xc.shape
    return pl.pallas_call(
        functools.partial(_cmp_mlp_kernel, nc=nc),
        grid=(2, b, g),
        in_specs=[pl.BlockSpec((1, 1, 1, nc, width), lambda c, bi, gi: (c, bi, gi, 0, 0)),
                  pl.BlockSpec((1, 2, width), lambda c, bi, gi: (c, 0, 0)),
                  pl.BlockSpec((1, 2, width, CMP_HIDDEN), lambda c, bi, gi: (c, 0, 0, 0)),
                  pl.BlockSpec((1, CMP_HIDDEN, DK), lambda c, bi, gi: (c, 0, 0)),
                  pl.BlockSpec((1, DK), lambda c, bi, gi: (0, 0))],
        out_specs=pl.BlockSpec((1, 1, 1, nc, DK), lambda c, bi, gi: (c, bi, gi, 0, 0)),
        out_shape=jax.ShapeDtypeStruct((2, b, g, nc, DK), BF16),
        compiler_params=_cp(("parallel", "parallel", "parallel")),
        name="cmp_mlp",
    )(xc, pe, w1, w2, k_norm0.reshape(1, DK))


def _cmp_attn_kernel(q_ref, k_ref, v_ref, c_ref, oc_ref, sb_ref, *, nc, n_slc):
    qb = pl.program_id(2)
    rows = NSA_REP * QB
    q = q_ref[0].reshape(rows, DK)
    s = _dot_nt(q, k_ref[0, 0, 0])
    t = qb * QB + _mod_pow2(_iota((rows, nc), 0), QB)
    valid = _iota((rows, nc), 1) * D_CMP + (L_CMP - 1) <= t
    s = jnp.where(valid, s, NEG)
    e = jnp.where(valid, jnp.exp2(s - jnp.max(s, axis=-1, keepdims=True)), 0.0)
    p = e / jnp.maximum(jnp.sum(e, axis=-1, keepdims=True), 1e-30)
    oc_ref[0] = _dot(p.astype(BF16), v_ref[0, 0, 0]).reshape(NSA_REP, QB, DK)

    ps = p[0:QB]
    for r in range(1, NSA_REP):
        ps = ps + p[r * QB:(r + 1) * QB]
    hi = ps.astype(BF16)
    lo = (ps - hi.astype(F32)).astype(BF16)
    imp = _dot(hi, c_ref[...]) + _dot(lo, c_ref[...])

    cur = _div_pow2(qb * QB + _iota((QB, n_slc), 0), L_SLC)
    j = _iota((QB, n_slc), 1)
    forced = (j == 0) | (j == cur) | (j == cur - 1)
    score = jnp.where(j <= cur, imp + jnp.where(forced, FORCE_BONUS, 0.0), NEG)
    work = score
    tau = None
    for _ in range(N_SEL):
        tau = jnp.max(work, axis=-1, keepdims=True)
        work = jnp.where(work >= tau, REMOVED, work)
    selected = (score > 0.5 * NEG) & (score >= tau)
    bias = jnp.where(selected, 0.0, NEG).astype(BF16)
    for si in range(n_slc // SEL_SUPER_BLOCKS):
        sb_ref[0, 0, 0, si] = bias[:, si * SEL_SUPER_BLOCKS:(si + 1) * SEL_SUPER_BLOCKS]


def cmp_attn_select(qn, kc, cmat):
    b, h, s, _ = qn.shape
    g = NSA_GROUPS
    nc = kc.shape[3]
    n_slc = s // L_SLC
    nsup = n_slc // SEL_SUPER_BLOCKS
    nqb = s // QB
    return pl.pallas_call(
        functools.partial(_cmp_attn_kernel, nc=nc, n_slc=n_slc),
        grid=(b, g, nqb),
        in_specs=[pl.BlockSpec((1, NSA_REP, QB, DK), lambda bi, gi, qi: (bi, gi, qi, 0)),
                  pl.BlockSpec((1, 1, 1, nc, DK), lambda bi, gi, qi: (0, bi, gi, 0, 0)),
                  pl.BlockSpec((1, 1, 1, nc, DK), lambda bi, gi, qi: (1, bi, gi, 0, 0)),
                  pl.BlockSpec((nc, n_slc), lambda bi, gi, qi: (0, 0))],
        out_specs=[pl.BlockSpec((1, NSA_REP, QB, DK), lambda bi, gi, qi: (bi, gi, qi, 0)),
                   pl.BlockSpec((1, 1, 1, nsup, QB, SEL_SUPER_BLOCKS),
                                lambda bi, gi, qi: (bi, gi, qi, 0, 0, 0))],
        out_shape=[jax.ShapeDtypeStruct((b, h, s, DK), F32),
                   jax.ShapeDtypeStruct((b, g, nqb, nsup, QB, SEL_SUPER_BLOCKS), BF16)],
        compiler_params=_cp(("parallel", "parallel", "parallel")),
        name="cmp_attn_select",
    )(qn, kc, kc, cmat)


def _win_attn_kernel(q_ref, *refs):
    nkb = WINDOW // QB + 1
    k_refs, v_refs, o_ref = refs[:nkb], refs[nkb:2 * nkb], refs[2 * nkb]
    qb = pl.program_id(2)
    rows = NSA_REP * QB
    nk = nkb * QB
    q = q_ref[0].reshape(rows, DK)
    k = jnp.concatenate([r[0, 0] for r in k_refs], axis=0)
    v = jnp.concatenate([r[0, 0] for r in v_refs], axis=0)
    s = _dot_nt(q, k)
    t = qb * QB + _mod_pow2(_iota((rows, nk), 0), QB)
    kpos = (qb - (nkb - 1)) * QB + _iota((rows, nk), 1)
    delta = t - kpos
    valid = (delta >= 0) & (delta < WINDOW) & (kpos >= 0)
    s = jnp.where(valid, s, NEG)
    e = jnp.where(valid, jnp.exp2(s - jnp.max(s, axis=-1, keepdims=True)), 0.0)
    p = e / jnp.maximum(jnp.sum(e, axis=-1, keepdims=True), 1e-30)
    o_ref[0] = _dot(p.astype(BF16), v).reshape(NSA_REP, QB, DK)


def win_attn(qn, kw, vw):
    b, h, s, _ = qn.shape
    g = NSA_GROUPS
    nkb = WINDOW // QB + 1
    nqb = s // QB

    def kv_spec(i):
        return pl.BlockSpec((1, 1, QB, DK),
                            lambda bi, gi, qi: (bi, gi, jnp.maximum(qi - (nkb - 1) + i, 0), 0))

    return pl.pallas_call(
        _win_attn_kernel,
        grid=(b, g, nqb),
        in_specs=[pl.BlockSpec((1, NSA_REP, QB, DK), lambda bi, gi, qi: (bi, gi, qi, 0))]
        + [kv_spec(i) for i in range(nkb)] + [kv_spec(i) for i in range(nkb)],
        out_specs=pl.BlockSpec((1, NSA_REP, QB, DK), lambda bi, gi, qi: (bi, gi, qi, 0)),
        out_shape=jax.ShapeDtypeStruct((b, h, s, DK), F32),
        compiler_params=_cp(("parallel", "parallel", "parallel")),
        name="win_attn",
    )(qn, *([kw] * nkb), *([vw] * nkb))


def _sel_attn_kernel(q_ref, sb_ref, k_ref, vt_ref, oc_ref, ow_ref, gl_ref, o_ref,
                     qp_s, m_s, acc_s, s0_s, s1_s, p0_s, p1_s, a0_s, a1_s):
    qb = pl.program_id(2)
    si = pl.program_id(3)
    rows = NSA_REP * QB
    last = (qb * QB) // SEL_SUPER

    @pl.when(si == 0)
    def _():
        qp_s[:, 0:DK] = q_ref[0].reshape(rows, DK)
        m_s[...] = jnp.full_like(m_s, NEG)
        acc_s[...] = jnp.zeros_like(acc_s)

    @pl.when(si <= last)
    def _():
        sb = sb_ref[0, 0, 0, 0]
        for r in range(NSA_REP):
            qp_s[r * QB:(r + 1) * QB, DK:2 * DK] = sb
        qv = qp_s[...]
        nsub = jnp.minimum(SEL_SUPER // SEL_SUB,
                           (qb * QB + QB - si * SEL_SUPER + SEL_SUB - 1) // SEL_SUB)

        def scores(i, s_ref):
            off = pl.multiple_of(i * SEL_SUB, SEL_SUB)
            s_ref[...] = _dot_nt(k_ref[0, 0, pl.ds(off, SEL_SUB), :], qv)

        def weights(i, s_ref, p_ref, a_ref, causal):
            st = s_ref[...]
            if causal:
                kpos = si * SEL_SUPER + i * SEL_SUB + _iota((SEL_SUB, rows), 0)
                t = qb * QB + _mod_pow2(_iota((SEL_SUB, rows), 1), QB)
                st = jnp.where(kpos <= t, st, NEG)
            m_old = m_s[...]
            m_new = jnp.maximum(m_old, jnp.max(st, axis=0, keepdims=True))
            p_ref[...] = jnp.exp2(st - m_new).astype(BF16)
            a_ref[...] = jnp.exp2(m_old - m_new)
            m_s[...] = m_new

        def accumulate(i, p_ref, a_ref):
            off = pl.multiple_of(i * SEL_SUB, SEL_SUB)
            acc_s[...] = a_ref[...] * acc_s[...] + _dot(vt_ref[0, 0, :, pl.ds(off, SEL_SUB)], p_ref[...])

        p1_s[...] = jnp.zeros_like(p1_s)
        a1_s[...] = jnp.ones_like(a1_s)
        scores(0, s0_s)
        n_pairs = (nsub - 1) // 2
        odd_tail = (nsub - 1) - 2 * n_pairs

        def pair(j, carry):
            scores(2 * j + 1, s1_s)
            weights(2 * j, s0_s, p0_s, a0_s, False)
            accumulate(jnp.maximum(2 * j - 1, 0), p1_s, a1_s)
            scores(2 * j + 2, s0_s)
            weights(2 * j + 1, s1_s, p1_s, a1_s, False)
            accumulate(2 * j, p0_s, a0_s)
            return carry

        lax.fori_loop(0, n_pairs, pair, 0)
        e = 2 * n_pairs

        @pl.when(odd_tail == 0)
        def _():
            weights(e, s0_s, p0_s, a0_s, True)
            accumulate(jnp.maximum(e - 1, 0), p1_s, a1_s)
            accumulate(e, p0_s, a0_s)

        @pl.when(odd_tail == 1)
        def _():
            scores(e + 1, s1_s)
            weights(e, s0_s, p0_s, a0_s, False)
            accumulate(jnp.maximum(e - 1, 0), p1_s, a1_s)
            weights(e + 1, s1_s, p1_s, a1_s, True)
            accumulate(e, p0_s, a0_s)
            accumulate(e + 1, p1_s, a1_s)

    @pl.when(si == last)
    def _():
        o_s = (acc_s[...] / jnp.maximum(acc_s[DK:DK + 1, :], 1e-30)).T[:, 0:DK]
        gate = jax.nn.sigmoid(gl_ref[0].reshape(rows, 3))
        out = (gate[:, 0:1] * oc_ref[0].reshape(rows, DK) + gate[:, 1:2] * o_s
               + gate[:, 2:3] * ow_ref[0].reshape(rows, DK))
        o_ref[0] = out.reshape(NSA_REP, QB, DK)


def sel_attn(qn, selb, ks, vs, oc, ow, gl):
    b, h, s, _ = qn.shape
    g = NSA_GROUPS
    nqb = s // QB
    nsup = s // SEL_SUPER
    qmap = lambda bi, gi, qi, si: (bi, gi, qi, 0)

    def eff(qi, si):
        return jnp.minimum(si, (qi * QB) // SEL_SUPER)

    return pl.pallas_call(
        _sel_attn_kernel,
        grid=(b, g, nqb, nsup),
        in_specs=[pl.BlockSpec((1, NSA_REP, QB, DK), qmap),
                  pl.BlockSpec((1, 1, 1, 1, QB, SEL_SUPER_BLOCKS),
                               lambda bi, gi, qi, si: (bi, gi, qi, eff(qi, si), 0, 0)),
                  pl.BlockSpec((1, 1, SEL_SUPER, 2 * DK), lambda bi, gi, qi, si: (bi, gi, eff(qi, si), 0)),
                  pl.BlockSpec((1, 1, 2 * DK, SEL_SUPER), lambda bi, gi, qi, si: (bi, gi, 0, eff(qi, si))),
                  pl.BlockSpec((1, NSA_REP, QB, DK), qmap),
                  pl.BlockSpec((1, NSA_REP, QB, DK), qmap),
                  pl.BlockSpec((1, NSA_REP, QB, 3), qmap)],
        out_specs=pl.BlockSpec((1, NSA_REP, QB, DK), qmap),
        out_shape=jax.ShapeDtypeStruct((b, h, s, DK), F32),
        scratch_shapes=[pltpu.VMEM((NSA_REP * QB, 2 * DK), BF16),
                        pltpu.VMEM((1, NSA_REP * QB), F32),
                        pltpu.VMEM((2 * DK, NSA_REP * QB), F32),
                        pltpu.VMEM((SEL_SUB, NSA_REP * QB), F32),
                        pltpu.VMEM((SEL_SUB, NSA_REP * QB), F32),
                        pltpu.VMEM((SEL_SUB, NSA_REP * QB), BF16),
                        pltpu.VMEM((SEL_SUB, NSA_REP * QB), BF16),
                        pltpu.VMEM((1, NSA_REP * QB), F32),
                        pltpu.VMEM((1, NSA_REP * QB), F32)],
        compiler_params=_cp(("parallel", "parallel", "parallel", "arbitrary")),
        name="sel_attn",
    )(qn, selb, ks, vs, oc, ow, gl)


def _sgu_kernel(z_ref, g_ref, b_ref, w_ref, bias_ref, o_ref, *, tm):
    z = _gelu(z_ref[...])
    u = z[:, :SGU_WIDTH]
    v = z[:, SGU_WIDTH:]
    mu = jnp.mean(v, axis=-1, keepdims=True)
    var = jnp.mean(jnp.square(v - mu), axis=-1, keepdims=True)
    vn = ((v - mu) * lax.rsqrt(var + EPS) * g_ref[...] + b_ref[...]).astype(BF16)
    tril = _iota((SGU_CHUNK, SGU_CHUNK), 1) <= _iota((SGU_CHUNK, SGU_CHUNK), 0)
    wm = [jnp.where(tril, w_ref[gi], 0.0).astype(BF16) for gi in range(SGU_GROUPS)]
    grp = _div_pow2(_iota((SGU_CHUNK, SGU_WIDTH), 1), SGU_WIDTH // SGU_GROUPS)
    for ch in range(tm // SGU_CHUNK):
        rs = slice(ch * SGU_CHUNK, (ch + 1) * SGU_CHUNK)
        mixed = bias_ref[...]
        for gi in range(SGU_GROUPS):
            mixed = mixed + jnp.where(grp == gi, _dot(wm[gi], vn[rs]), 0.0)
        o_ref[rs, :] = u[rs] * mixed


def sgu(z, ln_g, ln_b, w, bias_full, tm=512):
    t = z.shape[0]
    return pl.pallas_call(
        functools.partial(_sgu_kernel, tm=tm),
        grid=(t // tm,),
        in_specs=[pl.BlockSpec((tm, 2 * SGU_WIDTH), lambda i: (i, 0)),
                  pl.BlockSpec((1, SGU_WIDTH), lambda i: (0, 0)),
                  pl.BlockSpec((1, SGU_WIDTH), lambda i: (0, 0)),
                  pl.BlockSpec((SGU_GROUPS, SGU_CHUNK, SGU_CHUNK), lambda i: (0, 0, 0)),
                  pl.BlockSpec((SGU_CHUNK, SGU_WIDTH), lambda i: (0, 0))],
        out_specs=pl.BlockSpec((tm, SGU_WIDTH), lambda i: (i, 0)),
        out_shape=jax.ShapeDtypeStruct((t, SGU_WIDTH), F32),
        compiler_params=_cp(("parallel",)),
        name="sgu",
    )(z, ln_g.reshape(1, -1), ln_b.reshape(1, -1), w, bias_full)


def _pool_kernel(p_ref, h_ref, w_ref, sc_ref, o_ref, *, tp):
    i = pl.program_id(1)
    halo_rows = POOL_WINDOWS[-1]
    p = p_ref[0]
    halo = jnp.where(i > 0, h_ref[0], 0.0)
    ext = jnp.concatenate([halo, p], axis=0)
    sums = [ext]
    shift = 1
    for _ in POOL_WINDOWS:
        prev = sums[-1]
        sums.append(prev + pltpu.roll(prev, shift, 0))
        shift *= 2
    pos = i * tp + _iota((tp, POOL_WIDTH), 0)
    grp = _div_pow2(_iota((tp, POOL_WIDTH), 1), POOL_WIDTH // len(POOL_WINDOWS))
    d = jnp.zeros((tp, POOL_WIDTH), F32)
    for gi, w in enumerate(POOL_WINDOWS):
        mean = sums[gi + 1][halo_rows:halo_rows + tp] / jnp.minimum(pos + 1, w).astype(F32)
        d = jnp.where(grp == gi, mean, d)
    d = d - p
    o_ref[0] = _dot(d.astype(BF16), w_ref[...]) * sc_ref[...]


def pool(p, w_bd, scale, tp=512):
    b, s, c = p.shape
    halo_rows = POOL_WINDOWS[-1]
    return pl.pallas_call(
        functools.partial(_pool_kernel, tp=tp),
        grid=(b, s // tp),
        in_specs=[pl.BlockSpec((1, tp, c), lambda bi, i: (bi, i, 0)),
                  pl.BlockSpec((1, halo_rows, c),
                               lambda bi, i: (bi, jnp.maximum(i * (tp // halo_rows) - 1, 0), 0)),
                  pl.BlockSpec((c, c), lambda bi, i: (0, 0)),
                  pl.BlockSpec((1, c), lambda bi, i: (0, 0))],
        out_specs=pl.BlockSpec((1, tp, c), lambda bi, i: (bi, i, 0)),
        out_shape=jax.ShapeDtypeStruct((b, s, c), F32),
        compiler_params=_cp(("parallel", "parallel")),
        name="pool",
    )(p, p, w_bd, scale.reshape(1, c))


def _merge_kernel(x_ref, oa_ref, ob_ref, oc_ref, zm_ref, la_ref, lb_ref, lc_ref, wo_ref, o_ref):
    d = D_MODEL
    oa = jnp.concatenate([oa_ref[0, h].astype(BF16) for h in range(NSA_HEADS)], axis=-1)
    zm = zm_ref[0]
    merged = (jax.nn.sigmoid(zm[:, 0:d]) * _dot(oa, la_ref[...])
              + jax.nn.sigmoid(zm[:, d:2 * d]) * _dot(ob_ref[0].astype(BF16), lb_ref[...])
              + jax.nn.sigmoid(zm[:, 2 * d:3 * d]) * _dot(oc_ref[0].astype(BF16), lc_ref[...]))
    o_ref[0] = x_ref[0] + _dot(merged.astype(BF16), wo_ref[...])


def merge(x, oa, ob, oc, zm, la, lb, lc, wo, tm=512):
    b, s, d = x.shape
    row = lambda bi, i: (bi, i, 0)
    full = lambda bi, i: (0, 0)
    return pl.pallas_call(
        _merge_kernel,
        grid=(b, s // tm),
        in_specs=[pl.BlockSpec((1, tm, d), row),
                  pl.BlockSpec((1, NSA_HEADS, tm, DK), lambda bi, i: (bi, 0, i, 0)),
                  pl.BlockSpec((1, tm, SGU_WIDTH), row),
                  pl.BlockSpec((1, tm, POOL_WIDTH), row),
                  pl.BlockSpec((1, tm, 3 * d), row),
                  pl.BlockSpec(la.shape, full), pl.BlockSpec(lb.shape, full),
                  pl.BlockSpec(lc.shape, full), pl.BlockSpec(wo.shape, full)],
        out_specs=pl.BlockSpec((1, tm, d), row),
        out_shape=jax.ShapeDtypeStruct((b, s, d), F32),
        compiler_params=_cp(("parallel", "parallel")),
        name="merge",
    )(x, oa, ob, oc, zm, la, lb, lc, wo)


def _mem_kv_kernel(m_ref, g_ref, wk_ref, wv_ref, kn_ref, k_out, v_out):
    mh = _rms(m_ref[0], g_ref[...]).astype(BF16)
    k = _dot(mh, wk_ref[...])
    for h in range(XA_HEADS):
        hs = slice(h * XA_HD, (h + 1) * XA_HD)
        k_out[0, :, hs] = _rms(k[:, hs], kn_ref[...]).astype(BF16)
    v_out[0] = _dot(mh, wv_ref[...]).astype(BF16)


def mem_kv(mem, g, wk, wv, k_norm):
    b, m, d = mem.shape
    full = lambda bi: (0, 0)
    return pl.pallas_call(
        _mem_kv_kernel,
        grid=(b,),
        in_specs=[pl.BlockSpec((1, m, d), lambda bi: (bi, 0, 0)),
                  pl.BlockSpec((1, d), full),
                  pl.BlockSpec(wk.shape, full), pl.BlockSpec(wv.shape, full),
                  pl.BlockSpec((1, XA_HD), full)],
        out_specs=[pl.BlockSpec((1, m, XA_WIDTH), lambda bi: (bi, 0, 0))] * 2,
        out_shape=[jax.ShapeDtypeStruct((b, m, XA_WIDTH), BF16)] * 2,
        compiler_params=_cp(("parallel",)),
        name="mem_kv",
    )(mem, g.reshape(1, d), wk, wv, k_norm.reshape(1, XA_HD))


def _xattn_kernel(x_ref, g_ref, wq_ref, qn_ref, k_ref, v_ref, wo_ref, o_ref):
    x = x_ref[0]
    q = _dot(_rms(x, g_ref[...]).astype(BF16), wq_ref[...])
    k = k_ref[0]
    v = v_ref[0]
    outs = []
    for h in range(XA_HEADS):
        hs = slice(h * XA_HD, (h + 1) * XA_HD)
        qh = (_rms(q[:, hs], qn_ref[...])).astype(BF16)
        s = _dot_nt(qh, k[:, hs]) * (XA_HD ** -0.5)
        e = jnp.exp(s - jnp.max(s, axis=-1, keepdims=True))
        p = e / jnp.sum(e, axis=-1, keepdims=True)
        outs.append(_dot(p.astype(BF16), v[:, hs]).astype(BF16))
    o = jnp.concatenate(outs, axis=-1)
    o_ref[0] = x + _dot(o, wo_ref[...])


def xattn(x, g, wq, q_norm, k, v, wo, tm=512):
    b, s, d = x.shape
    m = k.shape[1]
    row = lambda bi, i: (bi, i, 0)
    full = lambda bi, i: (0, 0)
    return pl.pallas_call(
        _xattn_kernel,
        grid=(b, s // tm),
        in_specs=[pl.BlockSpec((1, tm, d), row),
                  pl.BlockSpec((1, d), full),
                  pl.BlockSpec(wq.shape, full),
                  pl.BlockSpec((1, XA_HD), full),
                  pl.BlockSpec((1, m, XA_WIDTH), lambda bi, i: (bi, 0, 0)),
                  pl.BlockSpec((1, m, XA_WIDTH), lambda bi, i: (bi, 0, 0)),
                  pl.BlockSpec(wo.shape, full)],
        out_specs=pl.BlockSpec((1, tm, d), row),
        out_shape=jax.ShapeDtypeStruct((b, s, d), F32),
        compiler_params=_cp(("parallel", "parallel")),
        name="xattn",
    )(x, g.reshape(1, d), wq, q_norm.reshape(1, XA_HD), k, v, wo)


def _top_rows(s, n, want_rank):
    out = []
    work = s
    rank = jnp.full(s.shape, float(s.shape[0] - 1), F32) if want_rank else None
    for i in range(n):
        m = jnp.max(work, axis=0, keepdims=True)
        out.append(m)
        hit = work >= m
        if want_rank:
            rank = jnp.where(hit, float(i), rank)
        work = jnp.where(hit, REMOVED, work)
    return out, rank


def _stack_rows(rows, pad_rows):
    tt = rows[0].shape[1]
    rowi = _iota((pad_rows, tt), 0)
    out = jnp.full((pad_rows, tt), REMOVED, F32)
    for i, r in enumerate(rows[:pad_rows]):
        out = jnp.where(rowi == i, r, out)
    return out


def _pair_sum_candidates(v1, v2, tt):
    row_full = _iota((24, tt), 0)
    row = _iota((8, tt), 0)
    v2_full = _stack_rows(v2, 24)
    v2_8 = _stack_rows(v2, 8)

    def shifted(k):
        return pltpu.roll(v2_8, k, 0)

    def pick(rows_from, first):
        out = v1[rows_from]
        for r in range(first + 1, 8):
            if rows_from + r - first < len(v1):
                out = jnp.where(row == r, v1[rows_from + r - first], out)
        return out

    return jnp.concatenate([
        jnp.where(row_full < 17, v1[0] + v2_full, REMOVED),
        v1[1] + v2_8,
        jnp.where(row < 5, v1[2] + v2_8,
                  jnp.where(row < 7, v1[5] + shifted(5), v1[8] + shifted(7))),
        jnp.where(row < 4, v1[3] + v2_8,
                  jnp.where(row < 7, v1[4] + shifted(4), v1[9] + shifted(7))),
        jnp.where(row < 2, v1[6] + v2_8,
                  jnp.where(row < 4, v1[7] + shifted(2), pick(10, 4) + v2[0])),
        jnp.where(row < 3, pick(14, 0) + v2[0], REMOVED),
    ], axis=0)


def _peer_kernel(x_ref, g_ref, wq_ref, k1_ref, k2_ref, u_ref, vt_ref, o_ref,
                 hbt_s, r2_s, na_s, e1_s, e2_s, acc_s, act0_s, act1_s, y0_s, y1_s, *, tt, ch, nch):
    c = pl.program_id(1)
    ntop = PEER_TOPK + 1
    assert ntop == 17 and nch % 2 == 0

    @pl.when(c == 0)
    def _():
        h = _rms(x_ref[...], g_ref[...])
        hbt_s[...] = h.T.astype(BF16)
        q = _dot(h.astype(BF16), wq_ref[...])
        acc_s[...] = jnp.zeros_like(acc_s)
        for hd in range(PEER_HEADS):
            qh = q[:, hd * 2 * PEER_HALF:(hd + 1) * 2 * PEER_HALF].astype(BF16)
            s1 = _dot_nt(k1_ref[...], qh)
            s2 = _dot_nt(k2_ref[...], qh)
            v1, _ = _top_rows(s1, ntop, False)
            v2, rank2 = _top_rows(s2, ntop, True)
            cand = _pair_sum_candidates(v1, v2, tt)
            top = v1[0] + v2[0]
            z = jnp.zeros((1, tt), F32)
            c_prev = top
            c_last = top
            work = cand
            for r in range(ntop):
                m = jnp.max(work, axis=0, keepdims=True)
                work = jnp.where(work >= m, REMOVED, work)
                if r < PEER_TOPK:
                    z = z + jnp.exp(m - top)
                c_prev, c_last = c_last, m
            thr = 0.5 * (c_prev + c_last)
            na = jnp.zeros((PEER_KEYS, tt), F32)
            for j in range(ntop):
                na = na + jnp.where(s1 + v2[j] >= thr, 1.0, 0.0)
            r2_s[hd] = rank2.astype(BF16)
            na_s[hd] = na
            e1_s[hd] = jnp.exp(s1 - v1[0])
            e2_s[hd] = (jnp.exp(s2 - v2[0]) / z).astype(BF16)

    n_slab = ch // PEER_KEYS
    half = tt // 2

    rows_mm = 256
    n_piece = (ch // rows_mm) * 2

    def project(act_ref, k):
        rg, hf = divmod(k, 2)
        rs = slice(rg * rows_mm, (rg + 1) * rows_mm)
        cs = slice(hf * half, (hf + 1) * half)
        act_ref[rs, cs] = _dot(u_ref[rs, :], hbt_s[:, cs])

    def gate_slab(chunk, al, act_ref, y_ref):
        rs = slice(al * PEER_KEYS, (al + 1) * PEER_KEYS)
        a = chunk * n_slab + al
        wc = jnp.zeros((PEER_KEYS, tt), BF16)
        for hd in range(PEER_HEADS):
            na = na_s[hd, pl.ds(a, 1), :].astype(BF16)
            e1 = e1_s[hd, pl.ds(a, 1), :].astype(BF16)
            wc = wc + jnp.where(r2_s[hd] < na, e2_s[hd] * e1, jnp.zeros((), BF16))
        y_ref[rs, :] = wc * _gelu(act_ref[rs, :].astype(BF16))

    def accumulate(y_ref, k):
        rg, hf = divmod(k, 2)
        rs = slice(rg * rows_mm, (rg + 1) * rows_mm)
        cs = slice(hf * half, (hf + 1) * half)
        acc_s[rs, cs] += _dot(vt_ref[rs, :], y_ref[:, cs])

    def step(chunk, act_new, act_old, y_new, y_old):
        assert n_piece == n_slab
        for k in range(n_slab):
            if act_new is not None:
                project(act_new, k)
            if y_old is not None:
                accumulate(y_old, k)
            if y_new is not None:
                gate_slab(chunk, k, act_old, y_new)

    @pl.when(c == 0)
    def _():
        step(None, act0_s, None, None, None)

    @pl.when(c == 1)
    def _():
        step(0, act1_s, act0_s, y0_s, None)

    @pl.when((c >= 2) & (c < nch) & (c % 2 == 0))
    def _():
        step(c - 1, act0_s, act1_s, y1_s, y0_s)

    @pl.when((c >= 2) & (c < nch) & (c % 2 == 1))
    def _():
        step(c - 1, act1_s, act0_s, y0_s, y1_s)

    @pl.when(c == nch)
    def _():
        step(nch - 1, None, act1_s, y1_s, y0_s)

    @pl.when(c == nch + 1)
    def _():
        step(None, None, None, None, y1_s)
        o_ref[...] = x_ref[...] + acc_s[...].T


def peer(x, g, wq, k1p, k2p, u, vt, tt=PEER_TT, ch=PEER_CH):
    t, d = x.shape
    ne = u.shape[0]
    nch = ne // ch
    return pl.pallas_call(
        functools.partial(_peer_kernel, tt=tt, ch=ch, nch=nch),
        grid=(t // tt, nch + 2),
        in_specs=[pl.BlockSpec((tt, d), lambda i, c: (i, 0)),
                  pl.BlockSpec((1, d), lambda i, c: (0, 0)),
                  pl.BlockSpec(wq.shape, lambda i, c: (0, 0)),
                  pl.BlockSpec(k1p.shape, lambda i, c: (0, 0)),
                  pl.BlockSpec(k2p.shape, lambda i, c: (0, 0)),
                  pl.BlockSpec((ch, d), lambda i, c: (jnp.minimum(c, nch - 1), 0)),
                  pl.BlockSpec((d, ch), lambda i, c: (0, jnp.clip(c - 2, 0, nch - 1)))],
        out_specs=pl.BlockSpec((tt, d), lambda i, c: (i, 0)),
        out_shape=jax.ShapeDtypeStruct((t, d), F32),
        scratch_shapes=[pltpu.VMEM((d, tt), BF16),
                        pltpu.VMEM((PEER_HEADS, PEER_KEYS, tt), BF16),
                        pltpu.VMEM((PEER_HEADS, PEER_KEYS, tt), F32),
                        pltpu.VMEM((PEER_HEADS, PEER_KEYS, tt), F32),
                        pltpu.VMEM((PEER_HEADS, PEER_KEYS, tt), BF16),
                        pltpu.VMEM((d, tt), F32),
                        pltpu.VMEM((ch, tt), F32),
                        pltpu.VMEM((ch, tt), F32),
                        pltpu.VMEM((ch, tt), BF16),
                        pltpu.VMEM((ch, tt), BF16)],
        compiler_params=_cp(("parallel", "arbitrary"), _PEER_FLAGS),
        name="peer",
    )(x, g.reshape(1, d), wq, k1p, k2p, u, vt)


_IN_WIDTHS = (NSA_HEADS * DK, 6 * NSA_GROUPS * DK, 3 * NSA_HEADS, 2 * SGU_WIDTH, POOL_WIDTH, 3 * D_MODEL)


def _in_proj_layout():
    splits, src = [], []
    o_src = o_dst = 0
    for w in _IN_WIDTHS:
        wp = -(-w // LANES) * LANES
        splits.append((o_dst, o_dst + wp))
        src.append((o_src, o_src + w))
        o_src += w
        o_dst += wp
    return splits, src, o_dst


def _cmp_to_slc(s):
    n_cmp_pad = s // D_CMP
    n_slc = s // L_SLC
    c0 = np.arange(n_cmp_pad)[:, None] * D_CMP
    s0 = np.arange(n_slc)[None, :] * L_SLC
    overlap = np.minimum(c0 + L_CMP, s0 + L_SLC) - np.maximum(c0, s0)
    m = np.maximum(overlap, 0) / D_CMP
    m[n_cmp_pad - 1] = 0.0
    return jnp.asarray(m, dtype=BF16)


def nsa_layer(zq, zkv, zg, q_norm, k_norm, cmp_pe, cmp_w1, cmp_w2, b, s):
    qn, kvc, ks, vs, kw, vw = nsa_prep(zq.reshape(b, s, -1), zkv.reshape(b, s, -1), q_norm, k_norm)
    nc = s // D_CMP
    xc = kvc.reshape(2, b, NSA_GROUPS, nc, D_CMP * DK)
    pe = cmp_pe.reshape(2, 2, D_CMP * DK)
    w1 = cmp_w1.reshape(2, 2, D_CMP * DK, CMP_HIDDEN).astype(BF16)
    kc = cmp_mlp(xc, pe, w1, cmp_w2.astype(BF16), k_norm[0])
    oc, selb = cmp_attn_select(qn, kc, _cmp_to_slc(s))
    ow = win_attn(qn, kw, vw)
    gl = zg[:, :3 * NSA_HEADS].reshape(b, s, NSA_HEADS, 3).transpose(0, 2, 1, 3)
    return sel_attn(qn, selb, ks, vs, oc, ow, gl)


def kernel(x, mem, mix_norm, w_in, nsa_q_norm, nsa_k_norm, cmp_pe, cmp_w1, cmp_w2, sgu_ln_g, sgu_ln_b, sgu_w, sgu_b, pool_w, pool_scale, lift_a, lift_b, lift_c, w_out, xa_norm, mem_norm, xa_wq, xa_wk, xa_wv, xa_q_norm, xa_k_norm, xa_wo, ffn_norm, peer_wq, peer_keys1, peer_keys2, peer_u, peer_v):
    b, s, d = x.shape
    t = b * s
    depth = w_in.shape[0]
    splits, src, n_pad = _in_proj_layout()
    zeros_half = jnp.zeros((PEER_KEYS, PEER_HALF), BF16)
    for l in range(depth):
        w_parts = []
        for (a0, a1), (d0, d1) in zip(src, splits):
            w_parts.append(jnp.pad(w_in[l][:, a0:a1], ((0, 0), (0, (d1 - d0) - (a1 - a0)))))
        w_pad = jnp.concatenate(w_parts, axis=1).astype(BF16)
        zq, zkv, zg, zs, zp, zm = norm_matmul(x.reshape(t, d), mix_norm[l], w_pad, splits, 256, "in_proj")

        oa = nsa_layer(zq, zkv, zg, nsa_q_norm[l], nsa_k_norm[l], cmp_pe[l], cmp_w1[l], cmp_w2[l], b, s)
        bias_full = jnp.repeat(sgu_b[l].T, SGU_WIDTH // SGU_GROUPS, axis=1)
        ob = sgu(zs, sgu_ln_g[l], sgu_ln_b[l], sgu_w[l], bias_full)
        cg = POOL_WIDTH // len(POOL_WINDOWS)
        w_bd = jnp.zeros((POOL_WIDTH, POOL_WIDTH), F32)
        for gi in range(len(POOL_WINDOWS)):
            w_bd = w_bd.at[gi * cg:(gi + 1) * cg, gi * cg:(gi + 1) * cg].set(pool_w[l, gi])
        oc = pool(zp.reshape(b, s, POOL_WIDTH), w_bd.astype(BF16), pool_scale[l])
        x = merge(x, oa, ob.reshape(b, s, SGU_WIDTH), oc, zm.reshape(b, s, 3 * d),
                  lift_a[l].astype(BF16), lift_b[l].astype(BF16), lift_c[l].astype(BF16),
                  w_out[l].astype(BF16))

        mk, mv = mem_kv(mem, mem_norm[l], xa_wk[l].astype(BF16), xa_wv[l].astype(BF16), xa_k_norm[l])
        x = xattn(x, xa_norm[l], xa_wq[l].astype(BF16), xa_q_norm[l], mk, mv, xa_wo[l].astype(BF16))

        k1p = jnp.concatenate([peer_keys1[l].astype(BF16), zeros_half], axis=1)
        k2p = jnp.concatenate([zeros_half, peer_keys2[l].astype(BF16)], axis=1)
        x = peer(x.reshape(t, d), ffn_norm[l], peer_wq[l].astype(BF16), k1p, k2p,
                 peer_u[l].astype(BF16), peer_v[l].T.astype(BF16)).reshape(b, s, d)
    return x
```

```python
import functools

import numpy as np
import jax
import jax.numpy as jnp
from jax import lax
from jax.experimental import pallas as pl
from jax.experimental.pallas import tpu as pltpu

F32 = jnp.float32
BF16 = jnp.bfloat16

EPS = 1e-6
LOG2E = 1.4426950408889634
NEG = -1e30
REMOVED = -3e38

D_MODEL = 1024
DK = 64
NSA_HEADS = 8
NSA_GROUPS = 2
NSA_REP = NSA_HEADS // NSA_GROUPS
L_CMP, D_CMP = 32, 16
CMP_HIDDEN = 128
L_SLC = 64
N_SEL = 16
WINDOW = 512
QB = 128
FORCE_BONUS = 1.0e3
SGU_WIDTH = 256
SGU_GROUPS = 4
SGU_CHUNK = 128
POOL_WIDTH = 256
POOL_WINDOWS = (2, 4, 8, 16)
XA_HEADS, XA_HD = 4, 128
XA_WIDTH = XA_HEADS * XA_HD
PEER_HEADS = 8
PEER_KEYS = 128
PEER_TOPK = 16
PEER_HALF = 64

LANES = 128
SEL_SUPER_BLOCKS = 64
SEL_SUPER = SEL_SUPER_BLOCKS * L_SLC
SEL_SUB = 512
PEER_TT = 512
PEER_CH = 1024
VMEM_LIMIT = 56 * 1024 * 1024


def _cp(sem):
    return pltpu.CompilerParams(dimension_semantics=sem, vmem_limit_bytes=VMEM_LIMIT)


def _gelu(x):
    return 0.5 * x * (1.0 + jnp.tanh(0.7978845608028654 * (x + 0.044715 * (x * x * x))))


def _rms(x, g):
    return x * lax.rsqrt(jnp.mean(x * x, axis=-1, keepdims=True) + EPS) * g


def _dot(a, b):
    return jnp.dot(a, b, preferred_element_type=F32)


def _dot_nt(a, b):
    return lax.dot_general(a, b, (((1,), (1,)), ((), ())), preferred_element_type=F32)


def _iota(shape, dim):
    return lax.broadcasted_iota(jnp.int32, shape, dim)


def _div_pow2(x, n):
    assert n & (n - 1) == 0
    return lax.shift_right_logical(x, jnp.int32(n.bit_length() - 1))


def _mod_pow2(x, n):
    assert n & (n - 1) == 0
    return x & (n - 1)


def _norm_matmul_kernel(x_ref, g_ref, w_ref, *o_refs, splits):
    hb = _rms(x_ref[...], g_ref[...]).astype(BF16)
    for o_ref, (a, b) in zip(o_refs, splits):
        o_ref[...] = _dot(hb, w_ref[:, a:b]).astype(o_ref.dtype)


def norm_matmul(x, g, w, splits, tm, name):
    t, d = x.shape
    n = w.shape[1]
    return pl.pallas_call(
        functools.partial(_norm_matmul_kernel, splits=splits),
        grid=(t // tm,),
        in_specs=[pl.BlockSpec((tm, d), lambda i: (i, 0)),
                  pl.BlockSpec((1, d), lambda i: (0, 0)),
                  pl.BlockSpec((d, n), lambda i: (0, 0))],
        out_specs=[pl.BlockSpec((tm, b - a), lambda i: (i, 0)) for a, b in splits],
        out_shape=[jax.ShapeDtypeStruct((t, b - a), F32) for a, b in splits],
        compiler_params=_cp(("parallel",)),
        name=name,
    )(x, g.reshape(1, d), w)


def _nsa_prep_kernel(zq_ref, zkv_ref, qn_ref, kn_ref,
                     q_out, kvc_out, ks_out, vs_out, kw_out, vw_out, *, ts):
    st = pl.program_id(1)
    zq = zq_ref[0]
    scale = DK ** -0.5 * LOG2E
    for h in range(NSA_HEADS):
        q_out[0, h] = (_rms(zq[:, h * DK:(h + 1) * DK], qn_ref[...]) * scale).astype(BF16)
    zkv = zkv_ref[0]

    def piece(i, g):
        o = i * NSA_GROUPS * DK + g * DK
        return zkv[:, o:o + DK]

    pos = st * ts + _iota((ts, SEL_SUPER_BLOCKS), 0)
    blk = _mod_pow2(_div_pow2(pos, L_SLC), SEL_SUPER_BLOCKS)
    onehot = jnp.where(_iota((ts, SEL_SUPER_BLOCKS), 1) == blk, 1.0, 0.0).astype(BF16)
    ones_col = jnp.where(_iota((ts, DK), 1) == 0, 1.0, 0.0)
    for g in range(NSA_GROUPS):
        kvc_out[0, 0, g] = piece(0, g)
        kvc_out[1, 0, g] = piece(1, g)
        ks = _rms(piece(2, g), kn_ref[1:2, :]).astype(BF16)
        ks_out[0, g] = jnp.concatenate([ks, onehot], axis=-1)
        vs_out[0, g] = jnp.concatenate([piece(3, g), ones_col], axis=-1).T.astype(BF16)
        kw_out[0, g] = _rms(piece(4, g), kn_ref[2:3, :]).astype(BF16)
        vw_out[0, g] = piece(5, g).astype(BF16)


def nsa_prep(zq, zkv, q_norm, k_norm, ts=512):
    b, s, _ = zq.shape
    g = NSA_GROUPS
    hm = lambda bi, si: (bi, 0, si, 0)
    return pl.pallas_call(
        functools.partial(_nsa_prep_kernel, ts=ts),
        grid=(b, s // ts),
        in_specs=[pl.BlockSpec((1, ts, NSA_HEADS * DK), lambda bi, si: (bi, si, 0)),
                  pl.BlockSpec((1, ts, 6 * g * DK), lambda bi, si: (bi, si, 0)),
                  pl.BlockSpec((1, DK), lambda bi, si: (0, 0)),
                  pl.BlockSpec((3, DK), lambda bi, si: (0, 0))],
        out_specs=[pl.BlockSpec((1, NSA_HEADS, ts, DK), hm),
                   pl.BlockSpec((2, 1, g, ts, DK), lambda bi, si: (0, bi, 0, si, 0)),
                   pl.BlockSpec((1, g, ts, 2 * DK), hm),
                   pl.BlockSpec((1, g, 2 * DK, ts), lambda bi, si: (bi, 0, 0, si)),
                   pl.BlockSpec((1, g, ts, DK), hm),
                   pl.BlockSpec((1, g, ts, DK), hm)],
        out_shape=[jax.ShapeDtypeStruct((b, NSA_HEADS, s, DK), BF16),
                   jax.ShapeDtypeStruct((2, b, g, s, DK), F32),
                   jax.ShapeDtypeStruct((b, g, s, 2 * DK), BF16),
                   jax.ShapeDtypeStruct((b, g, 2 * DK, s), BF16),
                   jax.ShapeDtypeStruct((b, g, s, DK), BF16),
                   jax.ShapeDtypeStruct((b, g, s, DK), BF16)],
        compiler_params=_cp(("parallel", "parallel")),
        name="nsa_prep",
    )(zq, zkv, q_norm.reshape(1, DK), k_norm)


def _cmp_mlp_kernel(x_ref, pe_ref, w1_ref, w2_ref, kn_ref, o_ref, *, nc):
    c = pl.program_id(0)
    x = x_ref[0, 0, 0]
    a = _dot((x + pe_ref[0, 0:1, :]).astype(BF16), w1_ref[0, 0])
    b = _dot((x + pe_ref[0, 1:2, :]).astype(BF16), w1_ref[0, 1])
    pre = a + pltpu.roll(b, nc - 1, 0)
    comp = _dot(_gelu(pre).astype(BF16), w2_ref[0])
    out = jnp.where(c == 0, _rms(comp, kn_ref[...]), comp)
    o_ref[0, 0, 0] = out.astype(BF16)


def cmp_mlp(xc, pe, w1, w2, k_norm0):
    _, b, g, nc, width = xc.shape
    return pl.pallas_call(
        functools.partial(_cmp_mlp_kernel, nc=nc),
        grid=(2, b, g),
        in_specs=[pl.BlockSpec((1, 1, 1, nc, width), lambda c, bi, gi: (c, bi, gi, 0, 0)),
                  pl.BlockSpec((1, 2, width), lambda c, bi, gi: (c, 0, 0)),
                  pl.BlockSpec((1, 2, width, CMP_HIDDEN), lambda c, bi, gi: (c, 0, 0, 0)),
                  pl.BlockSpec((1, CMP_HIDDEN, DK), lambda c, bi, gi: (c, 0, 0)),
                  pl.BlockSpec((1, DK), lambda c, bi, gi: (0, 0))],
        out_specs=pl.BlockSpec((1, 1, 1, nc, DK), lambda c, bi, gi: (c, bi, gi, 0, 0)),
        out_shape=jax.ShapeDtypeStruct((2, b, g, nc, DK), BF16),
        compiler_params=_cp(("parallel", "parallel", "parallel")),
        name="cmp_mlp",
    )(xc, pe, w1, w2, k_norm0.reshape(1, DK))


def _cmp_attn_kernel(q_ref, k_ref, v_ref, c_ref, oc_ref, sb_ref, *, nc, n_slc):
    qb = pl.program_id(2)
    rows = NSA_REP * QB
    q = q_ref[0].reshape(rows, DK)
    s = _dot_nt(q, k_ref[0, 0, 0])
    t = qb * QB + _mod_pow2(_iota((rows, nc), 0), QB)
    valid = _iota((rows, nc), 1) * D_CMP + (L_CMP - 1) <= t
    s = jnp.where(valid, s, NEG)
    e = jnp.where(valid, jnp.exp2(s - jnp.max(s, axis=-1, keepdims=True)), 0.0)
    p = e / jnp.maximum(jnp.sum(e, axis=-1, keepdims=True), 1e-30)
    oc_ref[0] = _dot(p.astype(BF16), v_ref[0, 0, 0]).reshape(NSA_REP, QB, DK)

    ps = p[0:QB]
    for r in range(1, NSA_REP):
        ps = ps + p[r * QB:(r + 1) * QB]
    hi = ps.astype(BF16)
    lo = (ps - hi.astype(F32)).astype(BF16)
    imp = _dot(hi, c_ref[...]) + _dot(lo, c_ref[...])

    cur = _div_pow2(qb * QB + _iota((QB, n_slc), 0), L_SLC)
    j = _iota((QB, n_slc), 1)
    forced = (j == 0) | (j == cur) | (j == cur - 1)
    score = jnp.where(j <= cur, imp + jnp.where(forced, FORCE_BONUS, 0.0), NEG)
    work = score
    tau = None
    for _ in range(N_SEL):
        tau = jnp.max(work, axis=-1, keepdims=True)
        work = jnp.where(work >= tau, REMOVED, work)
    selected = (score > 0.5 * NEG) & (score >= tau)
    bias = jnp.where(selected, 0.0, NEG).astype(BF16)
    for si in range(n_slc // SEL_SUPER_BLOCKS):
        sb_ref[0, 0, 0, si] = bias[:, si * SEL_SUPER_BLOCKS:(si + 1) * SEL_SUPER_BLOCKS]


def cmp_attn_select(qn, kc, cmat):
    b, h, s, _ = qn.shape
    g = NSA_GROUPS
    nc = kc.shape[3]
    n_slc = s // L_SLC
    nsup = n_slc // SEL_SUPER_BLOCKS
    nqb = s // QB
    return pl.pallas_call(
        functools.partial(_cmp_attn_kernel, nc=nc, n_slc=n_slc),
        grid=(b, g, nqb),
        in_specs=[pl.BlockSpec((1, NSA_REP, QB, DK), lambda bi, gi, qi: (bi, gi, qi, 0)),
                  pl.BlockSpec((1, 1, 1, nc, DK), lambda bi, gi, qi: (0, bi, gi, 0, 0)),
                  pl.BlockSpec((1, 1, 1, nc, DK), lambda bi, gi, qi: (1, bi, gi, 0, 0)),
                  pl.BlockSpec((nc, n_slc), lambda bi, gi, qi: (0, 0))],
        out_specs=[pl.BlockSpec((1, NSA_REP, QB, DK), lambda bi, gi, qi: (bi, gi, qi, 0)),
                   pl.BlockSpec((1, 1, 1, nsup, QB, SEL_SUPER_BLOCKS),
                                lambda bi, gi, qi: (bi, gi, qi, 0, 0, 0))],
        out_shape=[jax.ShapeDtypeStruct((b, h, s, DK), F32),
                   jax.ShapeDtypeStruct((b, g, nqb, nsup, QB, SEL_SUPER_BLOCKS), BF16)],
        compiler_params=_cp(("parallel", "parallel", "parallel")),
        name="cmp_attn_select",
    )(qn, kc, kc, cmat)


def _win_attn_kernel(q_ref, *refs):
    nkb = WINDOW // QB + 1
    k_refs, v_refs, o_ref = refs[:nkb], refs[nkb:2 * nkb], refs[2 * nkb]
    qb = pl.program_id(2)
    rows = NSA_REP * QB
    nk = nkb * QB
    q = q_ref[0].reshape(rows, DK)
    k = jnp.concatenate([r[0, 0] for r in k_refs], axis=0)
    v = jnp.concatenate([r[0, 0] for r in v_refs], axis=0)
    s = _dot_nt(q, k)
    t = qb * QB + _mod_pow2(_iota((rows, nk), 0), QB)
    kpos = (qb - (nkb - 1)) * QB + _iota((rows, nk), 1)
    delta = t - kpos
    valid = (delta >= 0) & (delta < WINDOW) & (kpos >= 0)
    s = jnp.where(valid, s, NEG)
    e = jnp.where(valid, jnp.exp2(s - jnp.max(s, axis=-1, keepdims=True)), 0.0)
    p = e / jnp.maximum(jnp.sum(e, axis=-1, keepdims=True), 1e-30)
    o_ref[0] = _dot(p.astype(BF16), v).reshape(NSA_REP, QB, DK)


def win_attn(qn, kw, vw):
    b, h, s, _ = qn.shape
    g = NSA_GROUPS
    nkb = WINDOW // QB + 1
    nqb = s // QB

    def kv_spec(i):
        return pl.BlockSpec((1, 1, QB, DK),
                            lambda bi, gi, qi: (bi, gi, jnp.maximum(qi - (nkb - 1) + i, 0), 0))

    return pl.pallas_call(
        _win_attn_kernel,
        grid=(b, g, nqb),
        in_specs=[pl.BlockSpec((1, NSA_REP, QB, DK), lambda bi, gi, qi: (bi, gi, qi, 0))]
        + [kv_spec(i) for i in range(nkb)] + [kv_spec(i) for i in range(nkb)],
        out_specs=pl.BlockSpec((1, NSA_REP, QB, DK), lambda bi, gi, qi: (bi, gi, qi, 0)),
        out_shape=jax.ShapeDtypeStruct((b, h, s, DK), F32),
        compiler_params=_cp(("parallel", "parallel", "parallel")),
        name="win_attn",
    )(qn, *([kw] * nkb), *([vw] * nkb))


def _sel_attn_kernel(q_ref, sb_ref, k_ref, vt_ref, oc_ref, ow_ref, gl_ref, o_ref,
                     qp_s, m_s, acc_s, s0_s, s1_s, p0_s, p1_s, a0_s, a1_s):
    qb = pl.program_id(2)
    si = pl.program_id(3)
    rows = NSA_REP * QB
    last = (qb * QB) // SEL_SUPER

    @pl.when(si == 0)
    def _():
        qp_s[:, 0:DK] = q_ref[0].reshape(rows, DK)
        m_s[...] = jnp.full_like(m_s, NEG)
        acc_s[...] = jnp.zeros_like(acc_s)

    @pl.when(si <= last)
    def _():
        sb = sb_ref[0, 0, 0, 0]
        for r in range(NSA_REP):
            qp_s[r * QB:(r + 1) * QB, DK:2 * DK] = sb
        qv = qp_s[...]
        nsub = jnp.minimum(SEL_SUPER // SEL_SUB,
                           (qb * QB + QB - si * SEL_SUPER + SEL_SUB - 1) // SEL_SUB)

        def scores(i, s_ref):
            off = pl.multiple_of(i * SEL_SUB, SEL_SUB)
            s_ref[...] = _dot_nt(k_ref[0, 0, pl.ds(off, SEL_SUB), :], qv)

        def weights(i, s_ref, p_ref, a_ref, causal):
            st = s_ref[...]
            if causal:
                kpos = si * SEL_SUPER + i * SEL_SUB + _iota((SEL_SUB, rows), 0)
                t = qb * QB + _mod_pow2(_iota((SEL_SUB, rows), 1), QB)
                st = jnp.where(kpos <= t, st, NEG)
            m_old = m_s[...]
            m_new = jnp.maximum(m_old, jnp.max(st, axis=0, keepdims=True))
            p_ref[...] = jnp.exp2(st - m_new).astype(BF16)
            a_ref[...] = jnp.exp2(m_old - m_new)
            m_s[...] = m_new

        def accumulate(i, p_ref, a_ref):
            off = pl.multiple_of(i * SEL_SUB, SEL_SUB)
            acc_s[...] = a_ref[...] * acc_s[...] + _dot(vt_ref[0, 0, :, pl.ds(off, SEL_SUB)], p_ref[...])

        p1_s[...] = jnp.zeros_like(p1_s)
        a1_s[...] = jnp.ones_like(a1_s)
        scores(0, s0_s)
        n_pairs = (nsub - 1) // 2
        odd_tail = (nsub - 1) - 2 * n_pairs

        def pair(j, carry):
            scores(2 * j + 1, s1_s)
            weights(2 * j, s0_s, p0_s, a0_s, False)
            accumulate(jnp.maximum(2 * j - 1, 0), p1_s, a1_s)
            scores(2 * j + 2, s0_s)
            weights(2 * j + 1, s1_s, p1_s, a1_s, False)
            accumulate(2 * j, p0_s, a0_s)
            return carry

        lax.fori_loop(0, n_pairs, pair, 0)
        e = 2 * n_pairs

        @pl.when(odd_tail == 0)
        def _():
            weights(e, s0_s, p0_s, a0_s, True)
            accumulate(jnp.maximum(e - 1, 0), p1_s, a1_s)
            accumulate(e, p0_s, a0_s)

        @pl.when(odd_tail == 1)
        def _():
            scores(e + 1, s1_s)
            weights(e, s0_s, p0_s, a0_s, False)
            accumulate(jnp.maximum(e - 1, 0), p1_s, a1_s)
            weights(e + 1, s1_s, p1_s, a1_s, True)
            accumulate(e, p0_s, a0_s)
            accumulate(e + 1, p1_s, a1_s)

    @pl.when(si == last)
    def _():
        o_s = (acc_s[...] / jnp.maximum(acc_s[DK:DK + 1, :], 1e-30)).T[:, 0:DK]
        gate = jax.nn.sigmoid(gl_ref[0].reshape(rows, 3))
        out = (gate[:, 0:1] * oc_ref[0].reshape(rows, DK) + gate[:, 1:2] * o_s
               + gate[:, 2:3] * ow_ref[0].reshape(rows, DK))
        o_ref[0] = out.reshape(NSA_REP, QB, DK)


def sel_attn(qn, selb, ks, vs, oc, ow, gl):
    b, h, s, _ = qn.shape
    g = NSA_GROUPS
    nqb = s // QB
    nsup = s // SEL_SUPER
    qmap = lambda bi, gi, qi, si: (bi, gi, qi, 0)

    def eff(qi, si):
        return jnp.minimum(si, (qi * QB) // SEL_SUPER)

    return pl.pallas_call(
        _sel_attn_kernel,
        grid=(b, g, nqb, nsup),
        in_specs=[pl.BlockSpec((1, NSA_REP, QB, DK), qmap),
                  pl.BlockSpec((1, 1, 1, 1, QB, SEL_SUPER_BLOCKS),
                               lambda bi, gi, qi, si: (bi, gi, qi, eff(qi, si), 0, 0)),
                  pl.BlockSpec((1, 1, SEL_SUPER, 2 * DK), lambda bi, gi, qi, si: (bi, gi, eff(qi, si), 0)),
                  pl.BlockSpec((1, 1, 2 * DK, SEL_SUPER), lambda bi, gi, qi, si: (bi, gi, 0, eff(qi, si))),
                  pl.BlockSpec((1, NSA_REP, QB, DK), qmap),
                  pl.BlockSpec((1, NSA_REP, QB, DK), qmap),
                  pl.BlockSpec((1, NSA_REP, QB, 3), qmap)],
        out_specs=pl.BlockSpec((1, NSA_REP, QB, DK), qmap),
        out_shape=jax.ShapeDtypeStruct((b, h, s, DK), F32),
        scratch_shapes=[pltpu.VMEM((NSA_REP * QB, 2 * DK), BF16),
                        pltpu.VMEM((1, NSA_REP * QB), F32),
                        pltpu.VMEM((2 * DK, NSA_REP * QB), F32),
                        pltpu.VMEM((SEL_SUB, NSA_REP * QB), F32),
                        pltpu.VMEM((SEL_SUB, NSA_REP * QB), F32),
                        pltpu.VMEM((SEL_SUB, NSA_REP * QB), BF16),
                        pltpu.VMEM((SEL_SUB, NSA_REP * QB), BF16),
                        pltpu.VMEM((1, NSA_REP * QB), F32),
                        pltpu.VMEM((1, NSA_REP * QB), F32)],
        compiler_params=_cp(("parallel", "parallel", "parallel", "arbitrary")),
        name="sel_attn",
    )(qn, selb, ks, vs, oc, ow, gl)


def _sgu_kernel(z_ref, g_ref, b_ref, w_ref, bias_ref, o_ref, *, tm):
    z = _gelu(z_ref[...])
    u = z[:, :SGU_WIDTH]
    v = z[:, SGU_WIDTH:]
    mu = jnp.mean(v, axis=-1, keepdims=True)
    var = jnp.mean(jnp.square(v - mu), axis=-1, keepdims=True)
    vn = ((v - mu) * lax.rsqrt(var + EPS) * g_ref[...] + b_ref[...]).astype(BF16)
    tril = _iota((SGU_CHUNK, SGU_CHUNK), 1) <= _iota((SGU_CHUNK, SGU_CHUNK), 0)
    wm = [jnp.where(tril, w_ref[gi], 0.0).astype(BF16) for gi in range(SGU_GROUPS)]
    grp = _div_pow2(_iota((SGU_CHUNK, SGU_WIDTH), 1), SGU_WIDTH // SGU_GROUPS)
    for ch in range(tm // SGU_CHUNK):
        rs = slice(ch * SGU_CHUNK, (ch + 1) * SGU_CHUNK)
        mixed = bias_ref[...]
        for gi in range(SGU_GROUPS):
            mixed = mixed + jnp.where(grp == gi, _dot(wm[gi], vn[rs]), 0.0)
        o_ref[rs, :] = u[rs] * mixed


def sgu(z, ln_g, ln_b, w, bias_full, tm=512):
    t = z.shape[0]
    return pl.pallas_call(
        functools.partial(_sgu_kernel, tm=tm),
        grid=(t // tm,),
        in_specs=[pl.BlockSpec((tm, 2 * SGU_WIDTH), lambda i: (i, 0)),
                  pl.BlockSpec((1, SGU_WIDTH), lambda i: (0, 0)),
                  pl.BlockSpec((1, SGU_WIDTH), lambda i: (0, 0)),
                  pl.BlockSpec((SGU_GROUPS, SGU_CHUNK, SGU_CHUNK), lambda i: (0, 0, 0)),
                  pl.BlockSpec((SGU_CHUNK, SGU_WIDTH), lambda i: (0, 0))],
        out_specs=pl.BlockSpec((tm, SGU_WIDTH), lambda i: (i, 0)),
        out_shape=jax.ShapeDtypeStruct((t, SGU_WIDTH), F32),
        compiler_params=_cp(("parallel",)),
        name="sgu",
    )(z, ln_g.reshape(1, -1), ln_b.reshape(1, -1), w, bias_full)


def _pool_kernel(p_ref, h_ref, w_ref, sc_ref, o_ref, *, tp):
    i = pl.program_id(1)
    halo_rows = POOL_WINDOWS[-1]
    p = p_ref[0]
    halo = jnp.where(i > 0, h_ref[0], 0.0)
    ext = jnp.concatenate([halo, p], axis=0)
    sums = [ext]
    shift = 1
    for _ in POOL_WINDOWS:
        prev = sums[-1]
        sums.append(prev + pltpu.roll(prev, shift, 0))
        shift *= 2
    pos = i * tp + _iota((tp, POOL_WIDTH), 0)
    grp = _div_pow2(_iota((tp, POOL_WIDTH), 1), POOL_WIDTH // len(POOL_WINDOWS))
    d = jnp.zeros((tp, POOL_WIDTH), F32)
    for gi, w in enumerate(POOL_WINDOWS):
        mean = sums[gi + 1][halo_rows:halo_rows + tp] / jnp.minimum(pos + 1, w).astype(F32)
        d = jnp.where(grp == gi, mean, d)
    d = d - p
    o_ref[0] = _dot(d.astype(BF16), w_ref[...]) * sc_ref[...]


def pool(p, w_bd, scale, tp=512):
    b, s, c = p.shape
    halo_rows = POOL_WINDOWS[-1]
    return pl.pallas_call(
        functools.partial(_pool_kernel, tp=tp),
        grid=(b, s // tp),
        in_specs=[pl.BlockSpec((1, tp, c), lambda bi, i: (bi, i, 0)),
                  pl.BlockSpec((1, halo_rows, c),
                               lambda bi, i: (bi, jnp.maximum(i * (tp // halo_rows) - 1, 0), 0)),
                  pl.BlockSpec((c, c), lambda bi, i: (0, 0)),
                  pl.BlockSpec((1, c), lambda bi, i: (0, 0))],
        out_specs=pl.BlockSpec((1, tp, c), lambda bi, i: (bi, i, 0)),
        out_shape=jax.ShapeDtypeStruct((b, s, c), F32),
        compiler_params=_cp(("parallel", "parallel")),
        name="pool",
    )(p, p, w_bd, scale.reshape(1, c))


def _merge_kernel(x_ref, oa_ref, ob_ref, oc_ref, zm_ref, la_ref, lb_ref, lc_ref, wo_ref, o_ref):
    d = D_MODEL
    oa = jnp.concatenate([oa_ref[0, h].astype(BF16) for h in range(NSA_HEADS)], axis=-1)
    zm = zm_ref[0]
    merged = (jax.nn.sigmoid(zm[:, 0:d]) * _dot(oa, la_ref[...])
              + jax.nn.sigmoid(zm[:, d:2 * d]) * _dot(ob_ref[0].astype(BF16), lb_ref[...])
              + jax.nn.sigmoid(zm[:, 2 * d:3 * d]) * _dot(oc_ref[0].astype(BF16), lc_ref[...]))
    o_ref[0] = x_ref[0] + _dot(merged.astype(BF16), wo_ref[...])


def merge(x, oa, ob, oc, zm, la, lb, lc, wo, tm=512):
    b, s, d = x.shape
    row = lambda bi, i: (bi, i, 0)
    full = lambda bi, i: (0, 0)
    return pl.pallas_call(
        _merge_kernel,
        grid=(b, s // tm),
        in_specs=[pl.BlockSpec((1, tm, d), row),
                  pl.BlockSpec((1, NSA_HEADS, tm, DK), lambda bi, i: (bi, 0, i, 0)),
                  pl.BlockSpec((1, tm, SGU_WIDTH), row),
                  pl.BlockSpec((1, tm, POOL_WIDTH), row),
                  pl.BlockSpec((1, tm, 3 * d), row),
                  pl.BlockSpec(la.shape, full), pl.BlockSpec(lb.shape, full),
                  pl.BlockSpec(lc.shape, full), pl.BlockSpec(wo.shape, full)],
        out_specs=pl.BlockSpec((1, tm, d), row),
        out_shape=jax.ShapeDtypeStruct((b, s, d), F32),
        compiler_params=_cp(("parallel", "parallel")),
        name="merge",
    )(x, oa, ob, oc, zm, la, lb, lc, wo)


def _mem_kv_kernel(m_ref, g_ref, wk_ref, wv_ref, kn_ref, k_out, v_out):
    mh = _rms(m_ref[0], g_ref[...]).astype(BF16)
    k = _dot(mh, wk_ref[...])
    for h in range(XA_HEADS):
        hs = slice(h * XA_HD, (h + 1) * XA_HD)
        k_out[0, :, hs] = _rms(k[:, hs], kn_ref[...]).astype(BF16)
    v_out[0] = _dot(mh, wv_ref[...]).astype(BF16)


def mem_kv(mem, g, wk, wv, k_norm):
    b, m, d = mem.shape
    full = lambda bi: (0, 0)
    return pl.pallas_call(
        _mem_kv_kernel,
        grid=(b,),
        in_specs=[pl.BlockSpec((1, m, d), lambda bi: (bi, 0, 0)),
                  pl.BlockSpec((1, d), full),
                  pl.BlockSpec(wk.shape, full), pl.BlockSpec(wv.shape, full),
                  pl.BlockSpec((1, XA_HD), full)],
        out_specs=[pl.BlockSpec((1, m, XA_WIDTH), lambda bi: (bi, 0, 0))] * 2,
        out_shape=[jax.ShapeDtypeStruct((b, m, XA_WIDTH), BF16)] * 2,
        compiler_params=_cp(("parallel",)),
        name="mem_kv",
    )(mem, g.reshape(1, d), wk, wv, k_norm.reshape(1, XA_HD))


def _xattn_kernel(x_ref, g_ref, wq_ref, qn_ref, k_ref, v_ref, wo_ref, o_ref):
    x = x_ref[0]
    q = _dot(_rms(x, g_ref[...]).astype(BF16), wq_ref[...])
    k = k_ref[0]
    v = v_ref[0]
    outs = []
    for h in range(XA_HEADS):
        hs = slice(h * XA_HD, (h + 1) * XA_HD)
        qh = (_rms(q[:, hs], qn_ref[...])).astype(BF16)
        s = _dot_nt(qh, k[:, hs]) * (XA_HD ** -0.5)
        e = jnp.exp(s - jnp.max(s, axis=-1, keepdims=True))
        p = e / jnp.sum(e, axis=-1, keepdims=True)
        outs.append(_dot(p.astype(BF16), v[:, hs]).astype(BF16))
    o = jnp.concatenate(outs, axis=-1)
    o_ref[0] = x + _dot(o, wo_ref[...])


def xattn(x, g, wq, q_norm, k, v, wo, tm=512):
    b, s, d = x.shape
    m = k.shape[1]
    row = lambda bi, i: (bi, i, 0)
    full = lambda bi, i: (0, 0)
    return pl.pallas_call(
        _xattn_kernel,
        grid=(b, s // tm),
        in_specs=[pl.BlockSpec((1, tm, d), row),
                  pl.BlockSpec((1, d), full),
                  pl.BlockSpec(wq.shape, full),
                  pl.BlockSpec((1, XA_HD), full),
                  pl.BlockSpec((1, m, XA_WIDTH), lambda bi, i: (bi, 0, 0)),
                  pl.BlockSpec((1, m, XA_WIDTH), lambda bi, i: (bi, 0, 0)),
                  pl.BlockSpec(wo.shape, full)],
        out_specs=pl.BlockSpec((1, tm, d), row),
        out_shape=jax.ShapeDtypeStruct((b, s, d), F32),
        compiler_params=_cp(("parallel", "parallel")),
        name="xattn",
    )(x, g.reshape(1, d), wq, q_norm.reshape(1, XA_HD), k, v, wo)


def _top_rows(s, n, want_rank):
    out = []
    work = s
    rank = jnp.full(s.shape, float(s.shape[0] - 1), F32) if want_rank else None
    for i in range(n):
        m = jnp.max(work, axis=0, keepdims=True)
        out.append(m)
        hit = work >= m
        if want_rank:
            rank = jnp.where(hit, float(i), rank)
        work = jnp.where(hit, REMOVED, work)
    return out, rank


def _stack_rows(rows, pad_rows):
    tt = rows[0].shape[1]
    rowi = _iota((pad_rows, tt), 0)
    out = jnp.full((pad_rows, tt), REMOVED, F32)
    for i, r in enumerate(rows[:pad_rows]):
        out = jnp.where(rowi == i, r, out)
    return out


def _pair_sum_candidates(v1, v2, tt):
    row_full = _iota((24, tt), 0)
    row = _iota((8, tt), 0)
    v2_full = _stack_rows(v2, 24)
    v2_8 = _stack_rows(v2, 8)

    def shifted(k):
        return pltpu.roll(v2_8, k, 0)

    def pick(rows_from, first):
        out = v1[rows_from]
        for r in range(first + 1, 8):
            if rows_from + r - first < len(v1):
                out = jnp.where(row == r, v1[rows_from + r - first], out)
        return out

    return jnp.concatenate([
        jnp.where(row_full < 17, v1[0] + v2_full, REMOVED),
        v1[1] + v2_8,
        jnp.where(row < 5, v1[2] + v2_8,
                  jnp.where(row < 7, v1[5] + shifted(5), v1[8] + shifted(7))),
        jnp.where(row < 4, v1[3] + v2_8,
                  jnp.where(row < 7, v1[4] + shifted(4), v1[9] + shifted(7))),
        jnp.where(row < 2, v1[6] + v2_8,
                  jnp.where(row < 4, v1[7] + shifted(2), pick(10, 4) + v2[0])),
        jnp.where(row < 3, pick(14, 0) + v2[0], REMOVED),
    ], axis=0)


def _peer_kernel(x_ref, g_ref, wq_ref, k1_ref, k2_ref, u_ref, vt_ref, o_ref,
                 hb_s, q_s, r2_s, na_s, e1_s, e2_s, acc_s, *, tt, ch):
    c = pl.program_id(1)
    ntop = PEER_TOPK + 1
    assert ntop == 17

    @pl.when(c == 0)
    def _():
        hb = _rms(x_ref[...], g_ref[...]).astype(BF16)
        hb_s[...] = hb
        q = _dot(hb, wq_ref[...])
        for hd in range(PEER_HEADS):
            q_s[hd] = q[:, hd * 2 * PEER_HALF:(hd + 1) * 2 * PEER_HALF].astype(BF16)
        acc_s[...] = jnp.zeros_like(acc_s)

        def route(hd, carry):
            qh = q_s[hd]
            s1 = _dot_nt(k1_ref[...], qh)
            s2 = _dot_nt(k2_ref[...], qh)
            v1, _ = _top_rows(s1, ntop, False)
            v2, rank2 = _top_rows(s2, ntop, True)
            cand = _pair_sum_candidates(v1, v2, tt)
            top = v1[0] + v2[0]
            z = jnp.zeros((1, tt), F32)
            c_prev = top
            c_last = top
            work = cand
            for r in range(ntop):
                m = jnp.max(work, axis=0, keepdims=True)
                work = jnp.where(work >= m, REMOVED, work)
                if r < PEER_TOPK:
                    z = z + jnp.exp(m - top)
                c_prev, c_last = c_last, m
            thr = 0.5 * (c_prev + c_last)
            na = jnp.zeros((PEER_KEYS, tt), F32)
            for j in range(ntop):
                na = na + jnp.where(s1 + v2[j] >= thr, 1.0, 0.0)
            r2_s[hd] = rank2.astype(BF16)
            na_s[hd] = na
            e1_s[hd] = jnp.exp(s1 - v1[0])
            e2_s[hd] = (jnp.exp(s2 - v2[0]) / z).astype(BF16)
            return carry

        lax.fori_loop(0, PEER_HEADS, route, 0)

    gl = _gelu(_dot_nt(u_ref[...], hb_s[...]).astype(BF16))
    ys = []
    for al in range(ch // PEER_KEYS):
        a = c * (ch // PEER_KEYS) + al
        wc = jnp.zeros((PEER_KEYS, tt), BF16)
        for hd in range(PEER_HEADS):
            na = na_s[hd, pl.ds(a, 1), :].astype(BF16)
            e1 = e1_s[hd, pl.ds(a, 1), :].astype(BF16)
            wc = wc + jnp.where(r2_s[hd] < na, e2_s[hd] * e1, jnp.zeros((), BF16))
        ys.append(wc * gl[al * PEER_KEYS:(al + 1) * PEER_KEYS])
    y = jnp.concatenate(ys, axis=0)
    acc_s[...] += _dot(vt_ref[...], y)

    @pl.when(c == pl.num_programs(1) - 1)
    def _():
        o_ref[...] = x_ref[...] + acc_s[...].T


def peer(x, g, wq, k1p, k2p, u, vt, tt=PEER_TT, ch=PEER_CH):
    t, d = x.shape
    ne = u.shape[0]
    return pl.pallas_call(
        functools.partial(_peer_kernel, tt=tt, ch=ch),
        grid=(t // tt, ne // ch),
        in_specs=[pl.BlockSpec((tt, d), lambda i, c: (i, 0)),
                  pl.BlockSpec((1, d), lambda i, c: (0, 0)),
                  pl.BlockSpec(wq.shape, lambda i, c: (0, 0)),
                  pl.BlockSpec(k1p.shape, lambda i, c: (0, 0)),
                  pl.BlockSpec(k2p.shape, lambda i, c: (0, 0)),
                  pl.BlockSpec((ch, d), lambda i, c: (c, 0)),
                  pl.BlockSpec((d, ch), lambda i, c: (0, c))],
        out_specs=pl.BlockSpec((tt, d), lambda i, c: (i, 0)),
        out_shape=jax.ShapeDtypeStruct((t, d), F32),
        scratch_shapes=[pltpu.VMEM((tt, d), BF16),
                        pltpu.VMEM((PEER_HEADS, tt, 2 * PEER_HALF), BF16),
                        pltpu.VMEM((PEER_HEADS, PEER_KEYS, tt), BF16),
                        pltpu.VMEM((PEER_HEADS, PEER_KEYS, tt), F32),
                        pltpu.VMEM((PEER_HEADS, PEER_KEYS, tt), F32),
                        pltpu.VMEM((PEER_HEADS, PEER_KEYS, tt), BF16),
                        pltpu.VMEM((d, tt), F32)],
        compiler_params=_cp(("parallel", "arbitrary")),
        name="peer",
    )(x, g.reshape(1, d), wq, k1p, k2p, u, vt)


_IN_WIDTHS = (NSA_HEADS * DK, 6 * NSA_GROUPS * DK, 3 * NSA_HEADS, 2 * SGU_WIDTH, POOL_WIDTH, 3 * D_MODEL)


def _in_proj_layout():
    splits, src = [], []
    o_src = o_dst = 0
    for w in _IN_WIDTHS:
        wp = -(-w // LANES) * LANES
        splits.append((o_dst, o_dst + wp))
        src.append((o_src, o_src + w))
        o_src += w
        o_dst += wp
    return splits, src, o_dst


def _cmp_to_slc(s):
    n_cmp_pad = s // D_CMP
    n_slc = s // L_SLC
    c0 = np.arange(n_cmp_pad)[:, None] * D_CMP
    s0 = np.arange(n_slc)[None, :] * L_SLC
    overlap = np.minimum(c0 + L_CMP, s0 + L_SLC) - np.maximum(c0, s0)
    m = np.maximum(overlap, 0) / D_CMP
    m[n_cmp_pad - 1] = 0.0
    return jnp.asarray(m, dtype=BF16)


def nsa_layer(zq, zkv, zg, q_norm, k_norm, cmp_pe, cmp_w1, cmp_w2, b, s):
    qn, kvc, ks, vs, kw, vw = nsa_prep(zq.reshape(b, s, -1), zkv.reshape(b, s, -1), q_norm, k_norm)
    nc = s // D_CMP
    xc = kvc.reshape(2, b, NSA_GROUPS, nc, D_CMP * DK)
    pe = cmp_pe.reshape(2, 2, D_CMP * DK)
    w1 = cmp_w1.reshape(2, 2, D_CMP * DK, CMP_HIDDEN).astype(BF16)
    kc = cmp_mlp(xc, pe, w1, cmp_w2.astype(BF16), k_norm[0])
    oc, selb = cmp_attn_select(qn, kc, _cmp_to_slc(s))
    ow = win_attn(qn, kw, vw)
    gl = zg[:, :3 * NSA_HEADS].reshape(b, s, NSA_HEADS, 3).transpose(0, 2, 1, 3)
    return sel_attn(qn, selb, ks, vs, oc, ow, gl)


def kernel(x, mem, mix_norm, w_in, nsa_q_norm, nsa_k_norm, cmp_pe, cmp_w1, cmp_w2, sgu_ln_g, sgu_ln_b, sgu_w, sgu_b, pool_w, pool_scale, lift_a, lift_b, lift_c, w_out, xa_norm, mem_norm, xa_wq, xa_wk, xa_wv, xa_q_norm, xa_k_norm, xa_wo, ffn_norm, peer_wq, peer_keys1, peer_keys2, peer_u, peer_v):
    b, s, d = x.shape
    t = b * s
    depth = w_in.shape[0]
    splits, src, n_pad = _in_proj_layout()
    zeros_half = jnp.zeros((PEER_KEYS, PEER_HALF), BF16)
    for l in range(depth):
        w_parts = []
        for (a0, a1), (d0, d1) in zip(src, splits):
            w_parts.append(jnp.pad(w_in[l][:, a0:a1], ((0, 0), (0, (d1 - d0) - (a1 - a0)))))
        w_pad = jnp.concatenate(w_parts, axis=1).astype(BF16)
        zq, zkv, zg, zs, zp, zm = norm_matmul(x.reshape(t, d), mix_norm[l], w_pad, splits, 256, "in_proj")

        oa = nsa_layer(zq, zkv, zg, nsa_q_norm[l], nsa_k_norm[l], cmp_pe[l], cmp_w1[l], cmp_w2[l], b, s)
        bias_full = jnp.repeat(sgu_b[l].T, SGU_WIDTH // SGU_GROUPS, axis=1)
        ob = sgu(zs, sgu_ln_g[l], sgu_ln_b[l], sgu_w[l], bias_full)
        cg = POOL_WIDTH // len(POOL_WINDOWS)
        w_bd = jnp.zeros((POOL_WIDTH, POOL_WIDTH), F32)
        for gi in range(len(POOL_WINDOWS)):
            w_bd = w_bd.at[gi * cg:(gi + 1) * cg, gi * cg:(gi + 1) * cg].set(pool_w[l, gi])
        oc = pool(zp.reshape(b, s, POOL_WIDTH), w_bd.astype(BF16), pool_scale[l])
        x = merge(x, oa, ob.reshape(b, s, SGU_WIDTH), oc, zm.reshape(b, s, 3 * d),
                  lift_a[l].astype(BF16), lift_b[l].astype(BF16), lift_c[l].astype(BF16),
                  w_out[l].astype(BF16))

        mk, mv = mem_kv(mem, mem_norm[l], xa_wk[l].astype(BF16), xa_wv[l].astype(BF16), xa_k_norm[l])
        x = xattn(x, xa_norm[l], xa_wq[l].astype(BF16), xa_q_norm[l], mk, mv, xa_wo[l].astype(BF16))

        k1p = jnp.concatenate([peer_keys1[l].astype(BF16), zeros_half], axis=1)
        k2p = jnp.concatenate([zeros_half, peer_keys2[l].astype(BF16)], axis=1)
        x = peer(x.reshape(t, d), ffn_norm[l], peer_wq[l].astype(BF16), k1p, k2p,
                 peer_u[l].astype(BF16), peer_v[l].T.astype(BF16)).reshape(b, s, d)
    return x
```

```python
import functools

import jax
import jax.numpy as jnp
from jax import lax
from jax.experimental import pallas as pl
from jax.experimental.pallas import tpu as pltpu

F32 = jnp.float32
BF16 = jnp.bfloat16

EPS = 1e-6
LOG2E = 1.4426950408889634
NEG = -1e30
REMOVED = -3e38

D_MODEL = 1024
DK = 64
NSA_HEADS = 8
NSA_GROUPS = 2
NSA_REP = NSA_HEADS // NSA_GROUPS
L_CMP, D_CMP = 32, 16
CMP_HIDDEN = 128
L_SLC = 64
N_SEL = 16
WINDOW = 512
QB = 128
FORCE_BONUS = 1.0e3
SGU_WIDTH = 256
SGU_GROUPS = 4
SGU_CHUNK = 128
POOL_WIDTH = 256
POOL_WINDOWS = (2, 4, 8, 16)
XA_HEADS, XA_HD = 4, 128
XA_WIDTH = XA_HEADS * XA_HD
PEER_HEADS = 8
PEER_KEYS = 128
PEER_TOPK = 16
PEER_HALF = 64

LANES = 128
SEL_SUPER_BLOCKS = 128
SEL_SUPER = SEL_SUPER_BLOCKS * L_SLC
SEL_KW = -(-(DK + SEL_SUPER_BLOCKS) // LANES) * LANES
SEL_VR = 2 * DK
SEL_SUB = 512
CMP_CHUNK = 256
CMP_HALO = 8
PEER_TT = 512
PEER_CH = 1024
VMEM_LIMIT = 56 * 1024 * 1024


def _cp(sem):
    return pltpu.CompilerParams(dimension_semantics=sem, vmem_limit_bytes=VMEM_LIMIT)


def _gelu(x):
    return 0.5 * x * (1.0 + jnp.tanh(0.7978845608028654 * (x + 0.044715 * (x * x * x))))


def _rms(x, g):
    return x * lax.rsqrt(jnp.mean(x * x, axis=-1, keepdims=True) + EPS) * g


def _dot(a, b):
    return jnp.dot(a, b, preferred_element_type=F32)


def _dot_nt(a, b):
    return lax.dot_general(a, b, (((1,), (1,)), ((), ())), preferred_element_type=F32)


def _iota(shape, dim):
    return lax.broadcasted_iota(jnp.int32, shape, dim)


def _div_pow2(x, n):
    assert n & (n - 1) == 0
    return lax.shift_right_logical(x, jnp.int32(n.bit_length() - 1))


def _mod_pow2(x, n):
    assert n & (n - 1) == 0
    return x & (n - 1)


def _norm_matmul_kernel(x_ref, g_ref, w_ref, *o_refs, splits):
    hb = _rms(x_ref[...], g_ref[...]).astype(BF16)
    for o_ref, (a, b) in zip(o_refs, splits):
        o_ref[...] = _dot(hb, w_ref[:, a:b]).astype(o_ref.dtype)


def norm_matmul(x, g, w, splits, tm, name):
    t, d = x.shape
    n = w.shape[1]
    return pl.pallas_call(
        functools.partial(_norm_matmul_kernel, splits=splits),
        grid=(t // tm,),
        in_specs=[pl.BlockSpec((tm, d), lambda i: (i, 0)),
                  pl.BlockSpec((1, d), lambda i: (0, 0)),
                  pl.BlockSpec((d, n), lambda i: (0, 0))],
        out_specs=[pl.BlockSpec((tm, b - a), lambda i: (i, 0)) for a, b in splits],
        out_shape=[jax.ShapeDtypeStruct((t, b - a), F32) for a, b in splits],
        compiler_params=_cp(("parallel",)),
        name=name,
    )(x, g.reshape(1, d), w)


def _nsa_prep_kernel(zq_ref, zkv_ref, qn_ref, kn_ref,
                     q_out, kvc_out, ks_out, vs_out, kw_out, vw_out, *, ts):
    st = pl.program_id(1)
    zq = zq_ref[0]
    scale = DK ** -0.5 * LOG2E
    for h in range(NSA_HEADS):
        q_out[0, h] = (_rms(zq[:, h * DK:(h + 1) * DK], qn_ref[...]) * scale).astype(BF16)
    zkv = zkv_ref[0]

    def piece(i, g):
        o = i * NSA_GROUPS * DK + g * DK
        return zkv[:, o:o + DK]

    pos = st * ts + _iota((ts, SEL_SUPER_BLOCKS), 0)
    blk = _mod_pow2(_div_pow2(pos, L_SLC), SEL_SUPER_BLOCKS)
    onehot = jnp.where(_iota((ts, SEL_SUPER_BLOCKS), 1) == blk, 1.0, 0.0).astype(BF16)
    key_pad = jnp.zeros((ts, SEL_KW - DK - SEL_SUPER_BLOCKS), BF16)
    ones_col = jnp.where(_iota((ts, SEL_VR - DK), 1) == 0, 1.0, 0.0)
    for g in range(NSA_GROUPS):
        kvc_out[0, 0, g] = piece(0, g)
        kvc_out[1, 0, g] = piece(1, g)
        ks = _rms(piece(2, g), kn_ref[1:2, :]).astype(BF16)
        ks_out[0, g] = jnp.concatenate([ks, onehot, key_pad], axis=-1)
        vs_out[0, g] = jnp.concatenate([piece(3, g), ones_col], axis=-1).T.astype(BF16)
        kw_out[0, g] = _rms(piece(4, g), kn_ref[2:3, :]).astype(BF16)
        vw_out[0, g] = jnp.concatenate([piece(5, g), ones_col], axis=-1).T.astype(BF16)


def nsa_prep(zq, zkv, q_norm, k_norm, ts=512):
    b, s, _ = zq.shape
    g = NSA_GROUPS
    hm = lambda bi, si: (bi, 0, si, 0)
    return pl.pallas_call(
        functools.partial(_nsa_prep_kernel, ts=ts),
        grid=(b, s // ts),
        in_specs=[pl.BlockSpec((1, ts, NSA_HEADS * DK), lambda bi, si: (bi, si, 0)),
                  pl.BlockSpec((1, ts, 6 * g * DK), lambda bi, si: (bi, si, 0)),
                  pl.BlockSpec((1, DK), lambda bi, si: (0, 0)),
                  pl.BlockSpec((3, DK), lambda bi, si: (0, 0))],
        out_specs=[pl.BlockSpec((1, NSA_HEADS, ts, DK), hm),
                   pl.BlockSpec((2, 1, g, ts, DK), lambda bi, si: (0, bi, 0, si, 0)),
                   pl.BlockSpec((1, g, ts, SEL_KW), hm),
                   pl.BlockSpec((1, g, SEL_VR, ts), lambda bi, si: (bi, 0, 0, si)),
                   pl.BlockSpec((1, g, ts, DK), hm),
                   pl.BlockSpec((1, g, SEL_VR, ts), lambda bi, si: (bi, 0, 0, si))],
        out_shape=[jax.ShapeDtypeStruct((b, NSA_HEADS, s, DK), BF16),
                   jax.ShapeDtypeStruct((2, b, g, s, DK), F32),
                   jax.ShapeDtypeStruct((b, g, s, SEL_KW), BF16),
                   jax.ShapeDtypeStruct((b, g, SEL_VR, s), BF16),
                   jax.ShapeDtypeStruct((b, g, s, DK), BF16),
                   jax.ShapeDtypeStruct((b, g, SEL_VR, s), BF16)],
        compiler_params=_cp(("parallel", "parallel")),
        name="nsa_prep",
    )(zq, zkv, q_norm.reshape(1, DK), k_norm)


def _cmp_mlp_kernel(x_ref, pe_ref, w1_ref, w2_ref, kn_ref, k_out, vt_out, *, nc):
    c = pl.program_id(2)
    x = x_ref[0, 0, 0]
    a = _dot((x + pe_ref[0, 0:1, :]).astype(BF16), w1_ref[0, 0])
    b = _dot((x + pe_ref[0, 1:2, :]).astype(BF16), w1_ref[0, 1])
    pre = a + pltpu.roll(b, nc - 1, 0)
    comp = _dot(_gelu(pre).astype(BF16), w2_ref[0])

    @pl.when(c == 0)
    def _():
        k_out[0, 0] = _rms(comp, kn_ref[...]).astype(BF16)

    @pl.when(c == 1)
    def _():
        ones_col = jnp.where(_iota((nc, SEL_VR - DK), 1) == 0, 1.0, 0.0)
        vt_out[0, 0] = jnp.concatenate([comp, ones_col], axis=-1).T.astype(BF16)


def cmp_mlp(xc, pe, w1, w2, k_norm0):
    _, b, g, nc, width = xc.shape
    return pl.pallas_call(
        functools.partial(_cmp_mlp_kernel, nc=nc),
        grid=(b, g, 2),
        in_specs=[pl.BlockSpec((1, 1, 1, nc, width), lambda bi, gi, c: (c, bi, gi, 0, 0)),
                  pl.BlockSpec((1, 2, width), lambda bi, gi, c: (c, 0, 0)),
                  pl.BlockSpec((1, 2, width, CMP_HIDDEN), lambda bi, gi, c: (c, 0, 0, 0)),
                  pl.BlockSpec((1, CMP_HIDDEN, DK), lambda bi, gi, c: (c, 0, 0)),
                  pl.BlockSpec((1, DK), lambda bi, gi, c: (0, 0))],
        out_specs=[pl.BlockSpec((1, 1, nc, DK), lambda bi, gi, c: (bi, gi, 0, 0)),
                   pl.BlockSpec((1, 1, SEL_VR, nc), lambda bi, gi, c: (bi, gi, 0, 0))],
        out_shape=[jax.ShapeDtypeStruct((b, g, nc, DK), BF16),
                   jax.ShapeDtypeStruct((b, g, SEL_VR, nc), BF16)],
        compiler_params=_cp(("parallel", "parallel", "arbitrary")),
        name="cmp_mlp",
    )(xc, pe, w1, w2, k_norm0.reshape(1, DK))


def _cmp_attn_kernel(q_ref, k_ref, vt_ref, oc_ref, sb_ref, s_s, acc_s, ps_s, *, nc, n_slc):
    qb = pl.program_id(2)
    rows = NSA_REP * QB
    ck = CMP_CHUNK
    q = q_ref[0].reshape(rows, DK)
    n_chunk = ((qb * QB + QB) // D_CMP + ck - 1) // ck
    t = qb * QB + _mod_pow2(_iota((ck, rows), 1), QB)

    def pass_a(ci, m):
        off = pl.multiple_of(ci * ck, ck)
        st = _dot_nt(k_ref[0, 0, pl.ds(off, ck), :], q)
        n = off + _iota((ck, rows), 0)
        st = jnp.where(n * D_CMP + (L_CMP - 1) <= t, st, NEG)
        s_s[ci] = st
        return jnp.maximum(m, jnp.max(st, axis=0, keepdims=True))

    m = lax.fori_loop(0, n_chunk, pass_a, jnp.full((1, rows), NEG, F32))
    acc_s[...] = jnp.zeros_like(acc_s)
    ps_s[...] = jnp.zeros_like(ps_s)

    def pass_b(ci, den):
        off = pl.multiple_of(ci * ck, ck)
        e = jnp.exp2(s_s[ci] - m)
        s_s[ci] = e
        acc_s[...] += _dot(vt_ref[0, 0, :, pl.ds(off, ck)], e.astype(BF16))
        return den + jnp.sum(e, axis=0, keepdims=True)

    den = lax.fori_loop(0, n_chunk, pass_b, jnp.zeros((1, rows), F32))
    seen = m > 0.5 * NEG
    inv_o = jnp.where(seen, 1.0 / jnp.maximum(acc_s[DK:DK + 1, :], 1e-30), 0.0)
    inv_p = jnp.where(seen, 1.0 / jnp.maximum(den, 1e-30), 0.0)
    oc_ref[0] = (acc_s[...] * inv_o).T[:, 0:DK].reshape(NSA_REP, QB, DK)

    def pass_c(ci, carry):
        off = pl.multiple_of(ci * ck, ck)
        p = s_s[ci] * inv_p
        ps = p[:, 0:QB]
        for r in range(1, NSA_REP):
            ps = ps + p[:, r * QB:(r + 1) * QB]
        ps_s[pl.ds(CMP_HALO + off, ck), :] = ps
        return carry

    lax.fori_loop(0, n_chunk, pass_c, 0)

    ratio = L_SLC // D_CMP
    imp = jnp.zeros((n_slc, QB), F32)
    for k in range(-(L_CMP // D_CMP - 1), ratio):
        overlap = min(k * D_CMP + L_CMP, L_SLC) - max(k * D_CMP, 0)
        imp = imp + (overlap / D_CMP) * ps_s[pl.ds(CMP_HALO + k, n_slc, stride=ratio), :]

    cur = _div_pow2(qb * QB + _iota((n_slc, QB), 1), L_SLC)
    j = _iota((n_slc, QB), 0)
    forced = (j == 0) | (j == cur) | (j == cur - 1)
    score = jnp.where(j <= cur, imp + jnp.where(forced, FORCE_BONUS, 0.0), NEG)
    work = score
    tau = None
    for _ in range(N_SEL):
        tau = jnp.max(work, axis=0, keepdims=True)
        work = jnp.where(work >= tau, REMOVED, work)
    selected = (score > 0.5 * NEG) & (score >= tau)
    bias = jnp.where(selected, 0.0, NEG).T.astype(BF16)
    for si in range(n_slc // SEL_SUPER_BLOCKS):
        sb_ref[0, 0, 0, si] = bias[:, si * SEL_SUPER_BLOCKS:(si + 1) * SEL_SUPER_BLOCKS]


def cmp_attn_select(qn, kc, vct):
    b, h, s, _ = qn.shape
    g = NSA_GROUPS
    nc = kc.shape[2]
    n_slc = s // L_SLC
    nsup = n_slc // SEL_SUPER_BLOCKS
    nqb = s // QB
    return pl.pallas_call(
        functools.partial(_cmp_attn_kernel, nc=nc, n_slc=n_slc),
        grid=(b, g, nqb),
        in_specs=[pl.BlockSpec((1, NSA_REP, QB, DK), lambda bi, gi, qi: (bi, gi, qi, 0)),
                  pl.BlockSpec((1, 1, nc, DK), lambda bi, gi, qi: (bi, gi, 0, 0)),
                  pl.BlockSpec((1, 1, SEL_VR, nc), lambda bi, gi, qi: (bi, gi, 0, 0))],
        out_specs=[pl.BlockSpec((1, NSA_REP, QB, DK), lambda bi, gi, qi: (bi, gi, qi, 0)),
                   pl.BlockSpec((1, 1, 1, nsup, QB, SEL_SUPER_BLOCKS),
                                lambda bi, gi, qi: (bi, gi, qi, 0, 0, 0))],
        out_shape=[jax.ShapeDtypeStruct((b, h, s, DK), F32),
                   jax.ShapeDtypeStruct((b, g, nqb, nsup, QB, SEL_SUPER_BLOCKS), BF16)],
        scratch_shapes=[pltpu.VMEM((nc // CMP_CHUNK, CMP_CHUNK, NSA_REP * QB), F32),
                        pltpu.VMEM((SEL_VR, NSA_REP * QB), F32),
                        pltpu.VMEM((CMP_HALO + nc, QB), F32)],
        compiler_params=_cp(("parallel", "parallel", "parallel")),
        name="cmp_attn_select",
    )(qn, kc, vct)


def _win_attn_kernel(q_ref, *refs):
    nkb = WINDOW // QB + 1
    k_refs, v_refs, o_ref = refs[:nkb], refs[nkb:2 * nkb], refs[2 * nkb]
    qb = pl.program_id(2)
    rows = NSA_REP * QB
    nk = nkb * QB
    q = q_ref[0].reshape(rows, DK)
    k = jnp.concatenate([r[0, 0] for r in k_refs], axis=0)
    vt = jnp.concatenate([r[0, 0] for r in v_refs], axis=-1)
    st = _dot_nt(k, q)
    t = qb * QB + _mod_pow2(_iota((nk, rows), 1), QB)
    kpos = (qb - (nkb - 1)) * QB + _iota((nk, rows), 0)
    delta = t - kpos
    valid = (delta >= 0) & (delta < WINDOW) & (kpos >= 0)
    st = jnp.where(valid, st, NEG)
    e = jnp.exp2(st - jnp.max(st, axis=0, keepdims=True)).astype(BF16)
    acc = _dot(vt, e)
    o = (acc / jnp.maximum(acc[DK:DK + 1, :], 1e-30)).T[:, 0:DK]
    o_ref[0] = o.reshape(NSA_REP, QB, DK)


def win_attn(qn, kw, vw):
    b, h, s, _ = qn.shape
    g = NSA_GROUPS
    nkb = WINDOW // QB + 1
    nqb = s // QB

    def k_spec(i):
        return pl.BlockSpec((1, 1, QB, DK),
                            lambda bi, gi, qi: (bi, gi, jnp.maximum(qi - (nkb - 1) + i, 0), 0))

    def vt_spec(i):
        return pl.BlockSpec((1, 1, SEL_VR, QB),
                            lambda bi, gi, qi: (bi, gi, 0, jnp.maximum(qi - (nkb - 1) + i, 0)))

    return pl.pallas_call(
        _win_attn_kernel,
        grid=(b, g, nqb),
        in_specs=[pl.BlockSpec((1, NSA_REP, QB, DK), lambda bi, gi, qi: (bi, gi, qi, 0))]
        + [k_spec(i) for i in range(nkb)] + [vt_spec(i) for i in range(nkb)],
        out_specs=pl.BlockSpec((1, NSA_REP, QB, DK), lambda bi, gi, qi: (bi, gi, qi, 0)),
        out_shape=jax.ShapeDtypeStruct((b, h, s, DK), F32),
        compiler_params=_cp(("parallel", "parallel", "parallel")),
        name="win_attn",
    )(qn, *([kw] * nkb), *([vw] * nkb))


def _sel_attn_kernel(q_ref, sb_ref, k_ref, vt_ref, oc_ref, ow_ref, gl_ref, o_ref,
                     qp_s, m_s, acc_s, s0_s, s1_s, p0_s, p1_s, a0_s, a1_s):
    qb = pl.program_id(2)
    si = pl.program_id(3)
    rows = NSA_REP * QB
    last = (qb * QB) // SEL_SUPER

    @pl.when(si == 0)
    def _():
        qp_s[...] = jnp.zeros_like(qp_s)
        qp_s[:, 0:DK] = q_ref[0].reshape(rows, DK)
        m_s[...] = jnp.full_like(m_s, NEG)
        acc_s[...] = jnp.zeros_like(acc_s)

    @pl.when(si <= last)
    def _():
        sb = sb_ref[0, 0, 0, 0]
        for r in range(NSA_REP):
            qp_s[r * QB:(r + 1) * QB, DK:DK + SEL_SUPER_BLOCKS] = sb
        qv = qp_s[...]
        nsub = jnp.minimum(SEL_SUPER // SEL_SUB,
                           (qb * QB + QB - si * SEL_SUPER + SEL_SUB - 1) // SEL_SUB)

        def scores(i, s_ref):
            off = pl.multiple_of(i * SEL_SUB, SEL_SUB)
            s_ref[...] = _dot_nt(k_ref[0, 0, pl.ds(off, SEL_SUB), :], qv)

        def weights(i, s_ref, p_ref, a_ref, causal):
            st = s_ref[...]
            if causal:
                kpos = si * SEL_SUPER + i * SEL_SUB + _iota((SEL_SUB, rows), 0)
                t = qb * QB + _mod_pow2(_iota((SEL_SUB, rows), 1), QB)
                st = jnp.where(kpos <= t, st, NEG)
            m_old = m_s[...]
            m_new = jnp.maximum(m_old, jnp.max(st, axis=0, keepdims=True))
            p_ref[...] = jnp.exp2(st - m_new).astype(BF16)
            a_ref[...] = jnp.exp2(m_old - m_new)
            m_s[...] = m_new

        def accumulate(i, p_ref, a_ref):
            off = pl.multiple_of(i * SEL_SUB, SEL_SUB)
            acc_s[...] = a_ref[...] * acc_s[...] + _dot(vt_ref[0, 0, :, pl.ds(off, SEL_SUB)], p_ref[...])

        p1_s[...] = jnp.zeros_like(p1_s)
        a1_s[...] = jnp.ones_like(a1_s)
        scores(0, s0_s)
        n_pairs = (nsub - 1) // 2
        odd_tail = (nsub - 1) - 2 * n_pairs

        def pair(j, carry):
            scores(2 * j + 1, s1_s)
            weights(2 * j, s0_s, p0_s, a0_s, False)
            accumulate(jnp.maximum(2 * j - 1, 0), p1_s, a1_s)
            scores(2 * j + 2, s0_s)
            weights(2 * j + 1, s1_s, p1_s, a1_s, False)
            accumulate(2 * j, p0_s, a0_s)
            return carry

        lax.fori_loop(0, n_pairs, pair, 0)
        e = 2 * n_pairs

        @pl.when(odd_tail == 0)
        def _():
            weights(e, s0_s, p0_s, a0_s, True)
            accumulate(jnp.maximum(e - 1, 0), p1_s, a1_s)
            accumulate(e, p0_s, a0_s)

        @pl.when(odd_tail == 1)
        def _():
            scores(e + 1, s1_s)
            weights(e, s0_s, p0_s, a0_s, False)
            accumulate(jnp.maximum(e - 1, 0), p1_s, a1_s)
            weights(e + 1, s1_s, p1_s, a1_s, True)
            accumulate(e, p0_s, a0_s)
            accumulate(e + 1, p1_s, a1_s)

    @pl.when(si == last)
    def _():
        o_s = (acc_s[...] / jnp.maximum(acc_s[DK:DK + 1, :], 1e-30)).T[:, 0:DK]
        gate = jax.nn.sigmoid(gl_ref[0].reshape(rows, 3))
        out = (gate[:, 0:1] * oc_ref[0].reshape(rows, DK) + gate[:, 1:2] * o_s
               + gate[:, 2:3] * ow_ref[0].reshape(rows, DK))
        o_ref[0] = out.reshape(NSA_REP, QB, DK)


def sel_attn(qn, selb, ks, vs, oc, ow, gl):
    b, h, s, _ = qn.shape
    g = NSA_GROUPS
    nqb = s // QB
    nsup = s // SEL_SUPER
    qmap = lambda bi, gi, qi, si: (bi, gi, qi, 0)

    def eff(qi, si):
        return jnp.minimum(si, (qi * QB) // SEL_SUPER)

    return pl.pallas_call(
        _sel_attn_kernel,
        grid=(b, g, nqb, nsup),
        in_specs=[pl.BlockSpec((1, NSA_REP, QB, DK), qmap),
                  pl.BlockSpec((1, 1, 1, 1, QB, SEL_SUPER_BLOCKS),
                               lambda bi, gi, qi, si: (bi, gi, qi, eff(qi, si), 0, 0)),
                  pl.BlockSpec((1, 1, SEL_SUPER, SEL_KW), lambda bi, gi, qi, si: (bi, gi, eff(qi, si), 0)),
                  pl.BlockSpec((1, 1, SEL_VR, SEL_SUPER), lambda bi, gi, qi, si: (bi, gi, 0, eff(qi, si))),
                  pl.BlockSpec((1, NSA_REP, QB, DK), qmap),
                  pl.BlockSpec((1, NSA_REP, QB, DK), qmap),
                  pl.BlockSpec((1, NSA_REP, QB, 3), qmap)],
        out_specs=pl.BlockSpec((1, NSA_REP, QB, DK), qmap),
        out_shape=jax.ShapeDtypeStruct((b, h, s, DK), F32),
        scratch_shapes=[pltpu.VMEM((NSA_REP * QB, SEL_KW), BF16),
                        pltpu.VMEM((1, NSA_REP * QB), F32),
                        pltpu.VMEM((SEL_VR, NSA_REP * QB), F32),
                        pltpu.VMEM((SEL_SUB, NSA_REP * QB), F32),
                        pltpu.VMEM((SEL_SUB, NSA_REP * QB), F32),
                        pltpu.VMEM((SEL_SUB, NSA_REP * QB), BF16),
                        pltpu.VMEM((SEL_SUB, NSA_REP * QB), BF16),
                        pltpu.VMEM((1, NSA_REP * QB), F32),
                        pltpu.VMEM((1, NSA_REP * QB), F32)],
        compiler_params=_cp(("parallel", "parallel", "parallel", "arbitrary")),
        name="sel_attn",
    )(qn, selb, ks, vs, oc, ow, gl)


def _sgu_kernel(z_ref, g_ref, b_ref, w_ref, bias_ref, o_ref, *, tm):
    z = _gelu(z_ref[...])
    u = z[:, :SGU_WIDTH]
    v = z[:, SGU_WIDTH:]
    mu = jnp.mean(v, axis=-1, keepdims=True)
    var = jnp.mean(jnp.square(v - mu), axis=-1, keepdims=True)
    vn = ((v - mu) * lax.rsqrt(var + EPS) * g_ref[...] + b_ref[...]).astype(BF16)
    tril = _iota((SGU_CHUNK, SGU_CHUNK), 1) <= _iota((SGU_CHUNK, SGU_CHUNK), 0)
    wm = [jnp.where(tril, w_ref[gi], 0.0).astype(BF16) for gi in range(SGU_GROUPS)]
    grp = _div_pow2(_iota((SGU_CHUNK, SGU_WIDTH), 1), SGU_WIDTH // SGU_GROUPS)
    for ch in range(tm // SGU_CHUNK):
        rs = slice(ch * SGU_CHUNK, (ch + 1) * SGU_CHUNK)
        mixed = bias_ref[...]
        for gi in range(SGU_GROUPS):
            mixed = mixed + jnp.where(grp == gi, _dot(wm[gi], vn[rs]), 0.0)
        o_ref[rs, :] = u[rs] * mixed


def sgu(z, ln_g, ln_b, w, bias_full, tm=512):
    t = z.shape[0]
    return pl.pallas_call(
        functools.partial(_sgu_kernel, tm=tm),
        grid=(t // tm,),
        in_specs=[pl.BlockSpec((tm, 2 * SGU_WIDTH), lambda i: (i, 0)),
                  pl.BlockSpec((1, SGU_WIDTH), lambda i: (0, 0)),
                  pl.BlockSpec((1, SGU_WIDTH), lambda i: (0, 0)),
                  pl.BlockSpec((SGU_GROUPS, SGU_CHUNK, SGU_CHUNK), lambda i: (0, 0, 0)),
                  pl.BlockSpec((SGU_CHUNK, SGU_WIDTH), lambda i: (0, 0))],
        out_specs=pl.BlockSpec((tm, SGU_WIDTH), lambda i: (i, 0)),
        out_shape=jax.ShapeDtypeStruct((t, SGU_WIDTH), F32),
        compiler_params=_cp(("parallel",)),
        name="sgu",
    )(z, ln_g.reshape(1, -1), ln_b.reshape(1, -1), w, bias_full)


def _pool_kernel(p_ref, h_ref, w_ref, sc_ref, o_ref, *, tp):
    i = pl.program_id(1)
    halo_rows = POOL_WINDOWS[-1]
    p = p_ref[0]
    halo = jnp.where(i > 0, h_ref[0], 0.0)
    ext = jnp.concatenate([halo, p], axis=0)
    sums = [ext]
    shift = 1
    for _ in POOL_WINDOWS:
        prev = sums[-1]
        sums.append(prev + pltpu.roll(prev, shift, 0))
        shift *= 2
    pos = i * tp + _iota((tp, POOL_WIDTH), 0)
    grp = _div_pow2(_iota((tp, POOL_WIDTH), 1), POOL_WIDTH // len(POOL_WINDOWS))
    d = jnp.zeros((tp, POOL_WIDTH), F32)
    for gi, w in enumerate(POOL_WINDOWS):
        mean = sums[gi + 1][halo_rows:halo_rows + tp] / jnp.minimum(pos + 1, w).astype(F32)
        d = jnp.where(grp == gi, mean, d)
    d = d - p
    o_ref[0] = _dot(d.astype(BF16), w_ref[...]) * sc_ref[...]


def pool(p, w_bd, scale, tp=512):
    b, s, c = p.shape
    halo_rows = POOL_WINDOWS[-1]
    return pl.pallas_call(
        functools.partial(_pool_kernel, tp=tp),
        grid=(b, s // tp),
        in_specs=[pl.BlockSpec((1, tp, c), lambda bi, i: (bi, i, 0)),
                  pl.BlockSpec((1, halo_rows, c),
                               lambda bi, i: (bi, jnp.maximum(i * (tp // halo_rows) - 1, 0), 0)),
                  pl.BlockSpec((c, c), lambda bi, i: (0, 0)),
                  pl.BlockSpec((1, c), lambda bi, i: (0, 0))],
        out_specs=pl.BlockSpec((1, tp, c), lambda bi, i: (bi, i, 0)),
        out_shape=jax.ShapeDtypeStruct((b, s, c), F32),
        compiler_params=_cp(("parallel", "parallel")),
        name="pool",
    )(p, p, w_bd, scale.reshape(1, c))


def _merge_kernel(x_ref, oa_ref, ob_ref, oc_ref, zm_ref, la_ref, lb_ref, lc_ref, wo_ref, o_ref):
    d = D_MODEL
    oa = jnp.concatenate([oa_ref[0, h].astype(BF16) for h in range(NSA_HEADS)], axis=-1)
    zm = zm_ref[0]
    merged = (jax.nn.sigmoid(zm[:, 0:d]) * _dot(oa, la_ref[...])
              + jax.nn.sigmoid(zm[:, d:2 * d]) * _dot(ob_ref[0].astype(BF16), lb_ref[...])
              + jax.nn.sigmoid(zm[:, 2 * d:3 * d]) * _dot(oc_ref[0].astype(BF16), lc_ref[...]))
    o_ref[0] = x_ref[0] + _dot(merged.astype(BF16), wo_ref[...])


def merge(x, oa, ob, oc, zm, la, lb, lc, wo, tm=512):
    b, s, d = x.shape
    row = lambda bi, i: (bi, i, 0)
    full = lambda bi, i: (0, 0)
    return pl.pallas_call(
        _merge_kernel,
        grid=(b, s // tm),
        in_specs=[pl.BlockSpec((1, tm, d), row),
                  pl.BlockSpec((1, NSA_HEADS, tm, DK), lambda bi, i: (bi, 0, i, 0)),
                  pl.BlockSpec((1, tm, SGU_WIDTH), row),
                  pl.BlockSpec((1, tm, POOL_WIDTH), row),
                  pl.BlockSpec((1, tm, 3 * d), row),
                  pl.BlockSpec(la.shape, full), pl.BlockSpec(lb.shape, full),
                  pl.BlockSpec(lc.shape, full), pl.BlockSpec(wo.shape, full)],
        out_specs=pl.BlockSpec((1, tm, d), row),
        out_shape=jax.ShapeDtypeStruct((b, s, d), F32),
        compiler_params=_cp(("parallel", "parallel")),
        name="merge",
    )(x, oa, ob, oc, zm, la, lb, lc, wo)


def _mem_kv_kernel(m_ref, g_ref, wk_ref, wv_ref, kn_ref, k_out, v_out):
    mh = _rms(m_ref[0], g_ref[...]).astype(BF16)
    k = _dot(mh, wk_ref[...])
    for h in range(XA_HEADS):
        hs = slice(h * XA_HD, (h + 1) * XA_HD)
        k_out[0, :, hs] = _rms(k[:, hs], kn_ref[...]).astype(BF16)
    v_out[0] = _dot(mh, wv_ref[...]).astype(BF16)


def mem_kv(mem, g, wk, wv, k_norm):
    b, m, d = mem.shape
    full = lambda bi: (0, 0)
    return pl.pallas_call(
        _mem_kv_kernel,
        grid=(b,),
        in_specs=[pl.BlockSpec((1, m, d), lambda bi: (bi, 0, 0)),
                  pl.BlockSpec((1, d), full),
                  pl.BlockSpec(wk.shape, full), pl.BlockSpec(wv.shape, full),
                  pl.BlockSpec((1, XA_HD), full)],
        out_specs=[pl.BlockSpec((1, m, XA_WIDTH), lambda bi: (bi, 0, 0))] * 2,
        out_shape=[jax.ShapeDtypeStruct((b, m, XA_WIDTH), BF16)] * 2,
        compiler_params=_cp(("parallel",)),
        name="mem_kv",
    )(mem, g.reshape(1, d), wk, wv, k_norm.reshape(1, XA_HD))


def _xattn_kernel(x_ref, g_ref, wq_ref, qn_ref, k_ref, v_ref, wo_ref, o_ref):
    x = x_ref[0]
    q = _dot(_rms(x, g_ref[...]).astype(BF16), wq_ref[...])
    k = k_ref[0]
    v = v_ref[0]
    outs = []
    for h in range(XA_HEADS):
        hs = slice(h * XA_HD, (h + 1) * XA_HD)
        qh = (_rms(q[:, hs], qn_ref[...])).astype(BF16)
        s = _dot_nt(qh, k[:, hs]) * (XA_HD ** -0.5)
        e = jnp.exp(s - jnp.max(s, axis=-1, keepdims=True))
        p = e / jnp.sum(e, axis=-1, keepdims=True)
        outs.append(_dot(p.astype(BF16), v[:, hs]).astype(BF16))
    o = jnp.concatenate(outs, axis=-1)
    o_ref[0] = x + _dot(o, wo_ref[...])


def xattn(x, g, wq, q_norm, k, v, wo, tm=512):
    b, s, d = x.shape
    m = k.shape[1]
    row = lambda bi, i: (bi, i, 0)
    full = lambda bi, i: (0, 0)
    return pl.pallas_call(
        _xattn_kernel,
        grid=(b, s // tm),
        in_specs=[pl.BlockSpec((1, tm, d), row),
                  pl.BlockSpec((1, d), full),
                  pl.BlockSpec(wq.shape, full),
                  pl.BlockSpec((1, XA_HD), full),
                  pl.BlockSpec((1, m, XA_WIDTH), lambda bi, i: (bi, 0, 0)),
                  pl.BlockSpec((1, m, XA_WIDTH), lambda bi, i: (bi, 0, 0)),
                  pl.BlockSpec(wo.shape, full)],
        out_specs=pl.BlockSpec((1, tm, d), row),
        out_shape=jax.ShapeDtypeStruct((b, s, d), F32),
        compiler_params=_cp(("parallel", "parallel")),
        name="xattn",
    )(x, g.reshape(1, d), wq, q_norm.reshape(1, XA_HD), k, v, wo)


def _top_rows(s, n, want_rank):
    out = []
    work = s
    rank = jnp.full(s.shape, float(s.shape[0] - 1), F32) if want_rank else None
    for i in range(n):
        m = jnp.max(work, axis=0, keepdims=True)
        out.append(m)
        hit = work >= m
        if want_rank:
            rank = jnp.where(hit, float(i), rank)
        work = jnp.where(hit, REMOVED, work)
    return out, rank


def _stack_rows(rows, pad_rows):
    tt = rows[0].shape[1]
    rowi = _iota((pad_rows, tt), 0)
    out = jnp.full((pad_rows, tt), REMOVED, F32)
    for i, r in enumerate(rows[:pad_rows]):
        out = jnp.where(rowi == i, r, out)
    return out


def _pair_sum_candidates(v1, v2, tt):
    row_full = _iota((24, tt), 0)
    row = _iota((8, tt), 0)
    v2_full = _stack_rows(v2, 24)
    v2_8 = _stack_rows(v2, 8)

    def shifted(k):
        return pltpu.roll(v2_8, k, 0)

    def pick(rows_from, first):
        out = v1[rows_from]
        for r in range(first + 1, 8):
            if rows_from + r - first < len(v1):
                out = jnp.where(row == r, v1[rows_from + r - first], out)
        return out

    return jnp.concatenate([
        jnp.where(row_full < 17, v1[0] + v2_full, REMOVED),
        v1[1] + v2_8,
        jnp.where(row < 5, v1[2] + v2_8,
                  jnp.where(row < 7, v1[5] + shifted(5), v1[8] + shifted(7))),
        jnp.where(row < 4, v1[3] + v2_8,
                  jnp.where(row < 7, v1[4] + shifted(4), v1[9] + shifted(7))),
        jnp.where(row < 2, v1[6] + v2_8,
                  jnp.where(row < 4, v1[7] + shifted(2), pick(10, 4) + v2[0])),
        jnp.where(row < 3, pick(14, 0) + v2[0], REMOVED),
    ], axis=0)


def _peer_kernel(x_ref, g_ref, wq_ref, k1_ref, k2_ref, u_ref, vt_ref, o_ref,
                 hb_s, q_s, r2_s, na_s, e1_s, e2_s, acc_s, *, tt, ch):
    c = pl.program_id(1)
    ntop = PEER_TOPK + 1
    assert ntop == 17

    @pl.when(c == 0)
    def _():
        hb = _rms(x_ref[...], g_ref[...]).astype(BF16)
        hb_s[...] = hb
        q = _dot(hb, wq_ref[...])
        for hd in range(PEER_HEADS):
            q_s[hd] = q[:, hd * 2 * PEER_HALF:(hd + 1) * 2 * PEER_HALF].astype(BF16)
        acc_s[...] = jnp.zeros_like(acc_s)

        def route(hd, carry):
            qh = q_s[hd]
            s1 = _dot_nt(k1_ref[...], qh)
            s2 = _dot_nt(k2_ref[...], qh)
            v1, _ = _top_rows(s1, ntop, False)
            v2, rank2 = _top_rows(s2, ntop, True)
            cand = _pair_sum_candidates(v1, v2, tt)
            top = v1[0] + v2[0]
            z = jnp.zeros((1, tt), F32)
            c_prev = top
            c_last = top
            work = cand
            for r in range(ntop):
                m = jnp.max(work, axis=0, keepdims=True)
                work = jnp.where(work >= m, REMOVED, work)
                if r < PEER_TOPK:
                    z = z + jnp.exp(m - top)
                c_prev, c_last = c_last, m
            thr = 0.5 * (c_prev + c_last)
            na = jnp.zeros((PEER_KEYS, tt), F32)
            for j in range(ntop):
                na = na + jnp.where(s1 + v2[j] >= thr, 1.0, 0.0)
            r2_s[hd] = rank2.astype(BF16)
            na_s[hd] = na
            e1_s[hd] = jnp.exp(s1 - v1[0])
            e2_s[hd] = (jnp.exp(s2 - v2[0]) / z).astype(BF16)
            return carry

        lax.fori_loop(0, PEER_HEADS, route, 0)

    gl = _gelu(_dot_nt(u_ref[...], hb_s[...]).astype(BF16))
    ys = []
    for al in range(ch // PEER_KEYS):
        a = c * (ch // PEER_KEYS) + al
        wc = jnp.zeros((PEER_KEYS, tt), BF16)
        for hd in range(PEER_HEADS):
            na = na_s[hd, pl.ds(a, 1), :].astype(BF16)
            e1 = e1_s[hd, pl.ds(a, 1), :].astype(BF16)
            wc = wc + jnp.where(r2_s[hd] < na, e2_s[hd] * e1, jnp.zeros((), BF16))
        ys.append(wc * gl[al * PEER_KEYS:(al + 1) * PEER_KEYS])
    y = jnp.concatenate(ys, axis=0)
    acc_s[...] += _dot(vt_ref[...], y)

    @pl.when(c == pl.num_programs(1) - 1)
    def _():
        o_ref[...] = x_ref[...] + acc_s[...].T


def peer(x, g, wq, k1p, k2p, u, vt, tt=PEER_TT, ch=PEER_CH):
    t, d = x.shape
    ne = u.shape[0]
    return pl.pallas_call(
        functools.partial(_peer_kernel, tt=tt, ch=ch),
        grid=(t // tt, ne // ch),
        in_specs=[pl.BlockSpec((tt, d), lambda i, c: (i, 0)),
                  pl.BlockSpec((1, d), lambda i, c: (0, 0)),
                  pl.BlockSpec(wq.shape, lambda i, c: (0, 0)),
                  pl.BlockSpec(k1p.shape, lambda i, c: (0, 0)),
                  pl.BlockSpec(k2p.shape, lambda i, c: (0, 0)),
                  pl.BlockSpec((ch, d), lambda i, c: (c, 0)),
                  pl.BlockSpec((d, ch), lambda i, c: (0, c))],
        out_specs=pl.BlockSpec((tt, d), lambda i, c: (i, 0)),
        out_shape=jax.ShapeDtypeStruct((t, d), F32),
        scratch_shapes=[pltpu.VMEM((tt, d), BF16),
                        pltpu.VMEM((PEER_HEADS, tt, 2 * PEER_HALF), BF16),
                        pltpu.VMEM((PEER_HEADS, PEER_KEYS, tt), BF16),
                        pltpu.VMEM((PEER_HEADS, PEER_KEYS, tt), F32),
                        pltpu.VMEM((PEER_HEADS, PEER_KEYS, tt), F32),
                        pltpu.VMEM((PEER_HEADS, PEER_KEYS, tt), BF16),
                        pltpu.VMEM((d, tt), F32)],
        compiler_params=_cp(("parallel", "arbitrary")),
        name="peer",
    )(x, g.reshape(1, d), wq, k1p, k2p, u, vt)


_IN_WIDTHS = (NSA_HEADS * DK, 6 * NSA_GROUPS * DK, 3 * NSA_HEADS, 2 * SGU_WIDTH, POOL_WIDTH, 3 * D_MODEL)


def _in_proj_layout():
    splits, src = [], []
    o_src = o_dst = 0
    for w in _IN_WIDTHS:
        wp = -(-w // LANES) * LANES
        splits.append((o_dst, o_dst + wp))
        src.append((o_src, o_src + w))
        o_src += w
        o_dst += wp
    return splits, src, o_dst


def nsa_layer(zq, zkv, zg, q_norm, k_norm, cmp_pe, cmp_w1, cmp_w2, b, s):
    qn, kvc, ks, vs, kw, vw = nsa_prep(zq.reshape(b, s, -1), zkv.reshape(b, s, -1), q_norm, k_norm)
    nc = s // D_CMP
    xc = kvc.reshape(2, b, NSA_GROUPS, nc, D_CMP * DK)
    pe = cmp_pe.reshape(2, 2, D_CMP * DK)
    w1 = cmp_w1.reshape(2, 2, D_CMP * DK, CMP_HIDDEN).astype(BF16)
    kc, vct = cmp_mlp(xc, pe, w1, cmp_w2.astype(BF16), k_norm[0])
    oc, selb = cmp_attn_select(qn, kc, vct)
    ow = win_attn(qn, kw, vw)
    gl = zg[:, :3 * NSA_HEADS].reshape(b, s, NSA_HEADS, 3).transpose(0, 2, 1, 3)
    return sel_attn(qn, selb, ks, vs, oc, ow, gl)


def kernel(x, mem, mix_norm, w_in, nsa_q_norm, nsa_k_norm, cmp_pe, cmp_w1, cmp_w2, sgu_ln_g, sgu_ln_b, sgu_w, sgu_b, pool_w, pool_scale, lift_a, lift_b, lift_c, w_out, xa_norm, mem_norm, xa_wq, xa_wk, xa_wv, xa_q_norm, xa_k_norm, xa_wo, ffn_norm, peer_wq, peer_keys1, peer_keys2, peer_u, peer_v):
    b, s, d = x.shape
    t = b * s
    depth = w_in.shape[0]
    splits, src, n_pad = _in_proj_layout()
    zeros_half = jnp.zeros((PEER_KEYS, PEER_HALF), BF16)
    for l in range(depth):
        w_parts = []
        for (a0, a1), (d0, d1) in zip(src, splits):
            w_parts.append(jnp.pad(w_in[l][:, a0:a1], ((0, 0), (0, (d1 - d0) - (a1 - a0)))))
        w_pad = jnp.concatenate(w_parts, axis=1).astype(BF16)
        zq, zkv, zg, zs, zp, zm = norm_matmul(x.reshape(t, d), mix_norm[l], w_pad, splits, 256, "in_proj")

        oa = nsa_layer(zq, zkv, zg, nsa_q_norm[l], nsa_k_norm[l], cmp_pe[l], cmp_w1[l], cmp_w2[l], b, s)
        bias_full = jnp.repeat(sgu_b[l].T, SGU_WIDTH // SGU_GROUPS, axis=1)
        ob = sgu(zs, sgu_ln_g[l], sgu_ln_b[l], sgu_w[l], bias_full)
        cg = POOL_WIDTH // len(POOL_WINDOWS)
        w_bd = jnp.zeros((POOL_WIDTH, POOL_WIDTH), F32)
        for gi in range(len(POOL_WINDOWS)):
            w_bd = w_bd.at[gi * cg:(gi + 1) * cg, gi * cg:(gi + 1) * cg].set(pool_w[l, gi])
        oc = pool(zp.reshape(b, s, POOL_WIDTH), w_bd.astype(BF16), pool_scale[l])
        x = merge(x, oa, ob.reshape(b, s, SGU_WIDTH), oc, zm.reshape(b, s, 3 * d),
                  lift_a[l].astype(BF16), lift_b[l].astype(BF16), lift_c[l].astype(BF16),
                  w_out[l].astype(BF16))

        mk, mv = mem_kv(mem, mem_norm[l], xa_wk[l].astype(BF16), xa_wv[l].astype(BF16), xa_k_norm[l])
        x = xattn(x, xa_norm[l], xa_wq[l].astype(BF16), xa_q_norm[l], mk, mv, xa_wo[l].astype(BF16))

        k1p = jnp.concatenate([peer_keys1[l].astype(BF16), zeros_half], axis=1)
        k2p = jnp.concatenate([zeros_half, peer_keys2[l].astype(BF16)], axis=1)
        x = peer(x.reshape(t, d), ffn_norm[l], peer_wq[l].astype(BF16), k1p, k2p,
                 peer_u[l].astype(BF16), peer_v[l].T.astype(BF16)).reshape(b, s, d)
    return x
```

```python
import functools

import jax
import jax.numpy as jnp
from jax import lax
from jax.experimental import pallas as pl
from jax.experimental.pallas import tpu as pltpu

F32 = jnp.float32
BF16 = jnp.bfloat16

EPS = 1e-6
LOG2E = 1.4426950408889634
NEG = -1e30
REMOVED = -3e38

D_MODEL = 1024
DK = 64
NSA_HEADS = 8
NSA_GROUPS = 2
NSA_REP = NSA_HEADS // NSA_GROUPS
L_CMP, D_CMP = 32, 16
CMP_HIDDEN = 128
L_SLC = 64
N_SEL = 16
WINDOW = 512
QB = 128
FORCE_BONUS = 1.0e3
SGU_WIDTH = 256
SGU_GROUPS = 4
SGU_CHUNK = 128
POOL_WIDTH = 256
POOL_WINDOWS = (2, 4, 8, 16)
XA_HEADS, XA_HD = 4, 128
XA_WIDTH = XA_HEADS * XA_HD
PEER_HEADS = 8
PEER_KEYS = 128
PEER_TOPK = 16
PEER_HALF = 64

LANES = 128
SEL_SUPER_BLOCKS = 128
SEL_SUPER = SEL_SUPER_BLOCKS * L_SLC
SEL_KW = -(-(DK + SEL_SUPER_BLOCKS) // LANES) * LANES
SEL_VR = 2 * DK
SEL_SUB = 512
CMP_CHUNK = 256
CMP_HALO = 8
PEER_TT = 512
PEER_CH = 1024
PEER_PROJ_ROWS = 256
VMEM_LIMIT = 56 * 1024 * 1024


def _cp(sem):
    return pltpu.CompilerParams(dimension_semantics=sem, vmem_limit_bytes=VMEM_LIMIT)


def _gelu(x):
    return 0.5 * x * (1.0 + jnp.tanh(0.7978845608028654 * (x + 0.044715 * (x * x * x))))


def _rms(x, g):
    return x * lax.rsqrt(jnp.mean(x * x, axis=-1, keepdims=True) + EPS) * g


def _dot(a, b):
    return jnp.dot(a, b, preferred_element_type=F32)


def _dot_nt(a, b):
    return lax.dot_general(a, b, (((1,), (1,)), ((), ())), preferred_element_type=F32)


def _iota(shape, dim):
    return lax.broadcasted_iota(jnp.int32, shape, dim)


def _div_pow2(x, n):
    assert n & (n - 1) == 0
    return lax.shift_right_logical(x, jnp.int32(n.bit_length() - 1))


def _mod_pow2(x, n):
    assert n & (n - 1) == 0
    return x & (n - 1)


def _norm_matmul_kernel(x_ref, g_ref, w_ref, *o_refs, splits):
    hb = _rms(x_ref[...], g_ref[...]).astype(BF16)
    for o_ref, (a, b) in zip(o_refs, splits):
        o_ref[...] = _dot(hb, w_ref[:, a:b]).astype(o_ref.dtype)


def norm_matmul(x, g, w, splits, tm, name):
    t, d = x.shape
    n = w.shape[1]
    return pl.pallas_call(
        functools.partial(_norm_matmul_kernel, splits=splits),
        grid=(t // tm,),
        in_specs=[pl.BlockSpec((tm, d), lambda i: (i, 0)),
                  pl.BlockSpec((1, d), lambda i: (0, 0)),
                  pl.BlockSpec((d, n), lambda i: (0, 0))],
        out_specs=[pl.BlockSpec((tm, b - a), lambda i: (i, 0)) for a, b in splits],
        out_shape=[jax.ShapeDtypeStruct((t, b - a), F32) for a, b in splits],
        compiler_params=_cp(("parallel",)),
        name=name,
    )(x, g.reshape(1, d), w)


def _nsa_prep_kernel(zq_ref, zkv_ref, qn_ref, kn_ref,
                     q_out, kvc_out, ks_out, vs_out, kw_out, vw_out, *, ts):
    st = pl.program_id(1)
    zq = zq_ref[0]
    scale = DK ** -0.5 * LOG2E
    for h in range(NSA_HEADS):
        q_out[0, h] = (_rms(zq[:, h * DK:(h + 1) * DK], qn_ref[...]) * scale).astype(BF16)
    zkv = zkv_ref[0]

    def piece(i, g):
        o = i * NSA_GROUPS * DK + g * DK
        return zkv[:, o:o + DK]

    pos = st * ts + _iota((ts, SEL_SUPER_BLOCKS), 0)
    blk = _mod_pow2(_div_pow2(pos, L_SLC), SEL_SUPER_BLOCKS)
    onehot = jnp.where(_iota((ts, SEL_SUPER_BLOCKS), 1) == blk, 1.0, 0.0).astype(BF16)
    key_pad = jnp.zeros((ts, SEL_KW - DK - SEL_SUPER_BLOCKS), BF16)
    ones_col = jnp.where(_iota((ts, SEL_VR - DK), 1) == 0, 1.0, 0.0)
    for g in range(NSA_GROUPS):
        kvc_out[0, 0, g] = piece(0, g)
        kvc_out[1, 0, g] = piece(1, g)
        ks = _rms(piece(2, g), kn_ref[1:2, :]).astype(BF16)
        ks_out[0, g] = jnp.concatenate([ks, onehot, key_pad], axis=-1)
        vs_out[0, g] = jnp.concatenate([piece(3, g), ones_col], axis=-1).T.astype(BF16)
        kw_out[0, g] = _rms(piece(4, g), kn_ref[2:3, :]).astype(BF16)
        vw_out[0, g] = jnp.concatenate([piece(5, g), ones_col], axis=-1).T.astype(BF16)


def nsa_prep(zq, zkv, q_norm, k_norm, ts=512):
    b, s, _ = zq.shape
    g = NSA_GROUPS
    hm = lambda bi, si: (bi, 0, si, 0)
    return pl.pallas_call(
        functools.partial(_nsa_prep_kernel, ts=ts),
        grid=(b, s // ts),
        in_specs=[pl.BlockSpec((1, ts, NSA_HEADS * DK), lambda bi, si: (bi, si, 0)),
                  pl.BlockSpec((1, ts, 6 * g * DK), lambda bi, si: (bi, si, 0)),
                  pl.BlockSpec((1, DK), lambda bi, si: (0, 0)),
                  pl.BlockSpec((3, DK), lambda bi, si: (0, 0))],
        out_specs=[pl.BlockSpec((1, NSA_HEADS, ts, DK), hm),
                   pl.BlockSpec((2, 1, g, ts, DK), lambda bi, si: (0, bi, 0, si, 0)),
                   pl.BlockSpec((1, g, ts, SEL_KW), hm),
                   pl.BlockSpec((1, g, SEL_VR, ts), lambda bi, si: (bi, 0, 0, si)),
                   pl.BlockSpec((1, g, ts, DK), hm),
                   pl.BlockSpec((1, g, SEL_VR, ts), lambda bi, si: (bi, 0, 0, si))],
        out_shape=[jax.ShapeDtypeStruct((b, NSA_HEADS, s, DK), BF16),
                   jax.ShapeDtypeStruct((2, b, g, s, DK), F32),
                   jax.ShapeDtypeStruct((b, g, s, SEL_KW), BF16),
                   jax.ShapeDtypeStruct((b, g, SEL_VR, s), BF16),
                   jax.ShapeDtypeStruct((b, g, s, DK), BF16),
                   jax.ShapeDtypeStruct((b, g, SEL_VR, s), BF16)],
        compiler_params=_cp(("parallel", "parallel")),
        name="nsa_prep",
    )(zq, zkv, q_norm.reshape(1, DK), k_norm)


def _cmp_mlp_kernel(x_ref, pe_ref, w1_ref, w2_ref, kn_ref, k_out, vt_out, *, nc):
    c = pl.program_id(2)
    x = x_ref[0, 0, 0]
    a = _dot((x + pe_ref[0, 0:1, :]).astype(BF16), w1_ref[0, 0])
    b = _dot((x + pe_ref[0, 1:2, :]).astype(BF16), w1_ref[0, 1])
    pre = a + pltpu.roll(b, nc - 1, 0)
    comp = _dot(_gelu(pre).astype(BF16), w2_ref[0])

    @pl.when(c == 0)
    def _():
        k_out[0, 0] = _rms(comp, kn_ref[...]).astype(BF16)

    @pl.when(c == 1)
    def _():
        ones_col = jnp.where(_iota((nc, SEL_VR - DK), 1) == 0, 1.0, 0.0)
        vt_out[0, 0] = jnp.concatenate([comp, ones_col], axis=-1).T.astype(BF16)


def cmp_mlp(xc, pe, w1, w2, k_norm0):
    _, b, g, nc, width = xc.shape
    return pl.pallas_call(
        functools.partial(_cmp_mlp_kernel, nc=nc),
        grid=(b, g, 2),
        in_specs=[pl.BlockSpec((1, 1, 1, nc, width), lambda bi, gi, c: (c, bi, gi, 0, 0)),
                  pl.BlockSpec((1, 2, width), lambda bi, gi, c: (c, 0, 0)),
                  pl.BlockSpec((1, 2, width, CMP_HIDDEN), lambda bi, gi, c: (c, 0, 0, 0)),
                  pl.BlockSpec((1, CMP_HIDDEN, DK), lambda bi, gi, c: (c, 0, 0)),
                  pl.BlockSpec((1, DK), lambda bi, gi, c: (0, 0))],
        out_specs=[pl.BlockSpec((1, 1, nc, DK), lambda bi, gi, c: (bi, gi, 0, 0)),
                   pl.BlockSpec((1, 1, SEL_VR, nc), lambda bi, gi, c: (bi, gi, 0, 0))],
        out_shape=[jax.ShapeDtypeStruct((b, g, nc, DK), BF16),
                   jax.ShapeDtypeStruct((b, g, SEL_VR, nc), BF16)],
        compiler_params=_cp(("parallel", "parallel", "arbitrary")),
        name="cmp_mlp",
    )(xc, pe, w1, w2, k_norm0.reshape(1, DK))


def _cmp_attn_kernel(q_ref, k_ref, vt_ref, oc_ref, sb_ref, s_s, acc_s, ps_s, *, nc, n_slc):
    qb = pl.program_id(2)
    rows = NSA_REP * QB
    ck = CMP_CHUNK
    q = q_ref[0].reshape(rows, DK)
    n_chunk = ((qb * QB + QB) // D_CMP + ck - 1) // ck
    t = qb * QB + _mod_pow2(_iota((ck, rows), 1), QB)

    def pass_a(ci, m):
        off = pl.multiple_of(ci * ck, ck)
        st = _dot_nt(k_ref[0, 0, pl.ds(off, ck), :], q)
        n = off + _iota((ck, rows), 0)
        st = jnp.where(n * D_CMP + (L_CMP - 1) <= t, st, NEG)
        s_s[ci] = st
        return jnp.maximum(m, jnp.max(st, axis=0, keepdims=True))

    m = lax.fori_loop(0, n_chunk, pass_a, jnp.full((1, rows), NEG, F32))
    acc_s[...] = jnp.zeros_like(acc_s)
    ps_s[...] = jnp.zeros_like(ps_s)

    def pass_b(ci, den):
        off = pl.multiple_of(ci * ck, ck)
        e = jnp.exp2(s_s[ci] - m)
        s_s[ci] = e
        acc_s[...] += _dot(vt_ref[0, 0, :, pl.ds(off, ck)], e.astype(BF16))
        return den + jnp.sum(e, axis=0, keepdims=True)

    den = lax.fori_loop(0, n_chunk, pass_b, jnp.zeros((1, rows), F32))
    seen = m > 0.5 * NEG
    inv_o = jnp.where(seen, 1.0 / jnp.maximum(acc_s[DK:DK + 1, :], 1e-30), 0.0)
    inv_p = jnp.where(seen, 1.0 / jnp.maximum(den, 1e-30), 0.0)
    oc_ref[0] = (acc_s[...] * inv_o).T[:, 0:DK].reshape(NSA_REP, QB, DK)

    def pass_c(ci, carry):
        off = pl.multiple_of(ci * ck, ck)
        p = s_s[ci] * inv_p
        ps = p[:, 0:QB]
        for r in range(1, NSA_REP):
            ps = ps + p[:, r * QB:(r + 1) * QB]
        ps_s[pl.ds(CMP_HALO + off, ck), :] = ps
        return carry

    lax.fori_loop(0, n_chunk, pass_c, 0)

    ratio = L_SLC // D_CMP
    imp = jnp.zeros((n_slc, QB), F32)
    for k in range(-(L_CMP // D_CMP - 1), ratio):
        overlap = min(k * D_CMP + L_CMP, L_SLC) - max(k * D_CMP, 0)
        imp = imp + (overlap / D_CMP) * ps_s[pl.ds(CMP_HALO + k, n_slc, stride=ratio), :]

    cur = _div_pow2(qb * QB + _iota((n_slc, QB), 1), L_SLC)
    j = _iota((n_slc, QB), 0)
    forced = (j == 0) | (j == cur) | (j == cur - 1)
    score = jnp.where(j <= cur, imp + jnp.where(forced, FORCE_BONUS, 0.0), NEG)
    work = score
    tau = None
    for _ in range(N_SEL):
        tau = jnp.max(work, axis=0, keepdims=True)
        work = jnp.where(work >= tau, REMOVED, work)
    selected = (score > 0.5 * NEG) & (score >= tau)
    bias = jnp.where(selected, 0.0, NEG).T.astype(BF16)
    for si in range(n_slc // SEL_SUPER_BLOCKS):
        sb_ref[0, 0, 0, si] = bias[:, si * SEL_SUPER_BLOCKS:(si + 1) * SEL_SUPER_BLOCKS]


def cmp_attn_select(qn, kc, vct):
    b, h, s, _ = qn.shape
    g = NSA_GROUPS
    nc = kc.shape[2]
    n_slc = s // L_SLC
    nsup = n_slc // SEL_SUPER_BLOCKS
    nqb = s // QB
    return pl.pallas_call(
        functools.partial(_cmp_attn_kernel, nc=nc, n_slc=n_slc),
        grid=(b, g, nqb),
        in_specs=[pl.BlockSpec((1, NSA_REP, QB, DK), lambda bi, gi, qi: (bi, gi, qi, 0)),
                  pl.BlockSpec((1, 1, nc, DK), lambda bi, gi, qi: (bi, gi, 0, 0)),
                  pl.BlockSpec((1, 1, SEL_VR, nc), lambda bi, gi, qi: (bi, gi, 0, 0))],
        out_specs=[pl.BlockSpec((1, NSA_REP, QB, DK), lambda bi, gi, qi: (bi, gi, qi, 0)),
                   pl.BlockSpec((1, 1, 1, nsup, QB, SEL_SUPER_BLOCKS),
                                lambda bi, gi, qi: (bi, gi, qi, 0, 0, 0))],
        out_shape=[jax.ShapeDtypeStruct((b, h, s, DK), F32),
                   jax.ShapeDtypeStruct((b, g, nqb, nsup, QB, SEL_SUPER_BLOCKS), BF16)],
        scratch_shapes=[pltpu.VMEM((nc // CMP_CHUNK, CMP_CHUNK, NSA_REP * QB), F32),
                        pltpu.VMEM((SEL_VR, NSA_REP * QB), F32),
                        pltpu.VMEM((CMP_HALO + nc, QB), F32)],
        compiler_params=_cp(("parallel", "parallel", "parallel")),
        name="cmp_attn_select",
    )(qn, kc, vct)


def _win_attn_kernel(q_ref, *refs):
    nkb = WINDOW // QB + 1
    k_refs, v_refs, o_ref = refs[:nkb], refs[nkb:2 * nkb], refs[2 * nkb]
    qb = pl.program_id(2)
    rows = NSA_REP * QB
    nk = nkb * QB
    q = q_ref[0].reshape(rows, DK)
    k = jnp.concatenate([r[0, 0] for r in k_refs], axis=0)
    vt = jnp.concatenate([r[0, 0] for r in v_refs], axis=-1)
    st = _dot_nt(k, q)
    t = qb * QB + _mod_pow2(_iota((nk, rows), 1), QB)
    kpos = (qb - (nkb - 1)) * QB + _iota((nk, rows), 0)
    delta = t - kpos
    valid = (delta >= 0) & (delta < WINDOW) & (kpos >= 0)
    st = jnp.where(valid, st, NEG)
    e = jnp.exp2(st - jnp.max(st, axis=0, keepdims=True)).astype(BF16)
    acc = _dot(vt, e)
    o = (acc / jnp.maximum(acc[DK:DK + 1, :], 1e-30)).T[:, 0:DK]
    o_ref[0] = o.reshape(NSA_REP, QB, DK)


def win_attn(qn, kw, vw):
    b, h, s, _ = qn.shape
    g = NSA_GROUPS
    nkb = WINDOW // QB + 1
    nqb = s // QB

    def k_spec(i):
        return pl.BlockSpec((1, 1, QB, DK),
                            lambda bi, gi, qi: (bi, gi, jnp.maximum(qi - (nkb - 1) + i, 0), 0))

    def vt_spec(i):
        return pl.BlockSpec((1, 1, SEL_VR, QB),
                            lambda bi, gi, qi: (bi, gi, 0, jnp.maximum(qi - (nkb - 1) + i, 0)))

    return pl.pallas_call(
        _win_attn_kernel,
        grid=(b, g, nqb),
        in_specs=[pl.BlockSpec((1, NSA_REP, QB, DK), lambda bi, gi, qi: (bi, gi, qi, 0))]
        + [k_spec(i) for i in range(nkb)] + [vt_spec(i) for i in range(nkb)],
        out_specs=pl.BlockSpec((1, NSA_REP, QB, DK), lambda bi, gi, qi: (bi, gi, qi, 0)),
        out_shape=jax.ShapeDtypeStruct((b, h, s, DK), F32),
        compiler_params=_cp(("parallel", "parallel", "parallel")),
        name="win_attn",
    )(qn, *([kw] * nkb), *([vw] * nkb))


def _sel_attn_kernel(q_ref, sb_ref, k_ref, vt_ref, oc_ref, ow_ref, gl_ref, o_ref,
                     qp_s, m_s, acc_s, s0_s, s1_s, p0_s, p1_s, a0_s, a1_s):
    qb = pl.program_id(2)
    si = pl.program_id(3)
    rows = NSA_REP * QB
    last = (qb * QB) // SEL_SUPER

    @pl.when(si == 0)
    def _():
        qp_s[...] = jnp.zeros_like(qp_s)
        qp_s[:, 0:DK] = q_ref[0].reshape(rows, DK)
        m_s[...] = jnp.full_like(m_s, NEG)
        acc_s[...] = jnp.zeros_like(acc_s)

    @pl.when(si <= last)
    def _():
        sb = sb_ref[0, 0, 0, 0]
        for r in range(NSA_REP):
            qp_s[r * QB:(r + 1) * QB, DK:DK + SEL_SUPER_BLOCKS] = sb
        qv = qp_s[...]
        nsub = jnp.minimum(SEL_SUPER // SEL_SUB,
                           (qb * QB + QB - si * SEL_SUPER + SEL_SUB - 1) // SEL_SUB)

        def scores(i, s_ref):
            off = pl.multiple_of(i * SEL_SUB, SEL_SUB)
            s_ref[...] = _dot_nt(k_ref[0, 0, pl.ds(off, SEL_SUB), :], qv)

        def weights(i, s_ref, p_ref, a_ref, causal):
            st = s_ref[...]
            if causal:
                kpos = si * SEL_SUPER + i * SEL_SUB + _iota((SEL_SUB, rows), 0)
                t = qb * QB + _mod_pow2(_iota((SEL_SUB, rows), 1), QB)
                st = jnp.where(kpos <= t, st, NEG)
            m_old = m_s[...]
            m_new = jnp.maximum(m_old, jnp.max(st, axis=0, keepdims=True))
            p_ref[...] = jnp.exp2(st - m_new).astype(BF16)
            a_ref[...] = jnp.exp2(m_old - m_new)
            m_s[...] = m_new

        def accumulate(i, p_ref, a_ref):
            off = pl.multiple_of(i * SEL_SUB, SEL_SUB)
            acc_s[...] = a_ref[...] * acc_s[...] + _dot(vt_ref[0, 0, :, pl.ds(off, SEL_SUB)], p_ref[...])

        p1_s[...] = jnp.zeros_like(p1_s)
        a1_s[...] = jnp.ones_like(a1_s)
        scores(0, s0_s)
        n_pairs = (nsub - 1) // 2
        odd_tail = (nsub - 1) - 2 * n_pairs

        def pair(j, carry):
            scores(2 * j + 1, s1_s)
            weights(2 * j, s0_s, p0_s, a0_s, False)
            accumulate(jnp.maximum(2 * j - 1, 0), p1_s, a1_s)
            scores(2 * j + 2, s0_s)
            weights(2 * j + 1, s1_s, p1_s, a1_s, False)
            accumulate(2 * j, p0_s, a0_s)
            return carry

        lax.fori_loop(0, n_pairs, pair, 0)
        e = 2 * n_pairs

        @pl.when(odd_tail == 0)
        def _():
            weights(e, s0_s, p0_s, a0_s, True)
            accumulate(jnp.maximum(e - 1, 0), p1_s, a1_s)
            accumulate(e, p0_s, a0_s)

        @pl.when(odd_tail == 1)
        def _():
            scores(e + 1, s1_s)
            weights(e, s0_s, p0_s, a0_s, False)
            accumulate(jnp.maximum(e - 1, 0), p1_s, a1_s)
            weights(e + 1, s1_s, p1_s, a1_s, True)
            accumulate(e, p0_s, a0_s)
            accumulate(e + 1, p1_s, a1_s)

    @pl.when(si == last)
    def _():
        o_s = (acc_s[...] / jnp.maximum(acc_s[DK:DK + 1, :], 1e-30)).T[:, 0:DK]
        gate = jax.nn.sigmoid(gl_ref[0].reshape(rows, 3))
        out = (gate[:, 0:1] * oc_ref[0].reshape(rows, DK) + gate[:, 1:2] * o_s
               + gate[:, 2:3] * ow_ref[0].reshape(rows, DK))
        o_ref[0] = out.reshape(NSA_REP, QB, DK)


def sel_attn(qn, selb, ks, vs, oc, ow, gl):
    b, h, s, _ = qn.shape
    g = NSA_GROUPS
    nqb = s // QB
    nsup = s // SEL_SUPER
    qmap = lambda bi, gi, qi, si: (bi, gi, qi, 0)

    def eff(qi, si):
        return jnp.minimum(si, (qi * QB) // SEL_SUPER)

    return pl.pallas_call(
        _sel_attn_kernel,
        grid=(b, g, nqb, nsup),
        in_specs=[pl.BlockSpec((1, NSA_REP, QB, DK), qmap),
                  pl.BlockSpec((1, 1, 1, 1, QB, SEL_SUPER_BLOCKS),
                               lambda bi, gi, qi, si: (bi, gi, qi, eff(qi, si), 0, 0)),
                  pl.BlockSpec((1, 1, SEL_SUPER, SEL_KW), lambda bi, gi, qi, si: (bi, gi, eff(qi, si), 0)),
                  pl.BlockSpec((1, 1, SEL_VR, SEL_SUPER), lambda bi, gi, qi, si: (bi, gi, 0, eff(qi, si))),
                  pl.BlockSpec((1, NSA_REP, QB, DK), qmap),
                  pl.BlockSpec((1, NSA_REP, QB, DK), qmap),
                  pl.BlockSpec((1, NSA_REP, QB, 3), qmap)],
        out_specs=pl.BlockSpec((1, NSA_REP, QB, DK), qmap),
        out_shape=jax.ShapeDtypeStruct((b, h, s, DK), F32),
        scratch_shapes=[pltpu.VMEM((NSA_REP * QB, SEL_KW), BF16),
                        pltpu.VMEM((1, NSA_REP * QB), F32),
                        pltpu.VMEM((SEL_VR, NSA_REP * QB), F32),
                        pltpu.VMEM((SEL_SUB, NSA_REP * QB), F32),
                        pltpu.VMEM((SEL_SUB, NSA_REP * QB), F32),
                        pltpu.VMEM((SEL_SUB, NSA_REP * QB), BF16),
                        pltpu.VMEM((SEL_SUB, NSA_REP * QB), BF16),
                        pltpu.VMEM((1, NSA_REP * QB), F32),
                        pltpu.VMEM((1, NSA_REP * QB), F32)],
        compiler_params=_cp(("parallel", "parallel", "parallel", "arbitrary")),
        name="sel_attn",
    )(qn, selb, ks, vs, oc, ow, gl)


def _sgu_kernel(z_ref, g_ref, b_ref, w_ref, bias_ref, o_ref, *, tm):
    z = _gelu(z_ref[...])
    u = z[:, :SGU_WIDTH]
    v = z[:, SGU_WIDTH:]
    mu = jnp.mean(v, axis=-1, keepdims=True)
    var = jnp.mean(jnp.square(v - mu), axis=-1, keepdims=True)
    vn = ((v - mu) * lax.rsqrt(var + EPS) * g_ref[...] + b_ref[...]).astype(BF16)
    tril = _iota((SGU_CHUNK, SGU_CHUNK), 1) <= _iota((SGU_CHUNK, SGU_CHUNK), 0)
    wm = [jnp.where(tril, w_ref[gi], 0.0).astype(BF16) for gi in range(SGU_GROUPS)]
    grp = _div_pow2(_iota((SGU_CHUNK, SGU_WIDTH), 1), SGU_WIDTH // SGU_GROUPS)
    for ch in range(tm // SGU_CHUNK):
        rs = slice(ch * SGU_CHUNK, (ch + 1) * SGU_CHUNK)
        mixed = bias_ref[...]
        for gi in range(SGU_GROUPS):
            mixed = mixed + jnp.where(grp == gi, _dot(wm[gi], vn[rs]), 0.0)
        o_ref[rs, :] = u[rs] * mixed


def sgu(z, ln_g, ln_b, w, bias_full, tm=512):
    t = z.shape[0]
    return pl.pallas_call(
        functools.partial(_sgu_kernel, tm=tm),
        grid=(t // tm,),
        in_specs=[pl.BlockSpec((tm, 2 * SGU_WIDTH), lambda i: (i, 0)),
                  pl.BlockSpec((1, SGU_WIDTH), lambda i: (0, 0)),
                  pl.BlockSpec((1, SGU_WIDTH), lambda i: (0, 0)),
                  pl.BlockSpec((SGU_GROUPS, SGU_CHUNK, SGU_CHUNK), lambda i: (0, 0, 0)),
                  pl.BlockSpec((SGU_CHUNK, SGU_WIDTH), lambda i: (0, 0))],
        out_specs=pl.BlockSpec((tm, SGU_WIDTH), lambda i: (i, 0)),
        out_shape=jax.ShapeDtypeStruct((t, SGU_WIDTH), F32),
        compiler_params=_cp(("parallel",)),
        name="sgu",
    )(z, ln_g.reshape(1, -1), ln_b.reshape(1, -1), w, bias_full)


def _pool_kernel(p_ref, h_ref, w_ref, sc_ref, o_ref, *, tp):
    i = pl.program_id(1)
    halo_rows = POOL_WINDOWS[-1]
    p = p_ref[0]
    halo = jnp.where(i > 0, h_ref[0], 0.0)
    ext = jnp.concatenate([halo, p], axis=0)
    sums = [ext]
    shift = 1
    for _ in POOL_WINDOWS:
        prev = sums[-1]
        sums.append(prev + pltpu.roll(prev, shift, 0))
        shift *= 2
    pos = i * tp + _iota((tp, POOL_WIDTH), 0)
    grp = _div_pow2(_iota((tp, POOL_WIDTH), 1), POOL_WIDTH // len(POOL_WINDOWS))
    d = jnp.zeros((tp, POOL_WIDTH), F32)
    for gi, w in enumerate(POOL_WINDOWS):
        mean = sums[gi + 1][halo_rows:halo_rows + tp] / jnp.minimum(pos + 1, w).astype(F32)
        d = jnp.where(grp == gi, mean, d)
    d = d - p
    o_ref[0] = _dot(d.astype(BF16), w_ref[...]) * sc_ref[...]


def pool(p, w_bd, scale, tp=512):
    b, s, c = p.shape
    halo_rows = POOL_WINDOWS[-1]
    return pl.pallas_call(
        functools.partial(_pool_kernel, tp=tp),
        grid=(b, s // tp),
        in_specs=[pl.BlockSpec((1, tp, c), lambda bi, i: (bi, i, 0)),
                  pl.BlockSpec((1, halo_rows, c),
                               lambda bi, i: (bi, jnp.maximum(i * (tp // halo_rows) - 1, 0), 0)),
                  pl.BlockSpec((c, c), lambda bi, i: (0, 0)),
                  pl.BlockSpec((1, c), lambda bi, i: (0, 0))],
        out_specs=pl.BlockSpec((1, tp, c), lambda bi, i: (bi, i, 0)),
        out_shape=jax.ShapeDtypeStruct((b, s, c), F32),
        compiler_params=_cp(("parallel", "parallel")),
        name="pool",
    )(p, p, w_bd, scale.reshape(1, c))


def _merge_kernel(x_ref, oa_ref, ob_ref, oc_ref, zm_ref, la_ref, lb_ref, lc_ref, wo_ref, o_ref):
    d = D_MODEL
    oa = jnp.concatenate([oa_ref[0, h].astype(BF16) for h in range(NSA_HEADS)], axis=-1)
    zm = zm_ref[0]
    merged = (jax.nn.sigmoid(zm[:, 0:d]) * _dot(oa, la_ref[...])
              + jax.nn.sigmoid(zm[:, d:2 * d]) * _dot(ob_ref[0].astype(BF16), lb_ref[...])
              + jax.nn.sigmoid(zm[:, 2 * d:3 * d]) * _dot(oc_ref[0].astype(BF16), lc_ref[...]))
    o_ref[0] = x_ref[0] + _dot(merged.astype(BF16), wo_ref[...])


def merge(x, oa, ob, oc, zm, la, lb, lc, wo, tm=512):
    b, s, d = x.shape
    row = lambda bi, i: (bi, i, 0)
    full = lambda bi, i: (0, 0)
    return pl.pallas_call(
        _merge_kernel,
        grid=(b, s // tm),
        in_specs=[pl.BlockSpec((1, tm, d), row),
                  pl.BlockSpec((1, NSA_HEADS, tm, DK), lambda bi, i: (bi, 0, i, 0)),
                  pl.BlockSpec((1, tm, SGU_WIDTH), row),
                  pl.BlockSpec((1, tm, POOL_WIDTH), row),
                  pl.BlockSpec((1, tm, 3 * d), row),
                  pl.BlockSpec(la.shape, full), pl.BlockSpec(lb.shape, full),
                  pl.BlockSpec(lc.shape, full), pl.BlockSpec(wo.shape, full)],
        out_specs=pl.BlockSpec((1, tm, d), row),
        out_shape=jax.ShapeDtypeStruct((b, s, d), F32),
        compiler_params=_cp(("parallel", "parallel")),
        name="merge",
    )(x, oa, ob, oc, zm, la, lb, lc, wo)


def _mem_kv_kernel(m_ref, g_ref, wk_ref, wv_ref, kn_ref, k_out, v_out):
    mh = _rms(m_ref[0], g_ref[...]).astype(BF16)
    k = _dot(mh, wk_ref[...])
    for h in range(XA_HEADS):
        hs = slice(h * XA_HD, (h + 1) * XA_HD)
        k_out[0, :, hs] = _rms(k[:, hs], kn_ref[...]).astype(BF16)
    v_out[0] = _dot(mh, wv_ref[...]).astype(BF16)


def mem_kv(mem, g, wk, wv, k_norm):
    b, m, d = mem.shape
    full = lambda bi: (0, 0)
    return pl.pallas_call(
        _mem_kv_kernel,
        grid=(b,),
        in_specs=[pl.BlockSpec((1, m, d), lambda bi: (bi, 0, 0)),
                  pl.BlockSpec((1, d), full),
                  pl.BlockSpec(wk.shape, full), pl.BlockSpec(wv.shape, full),
                  pl.BlockSpec((1, XA_HD), full)],
        out_specs=[pl.BlockSpec((1, m, XA_WIDTH), lambda bi: (bi, 0, 0))] * 2,
        out_shape=[jax.ShapeDtypeStruct((b, m, XA_WIDTH), BF16)] * 2,
        compiler_params=_cp(("parallel",)),
        name="mem_kv",
    )(mem, g.reshape(1, d), wk, wv, k_norm.reshape(1, XA_HD))


def _xattn_kernel(x_ref, g_ref, wq_ref, qn_ref, k_ref, v_ref, wo_ref, o_ref):
    x = x_ref[0]
    q = _dot(_rms(x, g_ref[...]).astype(BF16), wq_ref[...])
    k = k_ref[0]
    v = v_ref[0]
    outs = []
    for h in range(XA_HEADS):
        hs = slice(h * XA_HD, (h + 1) * XA_HD)
        qh = (_rms(q[:, hs], qn_ref[...])).astype(BF16)
        s = _dot_nt(qh, k[:, hs]) * (XA_HD ** -0.5)
        e = jnp.exp(s - jnp.max(s, axis=-1, keepdims=True))
        p = e / jnp.sum(e, axis=-1, keepdims=True)
        outs.append(_dot(p.astype(BF16), v[:, hs]).astype(BF16))
    o = jnp.concatenate(outs, axis=-1)
    o_ref[0] = x + _dot(o, wo_ref[...])


def xattn(x, g, wq, q_norm, k, v, wo, tm=512):
    b, s, d = x.shape
    m = k.shape[1]
    row = lambda bi, i: (bi, i, 0)
    full = lambda bi, i: (0, 0)
    return pl.pallas_call(
        _xattn_kernel,
        grid=(b, s // tm),
        in_specs=[pl.BlockSpec((1, tm, d), row),
                  pl.BlockSpec((1, d), full),
                  pl.BlockSpec(wq.shape, full),
                  pl.BlockSpec((1, XA_HD), full),
                  pl.BlockSpec((1, m, XA_WIDTH), lambda bi, i: (bi, 0, 0)),
                  pl.BlockSpec((1, m, XA_WIDTH), lambda bi, i: (bi, 0, 0)),
                  pl.BlockSpec(wo.shape, full)],
        out_specs=pl.BlockSpec((1, tm, d), row),
        out_shape=jax.ShapeDtypeStruct((b, s, d), F32),
        compiler_params=_cp(("parallel", "parallel")),
        name="xattn",
    )(x, g.reshape(1, d), wq, q_norm.reshape(1, XA_HD), k, v, wo)


def _top_rows(s, n, want_rank):
    out = []
    work = s
    rank = jnp.full(s.shape, float(s.shape[0] - 1), F32) if want_rank else None
    for i in range(n):
        m = jnp.max(work, axis=0, keepdims=True)
        out.append(m)
        hit = work >= m
        if want_rank:
            rank = jnp.where(hit, float(i), rank)
        work = jnp.where(hit, REMOVED, work)
    return out, rank


def _stack_rows(rows, pad_rows):
    tt = rows[0].shape[1]
    rowi = _iota((pad_rows, tt), 0)
    out = jnp.full((pad_rows, tt), REMOVED, F32)
    for i, r in enumerate(rows[:pad_rows]):
        out = jnp.where(rowi == i, r, out)
    return out


def _pair_sum_candidates(v1, v2, tt):
    row_full = _iota((24, tt), 0)
    row = _iota((8, tt), 0)
    v2_full = _stack_rows(v2, 24)
    v2_8 = _stack_rows(v2, 8)

    def shifted(k):
        return pltpu.roll(v2_8, k, 0)

    def pick(rows_from, first):
        out = v1[rows_from]
        for r in range(first + 1, 8):
            if rows_from + r - first < len(v1):
                out = jnp.where(row == r, v1[rows_from + r - first], out)
        return out

    return jnp.concatenate([
        jnp.where(row_full < 17, v1[0] + v2_full, REMOVED),
        v1[1] + v2_8,
        jnp.where(row < 5, v1[2] + v2_8,
                  jnp.where(row < 7, v1[5] + shifted(5), v1[8] + shifted(7))),
        jnp.where(row < 4, v1[3] + v2_8,
                  jnp.where(row < 7, v1[4] + shifted(4), v1[9] + shifted(7))),
        jnp.where(row < 2, v1[6] + v2_8,
                  jnp.where(row < 4, v1[7] + shifted(2), pick(10, 4) + v2[0])),
        jnp.where(row < 3, pick(14, 0) + v2[0], REMOVED),
    ], axis=0)


def _peer_kernel(x_ref, g_ref, wq_ref, k1_ref, k2_ref, u_ref, vt_ref, o_ref,
                 hbt_s, q_s, r2_s, na_s, e1_s, e2_s, acc_s, *, tt, ch):
    c = pl.program_id(1)
    ntop = PEER_TOPK + 1
    assert ntop == 17

    @pl.when(c == 0)
    def _():
        h = _rms(x_ref[...], g_ref[...])
        hbt_s[...] = h.T.astype(BF16)
        q = _dot(h.astype(BF16), wq_ref[...])
        for hd in range(PEER_HEADS):
            q_s[hd] = q[:, hd * 2 * PEER_HALF:(hd + 1) * 2 * PEER_HALF].astype(BF16)
        acc_s[...] = jnp.zeros_like(acc_s)

        def route(hd, carry):
            qh = q_s[hd]
            s1 = _dot_nt(k1_ref[...], qh)
            s2 = _dot_nt(k2_ref[...], qh)
            v1, _ = _top_rows(s1, ntop, False)
            v2, rank2 = _top_rows(s2, ntop, True)
            cand = _pair_sum_candidates(v1, v2, tt)
            top = v1[0] + v2[0]
            z = jnp.zeros((1, tt), F32)
            c_prev = top
            c_last = top
            work = cand
            for r in range(ntop):
                m = jnp.max(work, axis=0, keepdims=True)
                work = jnp.where(work >= m, REMOVED, work)
                if r < PEER_TOPK:
                    z = z + jnp.exp(m - top)
                c_prev, c_last = c_last, m
            thr = 0.5 * (c_prev + c_last)
            na = jnp.zeros((PEER_KEYS, tt), F32)
            for j in range(ntop):
                na = na + jnp.where(s1 + v2[j] >= thr, 1.0, 0.0)
            r2_s[hd] = rank2.astype(BF16)
            na_s[hd] = na
            e1_s[hd] = jnp.exp(s1 - v1[0])
            e2_s[hd] = (jnp.exp(s2 - v2[0]) / z).astype(BF16)
            return carry

        lax.fori_loop(0, PEER_HEADS, route, 0)

    n_slab = ch // PEER_KEYS
    slabs_per_group = PEER_PROJ_ROWS // PEER_KEYS
    n_group = ch // PEER_PROJ_ROWS

    def project(gi):
        rs = slice(gi * PEER_PROJ_ROWS, (gi + 1) * PEER_PROJ_ROWS)
        return _gelu(_dot(u_ref[rs, :], hbt_s[...]).astype(BF16))

    def gate(al, gl_rows):
        a = c * n_slab + al
        wc = jnp.zeros((PEER_KEYS, tt), BF16)
        for hd in range(PEER_HEADS):
            na = na_s[hd, pl.ds(a, 1), :].astype(BF16)
            e1 = e1_s[hd, pl.ds(a, 1), :].astype(BF16)
            wc = wc + jnp.where(r2_s[hd] < na, e2_s[hd] * e1, jnp.zeros((), BF16))
        return wc * gl_rows

    ys = []
    gl_next = project(0)
    for gi in range(n_group):
        gl = gl_next
        if gi + 1 < n_group:
            gl_next = project(gi + 1)
        for k in range(slabs_per_group):
            ys.append(gate(gi * slabs_per_group + k, gl[k * PEER_KEYS:(k + 1) * PEER_KEYS]))
    y = jnp.concatenate(ys, axis=0)
    acc_s[...] += _dot(vt_ref[...], y)

    @pl.when(c == pl.num_programs(1) - 1)
    def _():
        o_ref[...] = x_ref[...] + acc_s[...].T


def peer(x, g, wq, k1p, k2p, u, vt, tt=PEER_TT, ch=PEER_CH):
    t, d = x.shape
    ne = u.shape[0]
    return pl.pallas_call(
        functools.partial(_peer_kernel, tt=tt, ch=ch),
        grid=(t // tt, ne // ch),
        in_specs=[pl.BlockSpec((tt, d), lambda i, c: (i, 0)),
                  pl.BlockSpec((1, d), lambda i, c: (0, 0)),
                  pl.BlockSpec(wq.shape, lambda i, c: (0, 0)),
                  pl.BlockSpec(k1p.shape, lambda i, c: (0, 0)),
                  pl.BlockSpec(k2p.shape, lambda i, c: (0, 0)),
                  pl.BlockSpec((ch, d), lambda i, c: (c, 0)),
                  pl.BlockSpec((d, ch), lambda i, c: (0, c))],
        out_specs=pl.BlockSpec((tt, d), lambda i, c: (i, 0)),
        out_shape=jax.ShapeDtypeStruct((t, d), F32),
        scratch_shapes=[pltpu.VMEM((d, tt), BF16),
                        pltpu.VMEM((PEER_HEADS, tt, 2 * PEER_HALF), BF16),
                        pltpu.VMEM((PEER_HEADS, PEER_KEYS, tt), BF16),
                        pltpu.VMEM((PEER_HEADS, PEER_KEYS, tt), F32),
                        pltpu.VMEM((PEER_HEADS, PEER_KEYS, tt), F32),
                        pltpu.VMEM((PEER_HEADS, PEER_KEYS, tt), BF16),
                        pltpu.VMEM((d, tt), F32)],
        compiler_params=_cp(("parallel", "arbitrary")),
        name="peer",
    )(x, g.reshape(1, d), wq, k1p, k2p, u, vt)


_IN_WIDTHS = (NSA_HEADS * DK, 6 * NSA_GROUPS * DK, 3 * NSA_HEADS, 2 * SGU_WIDTH, POOL_WIDTH, 3 * D_MODEL)


def _in_proj_layout():
    splits, src = [], []
    o_src = o_dst = 0
    for w in _IN_WIDTHS:
        wp = -(-w // LANES) * LANES
        splits.append((o_dst, o_dst + wp))
        src.append((o_src, o_src + w))
        o_src += w
        o_dst += wp
    return splits, src, o_dst


def nsa_layer(zq, zkv, zg, q_norm, k_norm, cmp_pe, cmp_w1, cmp_w2, b, s):
    qn, kvc, ks, vs, kw, vw = nsa_prep(zq.reshape(b, s, -1), zkv.reshape(b, s, -1), q_norm, k_norm)
    nc = s // D_CMP
    xc = kvc.reshape(2, b, NSA_GROUPS, nc, D_CMP * DK)
    pe = cmp_pe.reshape(2, 2, D_CMP * DK)
    w1 = cmp_w1.reshape(2, 2, D_CMP * DK, CMP_HIDDEN).astype(BF16)
    kc, vct = cmp_mlp(xc, pe, w1, cmp_w2.astype(BF16), k_norm[0])
    oc, selb = cmp_attn_select(qn, kc, vct)
    ow = win_attn(qn, kw, vw)
    gl = zg[:, :3 * NSA_HEADS].reshape(b, s, NSA_HEADS, 3).transpose(0, 2, 1, 3)
    return sel_attn(qn, selb, ks, vs, oc, ow, gl)


def kernel(x, mem, mix_norm, w_in, nsa_q_norm, nsa_k_norm, cmp_pe, cmp_w1, cmp_w2, sgu_ln_g, sgu_ln_b, sgu_w, sgu_b, pool_w, pool_scale, lift_a, lift_b, lift_c, w_out, xa_norm, mem_norm, xa_wq, xa_wk, xa_wv, xa_q_norm, xa_k_norm, xa_wo, ffn_norm, peer_wq, peer_keys1, peer_keys2, peer_u, peer_v):
    b, s, d = x.shape
    t = b * s
    depth = w_in.shape[0]
    splits, src, n_pad = _in_proj_layout()
    zeros_half = jnp.zeros((PEER_KEYS, PEER_HALF), BF16)
    for l in range(depth):
        w_parts = []
        for (a0, a1), (d0, d1) in zip(src, splits):
            w_parts.append(jnp.pad(w_in[l][:, a0:a1], ((0, 0), (0, (d1 - d0) - (a1 - a0)))))
        w_pad = jnp.concatenate(w_parts, axis=1).astype(BF16)
        zq, zkv, zg, zs, zp, zm = norm_matmul(x.reshape(t, d), mix_norm[l], w_pad, splits, 256, "in_proj")

        oa = nsa_layer(zq, zkv, zg, nsa_q_norm[l], nsa_k_norm[l], cmp_pe[l], cmp_w1[l], cmp_w2[l], b, s)
        bias_full = jnp.repeat(sgu_b[l].T, SGU_WIDTH // SGU_GROUPS, axis=1)
        ob = sgu(zs, sgu_ln_g[l], sgu_ln_b[l], sgu_w[l], bias_full)
        cg = POOL_WIDTH // len(POOL_WINDOWS)
        w_bd = jnp.zeros((POOL_WIDTH, POOL_WIDTH), F32)
        for gi in range(len(POOL_WINDOWS)):
            w_bd = w_bd.at[gi * cg:(gi + 1) * cg, gi * cg:(gi + 1) * cg].set(pool_w[l, gi])
        oc = pool(zp.reshape(b, s, POOL_WIDTH), w_bd.astype(BF16), pool_scale[l])
        x = merge(x, oa, ob.reshape(b, s, SGU_WIDTH), oc, zm.reshape(b, s, 3 * d),
                  lift_a[l].astype(BF16), lift_b[l].astype(BF16), lift_c[l].astype(BF16),
                  w_out[l].astype(BF16))

        mk, mv = mem_kv(mem, mem_norm[l], xa_wk[l].astype(BF16), xa_wv[l].astype(BF16), xa_k_norm[l])
        x = xattn(x, xa_norm[l], xa_wq[l].astype(BF16), xa_q_norm[l], mk, mv, xa_wo[l].astype(BF16))

        k1p = jnp.concatenate([peer_keys1[l].astype(BF16), zeros_half], axis=1)
        k2p = jnp.concatenate([zeros_half, peer_keys2[l].astype(BF16)], axis=1)
        x = peer(x.reshape(t, d), ffn_norm[l], peer_wq[l].astype(BF16), k1p, k2p,
                 peer_u[l].astype(BF16), peer_v[l].T.astype(BF16)).reshape(b, s, d)
    return x
```

```python
import functools

import jax
import jax.numpy as jnp
from jax import lax
from jax.experimental import pallas as pl
from jax.experimental.pallas import tpu as pltpu

F32 = jnp.float32
BF16 = jnp.bfloat16

EPS = 1e-6
LOG2E = 1.4426950408889634
NEG = -1e30
REMOVED = -3e38

D_MODEL = 1024
DK = 64
NSA_HEADS = 8
NSA_GROUPS = 2
NSA_REP = NSA_HEADS // NSA_GROUPS
L_CMP, D_CMP = 32, 16
CMP_HIDDEN = 128
L_SLC = 64
N_SEL = 16
WINDOW = 512
QB = 128
FORCE_BONUS = 1.0e3
SGU_WIDTH = 256
SGU_GROUPS = 4
SGU_CHUNK = 128
POOL_WIDTH = 256
POOL_WINDOWS = (2, 4, 8, 16)
XA_HEADS, XA_HD = 4, 128
XA_WIDTH = XA_HEADS * XA_HD
PEER_HEADS = 8
PEER_KEYS = 128
PEER_TOPK = 16
PEER_HALF = 64

LANES = 128
SEL_VR = 2 * DK
SEL_SUB = 512
CMP_CHUNK = 256
CMP_HALO = 8
PEER_TT = 512
PEER_CH = 1024
PEER_PROJ_ROWS = 256
VMEM_LIMIT = 56 * 1024 * 1024


def _cp(sem):
    return pltpu.CompilerParams(dimension_semantics=sem, vmem_limit_bytes=VMEM_LIMIT)


def _gelu(x):
    return 0.5 * x * (1.0 + jnp.tanh(0.7978845608028654 * (x + 0.044715 * (x * x * x))))


def _rms(x, g):
    return x * lax.rsqrt(jnp.mean(x * x, axis=-1, keepdims=True) + EPS) * g


def _dot(a, b):
    return jnp.dot(a, b, preferred_element_type=F32)


def _dot_nt(a, b):
    return lax.dot_general(a, b, (((1,), (1,)), ((), ())), preferred_element_type=F32)


def _iota(shape, dim):
    return lax.broadcasted_iota(jnp.int32, shape, dim)


def _div_pow2(x, n):
    assert n & (n - 1) == 0
    return lax.shift_right_logical(x, jnp.int32(n.bit_length() - 1))


def _mod_pow2(x, n):
    assert n & (n - 1) == 0
    return x & (n - 1)


def _norm_matmul_kernel(x_ref, g_ref, w_ref, *o_refs, splits):
    hb = _rms(x_ref[...], g_ref[...]).astype(BF16)
    for o_ref, (a, b) in zip(o_refs, splits):
        o_ref[...] = _dot(hb, w_ref[:, a:b]).astype(o_ref.dtype)


def norm_matmul(x, g, w, splits, tm, name):
    t, d = x.shape
    n = w.shape[1]
    return pl.pallas_call(
        functools.partial(_norm_matmul_kernel, splits=splits),
        grid=(t // tm,),
        in_specs=[pl.BlockSpec((tm, d), lambda i: (i, 0)),
                  pl.BlockSpec((1, d), lambda i: (0, 0)),
                  pl.BlockSpec((d, n), lambda i: (0, 0))],
        out_specs=[pl.BlockSpec((tm, b - a), lambda i: (i, 0)) for a, b in splits],
        out_shape=[jax.ShapeDtypeStruct((t, b - a), F32) for a, b in splits],
        compiler_params=_cp(("parallel",)),
        name=name,
    )(x, g.reshape(1, d), w)


def _nsa_prep_kernel(zq_ref, zkv_ref, qn_ref, kn_ref,
                     q_out, kvc_out, ks_out, vs_out, kw_out, vw_out, *, ts):
    zq = zq_ref[0]
    scale = DK ** -0.5 * LOG2E
    for h in range(NSA_HEADS):
        q_out[0, h] = (_rms(zq[:, h * DK:(h + 1) * DK], qn_ref[...]) * scale).astype(BF16)
    zkv = zkv_ref[0]

    def piece(i, g):
        o = i * NSA_GROUPS * DK + g * DK
        return zkv[:, o:o + DK]

    ones_col = jnp.where(_iota((ts, SEL_VR - DK), 1) == 0, 1.0, 0.0)
    for g in range(NSA_GROUPS):
        kvc_out[0, 0, g] = piece(0, g)
        kvc_out[1, 0, g] = piece(1, g)
        ks_out[0, g] = _rms(piece(2, g), kn_ref[1:2, :]).astype(BF16)
        vs_out[0, g] = jnp.concatenate([piece(3, g), ones_col], axis=-1).T.astype(BF16)
        kw_out[0, g] = _rms(piece(4, g), kn_ref[2:3, :]).astype(BF16)
        vw_out[0, g] = jnp.concatenate([piece(5, g), ones_col], axis=-1).T.astype(BF16)


def nsa_prep(zq, zkv, q_norm, k_norm, ts=512):
    b, s, _ = zq.shape
    g = NSA_GROUPS
    hm = lambda bi, si: (bi, 0, si, 0)
    return pl.pallas_call(
        functools.partial(_nsa_prep_kernel, ts=ts),
        grid=(b, s // ts),
        in_specs=[pl.BlockSpec((1, ts, NSA_HEADS * DK), lambda bi, si: (bi, si, 0)),
                  pl.BlockSpec((1, ts, 6 * g * DK), lambda bi, si: (bi, si, 0)),
                  pl.BlockSpec((1, DK), lambda bi, si: (0, 0)),
                  pl.BlockSpec((3, DK), lambda bi, si: (0, 0))],
        out_specs=[pl.BlockSpec((1, NSA_HEADS, ts, DK), hm),
                   pl.BlockSpec((2, 1, g, ts, DK), lambda bi, si: (0, bi, 0, si, 0)),
                   pl.BlockSpec((1, g, ts, DK), hm),
                   pl.BlockSpec((1, g, SEL_VR, ts), lambda bi, si: (bi, 0, 0, si)),
                   pl.BlockSpec((1, g, ts, DK), hm),
                   pl.BlockSpec((1, g, SEL_VR, ts), lambda bi, si: (bi, 0, 0, si))],
        out_shape=[jax.ShapeDtypeStruct((b, NSA_HEADS, s, DK), BF16),
                   jax.ShapeDtypeStruct((2, b, g, s, DK), F32),
                   jax.ShapeDtypeStruct((b, g, s, DK), BF16),
                   jax.ShapeDtypeStruct((b, g, SEL_VR, s), BF16),
                   jax.ShapeDtypeStruct((b, g, s, DK), BF16),
                   jax.ShapeDtypeStruct((b, g, SEL_VR, s), BF16)],
        compiler_params=_cp(("parallel", "parallel")),
        name="nsa_prep",
    )(zq, zkv, q_norm.reshape(1, DK), k_norm)


def _cmp_mlp_kernel(x_ref, pe_ref, w1_ref, w2_ref, kn_ref, k_out, vt_out, *, nc):
    c = pl.program_id(2)
    x = x_ref[0, 0, 0]
    a = _dot((x + pe_ref[0, 0:1, :]).astype(BF16), w1_ref[0, 0])
    b = _dot((x + pe_ref[0, 1:2, :]).astype(BF16), w1_ref[0, 1])
    pre = a + pltpu.roll(b, nc - 1, 0)
    comp = _dot(_gelu(pre).astype(BF16), w2_ref[0])

    @pl.when(c == 0)
    def _():
        k_out[0, 0] = _rms(comp, kn_ref[...]).astype(BF16)

    @pl.when(c == 1)
    def _():
        ones_col = jnp.where(_iota((nc, SEL_VR - DK), 1) == 0, 1.0, 0.0)
        vt_out[0, 0] = jnp.concatenate([comp, ones_col], axis=-1).T.astype(BF16)


def cmp_mlp(xc, pe, w1, w2, k_norm0):
    _, b, g, nc, width = xc.shape
    return pl.pallas_call(
        functools.partial(_cmp_mlp_kernel, nc=nc),
        grid=(b, g, 2),
        in_specs=[pl.BlockSpec((1, 1, 1, nc, width), lambda bi, gi, c: (c, bi, gi, 0, 0)),
                  pl.BlockSpec((1, 2, width), lambda bi, gi, c: (c, 0, 0)),
                  pl.BlockSpec((1, 2, width, CMP_HIDDEN), lambda bi, gi, c: (c, 0, 0, 0)),
                  pl.BlockSpec((1, CMP_HIDDEN, DK), lambda bi, gi, c: (c, 0, 0)),
                  pl.BlockSpec((1, DK), lambda bi, gi, c: (0, 0))],
        out_specs=[pl.BlockSpec((1, 1, nc, DK), lambda bi, gi, c: (bi, gi, 0, 0)),
                   pl.BlockSpec((1, 1, SEL_VR, nc), lambda bi, gi, c: (bi, gi, 0, 0))],
        out_shape=[jax.ShapeDtypeStruct((b, g, nc, DK), BF16),
                   jax.ShapeDtypeStruct((b, g, SEL_VR, nc), BF16)],
        compiler_params=_cp(("parallel", "parallel", "arbitrary")),
        name="cmp_mlp",
    )(xc, pe, w1, w2, k_norm0.reshape(1, DK))


def _cmp_attn_kernel(q_ref, k_ref, vt_ref, oc_ref, sb_ref, s_s, acc_s, ps_s, *, nc, n_slc):
    qb = pl.program_id(2)
    rows = NSA_REP * QB
    ck = CMP_CHUNK
    q = q_ref[0].reshape(rows, DK)
    n_chunk = ((qb * QB + QB) // D_CMP + ck - 1) // ck
    t = qb * QB + _mod_pow2(_iota((ck, rows), 1), QB)

    def pass_a(ci, m):
        off = pl.multiple_of(ci * ck, ck)
        st = _dot_nt(k_ref[0, 0, pl.ds(off, ck), :], q)
        n = off + _iota((ck, rows), 0)
        st = jnp.where(n * D_CMP + (L_CMP - 1) <= t, st, NEG)
        s_s[ci] = st
        return jnp.maximum(m, jnp.max(st, axis=0, keepdims=True))

    m = lax.fori_loop(0, n_chunk, pass_a, jnp.full((1, rows), NEG, F32))
    acc_s[...] = jnp.zeros_like(acc_s)
    ps_s[...] = jnp.zeros_like(ps_s)

    def pass_b(ci, den):
        off = pl.multiple_of(ci * ck, ck)
        e = jnp.exp2(s_s[ci] - m)
        s_s[ci] = e
        acc_s[...] += _dot(vt_ref[0, 0, :, pl.ds(off, ck)], e.astype(BF16))
        return den + jnp.sum(e, axis=0, keepdims=True)

    den = lax.fori_loop(0, n_chunk, pass_b, jnp.zeros((1, rows), F32))
    seen = m > 0.5 * NEG
    inv_o = jnp.where(seen, 1.0 / jnp.maximum(acc_s[DK:DK + 1, :], 1e-30), 0.0)
    inv_p = jnp.where(seen, 1.0 / jnp.maximum(den, 1e-30), 0.0)
    oc_ref[0] = (acc_s[...] * inv_o).T[:, 0:DK].reshape(NSA_REP, QB, DK)

    def pass_c(ci, carry):
        off = pl.multiple_of(ci * ck, ck)
        p = s_s[ci] * inv_p
        ps = p[:, 0:QB]
        for r in range(1, NSA_REP):
            ps = ps + p[:, r * QB:(r + 1) * QB]
        ps_s[pl.ds(CMP_HALO + off, ck), :] = ps
        return carry

    lax.fori_loop(0, n_chunk, pass_c, 0)

    ratio = L_SLC // D_CMP
    imp = jnp.zeros((n_slc, QB), F32)
    for k in range(-(L_CMP // D_CMP - 1), ratio):
        overlap = min(k * D_CMP + L_CMP, L_SLC) - max(k * D_CMP, 0)
        imp = imp + (overlap / D_CMP) * ps_s[pl.ds(CMP_HALO + k, n_slc, stride=ratio), :]

    cur = _div_pow2(qb * QB + _iota((n_slc, QB), 1), L_SLC)
    j = _iota((n_slc, QB), 0)
    forced = (j == 0) | (j == cur) | (j == cur - 1)
    score = jnp.where(j <= cur, imp + jnp.where(forced, FORCE_BONUS, 0.0), NEG)
    work = score
    tau = None
    for _ in range(N_SEL):
        tau = jnp.max(work, axis=0, keepdims=True)
        work = jnp.where(work >= tau, REMOVED, work)
    selected = (score > 0.5 * NEG) & (score >= tau)
    sb_ref[0, 0, 0] = jnp.where(selected, 0.0, NEG)


def cmp_attn_select(qn, kc, vct):
    b, h, s, _ = qn.shape
    g = NSA_GROUPS
    nc = kc.shape[2]
    n_slc = s // L_SLC
    nqb = s // QB
    return pl.pallas_call(
        functools.partial(_cmp_attn_kernel, nc=nc, n_slc=n_slc),
        grid=(b, g, nqb),
        in_specs=[pl.BlockSpec((1, NSA_REP, QB, DK), lambda bi, gi, qi: (bi, gi, qi, 0)),
                  pl.BlockSpec((1, 1, nc, DK), lambda bi, gi, qi: (bi, gi, 0, 0)),
                  pl.BlockSpec((1, 1, SEL_VR, nc), lambda bi, gi, qi: (bi, gi, 0, 0))],
        out_specs=[pl.BlockSpec((1, NSA_REP, QB, DK), lambda bi, gi, qi: (bi, gi, qi, 0)),
                   pl.BlockSpec((1, 1, 1, n_slc, QB), lambda bi, gi, qi: (bi, gi, qi, 0, 0))],
        out_shape=[jax.ShapeDtypeStruct((b, h, s, DK), F32),
                   jax.ShapeDtypeStruct((b, g, nqb, n_slc, QB), F32)],
        scratch_shapes=[pltpu.VMEM((nc // CMP_CHUNK, CMP_CHUNK, NSA_REP * QB), F32),
                        pltpu.VMEM((SEL_VR, NSA_REP * QB), F32),
                        pltpu.VMEM((CMP_HALO + nc, QB), F32)],
        compiler_params=_cp(("parallel", "parallel", "parallel")),
        name="cmp_attn_select",
    )(qn, kc, vct)


def _win_attn_kernel(q_ref, *refs):
    nkb = WINDOW // QB + 1
    k_refs, v_refs, o_ref = refs[:nkb], refs[nkb:2 * nkb], refs[2 * nkb]
    qb = pl.program_id(2)
    rows = NSA_REP * QB
    nk = nkb * QB
    q = q_ref[0].reshape(rows, DK)
    k = jnp.concatenate([r[0, 0] for r in k_refs], axis=0)
    vt = jnp.concatenate([r[0, 0] for r in v_refs], axis=-1)
    st = _dot_nt(k, q)
    t = qb * QB + _mod_pow2(_iota((nk, rows), 1), QB)
    kpos = (qb - (nkb - 1)) * QB + _iota((nk, rows), 0)
    delta = t - kpos
    valid = (delta >= 0) & (delta < WINDOW) & (kpos >= 0)
    st = jnp.where(valid, st, NEG)
    e = jnp.exp2(st - jnp.max(st, axis=0, keepdims=True)).astype(BF16)
    acc = _dot(vt, e)
    o = (acc / jnp.maximum(acc[DK:DK + 1, :], 1e-30)).T[:, 0:DK]
    o_ref[0] = o.reshape(NSA_REP, QB, DK)


def win_attn(qn, kw, vw):
    b, h, s, _ = qn.shape
    g = NSA_GROUPS
    nkb = WINDOW // QB + 1
    nqb = s // QB

    def k_spec(i):
        return pl.BlockSpec((1, 1, QB, DK),
                            lambda bi, gi, qi: (bi, gi, jnp.maximum(qi - (nkb - 1) + i, 0), 0))

    def vt_spec(i):
        return pl.BlockSpec((1, 1, SEL_VR, QB),
                            lambda bi, gi, qi: (bi, gi, 0, jnp.maximum(qi - (nkb - 1) + i, 0)))

    return pl.pallas_call(
        _win_attn_kernel,
        grid=(b, g, nqb),
        in_specs=[pl.BlockSpec((1, NSA_REP, QB, DK), lambda bi, gi, qi: (bi, gi, qi, 0))]
        + [k_spec(i) for i in range(nkb)] + [vt_spec(i) for i in range(nkb)],
        out_specs=pl.BlockSpec((1, NSA_REP, QB, DK), lambda bi, gi, qi: (bi, gi, qi, 0)),
        out_shape=jax.ShapeDtypeStruct((b, h, s, DK), F32),
        compiler_params=_cp(("parallel", "parallel", "parallel")),
        name="win_attn",
    )(qn, *([kw] * nkb), *([vw] * nkb))


def _sel_attn_kernel(q_ref, sb_ref, k_ref, vt_ref, oc_ref, ow_ref, gl_ref, o_ref,
                     m_s, acc_s, s0_s, s1_s, p0_s, p1_s, a0_s, a1_s):
    qb = pl.program_id(2)
    rows = NSA_REP * QB
    blocks = SEL_SUB // L_SLC
    qv = q_ref[0].reshape(rows, DK)
    nsub = (qb * QB + QB + SEL_SUB - 1) // SEL_SUB
    m_s[...] = jnp.full_like(m_s, NEG)
    acc_s[...] = jnp.zeros_like(acc_s)

    def scores(i, s_ref):
        off = pl.multiple_of(i * SEL_SUB, SEL_SUB)
        s_ref[...] = _dot_nt(k_ref[0, 0, pl.ds(off, SEL_SUB), :], qv)

    def weights(i, s_ref, p_ref, a_ref, causal):
        bias = sb_ref[0, 0, 0, pl.ds(pl.multiple_of(i * blocks, blocks), blocks), :]
        bias = jnp.broadcast_to(bias[:, None, :], (blocks, L_SLC, QB)).reshape(SEL_SUB, QB)
        st = s_ref[...] + jnp.concatenate([bias] * NSA_REP, axis=1)
        if causal:
            kpos = i * SEL_SUB + _iota((SEL_SUB, rows), 0)
            t = qb * QB + _mod_pow2(_iota((SEL_SUB, rows), 1), QB)
            st = jnp.where(kpos <= t, st, NEG)
        m_old = m_s[...]
        m_new = jnp.maximum(m_old, jnp.max(st, axis=0, keepdims=True))
        p_ref[...] = jnp.exp2(st - m_new).astype(BF16)
        a_ref[...] = jnp.exp2(m_old - m_new)
        m_s[...] = m_new

    def accumulate(i, p_ref, a_ref):
        off = pl.multiple_of(i * SEL_SUB, SEL_SUB)
        acc_s[...] = a_ref[...] * acc_s[...] + _dot(vt_ref[0, 0, :, pl.ds(off, SEL_SUB)], p_ref[...])

    p1_s[...] = jnp.zeros_like(p1_s)
    a1_s[...] = jnp.ones_like(a1_s)
    scores(0, s0_s)
    n_pairs = (nsub - 1) // 2
    odd_tail = (nsub - 1) - 2 * n_pairs

    def pair(j, carry):
        scores(2 * j + 1, s1_s)
        weights(2 * j, s0_s, p0_s, a0_s, False)
        accumulate(jnp.maximum(2 * j - 1, 0), p1_s, a1_s)
        scores(2 * j + 2, s0_s)
        weights(2 * j + 1, s1_s, p1_s, a1_s, False)
        accumulate(2 * j, p0_s, a0_s)
        return carry

    lax.fori_loop(0, n_pairs, pair, 0)
    e = 2 * n_pairs

    @pl.when(odd_tail == 0)
    def _():
        weights(e, s0_s, p0_s, a0_s, True)
        accumulate(jnp.maximum(e - 1, 0), p1_s, a1_s)
        accumulate(e, p0_s, a0_s)

    @pl.when(odd_tail == 1)
    def _():
        scores(e + 1, s1_s)
        weights(e, s0_s, p0_s, a0_s, False)
        accumulate(jnp.maximum(e - 1, 0), p1_s, a1_s)
        weights(e + 1, s1_s, p1_s, a1_s, True)
        accumulate(e, p0_s, a0_s)
        accumulate(e + 1, p1_s, a1_s)

    o_s = (acc_s[...] / jnp.maximum(acc_s[DK:DK + 1, :], 1e-30)).T[:, 0:DK]
    gate = jax.nn.sigmoid(gl_ref[0].reshape(rows, 3))
    out = (gate[:, 0:1] * oc_ref[0].reshape(rows, DK) + gate[:, 1:2] * o_s
           + gate[:, 2:3] * ow_ref[0].reshape(rows, DK))
    o_ref[0] = out.reshape(NSA_REP, QB, DK)


def sel_attn(qn, selb, ks, vs, oc, ow, gl):
    b, h, s, _ = qn.shape
    g = NSA_GROUPS
    nqb = s // QB
    n_slc = s // L_SLC
    qmap = lambda bi, gi, qi: (bi, gi, qi, 0)
    return pl.pallas_call(
        _sel_attn_kernel,
        grid=(b, g, nqb),
        in_specs=[pl.BlockSpec((1, NSA_REP, QB, DK), qmap),
                  pl.BlockSpec((1, 1, 1, n_slc, QB), lambda bi, gi, qi: (bi, gi, qi, 0, 0)),
                  pl.BlockSpec((1, 1, s, DK), lambda bi, gi, qi: (bi, gi, 0, 0)),
                  pl.BlockSpec((1, 1, SEL_VR, s), lambda bi, gi, qi: (bi, gi, 0, 0)),
                  pl.BlockSpec((1, NSA_REP, QB, DK), qmap),
                  pl.BlockSpec((1, NSA_REP, QB, DK), qmap),
                  pl.BlockSpec((1, NSA_REP, QB, 3), qmap)],
        out_specs=pl.BlockSpec((1, NSA_REP, QB, DK), qmap),
        out_shape=jax.ShapeDtypeStruct((b, h, s, DK), F32),
        scratch_shapes=[pltpu.VMEM((1, NSA_REP * QB), F32),
                        pltpu.VMEM((SEL_VR, NSA_REP * QB), F32),
                        pltpu.VMEM((SEL_SUB, NSA_REP * QB), F32),
                        pltpu.VMEM((SEL_SUB, NSA_REP * QB), F32),
                        pltpu.VMEM((SEL_SUB, NSA_REP * QB), BF16),
                        pltpu.VMEM((SEL_SUB, NSA_REP * QB), BF16),
                        pltpu.VMEM((1, NSA_REP * QB), F32),
                        pltpu.VMEM((1, NSA_REP * QB), F32)],
        compiler_params=_cp(("parallel", "parallel", "parallel")),
        name="sel_attn",
    )(qn, selb, ks, vs, oc, ow, gl)


def _sgu_kernel(z_ref, g_ref, b_ref, w_ref, bias_ref, o_ref, *, tm):
    z = _gelu(z_ref[...])
    u = z[:, :SGU_WIDTH]
    v = z[:, SGU_WIDTH:]
    mu = jnp.mean(v, axis=-1, keepdims=True)
    var = jnp.mean(jnp.square(v - mu), axis=-1, keepdims=True)
    vn = ((v - mu) * lax.rsqrt(var + EPS) * g_ref[...] + b_ref[...]).astype(BF16)
    tril = _iota((SGU_CHUNK, SGU_CHUNK), 1) <= _iota((SGU_CHUNK, SGU_CHUNK), 0)
    wm = [jnp.where(tril, w_ref[gi], 0.0).astype(BF16) for gi in range(SGU_GROUPS)]
    grp = _div_pow2(_iota((SGU_CHUNK, SGU_WIDTH), 1), SGU_WIDTH // SGU_GROUPS)
    for ch in range(tm // SGU_CHUNK):
        rs = slice(ch * SGU_CHUNK, (ch + 1) * SGU_CHUNK)
        mixed = bias_ref[...]
        for gi in range(SGU_GROUPS):
            mixed = mixed + jnp.where(grp == gi, _dot(wm[gi], vn[rs]), 0.0)
        o_ref[rs, :] = u[rs] * mixed


def sgu(z, ln_g, ln_b, w, bias_full, tm=512):
    t = z.shape[0]
    return pl.pallas_call(
        functools.partial(_sgu_kernel, tm=tm),
        grid=(t // tm,),
        in_specs=[pl.BlockSpec((tm, 2 * SGU_WIDTH), lambda i: (i, 0)),
                  pl.BlockSpec((1, SGU_WIDTH), lambda i: (0, 0)),
                  pl.BlockSpec((1, SGU_WIDTH), lambda i: (0, 0)),
                  pl.BlockSpec((SGU_GROUPS, SGU_CHUNK, SGU_CHUNK), lambda i: (0, 0, 0)),
                  pl.BlockSpec((SGU_CHUNK, SGU_WIDTH), lambda i: (0, 0))],
        out_specs=pl.BlockSpec((tm, SGU_WIDTH), lambda i: (i, 0)),
        out_shape=jax.ShapeDtypeStruct((t, SGU_WIDTH), F32),
        compiler_params=_cp(("parallel",)),
        name="sgu",
    )(z, ln_g.reshape(1, -1), ln_b.reshape(1, -1), w, bias_full)


def _pool_kernel(p_ref, h_ref, w_ref, sc_ref, o_ref, *, tp):
    i = pl.program_id(1)
    halo_rows = POOL_WINDOWS[-1]
    p = p_ref[0]
    halo = jnp.where(i > 0, h_ref[0], 0.0)
    ext = jnp.concatenate([halo, p], axis=0)
    sums = [ext]
    shift = 1
    for _ in POOL_WINDOWS:
        prev = sums[-1]
        sums.append(prev + pltpu.roll(prev, shift, 0))
        shift *= 2
    pos = i * tp + _iota((tp, POOL_WIDTH), 0)
    grp = _div_pow2(_iota((tp, POOL_WIDTH), 1), POOL_WIDTH // len(POOL_WINDOWS))
    d = jnp.zeros((tp, POOL_WIDTH), F32)
    for gi, w in enumerate(POOL_WINDOWS):
        mean = sums[gi + 1][halo_rows:halo_rows + tp] / jnp.minimum(pos + 1, w).astype(F32)
        d = jnp.where(grp == gi, mean, d)
    d = d - p
    o_ref[0] = _dot(d.astype(BF16), w_ref[...]) * sc_ref[...]


def pool(p, w_bd, scale, tp=512):
    b, s, c = p.shape
    halo_rows = POOL_WINDOWS[-1]
    return pl.pallas_call(
        functools.partial(_pool_kernel, tp=tp),
        grid=(b, s // tp),
        in_specs=[pl.BlockSpec((1, tp, c), lambda bi, i: (bi, i, 0)),
                  pl.BlockSpec((1, halo_rows, c),
                               lambda bi, i: (bi, jnp.maximum(i * (tp // halo_rows) - 1, 0), 0)),
                  pl.BlockSpec((c, c), lambda bi, i: (0, 0)),
                  pl.BlockSpec((1, c), lambda bi, i: (0, 0))],
        out_specs=pl.BlockSpec((1, tp, c), lambda bi, i: (bi, i, 0)),
        out_shape=jax.ShapeDtypeStruct((b, s, c), F32),
        compiler_params=_cp(("parallel", "parallel")),
        name="pool",
    )(p, p, w_bd, scale.reshape(1, c))


def _merge_kernel(x_ref, oa_ref, ob_ref, oc_ref, zm_ref, la_ref, lb_ref, lc_ref, wo_ref, o_ref):
    d = D_MODEL
    oa = jnp.concatenate([oa_ref[0, h].astype(BF16) for h in range(NSA_HEADS)], axis=-1)
    zm = zm_ref[0]
    merged = (jax.nn.sigmoid(zm[:, 0:d]) * _dot(oa, la_ref[...])
              + jax.nn.sigmoid(zm[:, d:2 * d]) * _dot(ob_ref[0].astype(BF16), lb_ref[...])
              + jax.nn.sigmoid(zm[:, 2 * d:3 * d]) * _dot(oc_ref[0].astype(BF16), lc_ref[...]))
    o_ref[0] = x_ref[0] + _dot(merged.astype(BF16), wo_ref[...])


def merge(x, oa, ob, oc, zm, la, lb, lc, wo, tm=512):
    b, s, d = x.shape
    row = lambda bi, i: (bi, i, 0)
    full = lambda bi, i: (0, 0)
    return pl.pallas_call(
        _merge_kernel,
        grid=(b, s // tm),
        in_specs=[pl.BlockSpec((1, tm, d), row),
                  pl.BlockSpec((1, NSA_HEADS, tm, DK), lambda bi, i: (bi, 0, i, 0)),
                  pl.BlockSpec((1, tm, SGU_WIDTH), row),
                  pl.BlockSpec((1, tm, POOL_WIDTH), row),
                  pl.BlockSpec((1, tm, 3 * d), row),
                  pl.BlockSpec(la.shape, full), pl.BlockSpec(lb.shape, full),
                  pl.BlockSpec(lc.shape, full), pl.BlockSpec(wo.shape, full)],
        out_specs=pl.BlockSpec((1, tm, d), row),
        out_shape=jax.ShapeDtypeStruct((b, s, d), F32),
        compiler_params=_cp(("parallel", "parallel")),
        name="merge",
    )(x, oa, ob, oc, zm, la, lb, lc, wo)


def _mem_kv_kernel(m_ref, g_ref, wk_ref, wv_ref, kn_ref, k_out, v_out):
    mh = _rms(m_ref[0], g_ref[...]).astype(BF16)
    k = _dot(mh, wk_ref[...])
    for h in range(XA_HEADS):
        hs = slice(h * XA_HD, (h + 1) * XA_HD)
        k_out[0, :, hs] = _rms(k[:, hs], kn_ref[...]).astype(BF16)
    v_out[0] = _dot(mh, wv_ref[...]).astype(BF16)


def mem_kv(mem, g, wk, wv, k_norm):
    b, m, d = mem.shape
    full = lambda bi: (0, 0)
    return pl.pallas_call(
        _mem_kv_kernel,
        grid=(b,),
        in_specs=[pl.BlockSpec((1, m, d), lambda bi: (bi, 0, 0)),
                  pl.BlockSpec((1, d), full),
                  pl.BlockSpec(wk.shape, full), pl.BlockSpec(wv.shape, full),
                  pl.BlockSpec((1, XA_HD), full)],
        out_specs=[pl.BlockSpec((1, m, XA_WIDTH), lambda bi: (bi, 0, 0))] * 2,
        out_shape=[jax.ShapeDtypeStruct((b, m, XA_WIDTH), BF16)] * 2,
        compiler_params=_cp(("parallel",)),
        name="mem_kv",
    )(mem, g.reshape(1, d), wk, wv, k_norm.reshape(1, XA_HD))


def _xattn_kernel(x_ref, g_ref, wq_ref, qn_ref, k_ref, v_ref, wo_ref, o_ref):
    x = x_ref[0]
    q = _dot(_rms(x, g_ref[...]).astype(BF16), wq_ref[...])
    k = k_ref[0]
    v = v_ref[0]
    outs = []
    for h in range(XA_HEADS):
        hs = slice(h * XA_HD, (h + 1) * XA_HD)
        qh = (_rms(q[:, hs], qn_ref[...])).astype(BF16)
        s = _dot_nt(qh, k[:, hs]) * (XA_HD ** -0.5)
        e = jnp.exp(s - jnp.max(s, axis=-1, keepdims=True))
        p = e / jnp.sum(e, axis=-1, keepdims=True)
        outs.append(_dot(p.astype(BF16), v[:, hs]).astype(BF16))
    o = jnp.concatenate(outs, axis=-1)
    o_ref[0] = x + _dot(o, wo_ref[...])


def xattn(x, g, wq, q_norm, k, v, wo, tm=512):
    b, s, d = x.shape
    m = k.shape[1]
    row = lambda bi, i: (bi, i, 0)
    full = lambda bi, i: (0, 0)
    return pl.pallas_call(
        _xattn_kernel,
        grid=(b, s // tm),
        in_specs=[pl.BlockSpec((1, tm, d), row),
                  pl.BlockSpec((1, d), full),
                  pl.BlockSpec(wq.shape, full),
                  pl.BlockSpec((1, XA_HD), full),
                  pl.BlockSpec((1, m, XA_WIDTH), lambda bi, i: (bi, 0, 0)),
                  pl.BlockSpec((1, m, XA_WIDTH), lambda bi, i: (bi, 0, 0)),
                  pl.BlockSpec(wo.shape, full)],
        out_specs=pl.BlockSpec((1, tm, d), row),
        out_shape=jax.ShapeDtypeStruct((b, s, d), F32),
        compiler_params=_cp(("parallel", "parallel")),
        name="xattn",
    )(x, g.reshape(1, d), wq, q_norm.reshape(1, XA_HD), k, v, wo)


def _top_rows(s, n, want_rank):
    out = []
    work = s
    rank = jnp.full(s.shape, float(s.shape[0] - 1), F32) if want_rank else None
    for i in range(n):
        m = jnp.max(work, axis=0, keepdims=True)
        out.append(m)
        hit = work >= m
        if want_rank:
            rank = jnp.where(hit, float(i), rank)
        work = jnp.where(hit, REMOVED, work)
    return out, rank


def _stack_rows(rows, pad_rows):
    tt = rows[0].shape[1]
    rowi = _iota((pad_rows, tt), 0)
    out = jnp.full((pad_rows, tt), REMOVED, F32)
    for i, r in enumerate(rows[:pad_rows]):
        out = jnp.where(rowi == i, r, out)
    return out


def _pair_sum_candidates(v1, v2, tt):
    row_full = _iota((24, tt), 0)
    row = _iota((8, tt), 0)
    v2_full = _stack_rows(v2, 24)
    v2_8 = _stack_rows(v2, 8)

    def shifted(k):
        return pltpu.roll(v2_8, k, 0)

    def pick(rows_from, first):
        out = v1[rows_from]
        for r in range(first + 1, 8):
            if rows_from + r - first < len(v1):
                out = jnp.where(row == r, v1[rows_from + r - first], out)
        return out

    return jnp.concatenate([
        jnp.where(row_full < 17, v1[0] + v2_full, REMOVED),
        v1[1] + v2_8,
        jnp.where(row < 5, v1[2] + v2_8,
                  jnp.where(row < 7, v1[5] + shifted(5), v1[8] + shifted(7))),
        jnp.where(row < 4, v1[3] + v2_8,
                  jnp.where(row < 7, v1[4] + shifted(4), v1[9] + shifted(7))),
        jnp.where(row < 2, v1[6] + v2_8,
                  jnp.where(row < 4, v1[7] + shifted(2), pick(10, 4) + v2[0])),
        jnp.where(row < 3, pick(14, 0) + v2[0], REMOVED),
    ], axis=0)


def _peer_kernel(x_ref, g_ref, wq_ref, k1_ref, k2_ref, u_ref, vt_ref, o_ref,
                 hbt_s, q_s, r2_s, na_s, e1_s, e2_s, acc_s, *, tt, ch):
    c = pl.program_id(1)
    ntop = PEER_TOPK + 1
    assert ntop == 17

    @pl.when(c == 0)
    def _():
        h = _rms(x_ref[...], g_ref[...])
        hbt_s[...] = h.T.astype(BF16)
        q = _dot(h.astype(BF16), wq_ref[...])
        for hd in range(PEER_HEADS):
            q_s[hd] = q[:, hd * 2 * PEER_HALF:(hd + 1) * 2 * PEER_HALF].astype(BF16)
        acc_s[...] = jnp.zeros_like(acc_s)

        def route(hd, carry):
            qh = q_s[hd]
            s1 = _dot_nt(k1_ref[...], qh)
            s2 = _dot_nt(k2_ref[...], qh)
            v1, _ = _top_rows(s1, ntop, False)
            v2, rank2 = _top_rows(s2, ntop, True)
            cand = _pair_sum_candidates(v1, v2, tt)
            top = v1[0] + v2[0]
            z = jnp.zeros((1, tt), F32)
            c_prev = top
            c_last = top
            work = cand
            for r in range(ntop):
                m = jnp.max(work, axis=0, keepdims=True)
                work = jnp.where(work >= m, REMOVED, work)
                if r < PEER_TOPK:
                    z = z + jnp.exp(m - top)
                c_prev, c_last = c_last, m
            thr = 0.5 * (c_prev + c_last)
            na = jnp.zeros((PEER_KEYS, tt), F32)
            for j in range(ntop):
                na = na + jnp.where(s1 + v2[j] >= thr, 1.0, 0.0)
            r2_s[hd] = rank2.astype(BF16)
            na_s[hd] = na
            e1_s[hd] = jnp.exp(s1 - v1[0])
            e2_s[hd] = (jnp.exp(s2 - v2[0]) / z).astype(BF16)
            return carry

        lax.fori_loop(0, PEER_HEADS, route, 0)

    n_slab = ch // PEER_KEYS
    slabs_per_group = PEER_PROJ_ROWS // PEER_KEYS
    n_group = ch // PEER_PROJ_ROWS

    def project(gi):
        rs = slice(gi * PEER_PROJ_ROWS, (gi + 1) * PEER_PROJ_ROWS)
        return _gelu(_dot(u_ref[rs, :], hbt_s[...]).astype(BF16))

    def gate(al, gl_rows):
        a = c * n_slab + al
        wc = jnp.zeros((PEER_KEYS, tt), BF16)
        for hd in range(PEER_HEADS):
            na = na_s[hd, pl.ds(a, 1), :].astype(BF16)
            e1 = e1_s[hd, pl.ds(a, 1), :].astype(BF16)
            wc = wc + jnp.where(r2_s[hd] < na, e2_s[hd] * e1, jnp.zeros((), BF16))
        return wc * gl_rows

    ys = []
    gl_next = project(0)
    for gi in range(n_group):
        gl = gl_next
        if gi + 1 < n_group:
            gl_next = project(gi + 1)
        for k in range(slabs_per_group):
            ys.append(gate(gi * slabs_per_group + k, gl[k * PEER_KEYS:(k + 1) * PEER_KEYS]))
    y = jnp.concatenate(ys, axis=0)
    acc_s[...] += _dot(vt_ref[...], y)

    @pl.when(c == pl.num_programs(1) - 1)
    def _():
        o_ref[...] = x_ref[...] + acc_s[...].T


def peer(x, g, wq, k1p, k2p, u, vt, tt=PEER_TT, ch=PEER_CH):
    t, d = x.shape
    ne = u.shape[0]
    return pl.pallas_call(
        functools.partial(_peer_kernel, tt=tt, ch=ch),
        grid=(t // tt, ne // ch),
        in_specs=[pl.BlockSpec((tt, d), lambda i, c: (i, 0)),
                  pl.BlockSpec((1, d), lambda i, c: (0, 0)),
                  pl.BlockSpec(wq.shape, lambda i, c: (0, 0)),
                  pl.BlockSpec(k1p.shape, lambda i, c: (0, 0)),
                  pl.BlockSpec(k2p.shape, lambda i, c: (0, 0)),
                  pl.BlockSpec((ch, d), lambda i, c: (c, 0)),
                  pl.BlockSpec((d, ch), lambda i, c: (0, c))],
        out_specs=pl.BlockSpec((tt, d), lambda i, c: (i, 0)),
        out_shape=jax.ShapeDtypeStruct((t, d), F32),
        scratch_shapes=[pltpu.VMEM((d, tt), BF16),
                        pltpu.VMEM((PEER_HEADS, tt, 2 * PEER_HALF), BF16),
                        pltpu.VMEM((PEER_HEADS, PEER_KEYS, tt), BF16),
                        pltpu.VMEM((PEER_HEADS, PEER_KEYS, tt), F32),
                        pltpu.VMEM((PEER_HEADS, PEER_KEYS, tt), F32),
                        pltpu.VMEM((PEER_HEADS, PEER_KEYS, tt), BF16),
                        pltpu.VMEM((d, tt), F32)],
        compiler_params=_cp(("parallel", "arbitrary")),
        name="peer",
    )(x, g.reshape(1, d), wq, k1p, k2p, u, vt)


_IN_WIDTHS = (NSA_HEADS * DK, 6 * NSA_GROUPS * DK, 3 * NSA_HEADS, 2 * SGU_WIDTH, POOL_WIDTH, 3 * D_MODEL)


def _in_proj_layout():
    splits, src = [], []
    o_src = o_dst = 0
    for w in _IN_WIDTHS:
        wp = -(-w // LANES) * LANES
        splits.append((o_dst, o_dst + wp))
        src.append((o_src, o_src + w))
        o_src += w
        o_dst += wp
    return splits, src, o_dst


def nsa_layer(zq, zkv, zg, q_norm, k_norm, cmp_pe, cmp_w1, cmp_w2, b, s):
    qn, kvc, ks, vs, kw, vw = nsa_prep(zq.reshape(b, s, -1), zkv.reshape(b, s, -1), q_norm, k_norm)
    nc = s // D_CMP
    xc = kvc.reshape(2, b, NSA_GROUPS, nc, D_CMP * DK)
    pe = cmp_pe.reshape(2, 2, D_CMP * DK)
    w1 = cmp_w1.reshape(2, 2, D_CMP * DK, CMP_HIDDEN).astype(BF16)
    kc, vct = cmp_mlp(xc, pe, w1, cmp_w2.astype(BF16), k_norm[0])
    oc, selb = cmp_attn_select(qn, kc, vct)
    ow = win_attn(qn, kw, vw)
    gl = zg[:, :3 * NSA_HEADS].reshape(b, s, NSA_HEADS, 3).transpose(0, 2, 1, 3)
    return sel_attn(qn, selb, ks, vs, oc, ow, gl)


def kernel(x, mem, mix_norm, w_in, nsa_q_norm, nsa_k_norm, cmp_pe, cmp_w1, cmp_w2, sgu_ln_g, sgu_ln_b, sgu_w, sgu_b, pool_w, pool_scale, lift_a, lift_b, lift_c, w_out, xa_norm, mem_norm, xa_wq, xa_wk, xa_wv, xa_q_norm, xa_k_norm, xa_wo, ffn_norm, peer_wq, peer_keys1, peer_keys2, peer_u, peer_v):
    b, s, d = x.shape
    t = b * s
    depth = w_in.shape[0]
    splits, src, n_pad = _in_proj_layout()
    zeros_half = jnp.zeros((PEER_KEYS, PEER_HALF), BF16)
    for l in range(depth):
        w_parts = []
        for (a0, a1), (d0, d1) in zip(src, splits):
            w_parts.append(jnp.pad(w_in[l][:, a0:a1], ((0, 0), (0, (d1 - d0) - (a1 - a0)))))
        w_pad = jnp.concatenate(w_parts, axis=1).astype(BF16)
        zq, zkv, zg, zs, zp, zm = norm_matmul(x.reshape(t, d), mix_norm[l], w_pad, splits, 256, "in_proj")

        oa = nsa_layer(zq, zkv, zg, nsa_q_norm[l], nsa_k_norm[l], cmp_pe[l], cmp_w1[l], cmp_w2[l], b, s)
        bias_full = jnp.repeat(sgu_b[l].T, SGU_WIDTH // SGU_GROUPS, axis=1)
        ob = sgu(zs, sgu_ln_g[l], sgu_ln_b[l], sgu_w[l], bias_full)
        cg = POOL_WIDTH // len(POOL_WINDOWS)
        w_bd = jnp.zeros((POOL_WIDTH, POOL_WIDTH), F32)
        for gi in range(len(POOL_WINDOWS)):
            w_bd = w_bd.at[gi * cg:(gi + 1) * cg, gi * cg:(gi + 1) * cg].set(pool_w[l, gi])
        oc = pool(zp.reshape(b, s, POOL_WIDTH), w_bd.astype(BF16), pool_scale[l])
        x = merge(x, oa, ob.reshape(b, s, SGU_WIDTH), oc, zm.reshape(b, s, 3 * d),
                  lift_a[l].astype(BF16), lift_b[l].astype(BF16), lift_c[l].astype(BF16),
                  w_out[l].astype(BF16))

        mk, mv = mem_kv(mem, mem_norm[l], xa_wk[l].astype(BF16), xa_wv[l].astype(BF16), xa_k_norm[l])
        x = xattn(x, xa_norm[l], xa_wq[l].astype(BF16), xa_q_norm[l], mk, mv, xa_wo[l].astype(BF16))

        k1p = jnp.concatenate([peer_keys1[l].astype(BF16), zeros_half], axis=1)
        k2p = jnp.concatenate([zeros_half, peer_keys2[l].astype(BF16)], axis=1)
        x = peer(x.reshape(t, d), ffn_norm[l], peer_wq[l].astype(BF16), k1p, k2p,
                 peer_u[l].astype(BF16), peer_v[l].T.astype(BF16)).reshape(b, s, d)
    return x
```

```python
import functools

import jax
import jax.numpy as jnp
from jax import lax
from jax.experimental import pallas as pl
from jax.experimental.pallas import tpu as pltpu

F32 = jnp.float32
BF16 = jnp.bfloat16

EPS = 1e-6
LOG2E = 1.4426950408889634
NEG = -1e30
REMOVED = -3e38

D_MODEL = 1024
DK = 64
NSA_HEADS = 8
NSA_GROUPS = 2
NSA_REP = NSA_HEADS // NSA_GROUPS
L_CMP, D_CMP = 32, 16
CMP_HIDDEN = 128
L_SLC = 64
N_SEL = 16
WINDOW = 512
QB = 128
FORCE_BONUS = 1.0e3
SGU_WIDTH = 256
SGU_GROUPS = 4
SGU_CHUNK = 128
POOL_WIDTH = 256
POOL_WINDOWS = (2, 4, 8, 16)
XA_HEADS, XA_HD = 4, 128
XA_WIDTH = XA_HEADS * XA_HD
PEER_HEADS = 8
PEER_KEYS = 128
PEER_TOPK = 16
PEER_HALF = 64

LANES = 128
SEL_VR = 2 * DK
SEL_SUB = 512
CMP_CHUNK = 256
CMP_HALO = 8
PEER_TT = 512
PEER_CH = 1024
PEER_PROJ_ROWS = 256
VMEM_LIMIT = 56 * 1024 * 1024


def _cp(sem):
    return pltpu.CompilerParams(dimension_semantics=sem, vmem_limit_bytes=VMEM_LIMIT)


def _gelu(x):
    return 0.5 * x * (1.0 + jnp.tanh(0.7978845608028654 * (x + 0.044715 * (x * x * x))))


def _rms(x, g):
    return x * lax.rsqrt(jnp.mean(x * x, axis=-1, keepdims=True) + EPS) * g


def _dot(a, b):
    return jnp.dot(a, b, preferred_element_type=F32)


def _dot_nt(a, b):
    return lax.dot_general(a, b, (((1,), (1,)), ((), ())), preferred_element_type=F32)


def _iota(shape, dim):
    return lax.broadcasted_iota(jnp.int32, shape, dim)


def _div_pow2(x, n):
    assert n & (n - 1) == 0
    return lax.shift_right_logical(x, jnp.int32(n.bit_length() - 1))


def _mod_pow2(x, n):
    assert n & (n - 1) == 0
    return x & (n - 1)


def _norm_matmul_kernel(x_ref, g_ref, w_ref, *o_refs, splits):
    hb = _rms(x_ref[...], g_ref[...]).astype(BF16)
    for o_ref, (a, b) in zip(o_refs, splits):
        o_ref[...] = _dot(hb, w_ref[:, a:b]).astype(o_ref.dtype)


def norm_matmul(x, g, w, splits, tm, name):
    t, d = x.shape
    n = w.shape[1]
    return pl.pallas_call(
        functools.partial(_norm_matmul_kernel, splits=splits),
        grid=(t // tm,),
        in_specs=[pl.BlockSpec((tm, d), lambda i: (i, 0)),
                  pl.BlockSpec((1, d), lambda i: (0, 0)),
                  pl.BlockSpec((d, n), lambda i: (0, 0))],
        out_specs=[pl.BlockSpec((tm, b - a), lambda i: (i, 0)) for a, b in splits],
        out_shape=[jax.ShapeDtypeStruct((t, b - a), F32) for a, b in splits],
        compiler_params=_cp(("parallel",)),
        name=name,
    )(x, g.reshape(1, d), w)


def _nsa_prep_kernel(zq_ref, zkv_ref, qn_ref, kn_ref,
                     q_out, kvc_out, ks_out, vs_out, kw_out, vw_out, *, ts):
    zq = zq_ref[0]
    scale = DK ** -0.5 * LOG2E
    for h in range(NSA_HEADS):
        q_out[0, h] = (_rms(zq[:, h * DK:(h + 1) * DK], qn_ref[...]) * scale).astype(BF16)
    zkv = zkv_ref[0]

    def piece(i, g):
        o = i * NSA_GROUPS * DK + g * DK
        return zkv[:, o:o + DK]

    ones_col = jnp.where(_iota((ts, SEL_VR - DK), 1) == 0, 1.0, 0.0)
    for g in range(NSA_GROUPS):
        kvc_out[0, 0, g] = piece(0, g)
        kvc_out[1, 0, g] = piece(1, g)
        ks_out[0, g] = _rms(piece(2, g), kn_ref[1:2, :]).astype(BF16)
        vs_out[0, g] = jnp.concatenate([piece(3, g), ones_col], axis=-1).T.astype(BF16)
        kw_out[0, g] = _rms(piece(4, g), kn_ref[2:3, :]).astype(BF16)
        vw_out[0, g] = jnp.concatenate([piece(5, g), ones_col], axis=-1).T.astype(BF16)


def nsa_prep(zq, zkv, q_norm, k_norm, ts=512):
    b, s, _ = zq.shape
    g = NSA_GROUPS
    hm = lambda bi, si: (bi, 0, si, 0)
    return pl.pallas_call(
        functools.partial(_nsa_prep_kernel, ts=ts),
        grid=(b, s // ts),
        in_specs=[pl.BlockSpec((1, ts, NSA_HEADS * DK), lambda bi, si: (bi, si, 0)),
                  pl.BlockSpec((1, ts, 6 * g * DK), lambda bi, si: (bi, si, 0)),
                  pl.BlockSpec((1, DK), lambda bi, si: (0, 0)),
                  pl.BlockSpec((3, DK), lambda bi, si: (0, 0))],
        out_specs=[pl.BlockSpec((1, NSA_HEADS, ts, DK), hm),
                   pl.BlockSpec((2, 1, g, ts, DK), lambda bi, si: (0, bi, 0, si, 0)),
                   pl.BlockSpec((1, g, ts, DK), hm),
                   pl.BlockSpec((1, g, SEL_VR, ts), lambda bi, si: (bi, 0, 0, si)),
                   pl.BlockSpec((1, g, ts, DK), hm),
                   pl.BlockSpec((1, g, SEL_VR, ts), lambda bi, si: (bi, 0, 0, si))],
        out_shape=[jax.ShapeDtypeStruct((b, NSA_HEADS, s, DK), BF16),
                   jax.ShapeDtypeStruct((2, b, g, s, DK), F32),
                   jax.ShapeDtypeStruct((b, g, s, DK), BF16),
                   jax.ShapeDtypeStruct((b, g, SEL_VR, s), BF16),
                   jax.ShapeDtypeStruct((b, g, s, DK), BF16),
                   jax.ShapeDtypeStruct((b, g, SEL_VR, s), BF16)],
        compiler_params=_cp(("parallel", "parallel")),
        name="nsa_prep",
    )(zq, zkv, q_norm.reshape(1, DK), k_norm)


def _cmp_mlp_kernel(x_ref, pe_ref, w1_ref, w2_ref, kn_ref, k_out, vt_out, *, nc):
    c = pl.program_id(2)
    x = x_ref[0, 0, 0]
    a = _dot((x + pe_ref[0, 0:1, :]).astype(BF16), w1_ref[0, 0])
    b = _dot((x + pe_ref[0, 1:2, :]).astype(BF16), w1_ref[0, 1])
    pre = a + pltpu.roll(b, nc - 1, 0)
    comp = _dot(_gelu(pre).astype(BF16), w2_ref[0])

    @pl.when(c == 0)
    def _():
        k_out[0, 0] = _rms(comp, kn_ref[...]).astype(BF16)

    @pl.when(c == 1)
    def _():
        ones_col = jnp.where(_iota((nc, SEL_VR - DK), 1) == 0, 1.0, 0.0)
        vt_out[0, 0] = jnp.concatenate([comp, ones_col], axis=-1).T.astype(BF16)


def cmp_mlp(xc, pe, w1, w2, k_norm0):
    _, b, g, nc, width = xc.shape
    return pl.pallas_call(
        functools.partial(_cmp_mlp_kernel, nc=nc),
        grid=(b, g, 2),
        in_specs=[pl.BlockSpec((1, 1, 1, nc, width), lambda bi, gi, c: (c, bi, gi, 0, 0)),
                  pl.BlockSpec((1, 2, width), lambda bi, gi, c: (c, 0, 0)),
                  pl.BlockSpec((1, 2, width, CMP_HIDDEN), lambda bi, gi, c: (c, 0, 0, 0)),
                  pl.BlockSpec((1, CMP_HIDDEN, DK), lambda bi, gi, c: (c, 0, 0)),
                  pl.BlockSpec((1, DK), lambda bi, gi, c: (0, 0))],
        out_specs=[pl.BlockSpec((1, 1, nc, DK), lambda bi, gi, c: (bi, gi, 0, 0)),
                   pl.BlockSpec((1, 1, SEL_VR, nc), lambda bi, gi, c: (bi, gi, 0, 0))],
        out_shape=[jax.ShapeDtypeStruct((b, g, nc, DK), BF16),
                   jax.ShapeDtypeStruct((b, g, SEL_VR, nc), BF16)],
        compiler_params=_cp(("parallel", "parallel", "arbitrary")),
        name="cmp_mlp",
    )(xc, pe, w1, w2, k_norm0.reshape(1, DK))


def _cmp_attn_kernel(q_ref, k_ref, vt_ref, oc_ref, sb_ref, s_s, acc_s, ps_s, *, nc, n_slc):
    qb = pl.program_id(2)
    rows = NSA_REP * QB
    ck = CMP_CHUNK
    q = q_ref[0].reshape(rows, DK)
    n_chunk = ((qb * QB + QB) // D_CMP + ck - 1) // ck
    t = qb * QB + _mod_pow2(_iota((ck, rows), 1), QB)

    def pass_a(ci, m):
        off = pl.multiple_of(ci * ck, ck)
        st = _dot_nt(k_ref[0, 0, pl.ds(off, ck), :], q)
        n = off + _iota((ck, rows), 0)
        st = jnp.where(n * D_CMP + (L_CMP - 1) <= t, st, NEG)
        s_s[ci] = st
        return jnp.maximum(m, jnp.max(st, axis=0, keepdims=True))

    m = lax.fori_loop(0, n_chunk, pass_a, jnp.full((1, rows), NEG, F32))
    acc_s[...] = jnp.zeros_like(acc_s)
    ps_s[...] = jnp.zeros_like(ps_s)

    def pass_b(ci, den):
        off = pl.multiple_of(ci * ck, ck)
        e = jnp.exp2(s_s[ci] - m)
        s_s[ci] = e
        acc_s[...] += _dot(vt_ref[0, 0, :, pl.ds(off, ck)], e.astype(BF16))
        return den + jnp.sum(e, axis=0, keepdims=True)

    den = lax.fori_loop(0, n_chunk, pass_b, jnp.zeros((1, rows), F32))
    seen = m > 0.5 * NEG
    inv_o = jnp.where(seen, 1.0 / jnp.maximum(acc_s[DK:DK + 1, :], 1e-30), 0.0)
    inv_p = jnp.where(seen, 1.0 / jnp.maximum(den, 1e-30), 0.0)
    oc_ref[0] = (acc_s[...] * inv_o).T[:, 0:DK].reshape(NSA_REP, QB, DK)

    def pass_c(ci, carry):
        off = pl.multiple_of(ci * ck, ck)
        p = s_s[ci] * inv_p
        ps = p[:, 0:QB]
        for r in range(1, NSA_REP):
            ps = ps + p[:, r * QB:(r + 1) * QB]
        ps_s[pl.ds(CMP_HALO + off, ck), :] = ps
        return carry

    lax.fori_loop(0, n_chunk, pass_c, 0)

    ratio = L_SLC // D_CMP
    imp = jnp.zeros((n_slc, QB), F32)
    for k in range(-(L_CMP // D_CMP - 1), ratio):
        overlap = min(k * D_CMP + L_CMP, L_SLC) - max(k * D_CMP, 0)
        imp = imp + (overlap / D_CMP) * ps_s[pl.ds(CMP_HALO + k, n_slc, stride=ratio), :]

    cur = _div_pow2(qb * QB + _iota((n_slc, QB), 1), L_SLC)
    j = _iota((n_slc, QB), 0)
    forced = (j == 0) | (j == cur) | (j == cur - 1)
    score = jnp.where(j <= cur, imp + jnp.where(forced, FORCE_BONUS, 0.0), NEG)
    work = score
    tau = None
    for _ in range(N_SEL):
        tau = jnp.max(work, axis=0, keepdims=True)
        work = jnp.where(work >= tau, REMOVED, work)
    selected = (score > 0.5 * NEG) & (score >= tau)
    sb_ref[0, 0, 0] = jnp.where(selected, 0.0, NEG)


def cmp_attn_select(qn, kc, vct):
    b, h, s, _ = qn.shape
    g = NSA_GROUPS
    nc = kc.shape[2]
    n_slc = s // L_SLC
    nqb = s // QB
    return pl.pallas_call(
        functools.partial(_cmp_attn_kernel, nc=nc, n_slc=n_slc),
        grid=(b, g, nqb),
        in_specs=[pl.BlockSpec((1, NSA_REP, QB, DK), lambda bi, gi, qi: (bi, gi, qi, 0)),
                  pl.BlockSpec((1, 1, nc, DK), lambda bi, gi, qi: (bi, gi, 0, 0)),
                  pl.BlockSpec((1, 1, SEL_VR, nc), lambda bi, gi, qi: (bi, gi, 0, 0))],
        out_specs=[pl.BlockSpec((1, NSA_REP, QB, DK), lambda bi, gi, qi: (bi, gi, qi, 0)),
                   pl.BlockSpec((1, 1, 1, n_slc, QB), lambda bi, gi, qi: (bi, gi, qi, 0, 0))],
        out_shape=[jax.ShapeDtypeStruct((b, h, s, DK), F32),
                   jax.ShapeDtypeStruct((b, g, nqb, n_slc, QB), F32)],
        scratch_shapes=[pltpu.VMEM((nc // CMP_CHUNK, CMP_CHUNK, NSA_REP * QB), F32),
                        pltpu.VMEM((SEL_VR, NSA_REP * QB), F32),
                        pltpu.VMEM((CMP_HALO + nc, QB), F32)],
        compiler_params=_cp(("parallel", "parallel", "parallel")),
        name="cmp_attn_select",
    )(qn, kc, vct)


def _win_attn_kernel(q_ref, *refs):
    nkb = WINDOW // QB + 1
    k_refs, v_refs, o_ref = refs[:nkb], refs[nkb:2 * nkb], refs[2 * nkb]
    qb = pl.program_id(2)
    rows = NSA_REP * QB
    nk = nkb * QB
    q = q_ref[0].reshape(rows, DK)
    k = jnp.concatenate([r[0, 0] for r in k_refs], axis=0)
    vt = jnp.concatenate([r[0, 0] for r in v_refs], axis=-1)
    st = _dot_nt(k, q)
    t = qb * QB + _mod_pow2(_iota((nk, rows), 1), QB)
    kpos = (qb - (nkb - 1)) * QB + _iota((nk, rows), 0)
    delta = t - kpos
    valid = (delta >= 0) & (delta < WINDOW) & (kpos >= 0)
    st = jnp.where(valid, st, NEG)
    e = jnp.exp2(st - jnp.max(st, axis=0, keepdims=True)).astype(BF16)
    acc = _dot(vt, e)
    o = (acc / jnp.maximum(acc[DK:DK + 1, :], 1e-30)).T[:, 0:DK]
    o_ref[0] = o.reshape(NSA_REP, QB, DK)


def win_attn(qn, kw, vw):
    b, h, s, _ = qn.shape
    g = NSA_GROUPS
    nkb = WINDOW // QB + 1
    nqb = s // QB

    def k_spec(i):
        return pl.BlockSpec((1, 1, QB, DK),
                            lambda bi, gi, qi: (bi, gi, jnp.maximum(qi - (nkb - 1) + i, 0), 0))

    def vt_spec(i):
        return pl.BlockSpec((1, 1, SEL_VR, QB),
                            lambda bi, gi, qi: (bi, gi, 0, jnp.maximum(qi - (nkb - 1) + i, 0)))

    return pl.pallas_call(
        _win_attn_kernel,
        grid=(b, g, nqb),
        in_specs=[pl.BlockSpec((1, NSA_REP, QB, DK), lambda bi, gi, qi: (bi, gi, qi, 0))]
        + [k_spec(i) for i in range(nkb)] + [vt_spec(i) for i in range(nkb)],
        out_specs=pl.BlockSpec((1, NSA_REP, QB, DK), lambda bi, gi, qi: (bi, gi, qi, 0)),
        out_shape=jax.ShapeDtypeStruct((b, h, s, DK), F32),
        compiler_params=_cp(("parallel", "parallel", "parallel")),
        name="win_attn",
    )(qn, *([kw] * nkb), *([vw] * nkb))


def _sel_attn_kernel(q_ref, sb_ref, k_ref, vt_ref, oc_ref, ow_ref, gl_ref, o_ref,
                     m_s, acc_s, s0_s, s1_s, p0_s, p1_s, a0_s, a1_s):
    qb = pl.program_id(2)
    rows = NSA_REP * QB
    blocks = SEL_SUB // L_SLC
    qv = q_ref[0].reshape(rows, DK)
    nsub = (qb * QB + QB + SEL_SUB - 1) // SEL_SUB
    m_s[...] = jnp.full_like(m_s, NEG)
    acc_s[...] = jnp.zeros_like(acc_s)

    def scores(i, s_ref):
        off = pl.multiple_of(i * SEL_SUB, SEL_SUB)
        s_ref[...] = _dot_nt(k_ref[0, 0, pl.ds(off, SEL_SUB), :], qv)

    def weights(i, s_ref, p_ref, a_ref, causal):
        bias = sb_ref[0, 0, 0, pl.ds(pl.multiple_of(i * blocks, blocks), blocks), :]
        bias = jnp.broadcast_to(bias[:, None, :], (blocks, L_SLC, QB)).reshape(SEL_SUB, QB)
        st = s_ref[...] + jnp.concatenate([bias] * NSA_REP, axis=1)
        if causal:
            kpos = i * SEL_SUB + _iota((SEL_SUB, rows), 0)
            t = qb * QB + _mod_pow2(_iota((SEL_SUB, rows), 1), QB)
            st = jnp.where(kpos <= t, st, NEG)
        m_old = m_s[...]
        m_new = jnp.maximum(m_old, jnp.max(st, axis=0, keepdims=True))
        p_ref[...] = jnp.exp2(st - m_new).astype(BF16)
        a_ref[...] = jnp.exp2(m_old - m_new)
        m_s[...] = m_new

    def accumulate(i, p_ref, a_ref):
        off = pl.multiple_of(i * SEL_SUB, SEL_SUB)
        acc_s[...] = a_ref[...] * acc_s[...] + _dot(vt_ref[0, 0, :, pl.ds(off, SEL_SUB)], p_ref[...])

    p1_s[...] = jnp.zeros_like(p1_s)
    a1_s[...] = jnp.ones_like(a1_s)
    scores(0, s0_s)

    def two_tiles(i):
        scores(i + 1, s1_s)
        weights(i, s0_s, p0_s, a0_s, False)
        accumulate(jnp.maximum(i - 1, 0), p1_s, a1_s)
        scores(i + 2, s0_s)
        weights(i + 1, s1_s, p1_s, a1_s, False)
        accumulate(i, p0_s, a0_s)

    n_quads = (nsub - 1) // 4
    n_pairs = ((nsub - 1) - 4 * n_quads) // 2
    odd_tail = (nsub - 1) - 4 * n_quads - 2 * n_pairs

    def quad(j, carry):
        two_tiles(4 * j)
        two_tiles(4 * j + 2)
        return carry

    def pair(j, carry):
        two_tiles(4 * n_quads + 2 * j)
        return carry

    lax.fori_loop(0, n_quads, quad, 0)
    lax.fori_loop(0, n_pairs, pair, 0)
    e = 4 * n_quads + 2 * n_pairs

    @pl.when(odd_tail == 0)
    def _():
        weights(e, s0_s, p0_s, a0_s, True)
        accumulate(jnp.maximum(e - 1, 0), p1_s, a1_s)
        accumulate(e, p0_s, a0_s)

    @pl.when(odd_tail == 1)
    def _():
        scores(e + 1, s1_s)
        weights(e, s0_s, p0_s, a0_s, False)
        accumulate(jnp.maximum(e - 1, 0), p1_s, a1_s)
        weights(e + 1, s1_s, p1_s, a1_s, True)
        accumulate(e, p0_s, a0_s)
        accumulate(e + 1, p1_s, a1_s)

    o_s = (acc_s[...] / jnp.maximum(acc_s[DK:DK + 1, :], 1e-30)).T[:, 0:DK]
    gate = jax.nn.sigmoid(gl_ref[0].reshape(rows, 3))
    out = (gate[:, 0:1] * oc_ref[0].reshape(rows, DK) + gate[:, 1:2] * o_s
           + gate[:, 2:3] * ow_ref[0].reshape(rows, DK))
    o_ref[0] = out.reshape(NSA_REP, QB, DK)


def sel_attn(qn, selb, ks, vs, oc, ow, gl):
    b, h, s, _ = qn.shape
    g = NSA_GROUPS
    nqb = s // QB
    n_slc = s // L_SLC
    qmap = lambda bi, gi, qi: (bi, gi, qi, 0)
    return pl.pallas_call(
        _sel_attn_kernel,
        grid=(b, g, nqb),
        in_specs=[pl.BlockSpec((1, NSA_REP, QB, DK), qmap),
                  pl.BlockSpec((1, 1, 1, n_slc, QB), lambda bi, gi, qi: (bi, gi, qi, 0, 0)),
                  pl.BlockSpec((1, 1, s, DK), lambda bi, gi, qi: (bi, gi, 0, 0)),
                  pl.BlockSpec((1, 1, SEL_VR, s), lambda bi, gi, qi: (bi, gi, 0, 0)),
                  pl.BlockSpec((1, NSA_REP, QB, DK), qmap),
                  pl.BlockSpec((1, NSA_REP, QB, DK), qmap),
                  pl.BlockSpec((1, NSA_REP, QB, 3), qmap)],
        out_specs=pl.BlockSpec((1, NSA_REP, QB, DK), qmap),
        out_shape=jax.ShapeDtypeStruct((b, h, s, DK), F32),
        scratch_shapes=[pltpu.VMEM((1, NSA_REP * QB), F32),
                        pltpu.VMEM((SEL_VR, NSA_REP * QB), F32),
                        pltpu.VMEM((SEL_SUB, NSA_REP * QB), F32),
                        pltpu.VMEM((SEL_SUB, NSA_REP * QB), F32),
                        pltpu.VMEM((SEL_SUB, NSA_REP * QB), BF16),
                        pltpu.VMEM((SEL_SUB, NSA_REP * QB), BF16),
                        pltpu.VMEM((1, NSA_REP * QB), F32),
                        pltpu.VMEM((1, NSA_REP * QB), F32)],
        compiler_params=_cp(("parallel", "parallel", "parallel")),
        name="sel_attn",
    )(qn, selb, ks, vs, oc, ow, gl)


def _sgu_kernel(z_ref, g_ref, b_ref, w_ref, bias_ref, o_ref, *, tm):
    z = _gelu(z_ref[...])
    u = z[:, :SGU_WIDTH]
    v = z[:, SGU_WIDTH:]
    mu = jnp.mean(v, axis=-1, keepdims=True)
    var = jnp.mean(jnp.square(v - mu), axis=-1, keepdims=True)
    vn = ((v - mu) * lax.rsqrt(var + EPS) * g_ref[...] + b_ref[...]).astype(BF16)
    tril = _iota((SGU_CHUNK, SGU_CHUNK), 1) <= _iota((SGU_CHUNK, SGU_CHUNK), 0)
    wm = [jnp.where(tril, w_ref[gi], 0.0).astype(BF16) for gi in range(SGU_GROUPS)]
    grp = _div_pow2(_iota((SGU_CHUNK, SGU_WIDTH), 1), SGU_WIDTH // SGU_GROUPS)
    for ch in range(tm // SGU_CHUNK):
        rs = slice(ch * SGU_CHUNK, (ch + 1) * SGU_CHUNK)
        mixed = bias_ref[...]
        for gi in range(SGU_GROUPS):
            mixed = mixed + jnp.where(grp == gi, _dot(wm[gi], vn[rs]), 0.0)
        o_ref[rs, :] = u[rs] * mixed


def sgu(z, ln_g, ln_b, w, bias_full, tm=512):
    t = z.shape[0]
    return pl.pallas_call(
        functools.partial(_sgu_kernel, tm=tm),
        grid=(t // tm,),
        in_specs=[pl.BlockSpec((tm, 2 * SGU_WIDTH), lambda i: (i, 0)),
                  pl.BlockSpec((1, SGU_WIDTH), lambda i: (0, 0)),
                  pl.BlockSpec((1, SGU_WIDTH), lambda i: (0, 0)),
                  pl.BlockSpec((SGU_GROUPS, SGU_CHUNK, SGU_CHUNK), lambda i: (0, 0, 0)),
                  pl.BlockSpec((SGU_CHUNK, SGU_WIDTH), lambda i: (0, 0))],
        out_specs=pl.BlockSpec((tm, SGU_WIDTH), lambda i: (i, 0)),
        out_shape=jax.ShapeDtypeStruct((t, SGU_WIDTH), F32),
        compiler_params=_cp(("parallel",)),
        name="sgu",
    )(z, ln_g.reshape(1, -1), ln_b.reshape(1, -1), w, bias_full)


def _pool_kernel(p_ref, h_ref, w_ref, sc_ref, o_ref, *, tp):
    i = pl.program_id(1)
    halo_rows = POOL_WINDOWS[-1]
    p = p_ref[0]
    halo = jnp.where(i > 0, h_ref[0], 0.0)
    ext = jnp.concatenate([halo, p], axis=0)
    sums = [ext]
    shift = 1
    for _ in POOL_WINDOWS:
        prev = sums[-1]
        sums.append(prev + pltpu.roll(prev, shift, 0))
        shift *= 2
    pos = i * tp + _iota((tp, POOL_WIDTH), 0)
    grp = _div_pow2(_iota((tp, POOL_WIDTH), 1), POOL_WIDTH // len(POOL_WINDOWS))
    d = jnp.zeros((tp, POOL_WIDTH), F32)
    for gi, w in enumerate(POOL_WINDOWS):
        mean = sums[gi + 1][halo_rows:halo_rows + tp] / jnp.minimum(pos + 1, w).astype(F32)
        d = jnp.where(grp == gi, mean, d)
    d = d - p
    o_ref[0] = _dot(d.astype(BF16), w_ref[...]) * sc_ref[...]


def pool(p, w_bd, scale, tp=512):
    b, s, c = p.shape
    halo_rows = POOL_WINDOWS[-1]
    return pl.pallas_call(
        functools.partial(_pool_kernel, tp=tp),
        grid=(b, s // tp),
        in_specs=[pl.BlockSpec((1, tp, c), lambda bi, i: (bi, i, 0)),
                  pl.BlockSpec((1, halo_rows, c),
                               lambda bi, i: (bi, jnp.maximum(i * (tp // halo_rows) - 1, 0), 0)),
                  pl.BlockSpec((c, c), lambda bi, i: (0, 0)),
                  pl.BlockSpec((1, c), lambda bi, i: (0, 0))],
        out_specs=pl.BlockSpec((1, tp, c), lambda bi, i: (bi, i, 0)),
        out_shape=jax.ShapeDtypeStruct((b, s, c), F32),
        compiler_params=_cp(("parallel", "parallel")),
        name="pool",
    )(p, p, w_bd, scale.reshape(1, c))


def _merge_kernel(x_ref, oa_ref, ob_ref, oc_ref, zm_ref, la_ref, lb_ref, lc_ref, wo_ref, o_ref):
    d = D_MODEL
    oa = jnp.concatenate([oa_ref[0, h].astype(BF16) for h in range(NSA_HEADS)], axis=-1)
    zm = zm_ref[0]
    merged = (jax.nn.sigmoid(zm[:, 0:d]) * _dot(oa, la_ref[...])
              + jax.nn.sigmoid(zm[:, d:2 * d]) * _dot(ob_ref[0].astype(BF16), lb_ref[...])
              + jax.nn.sigmoid(zm[:, 2 * d:3 * d]) * _dot(oc_ref[0].astype(BF16), lc_ref[...]))
    o_ref[0] = x_ref[0] + _dot(merged.astype(BF16), wo_ref[...])


def merge(x, oa, ob, oc, zm, la, lb, lc, wo, tm=512):
    b, s, d = x.shape
    row = lambda bi, i: (bi, i, 0)
    full = lambda bi, i: (0, 0)
    return pl.pallas_call(
        _merge_kernel,
        grid=(b, s // tm),
        in_specs=[pl.BlockSpec((1, tm, d), row),
                  pl.BlockSpec((1, NSA_HEADS, tm, DK), lambda bi, i: (bi, 0, i, 0)),
                  pl.BlockSpec((1, tm, SGU_WIDTH), row),
                  pl.BlockSpec((1, tm, POOL_WIDTH), row),
                  pl.BlockSpec((1, tm, 3 * d), row),
                  pl.BlockSpec(la.shape, full), pl.BlockSpec(lb.shape, full),
                  pl.BlockSpec(lc.shape, full), pl.BlockSpec(wo.shape, full)],
        out_specs=pl.BlockSpec((1, tm, d), row),
        out_shape=jax.ShapeDtypeStruct((b, s, d), F32),
        compiler_params=_cp(("parallel", "parallel")),
        name="merge",
    )(x, oa, ob, oc, zm, la, lb, lc, wo)


def _mem_kv_kernel(m_ref, g_ref, wk_ref, wv_ref, kn_ref, k_out, v_out):
    mh = _rms(m_ref[0], g_ref[...]).astype(BF16)
    k = _dot(mh, wk_ref[...])
    for h in range(XA_HEADS):
        hs = slice(h * XA_HD, (h + 1) * XA_HD)
        k_out[0, :, hs] = _rms(k[:, hs], kn_ref[...]).astype(BF16)
    v_out[0] = _dot(mh, wv_ref[...]).astype(BF16)


def mem_kv(mem, g, wk, wv, k_norm):
    b, m, d = mem.shape
    full = lambda bi: (0, 0)
    return pl.pallas_call(
        _mem_kv_kernel,
        grid=(b,),
        in_specs=[pl.BlockSpec((1, m, d), lambda bi: (bi, 0, 0)),
                  pl.BlockSpec((1, d), full),
                  pl.BlockSpec(wk.shape, full), pl.BlockSpec(wv.shape, full),
                  pl.BlockSpec((1, XA_HD), full)],
        out_specs=[pl.BlockSpec((1, m, XA_WIDTH), lambda bi: (bi, 0, 0))] * 2,
        out_shape=[jax.ShapeDtypeStruct((b, m, XA_WIDTH), BF16)] * 2,
        compiler_params=_cp(("parallel",)),
        name="mem_kv",
    )(mem, g.reshape(1, d), wk, wv, k_norm.reshape(1, XA_HD))


def _xattn_kernel(x_ref, g_ref, wq_ref, qn_ref, k_ref, v_ref, wo_ref, o_ref):
    x = x_ref[0]
    q = _dot(_rms(x, g_ref[...]).astype(BF16), wq_ref[...])
    k = k_ref[0]
    v = v_ref[0]
    outs = []
    for h in range(XA_HEADS):
        hs = slice(h * XA_HD, (h + 1) * XA_HD)
        qh = (_rms(q[:, hs], qn_ref[...])).astype(BF16)
        s = _dot_nt(qh, k[:, hs]) * (XA_HD ** -0.5)
        e = jnp.exp(s - jnp.max(s, axis=-1, keepdims=True))
        p = e / jnp.sum(e, axis=-1, keepdims=True)
        outs.append(_dot(p.astype(BF16), v[:, hs]).astype(BF16))
    o = jnp.concatenate(outs, axis=-1)
    o_ref[0] = x + _dot(o, wo_ref[...])


def xattn(x, g, wq, q_norm, k, v, wo, tm=512):
    b, s, d = x.shape
    m = k.shape[1]
    row = lambda bi, i: (bi, i, 0)
    full = lambda bi, i: (0, 0)
    return pl.pallas_call(
        _xattn_kernel,
        grid=(b, s // tm),
        in_specs=[pl.BlockSpec((1, tm, d), row),
                  pl.BlockSpec((1, d), full),
                  pl.BlockSpec(wq.shape, full),
                  pl.BlockSpec((1, XA_HD), full),
                  pl.BlockSpec((1, m, XA_WIDTH), lambda bi, i: (bi, 0, 0)),
                  pl.BlockSpec((1, m, XA_WIDTH), lambda bi, i: (bi, 0, 0)),
                  pl.BlockSpec(wo.shape, full)],
        out_specs=pl.BlockSpec((1, tm, d), row),
        out_shape=jax.ShapeDtypeStruct((b, s, d), F32),
        compiler_params=_cp(("parallel", "parallel")),
        name="xattn",
    )(x, g.reshape(1, d), wq, q_norm.reshape(1, XA_HD), k, v, wo)


def _top_rows(s, n, want_rank):
    out = []
    work = s
    rank = jnp.full(s.shape, float(s.shape[0] - 1), F32) if want_rank else None
    for i in range(n):
        m = jnp.max(work, axis=0, keepdims=True)
        out.append(m)
        hit = work >= m
        if want_rank:
            rank = jnp.where(hit, float(i), rank)
        work = jnp.where(hit, REMOVED, work)
    return out, rank


def _stack_rows(rows, pad_rows):
    tt = rows[0].shape[1]
    rowi = _iota((pad_rows, tt), 0)
    out = jnp.full((pad_rows, tt), REMOVED, F32)
    for i, r in enumerate(rows[:pad_rows]):
        out = jnp.where(rowi == i, r, out)
    return out


def _pair_sum_candidates(v1, v2, tt):
    row_full = _iota((24, tt), 0)
    row = _iota((8, tt), 0)
    v2_full = _stack_rows(v2, 24)
    v2_8 = _stack_rows(v2, 8)

    def shifted(k):
        return pltpu.roll(v2_8, k, 0)

    def pick(rows_from, first):
        out = v1[rows_from]
        for r in range(first + 1, 8):
            if rows_from + r - first < len(v1):
                out = jnp.where(row == r, v1[rows_from + r - first], out)
        return out

    return jnp.concatenate([
        jnp.where(row_full < 17, v1[0] + v2_full, REMOVED),
        v1[1] + v2_8,
        jnp.where(row < 5, v1[2] + v2_8,
                  jnp.where(row < 7, v1[5] + shifted(5), v1[8] + shifted(7))),
        jnp.where(row < 4, v1[3] + v2_8,
                  jnp.where(row < 7, v1[4] + shifted(4), v1[9] + shifted(7))),
        jnp.where(row < 2, v1[6] + v2_8,
                  jnp.where(row < 4, v1[7] + shifted(2), pick(10, 4) + v2[0])),
        jnp.where(row < 3, pick(14, 0) + v2[0], REMOVED),
    ], axis=0)


def _peer_kernel(x_ref, g_ref, wq_ref, k1_ref, k2_ref, u_ref, vt_ref, o_ref,
                 hbt_s, q_s, r2_s, na_s, e1_s, e2_s, acc_s, *, tt, ch):
    c = pl.program_id(1)
    ntop = PEER_TOPK + 1
    assert ntop == 17

    @pl.when(c == 0)
    def _():
        h = _rms(x_ref[...], g_ref[...])
        hbt_s[...] = h.T.astype(BF16)
        q = _dot(h.astype(BF16), wq_ref[...])
        for hd in range(PEER_HEADS):
            q_s[hd] = q[:, hd * 2 * PEER_HALF:(hd + 1) * 2 * PEER_HALF].astype(BF16)
        acc_s[...] = jnp.zeros_like(acc_s)

        def route(hd, carry):
            qh = q_s[hd]
            s1 = _dot_nt(k1_ref[...], qh)
            s2 = _dot_nt(k2_ref[...], qh)
            v1, _ = _top_rows(s1, ntop, False)
            v2, rank2 = _top_rows(s2, ntop, True)
            cand = _pair_sum_candidates(v1, v2, tt)
            top = v1[0] + v2[0]
            z = jnp.zeros((1, tt), F32)
            c_prev = top
            c_last = top
            work = cand
            for r in range(ntop):
                m = jnp.max(work, axis=0, keepdims=True)
                work = jnp.where(work >= m, REMOVED, work)
                if r < PEER_TOPK:
                    z = z + jnp.exp(m - top)
                c_prev, c_last = c_last, m
            thr = 0.5 * (c_prev + c_last)
            na = jnp.zeros((PEER_KEYS, tt), F32)
            for j in range(ntop):
                na = jnp.where(s1 + v2[j] >= thr, float(j + 1), na)
            r2_s[hd] = rank2.astype(BF16)
            na_s[hd] = na
            e1_s[hd] = jnp.exp(s1 - v1[0])
            e2_s[hd] = (jnp.exp(s2 - v2[0]) / z).astype(BF16)
            return carry

        lax.fori_loop(0, PEER_HEADS, route, 0)

    n_slab = ch // PEER_KEYS
    slabs_per_group = PEER_PROJ_ROWS // PEER_KEYS
    n_group = ch // PEER_PROJ_ROWS

    def project(gi):
        rs = slice(gi * PEER_PROJ_ROWS, (gi + 1) * PEER_PROJ_ROWS)
        return _gelu(_dot(u_ref[rs, :], hbt_s[...]).astype(BF16))

    def gate(al, gl_rows):
        a = c * n_slab + al
        wc = jnp.zeros((PEER_KEYS, tt), BF16)
        for hd in range(PEER_HEADS):
            na = na_s[hd, pl.ds(a, 1), :].astype(BF16)
            e1 = e1_s[hd, pl.ds(a, 1), :].astype(BF16)
            wc = wc + jnp.where(r2_s[hd] < na, e2_s[hd] * e1, jnp.zeros((), BF16))
        return wc * gl_rows

    ys = []
    gl_next = project(0)
    for gi in range(n_group):
        gl = gl_next
        if gi + 1 < n_group:
            gl_next = project(gi + 1)
        for k in range(slabs_per_group):
            ys.append(gate(gi * slabs_per_group + k, gl[k * PEER_KEYS:(k + 1) * PEER_KEYS]))
    y = jnp.concatenate(ys, axis=0)
    acc_s[...] += _dot(vt_ref[0], y)

    @pl.when(c == pl.num_programs(1) - 1)
    def _():
        o_ref[...] = x_ref[...] + acc_s[...].T


def peer(x, g, wq, k1p, k2p, u, vt, tt=PEER_TT, ch=PEER_CH):
    t, d = x.shape
    ne = u.shape[0]
    return pl.pallas_call(
        functools.partial(_peer_kernel, tt=tt, ch=ch),
        grid=(t // tt, ne // ch),
        in_specs=[pl.BlockSpec((tt, d), lambda i, c: (i, 0)),
                  pl.BlockSpec((1, d), lambda i, c: (0, 0)),
                  pl.BlockSpec(wq.shape, lambda i, c: (0, 0)),
                  pl.BlockSpec(k1p.shape, lambda i, c: (0, 0)),
                  pl.BlockSpec(k2p.shape, lambda i, c: (0, 0)),
                  pl.BlockSpec((ch, d), lambda i, c: (c, 0)),
                  pl.BlockSpec((1, d, ch), lambda i, c: (c, 0, 0))],
        out_specs=pl.BlockSpec((tt, d), lambda i, c: (i, 0)),
        out_shape=jax.ShapeDtypeStruct((t, d), F32),
        scratch_shapes=[pltpu.VMEM((d, tt), BF16),
                        pltpu.VMEM((PEER_HEADS, tt, 2 * PEER_HALF), BF16),
                        pltpu.VMEM((PEER_HEADS, PEER_KEYS, tt), BF16),
                        pltpu.VMEM((PEER_HEADS, PEER_KEYS, tt), F32),
                        pltpu.VMEM((PEER_HEADS, PEER_KEYS, tt), F32),
                        pltpu.VMEM((PEER_HEADS, PEER_KEYS, tt), BF16),
                        pltpu.VMEM((d, tt), F32)],
        compiler_params=_cp(("parallel", "arbitrary")),
        name="peer",
    )(x, g.reshape(1, d), wq, k1p, k2p, u, vt)


_IN_WIDTHS = (NSA_HEADS * DK, 6 * NSA_GROUPS * DK, 3 * NSA_HEADS, 2 * SGU_WIDTH, POOL_WIDTH, 3 * D_MODEL)


def _in_proj_layout():
    splits, src = [], []
    o_src = o_dst = 0
    for w in _IN_WIDTHS:
        wp = -(-w // LANES) * LANES
        splits.append((o_dst, o_dst + wp))
        src.append((o_src, o_src + w))
        o_src += w
        o_dst += wp
    return splits, src, o_dst


def _chunked_transpose(v):
    ne, d = v.shape
    return v.astype(BF16).reshape(ne // PEER_CH, PEER_CH, d).transpose(0, 2, 1)


def nsa_layer(zq, zkv, zg, q_norm, k_norm, cmp_pe, cmp_w1, cmp_w2, b, s):
    qn, kvc, ks, vs, kw, vw = nsa_prep(zq.reshape(b, s, -1), zkv.reshape(b, s, -1), q_norm, k_norm)
    nc = s // D_CMP
    xc = kvc.reshape(2, b, NSA_GROUPS, nc, D_CMP * DK)
    pe = cmp_pe.reshape(2, 2, D_CMP * DK)
    w1 = cmp_w1.reshape(2, 2, D_CMP * DK, CMP_HIDDEN).astype(BF16)
    kc, vct = cmp_mlp(xc, pe, w1, cmp_w2.astype(BF16), k_norm[0])
    oc, selb = cmp_attn_select(qn, kc, vct)
    ow = win_attn(qn, kw, vw)
    gl = zg[:, :3 * NSA_HEADS].reshape(b, s, NSA_HEADS, 3).transpose(0, 2, 1, 3)
    return sel_attn(qn, selb, ks, vs, oc, ow, gl)


def kernel(x, mem, mix_norm, w_in, nsa_q_norm, nsa_k_norm, cmp_pe, cmp_w1, cmp_w2, sgu_ln_g, sgu_ln_b, sgu_w, sgu_b, pool_w, pool_scale, lift_a, lift_b, lift_c, w_out, xa_norm, mem_norm, xa_wq, xa_wk, xa_wv, xa_q_norm, xa_k_norm, xa_wo, ffn_norm, peer_wq, peer_keys1, peer_keys2, peer_u, peer_v):
    b, s, d = x.shape
    t = b * s
    depth = w_in.shape[0]
    splits, src, n_pad = _in_proj_layout()
    zeros_half = jnp.zeros((PEER_KEYS, PEER_HALF), BF16)
    for l in range(depth):
        w_parts = []
        for (a0, a1), (d0, d1) in zip(src, splits):
            w_parts.append(jnp.pad(w_in[l][:, a0:a1], ((0, 0), (0, (d1 - d0) - (a1 - a0)))))
        w_pad = jnp.concatenate(w_parts, axis=1).astype(BF16)
        zq, zkv, zg, zs, zp, zm = norm_matmul(x.reshape(t, d), mix_norm[l], w_pad, splits, 256, "in_proj")

        oa = nsa_layer(zq, zkv, zg, nsa_q_norm[l], nsa_k_norm[l], cmp_pe[l], cmp_w1[l], cmp_w2[l], b, s)
        bias_full = jnp.repeat(sgu_b[l].T, SGU_WIDTH // SGU_GROUPS, axis=1)
        ob = sgu(zs, sgu_ln_g[l], sgu_ln_b[l], sgu_w[l], bias_full)
        cg = POOL_WIDTH // len(POOL_WINDOWS)
        w_bd = jnp.zeros((POOL_WIDTH, POOL_WIDTH), F32)
        for gi in range(len(POOL_WINDOWS)):
            w_bd = w_bd.at[gi * cg:(gi + 1) * cg, gi * cg:(gi + 1) * cg].set(pool_w[l, gi])
        oc = pool(zp.reshape(b, s, POOL_WIDTH), w_bd.astype(BF16), pool_scale[l])
        x = merge(x, oa, ob.reshape(b, s, SGU_WIDTH), oc, zm.reshape(b, s, 3 * d),
                  lift_a[l].astype(BF16), lift_b[l].astype(BF16), lift_c[l].astype(BF16),
                  w_out[l].astype(BF16))

        mk, mv = mem_kv(mem, mem_norm[l], xa_wk[l].astype(BF16), xa_wv[l].astype(BF16), xa_k_norm[l])
        x = xattn(x, xa_norm[l], xa_wq[l].astype(BF16), xa_q_norm[l], mk, mv, xa_wo[l].astype(BF16))

        k1p = jnp.concatenate([peer_keys1[l].astype(BF16), zeros_half], axis=1)
        k2p = jnp.concatenate([zeros_half, peer_keys2[l].astype(BF16)], axis=1)
        x = peer(x.reshape(t, d), ffn_norm[l], peer_wq[l].astype(BF16), k1p, k2p,
                 peer_u[l].astype(BF16), _chunked_transpose(peer_v[l])).reshape(b, s, d)
    return x
```

```python
import functools

import jax
import jax.numpy as jnp
from jax import lax
from jax.experimental import pallas as pl
from jax.experimental.pallas import tpu as pltpu

F32 = jnp.float32
BF16 = jnp.bfloat16

EPS = 1e-6
LOG2E = 1.4426950408889634
NEG = -1e30
REMOVED = -3e38

D_MODEL = 1024
DK = 64
NSA_HEADS = 8
NSA_GROUPS = 2
NSA_REP = NSA_HEADS // NSA_GROUPS
L_CMP, D_CMP = 32, 16
CMP_HIDDEN = 128
L_SLC = 64
N_SEL = 16
WINDOW = 512
QB = 128
FORCE_BONUS = 1.0e3
SGU_WIDTH = 256
SGU_GROUPS = 4
SGU_CHUNK = 128
POOL_WIDTH = 256
POOL_WINDOWS = (2, 4, 8, 16)
XA_HEADS, XA_HD = 4, 128
XA_WIDTH = XA_HEADS * XA_HD
PEER_HEADS = 8
PEER_KEYS = 128
PEER_TOPK = 16
PEER_HALF = 64

LANES = 128
SEL_VR = 2 * DK
SEL_SUB = 512
SEL_UNROLL = 8
WIN_QBLOCKS = 2
CMP_CHUNK = 256
CMP_HALO = 8
PEER_TT = 512
PEER_CH = 1024
PEER_PROJ_ROWS = 256
VMEM_LIMIT = 56 * 1024 * 1024


def _cp(sem):
    return pltpu.CompilerParams(dimension_semantics=sem, vmem_limit_bytes=VMEM_LIMIT)


def _gelu(x):
    return 0.5 * x * (1.0 + jnp.tanh(0.7978845608028654 * (x + 0.044715 * (x * x * x))))


def _rms(x, g):
    return x * lax.rsqrt(jnp.mean(x * x, axis=-1, keepdims=True) + EPS) * g


def _dot(a, b):
    return jnp.dot(a, b, preferred_element_type=F32)


def _dot_nt(a, b):
    return lax.dot_general(a, b, (((1,), (1,)), ((), ())), preferred_element_type=F32)


def _iota(shape, dim):
    return lax.broadcasted_iota(jnp.int32, shape, dim)


def _div_pow2(x, n):
    assert n & (n - 1) == 0
    return lax.shift_right_logical(x, jnp.int32(n.bit_length() - 1))


def _mod_pow2(x, n):
    assert n & (n - 1) == 0
    return x & (n - 1)


def _norm_matmul_kernel(x_ref, g_ref, w_ref, *o_refs, splits):
    hb = _rms(x_ref[...], g_ref[...]).astype(BF16)
    for o_ref, (a, b) in zip(o_refs, splits):
        o_ref[...] = _dot(hb, w_ref[:, a:b]).astype(o_ref.dtype)


def norm_matmul(x, g, w, splits, tm, name):
    t, d = x.shape
    n = w.shape[1]
    return pl.pallas_call(
        functools.partial(_norm_matmul_kernel, splits=splits),
        grid=(t // tm,),
        in_specs=[pl.BlockSpec((tm, d), lambda i: (i, 0)),
                  pl.BlockSpec((1, d), lambda i: (0, 0)),
                  pl.BlockSpec((d, n), lambda i: (0, 0))],
        out_specs=[pl.BlockSpec((tm, b - a), lambda i: (i, 0)) for a, b in splits],
        out_shape=[jax.ShapeDtypeStruct((t, b - a), F32) for a, b in splits],
        compiler_params=_cp(("parallel",)),
        name=name,
    )(x, g.reshape(1, d), w)


def _nsa_prep_kernel(zq_ref, zkv_ref, qn_ref, kn_ref,
                     q_out, kvc_out, ks_out, vs_out, kw_out, vw_out, *, ts):
    zq = zq_ref[0]
    scale = DK ** -0.5 * LOG2E
    for h in range(NSA_HEADS):
        q_out[0, h] = (_rms(zq[:, h * DK:(h + 1) * DK], qn_ref[...]) * scale).astype(BF16)
    zkv = zkv_ref[0]

    def piece(i, g):
        o = i * NSA_GROUPS * DK + g * DK
        return zkv[:, o:o + DK]

    ones_col = jnp.where(_iota((ts, SEL_VR - DK), 1) == 0, 1.0, 0.0)
    for g in range(NSA_GROUPS):
        kvc_out[0, 0, g] = piece(0, g)
        kvc_out[1, 0, g] = piece(1, g)
        ks_out[0, g] = _rms(piece(2, g), kn_ref[1:2, :]).astype(BF16)
        vs_out[0, g] = jnp.concatenate([piece(3, g), ones_col], axis=-1).T.astype(BF16)
        kw_out[0, g] = _rms(piece(4, g), kn_ref[2:3, :]).astype(BF16)
        vw_out[0, g] = jnp.concatenate([piece(5, g), ones_col], axis=-1).T.astype(BF16)


def nsa_prep(zq, zkv, q_norm, k_norm, ts=512):
    b, s, _ = zq.shape
    g = NSA_GROUPS
    hm = lambda bi, si: (bi, 0, si, 0)
    return pl.pallas_call(
        functools.partial(_nsa_prep_kernel, ts=ts),
        grid=(b, s // ts),
        in_specs=[pl.BlockSpec((1, ts, NSA_HEADS * DK), lambda bi, si: (bi, si, 0)),
                  pl.BlockSpec((1, ts, 6 * g * DK), lambda bi, si: (bi, si, 0)),
                  pl.BlockSpec((1, DK), lambda bi, si: (0, 0)),
                  pl.BlockSpec((3, DK), lambda bi, si: (0, 0))],
        out_specs=[pl.BlockSpec((1, NSA_HEADS, ts, DK), hm),
                   pl.BlockSpec((2, 1, g, ts, DK), lambda bi, si: (0, bi, 0, si, 0)),
                   pl.BlockSpec((1, g, ts, DK), hm),
                   pl.BlockSpec((1, g, SEL_VR, ts), lambda bi, si: (bi, 0, 0, si)),
                   pl.BlockSpec((1, g, ts, DK), hm),
                   pl.BlockSpec((1, g, SEL_VR, ts), lambda bi, si: (bi, 0, 0, si))],
        out_shape=[jax.ShapeDtypeStruct((b, NSA_HEADS, s, DK), BF16),
                   jax.ShapeDtypeStruct((2, b, g, s, DK), F32),
                   jax.ShapeDtypeStruct((b, g, s, DK), BF16),
                   jax.ShapeDtypeStruct((b, g, SEL_VR, s), BF16),
                   jax.ShapeDtypeStruct((b, g, s, DK), BF16),
                   jax.ShapeDtypeStruct((b, g, SEL_VR, s), BF16)],
        compiler_params=_cp(("parallel", "parallel")),
        name="nsa_prep",
    )(zq, zkv, q_norm.reshape(1, DK), k_norm)


def _cmp_mlp_kernel(x_ref, pe_ref, w1_ref, w2_ref, kn_ref, k_out, vt_out, *, nc):
    c = pl.program_id(2)
    x = x_ref[0, 0, 0]
    a = _dot((x + pe_ref[0, 0:1, :]).astype(BF16), w1_ref[0, 0])
    b = _dot((x + pe_ref[0, 1:2, :]).astype(BF16), w1_ref[0, 1])
    pre = a + pltpu.roll(b, nc - 1, 0)
    comp = _dot(_gelu(pre).astype(BF16), w2_ref[0])

    @pl.when(c == 0)
    def _():
        k_out[0, 0] = _rms(comp, kn_ref[...]).astype(BF16)

    @pl.when(c == 1)
    def _():
        ones_col = jnp.where(_iota((nc, SEL_VR - DK), 1) == 0, 1.0, 0.0)
        vt_out[0, 0] = jnp.concatenate([comp, ones_col], axis=-1).T.astype(BF16)


def cmp_mlp(xc, pe, w1, w2, k_norm0):
    _, b, g, nc, width = xc.shape
    return pl.pallas_call(
        functools.partial(_cmp_mlp_kernel, nc=nc),
        grid=(b, g, 2),
        in_specs=[pl.BlockSpec((1, 1, 1, nc, width), lambda bi, gi, c: (c, bi, gi, 0, 0)),
                  pl.BlockSpec((1, 2, width), lambda bi, gi, c: (c, 0, 0)),
                  pl.BlockSpec((1, 2, width, CMP_HIDDEN), lambda bi, gi, c: (c, 0, 0, 0)),
                  pl.BlockSpec((1, CMP_HIDDEN, DK), lambda bi, gi, c: (c, 0, 0)),
                  pl.BlockSpec((1, DK), lambda bi, gi, c: (0, 0))],
        out_specs=[pl.BlockSpec((1, 1, nc, DK), lambda bi, gi, c: (bi, gi, 0, 0)),
                   pl.BlockSpec((1, 1, SEL_VR, nc), lambda bi, gi, c: (bi, gi, 0, 0))],
        out_shape=[jax.ShapeDtypeStruct((b, g, nc, DK), BF16),
                   jax.ShapeDtypeStruct((b, g, SEL_VR, nc), BF16)],
        compiler_params=_cp(("parallel", "parallel", "arbitrary")),
        name="cmp_mlp",
    )(xc, pe, w1, w2, k_norm0.reshape(1, DK))


def _cmp_attn_kernel(q_ref, k_ref, vt_ref, oc_ref, sb_ref, s_s, acc_s, ps_s, *, nc, n_slc):
    qb = pl.program_id(2)
    rows = NSA_REP * QB
    ck = CMP_CHUNK
    q = q_ref[0].reshape(rows, DK)
    n_chunk = ((qb * QB + QB) // D_CMP + ck - 1) // ck
    t = qb * QB + _mod_pow2(_iota((ck, rows), 1), QB)

    def pass_a(ci, m):
        off = pl.multiple_of(ci * ck, ck)
        st = _dot_nt(k_ref[0, 0, pl.ds(off, ck), :], q)
        n = off + _iota((ck, rows), 0)
        st = jnp.where(n * D_CMP + (L_CMP - 1) <= t, st, NEG)
        s_s[ci] = st
        return jnp.maximum(m, jnp.max(st, axis=0, keepdims=True))

    m = lax.fori_loop(0, n_chunk, pass_a, jnp.full((1, rows), NEG, F32))
    acc_s[...] = jnp.zeros_like(acc_s)
    ps_s[...] = jnp.zeros_like(ps_s)

    def pass_b(ci, den):
        off = pl.multiple_of(ci * ck, ck)
        e = jnp.exp2(s_s[ci] - m)
        s_s[ci] = e
        acc_s[...] += _dot(vt_ref[0, 0, :, pl.ds(off, ck)], e.astype(BF16))
        return den + jnp.sum(e, axis=0, keepdims=True)

    den = lax.fori_loop(0, n_chunk, pass_b, jnp.zeros((1, rows), F32))
    seen = m > 0.5 * NEG
    inv_o = jnp.where(seen, 1.0 / jnp.maximum(acc_s[DK:DK + 1, :], 1e-30), 0.0)
    inv_p = jnp.where(seen, 1.0 / jnp.maximum(den, 1e-30), 0.0)
    oc_ref[0] = (acc_s[...] * inv_o).T[:, 0:DK].reshape(NSA_REP, QB, DK)

    def pass_c(ci, carry):
        off = pl.multiple_of(ci * ck, ck)
        p = s_s[ci] * inv_p
        ps = p[:, 0:QB]
        for r in range(1, NSA_REP):
            ps = ps + p[:, r * QB:(r + 1) * QB]
        ps_s[pl.ds(CMP_HALO + off, ck), :] = ps
        return carry

    lax.fori_loop(0, n_chunk, pass_c, 0)

    ratio = L_SLC // D_CMP
    imp = jnp.zeros((n_slc, QB), F32)
    for k in range(-(L_CMP // D_CMP - 1), ratio):
        overlap = min(k * D_CMP + L_CMP, L_SLC) - max(k * D_CMP, 0)
        imp = imp + (overlap / D_CMP) * ps_s[pl.ds(CMP_HALO + k, n_slc, stride=ratio), :]

    cur = _div_pow2(qb * QB + _iota((n_slc, QB), 1), L_SLC)
    j = _iota((n_slc, QB), 0)
    forced = (j == 0) | (j == cur) | (j == cur - 1)
    score = jnp.where(j <= cur, imp + jnp.where(forced, FORCE_BONUS, 0.0), NEG)
    work = score
    tau = None
    for _ in range(N_SEL):
        tau = jnp.max(work, axis=0, keepdims=True)
        work = jnp.where(work >= tau, REMOVED, work)
    selected = (score > 0.5 * NEG) & (score >= tau)
    sb_ref[0, 0, 0] = jnp.where(selected, 0.0, NEG)


def cmp_attn_select(qn, kc, vct):
    b, h, s, _ = qn.shape
    g = NSA_GROUPS
    nc = kc.shape[2]
    n_slc = s // L_SLC
    nqb = s // QB
    return pl.pallas_call(
        functools.partial(_cmp_attn_kernel, nc=nc, n_slc=n_slc),
        grid=(b, g, nqb),
        in_specs=[pl.BlockSpec((1, NSA_REP, QB, DK), lambda bi, gi, qi: (bi, gi, qi, 0)),
                  pl.BlockSpec((1, 1, nc, DK), lambda bi, gi, qi: (bi, gi, 0, 0)),
                  pl.BlockSpec((1, 1, SEL_VR, nc), lambda bi, gi, qi: (bi, gi, 0, 0))],
        out_specs=[pl.BlockSpec((1, NSA_REP, QB, DK), lambda bi, gi, qi: (bi, gi, qi, 0)),
                   pl.BlockSpec((1, 1, 1, n_slc, QB), lambda bi, gi, qi: (bi, gi, qi, 0, 0))],
        out_shape=[jax.ShapeDtypeStruct((b, h, s, DK), F32),
                   jax.ShapeDtypeStruct((b, g, nqb, n_slc, QB), F32)],
        scratch_shapes=[pltpu.VMEM((nc // CMP_CHUNK, CMP_CHUNK, NSA_REP * QB), F32),
                        pltpu.VMEM((SEL_VR, NSA_REP * QB), F32),
                        pltpu.VMEM((CMP_HALO + nc, QB), F32)],
        compiler_params=_cp(("parallel", "parallel", "parallel")),
        name="cmp_attn_select",
    )(qn, kc, vct)


def _win_attn_kernel(q_ref, *refs):
    nkb = WINDOW // QB + 1
    nrefs = nkb + WIN_QBLOCKS - 1
    k_refs, v_refs, o_ref = refs[:nrefs], refs[nrefs:2 * nrefs], refs[2 * nrefs]
    rows = NSA_REP * QB
    nk = nkb * QB
    for hf in range(WIN_QBLOCKS):
        qb = pl.program_id(2) * WIN_QBLOCKS + hf
        qs = slice(hf * QB, (hf + 1) * QB)
        q = q_ref[0, :, qs, :].reshape(rows, DK)
        k = jnp.concatenate([r[0, 0] for r in k_refs[hf:hf + nkb]], axis=0)
        vt = jnp.concatenate([r[0, 0] for r in v_refs[hf:hf + nkb]], axis=-1)
        st = _dot_nt(k, q)
        t = qb * QB + _mod_pow2(_iota((nk, rows), 1), QB)
        kpos = (qb - (nkb - 1)) * QB + _iota((nk, rows), 0)
        delta = t - kpos
        valid = (delta >= 0) & (delta < WINDOW) & (kpos >= 0)
        st = jnp.where(valid, st, NEG)
        e = jnp.exp2(st - jnp.max(st, axis=0, keepdims=True)).astype(BF16)
        acc = _dot(vt, e)
        o = (acc / jnp.maximum(acc[DK:DK + 1, :], 1e-30)).T[:, 0:DK]
        o_ref[0, :, qs, :] = o.reshape(NSA_REP, QB, DK)


def win_attn(qn, kw, vw):
    b, h, s, _ = qn.shape
    g = NSA_GROUPS
    nkb = WINDOW // QB + 1
    nrefs = nkb + WIN_QBLOCKS - 1
    nqb = s // QB

    def first_block(qi, i):
        return jnp.maximum(qi * WIN_QBLOCKS - (nkb - 1) + i, 0)

    def k_spec(i):
        return pl.BlockSpec((1, 1, QB, DK), lambda bi, gi, qi: (bi, gi, first_block(qi, i), 0))

    def vt_spec(i):
        return pl.BlockSpec((1, 1, SEL_VR, QB), lambda bi, gi, qi: (bi, gi, 0, first_block(qi, i)))

    qspec = pl.BlockSpec((1, NSA_REP, WIN_QBLOCKS * QB, DK), lambda bi, gi, qi: (bi, gi, qi, 0))
    return pl.pallas_call(
        _win_attn_kernel,
        grid=(b, g, nqb // WIN_QBLOCKS),
        in_specs=[qspec] + [k_spec(i) for i in range(nrefs)] + [vt_spec(i) for i in range(nrefs)],
        out_specs=qspec,
        out_shape=jax.ShapeDtypeStruct((b, h, s, DK), F32),
        compiler_params=_cp(("parallel", "parallel", "parallel")),
        name="win_attn",
    )(qn, *([kw] * nrefs), *([vw] * nrefs))


def _sel_attn_kernel(q_ref, sb_ref, k_ref, vt_ref, oc_ref, ow_ref, gl_ref, o_ref,
                     m_s, acc_s, s0_s, s1_s, p0_s, p1_s, a0_s, a1_s):
    qb = pl.program_id(2)
    rows = NSA_REP * QB
    blocks = SEL_SUB // L_SLC
    qv = q_ref[0].reshape(rows, DK)
    nsub = (qb * QB + QB + SEL_SUB - 1) // SEL_SUB
    m_s[...] = jnp.full_like(m_s, NEG)
    acc_s[...] = jnp.zeros_like(acc_s)

    def scores(i, s_ref):
        off = pl.multiple_of(i * SEL_SUB, SEL_SUB)
        s_ref[...] = _dot_nt(k_ref[0, 0, pl.ds(off, SEL_SUB), :], qv)

    def weights(i, s_ref, p_ref, a_ref, causal):
        bias = sb_ref[0, 0, 0, pl.ds(pl.multiple_of(i * blocks, blocks), blocks), :]
        bias = jnp.broadcast_to(bias[:, None, :], (blocks, L_SLC, QB)).reshape(SEL_SUB, QB)
        st = s_ref[...] + jnp.concatenate([bias] * NSA_REP, axis=1)
        if causal:
            kpos = i * SEL_SUB + _iota((SEL_SUB, rows), 0)
            t = qb * QB + _mod_pow2(_iota((SEL_SUB, rows), 1), QB)
            st = jnp.where(kpos <= t, st, NEG)
        m_old = m_s[...]
        m_new = jnp.maximum(m_old, jnp.max(st, axis=0, keepdims=True))
        p_ref[...] = jnp.exp2(st - m_new).astype(BF16)
        a_ref[...] = jnp.exp2(m_old - m_new)
        m_s[...] = m_new

    def accumulate(i, p_ref, a_ref):
        off = pl.multiple_of(i * SEL_SUB, SEL_SUB)
        acc_s[...] = a_ref[...] * acc_s[...] + _dot(vt_ref[0, 0, :, pl.ds(off, SEL_SUB)], p_ref[...])

    p1_s[...] = jnp.zeros_like(p1_s)
    a1_s[...] = jnp.ones_like(a1_s)
    scores(0, s0_s)

    def two_tiles(i):
        scores(i + 1, s1_s)
        weights(i, s0_s, p0_s, a0_s, False)
        accumulate(jnp.maximum(i - 1, 0), p1_s, a1_s)
        scores(i + 2, s0_s)
        weights(i + 1, s1_s, p1_s, a1_s, False)
        accumulate(i, p0_s, a0_s)

    e = 0
    left = nsub - 1
    group = SEL_UNROLL
    while group >= 2:
        n_group = left // group

        def body(j, carry, group=group, base=e):
            for k in range(0, group, 2):
                two_tiles(base + group * j + k)
            return carry

        lax.fori_loop(0, n_group, body, 0)
        e = e + group * n_group
        left = left - group * n_group
        group //= 2
    odd_tail = left

    @pl.when(odd_tail == 0)
    def _():
        weights(e, s0_s, p0_s, a0_s, True)
        accumulate(jnp.maximum(e - 1, 0), p1_s, a1_s)
        accumulate(e, p0_s, a0_s)

    @pl.when(odd_tail == 1)
    def _():
        scores(e + 1, s1_s)
        weights(e, s0_s, p0_s, a0_s, False)
        accumulate(jnp.maximum(e - 1, 0), p1_s, a1_s)
        weights(e + 1, s1_s, p1_s, a1_s, True)
        accumulate(e, p0_s, a0_s)
        accumulate(e + 1, p1_s, a1_s)

    o_s = (acc_s[...] / jnp.maximum(acc_s[DK:DK + 1, :], 1e-30)).T[:, 0:DK]
    gate = jax.nn.sigmoid(gl_ref[0].reshape(rows, 3))
    out = (gate[:, 0:1] * oc_ref[0].reshape(rows, DK) + gate[:, 1:2] * o_s
           + gate[:, 2:3] * ow_ref[0].reshape(rows, DK))
    o_ref[0] = out.reshape(NSA_REP, QB, DK)


def sel_attn(qn, selb, ks, vs, oc, ow, gl):
    b, h, s, _ = qn.shape
    g = NSA_GROUPS
    nqb = s // QB
    n_slc = s // L_SLC
    qmap = lambda bi, gi, qi: (bi, gi, qi, 0)
    return pl.pallas_call(
        _sel_attn_kernel,
        grid=(b, g, nqb),
        in_specs=[pl.BlockSpec((1, NSA_REP, QB, DK), qmap),
                  pl.BlockSpec((1, 1, 1, n_slc, QB), lambda bi, gi, qi: (bi, gi, qi, 0, 0)),
                  pl.BlockSpec((1, 1, s, DK), lambda bi, gi, qi: (bi, gi, 0, 0)),
                  pl.BlockSpec((1, 1, SEL_VR, s), lambda bi, gi, qi: (bi, gi, 0, 0)),
                  pl.BlockSpec((1, NSA_REP, QB, DK), qmap),
                  pl.BlockSpec((1, NSA_REP, QB, DK), qmap),
                  pl.BlockSpec((1, NSA_REP, QB, 3), qmap)],
        out_specs=pl.BlockSpec((1, NSA_REP, QB, DK), qmap),
        out_shape=jax.ShapeDtypeStruct((b, h, s, DK), F32),
        scratch_shapes=[pltpu.VMEM((1, NSA_REP * QB), F32),
                        pltpu.VMEM((SEL_VR, NSA_REP * QB), F32),
                        pltpu.VMEM((SEL_SUB, NSA_REP * QB), F32),
                        pltpu.VMEM((SEL_SUB, NSA_REP * QB), F32),
                        pltpu.VMEM((SEL_SUB, NSA_REP * QB), BF16),
                        pltpu.VMEM((SEL_SUB, NSA_REP * QB), BF16),
                        pltpu.VMEM((1, NSA_REP * QB), F32),
                        pltpu.VMEM((1, NSA_REP * QB), F32)],
        compiler_params=_cp(("parallel", "parallel", "parallel")),
        name="sel_attn",
    )(qn, selb, ks, vs, oc, ow, gl)


def _sgu_kernel(z_ref, g_ref, b_ref, w_ref, bias_ref, o_ref, *, tm):
    z = _gelu(z_ref[...])
    u = z[:, :SGU_WIDTH]
    v = z[:, SGU_WIDTH:]
    mu = jnp.mean(v, axis=-1, keepdims=True)
    var = jnp.mean(jnp.square(v - mu), axis=-1, keepdims=True)
    vn = ((v - mu) * lax.rsqrt(var + EPS) * g_ref[...] + b_ref[...]).astype(BF16)
    tril = _iota((SGU_CHUNK, SGU_CHUNK), 1) <= _iota((SGU_CHUNK, SGU_CHUNK), 0)
    wm = [jnp.where(tril, w_ref[gi], 0.0).astype(BF16) for gi in range(SGU_GROUPS)]
    grp = _div_pow2(_iota((SGU_CHUNK, SGU_WIDTH), 1), SGU_WIDTH // SGU_GROUPS)
    for ch in range(tm // SGU_CHUNK):
        rs = slice(ch * SGU_CHUNK, (ch + 1) * SGU_CHUNK)
        mixed = bias_ref[...]
        for gi in range(SGU_GROUPS):
            mixed = mixed + jnp.where(grp == gi, _dot(wm[gi], vn[rs]), 0.0)
        o_ref[rs, :] = u[rs] * mixed


def sgu(z, ln_g, ln_b, w, bias_full, tm=512):
    t = z.shape[0]
    return pl.pallas_call(
        functools.partial(_sgu_kernel, tm=tm),
        grid=(t // tm,),
        in_specs=[pl.BlockSpec((tm, 2 * SGU_WIDTH), lambda i: (i, 0)),
                  pl.BlockSpec((1, SGU_WIDTH), lambda i: (0, 0)),
                  pl.BlockSpec((1, SGU_WIDTH), lambda i: (0, 0)),
                  pl.BlockSpec((SGU_GROUPS, SGU_CHUNK, SGU_CHUNK), lambda i: (0, 0, 0)),
                  pl.BlockSpec((SGU_CHUNK, SGU_WIDTH), lambda i: (0, 0))],
        out_specs=pl.BlockSpec((tm, SGU_WIDTH), lambda i: (i, 0)),
        out_shape=jax.ShapeDtypeStruct((t, SGU_WIDTH), F32),
        compiler_params=_cp(("parallel",)),
        name="sgu",
    )(z, ln_g.reshape(1, -1), ln_b.reshape(1, -1), w, bias_full)


def _pool_kernel(p_ref, h_ref, w_ref, sc_ref, o_ref, *, tp):
    i = pl.program_id(1)
    halo_rows = POOL_WINDOWS[-1]
    p = p_ref[0]
    halo = jnp.where(i > 0, h_ref[0], 0.0)
    ext = jnp.concatenate([halo, p], axis=0)
    sums = [ext]
    shift = 1
    for _ in POOL_WINDOWS:
        prev = sums[-1]
        sums.append(prev + pltpu.roll(prev, shift, 0))
        shift *= 2
    pos = i * tp + _iota((tp, POOL_WIDTH), 0)
    grp = _div_pow2(_iota((tp, POOL_WIDTH), 1), POOL_WIDTH // len(POOL_WINDOWS))
    d = jnp.zeros((tp, POOL_WIDTH), F32)
    for gi, w in enumerate(POOL_WINDOWS):
        mean = sums[gi + 1][halo_rows:halo_rows + tp] / jnp.minimum(pos + 1, w).astype(F32)
        d = jnp.where(grp == gi, mean, d)
    d = d - p
    o_ref[0] = _dot(d.astype(BF16), w_ref[...]) * sc_ref[...]


def pool(p, w_bd, scale, tp=512):
    b, s, c = p.shape
    halo_rows = POOL_WINDOWS[-1]
    return pl.pallas_call(
        functools.partial(_pool_kernel, tp=tp),
        grid=(b, s // tp),
        in_specs=[pl.BlockSpec((1, tp, c), lambda bi, i: (bi, i, 0)),
                  pl.BlockSpec((1, halo_rows, c),
                               lambda bi, i: (bi, jnp.maximum(i * (tp // halo_rows) - 1, 0), 0)),
                  pl.BlockSpec((c, c), lambda bi, i: (0, 0)),
                  pl.BlockSpec((1, c), lambda bi, i: (0, 0))],
        out_specs=pl.BlockSpec((1, tp, c), lambda bi, i: (bi, i, 0)),
        out_shape=jax.ShapeDtypeStruct((b, s, c), F32),
        compiler_params=_cp(("parallel", "parallel")),
        name="pool",
    )(p, p, w_bd, scale.reshape(1, c))


def _merge_kernel(x_ref, oa_ref, ob_ref, oc_ref, zm_ref, la_ref, lb_ref, lc_ref, wo_ref, o_ref):
    d = D_MODEL
    oa = jnp.concatenate([oa_ref[0, h].astype(BF16) for h in range(NSA_HEADS)], axis=-1)
    zm = zm_ref[0]
    merged = (jax.nn.sigmoid(zm[:, 0:d]) * _dot(oa, la_ref[...])
              + jax.nn.sigmoid(zm[:, d:2 * d]) * _dot(ob_ref[0].astype(BF16), lb_ref[...])
              + jax.nn.sigmoid(zm[:, 2 * d:3 * d]) * _dot(oc_ref[0].astype(BF16), lc_ref[...]))
    o_ref[0] = x_ref[0] + _dot(merged.astype(BF16), wo_ref[...])


def merge(x, oa, ob, oc, zm, la, lb, lc, wo, tm=512):
    b, s, d = x.shape
    row = lambda bi, i: (bi, i, 0)
    full = lambda bi, i: (0, 0)
    return pl.pallas_call(
        _merge_kernel,
        grid=(b, s // tm),
        in_specs=[pl.BlockSpec((1, tm, d), row),
                  pl.BlockSpec((1, NSA_HEADS, tm, DK), lambda bi, i: (bi, 0, i, 0)),
                  pl.BlockSpec((1, tm, SGU_WIDTH), row),
                  pl.BlockSpec((1, tm, POOL_WIDTH), row),
                  pl.BlockSpec((1, tm, 3 * d), row),
                  pl.BlockSpec(la.shape, full), pl.BlockSpec(lb.shape, full),
                  pl.BlockSpec(lc.shape, full), pl.BlockSpec(wo.shape, full)],
        out_specs=pl.BlockSpec((1, tm, d), row),
        out_shape=jax.ShapeDtypeStruct((b, s, d), F32),
        compiler_params=_cp(("parallel", "parallel")),
        name="merge",
    )(x, oa, ob, oc, zm, la, lb, lc, wo)


def _mem_kv_kernel(m_ref, g_ref, wk_ref, wv_ref, kn_ref, k_out, v_out):
    mh = _rms(m_ref[0], g_ref[...]).astype(BF16)
    k = _dot(mh, wk_ref[...])
    for h in range(XA_HEADS):
        hs = slice(h * XA_HD, (h + 1) * XA_HD)
        k_out[0, :, hs] = _rms(k[:, hs], kn_ref[...]).astype(BF16)
    v_out[0] = _dot(mh, wv_ref[...]).astype(BF16)


def mem_kv(mem, g, wk, wv, k_norm):
    b, m, d = mem.shape
    full = lambda bi: (0, 0)
    return pl.pallas_call(
        _mem_kv_kernel,
        grid=(b,),
        in_specs=[pl.BlockSpec((1, m, d), lambda bi: (bi, 0, 0)),
                  pl.BlockSpec((1, d), full),
                  pl.BlockSpec(wk.shape, full), pl.BlockSpec(wv.shape, full),
                  pl.BlockSpec((1, XA_HD), full)],
        out_specs=[pl.BlockSpec((1, m, XA_WIDTH), lambda bi: (bi, 0, 0))] * 2,
        out_shape=[jax.ShapeDtypeStruct((b, m, XA_WIDTH), BF16)] * 2,
        compiler_params=_cp(("parallel",)),
        name="mem_kv",
    )(mem, g.reshape(1, d), wk, wv, k_norm.reshape(1, XA_HD))


def _xattn_kernel(x_ref, g_ref, wq_ref, qn_ref, k_ref, v_ref, wo_ref, o_ref):
    x = x_ref[0]
    q = _dot(_rms(x, g_ref[...]).astype(BF16), wq_ref[...])
    k = k_ref[0]
    v = v_ref[0]
    outs = []
    for h in range(XA_HEADS):
        hs = slice(h * XA_HD, (h + 1) * XA_HD)
        qh = (_rms(q[:, hs], qn_ref[...])).astype(BF16)
        s = _dot_nt(qh, k[:, hs]) * (XA_HD ** -0.5)
        e = jnp.exp(s - jnp.max(s, axis=-1, keepdims=True))
        p = e / jnp.sum(e, axis=-1, keepdims=True)
        outs.append(_dot(p.astype(BF16), v[:, hs]).astype(BF16))
    o = jnp.concatenate(outs, axis=-1)
    o_ref[0] = x + _dot(o, wo_ref[...])


def xattn(x, g, wq, q_norm, k, v, wo, tm=512):
    b, s, d = x.shape
    m = k.shape[1]
    row = lambda bi, i: (bi, i, 0)
    full = lambda bi, i: (0, 0)
    return pl.pallas_call(
        _xattn_kernel,
        grid=(b, s // tm),
        in_specs=[pl.BlockSpec((1, tm, d), row),
                  pl.BlockSpec((1, d), full),
                  pl.BlockSpec(wq.shape, full),
                  pl.BlockSpec((1, XA_HD), full),
                  pl.BlockSpec((1, m, XA_WIDTH), lambda bi, i: (bi, 0, 0)),
                  pl.BlockSpec((1, m, XA_WIDTH), lambda bi, i: (bi, 0, 0)),
                  pl.BlockSpec(wo.shape, full)],
        out_specs=pl.BlockSpec((1, tm, d), row),
        out_shape=jax.ShapeDtypeStruct((b, s, d), F32),
        compiler_params=_cp(("parallel", "parallel")),
        name="xattn",
    )(x, g.reshape(1, d), wq, q_norm.reshape(1, XA_HD), k, v, wo)


def _top_rows(s, n, want_rank):
    out = []
    work = s
    rank = jnp.full(s.shape, float(s.shape[0] - 1), F32) if want_rank else None
    for i in range(n):
        m = jnp.max(work, axis=0, keepdims=True)
        out.append(m)
        hit = work >= m
        if want_rank:
            rank = jnp.where(hit, float(i), rank)
        work = jnp.where(hit, REMOVED, work)
    return out, rank


def _stack_rows(rows, pad_rows):
    tt = rows[0].shape[1]
    rowi = _iota((pad_rows, tt), 0)
    out = jnp.full((pad_rows, tt), REMOVED, F32)
    for i, r in enumerate(rows[:pad_rows]):
        out = jnp.where(rowi == i, r, out)
    return out


def _pair_sum_candidates(v1, v2, tt):
    row_full = _iota((24, tt), 0)
    row = _iota((8, tt), 0)
    v2_full = _stack_rows(v2, 24)
    v2_8 = _stack_rows(v2, 8)

    def shifted(k):
        return pltpu.roll(v2_8, k, 0)

    def pick(rows_from, first):
        out = v1[rows_from]
        for r in range(first + 1, 8):
            if rows_from + r - first < len(v1):
                out = jnp.where(row == r, v1[rows_from + r - first], out)
        return out

    return jnp.concatenate([
        jnp.where(row_full < 17, v1[0] + v2_full, REMOVED),
        v1[1] + v2_8,
        jnp.where(row < 5, v1[2] + v2_8,
                  jnp.where(row < 7, v1[5] + shifted(5), v1[8] + shifted(7))),
        jnp.where(row < 4, v1[3] + v2_8,
                  jnp.where(row < 7, v1[4] + shifted(4), v1[9] + shifted(7))),
        jnp.where(row < 2, v1[6] + v2_8,
                  jnp.where(row < 4, v1[7] + shifted(2), pick(10, 4) + v2[0])),
        jnp.where(row < 3, pick(14, 0) + v2[0], REMOVED),
    ], axis=0)


def _peer_kernel(x_ref, g_ref, wq_ref, k1_ref, k2_ref, u_ref, vt_ref, o_ref,
                 hbt_s, q_s, r2_s, na_s, e1_s, e2_s, acc_s, *, tt, ch):
    c = pl.program_id(1)
    ntop = PEER_TOPK + 1
    assert ntop == 17

    @pl.when(c == 0)
    def _():
        h = _rms(x_ref[...], g_ref[...])
        hbt_s[...] = h.T.astype(BF16)
        q = _dot(h.astype(BF16), wq_ref[...])
        for hd in range(PEER_HEADS):
            q_s[hd] = q[:, hd * 2 * PEER_HALF:(hd + 1) * 2 * PEER_HALF].astype(BF16)
        acc_s[...] = jnp.zeros_like(acc_s)

        def route(hd, carry):
            qh = q_s[hd]
            s1 = _dot_nt(k1_ref[...], qh)
            s2 = _dot_nt(k2_ref[...], qh)
            v1, _ = _top_rows(s1, ntop, False)
            v2, rank2 = _top_rows(s2, ntop, True)
            cand = _pair_sum_candidates(v1, v2, tt)
            top = v1[0] + v2[0]
            z = jnp.zeros((1, tt), F32)
            c_prev = top
            c_last = top
            work = cand
            for r in range(ntop):
                m = jnp.max(work, axis=0, keepdims=True)
                work = jnp.where(work >= m, REMOVED, work)
                if r < PEER_TOPK:
                    z = z + jnp.exp(m - top)
                c_prev, c_last = c_last, m
            thr = 0.5 * (c_prev + c_last)
            na = jnp.zeros((PEER_KEYS, tt), F32)
            for j in range(ntop):
                na = jnp.where(s1 + v2[j] >= thr, float(j + 1), na)
            r2_s[hd] = rank2.astype(BF16)
            na_s[hd] = na
            e1_s[hd] = jnp.exp(s1 - v1[0])
            e2_s[hd] = (jnp.exp(s2 - v2[0]) / z).astype(BF16)
            return carry

        lax.fori_loop(0, PEER_HEADS, route, 0, unroll=2)

    n_slab = ch // PEER_KEYS
    slabs_per_group = PEER_PROJ_ROWS // PEER_KEYS
    n_group = ch // PEER_PROJ_ROWS

    def project(gi):
        rs = slice(gi * PEER_PROJ_ROWS, (gi + 1) * PEER_PROJ_ROWS)
        return _gelu(_dot(u_ref[rs, :], hbt_s[...]).astype(BF16))

    def gate(al, gl_rows):
        a = c * n_slab + al
        wc = jnp.zeros((PEER_KEYS, tt), BF16)
        for hd in range(PEER_HEADS):
            na = na_s[hd, pl.ds(a, 1), :].astype(BF16)
            e1 = e1_s[hd, pl.ds(a, 1), :].astype(BF16)
            wc = wc + jnp.where(r2_s[hd] < na, e2_s[hd] * e1, jnp.zeros((), BF16))
        return wc * gl_rows

    ys = []
    gl_next = project(0)
    for gi in range(n_group):
        gl = gl_next
        if gi + 1 < n_group:
            gl_next = project(gi + 1)
        for k in range(slabs_per_group):
            ys.append(gate(gi * slabs_per_group + k, gl[k * PEER_KEYS:(k + 1) * PEER_KEYS]))
    y = jnp.concatenate(ys, axis=0)
    acc_s[...] += _dot(vt_ref[0], y)

    @pl.when(c == pl.num_programs(1) - 1)
    def _():
        o_ref[...] = x_ref[...] + acc_s[...].T


def peer(x, g, wq, k1p, k2p, u, vt, tt=PEER_TT, ch=PEER_CH):
    t, d = x.shape
    ne = u.shape[0]
    return pl.pallas_call(
        functools.partial(_peer_kernel, tt=tt, ch=ch),
        grid=(t // tt, ne // ch),
        in_specs=[pl.BlockSpec((tt, d), lambda i, c: (i, 0)),
                  pl.BlockSpec((1, d), lambda i, c: (0, 0)),
                  pl.BlockSpec(wq.shape, lambda i, c: (0, 0)),
                  pl.BlockSpec(k1p.shape, lambda i, c: (0, 0)),
                  pl.BlockSpec(k2p.shape, lambda i, c: (0, 0)),
                  pl.BlockSpec((ch, d), lambda i, c: (c, 0)),
                  pl.BlockSpec((1, d, ch), lambda i, c: (c, 0, 0))],
        out_specs=pl.BlockSpec((tt, d), lambda i, c: (i, 0)),
        out_shape=jax.ShapeDtypeStruct((t, d), F32),
        scratch_shapes=[pltpu.VMEM((d, tt), BF16),
                        pltpu.VMEM((PEER_HEADS, tt, 2 * PEER_HALF), BF16),
                        pltpu.VMEM((PEER_HEADS, PEER_KEYS, tt), BF16),
                        pltpu.VMEM((PEER_HEADS, PEER_KEYS, tt), F32),
                        pltpu.VMEM((PEER_HEADS, PEER_KEYS, tt), F32),
                        pltpu.VMEM((PEER_HEADS, PEER_KEYS, tt), BF16),
                        pltpu.VMEM((d, tt), F32)],
        compiler_params=_cp(("parallel", "arbitrary")),
        name="peer",
    )(x, g.reshape(1, d), wq, k1p, k2p, u, vt)


_IN_WIDTHS = (NSA_HEADS * DK, 6 * NSA_GROUPS * DK, 3 * NSA_HEADS, 2 * SGU_WIDTH, POOL_WIDTH, 3 * D_MODEL)


def _in_proj_layout():
    splits, src = [], []
    o_src = o_dst = 0
    for w in _IN_WIDTHS:
        wp = -(-w // LANES) * LANES
        splits.append((o_dst, o_dst + wp))
        src.append((o_src, o_src + w))
        o_src += w
        o_dst += wp
    return splits, src, o_dst


def _chunked_transpose(v):
    ne, d = v.shape
    return v.astype(BF16).reshape(ne // PEER_CH, PEER_CH, d).transpose(0, 2, 1)


def nsa_layer(zq, zkv, zg, q_norm, k_norm, cmp_pe, cmp_w1, cmp_w2, b, s):
    qn, kvc, ks, vs, kw, vw = nsa_prep(zq.reshape(b, s, -1), zkv.reshape(b, s, -1), q_norm, k_norm)
    nc = s // D_CMP
    xc = kvc.reshape(2, b, NSA_GROUPS, nc, D_CMP * DK)
    pe = cmp_pe.reshape(2, 2, D_CMP * DK)
    w1 = cmp_w1.reshape(2, 2, D_CMP * DK, CMP_HIDDEN).astype(BF16)
    kc, vct = cmp_mlp(xc, pe, w1, cmp_w2.astype(BF16), k_norm[0])
    oc, selb = cmp_attn_select(qn, kc, vct)
    ow = win_attn(qn, kw, vw)
    gl = zg[:, :3 * NSA_HEADS].reshape(b, s, NSA_HEADS, 3).transpose(0, 2, 1, 3)
    return sel_attn(qn, selb, ks, vs, oc, ow, gl)


def kernel(x, mem, mix_norm, w_in, nsa_q_norm, nsa_k_norm, cmp_pe, cmp_w1, cmp_w2, sgu_ln_g, sgu_ln_b, sgu_w, sgu_b, pool_w, pool_scale, lift_a, lift_b, lift_c, w_out, xa_norm, mem_norm, xa_wq, xa_wk, xa_wv, xa_q_norm, xa_k_norm, xa_wo, ffn_norm, peer_wq, peer_keys1, peer_keys2, peer_u, peer_v):
    b, s, d = x.shape
    t = b * s
    depth = w_in.shape[0]
    splits, src, n_pad = _in_proj_layout()
    zeros_half = jnp.zeros((PEER_KEYS, PEER_HALF), BF16)
    for l in range(depth):
        w_parts = []
        for (a0, a1), (d0, d1) in zip(src, splits):
            w_parts.append(jnp.pad(w_in[l][:, a0:a1], ((0, 0), (0, (d1 - d0) - (a1 - a0)))))
        w_pad = jnp.concatenate(w_parts, axis=1).astype(BF16)
        zq, zkv, zg, zs, zp, zm = norm_matmul(x.reshape(t, d), mix_norm[l], w_pad, splits, 256, "in_proj")

        oa = nsa_layer(zq, zkv, zg, nsa_q_norm[l], nsa_k_norm[l], cmp_pe[l], cmp_w1[l], cmp_w2[l], b, s)
        bias_full = jnp.repeat(sgu_b[l].T, SGU_WIDTH // SGU_GROUPS, axis=1)
        ob = sgu(zs, sgu_ln_g[l], sgu_ln_b[l], sgu_w[l], bias_full)
        cg = POOL_WIDTH // len(POOL_WINDOWS)
        w_bd = jnp.zeros((POOL_WIDTH, POOL_WIDTH), F32)
        for gi in range(len(POOL_WINDOWS)):
            w_bd = w_bd.at[gi * cg:(gi + 1) * cg, gi * cg:(gi + 1) * cg].set(pool_w[l, gi])
        oc = pool(zp.reshape(b, s, POOL_WIDTH), w_bd.astype(BF16), pool_scale[l])
        x = merge(x, oa, ob.reshape(b, s, SGU_WIDTH), oc, zm.reshape(b, s, 3 * d),
                  lift_a[l].astype(BF16), lift_b[l].astype(BF16), lift_c[l].astype(BF16),
                  w_out[l].astype(BF16))

        mk, mv = mem_kv(mem, mem_norm[l], xa_wk[l].astype(BF16), xa_wv[l].astype(BF16), xa_k_norm[l])
        x = xattn(x, xa_norm[l], xa_wq[l].astype(BF16), xa_q_norm[l], mk, mv, xa_wo[l].astype(BF16))

        k1p = jnp.concatenate([peer_keys1[l].astype(BF16), zeros_half], axis=1)
        k2p = jnp.concatenate([zeros_half, peer_keys2[l].astype(BF16)], axis=1)
        x = peer(x.reshape(t, d), ffn_norm[l], peer_wq[l].astype(BF16), k1p, k2p,
                 peer_u[l].astype(BF16), _chunked_transpose(peer_v[l])).reshape(b, s, d)
    return x
```

```python
import functools

import jax
import jax.numpy as jnp
from jax import lax
from jax.experimental import pallas as pl
from jax.experimental.pallas import tpu as pltpu

F32 = jnp.float32
BF16 = jnp.bfloat16

EPS = 1e-6
LOG2E = 1.4426950408889634
NEG = -1e30
REMOVED = -3e38

D_MODEL = 1024
DK = 64
NSA_HEADS = 8
NSA_GROUPS = 2
NSA_REP = NSA_HEADS // NSA_GROUPS
L_CMP, D_CMP = 32, 16
CMP_HIDDEN = 128
L_SLC = 64
N_SEL = 16
WINDOW = 512
QB = 128
FORCE_BONUS = 1.0e3
SGU_WIDTH = 256
SGU_GROUPS = 4
SGU_CHUNK = 128
POOL_WIDTH = 256
POOL_WINDOWS = (2, 4, 8, 16)
XA_HEADS, XA_HD = 4, 128
XA_WIDTH = XA_HEADS * XA_HD
PEER_HEADS = 8
PEER_KEYS = 128
PEER_TOPK = 16
PEER_HALF = 64

LANES = 128
SEL_VR = 2 * DK
SEL_SUB = 512
SEL_UNROLL = 8
WIN_QBLOCKS = 4
CMP_CHUNK = 256
CMP_HALO = 8
PEER_TT = 512
PEER_CH = 1024
PEER_PROJ_ROWS = 128
VMEM_LIMIT = 56 * 1024 * 1024


def _cp(sem):
    return pltpu.CompilerParams(dimension_semantics=sem, vmem_limit_bytes=VMEM_LIMIT)


def _gelu(x):
    return 0.5 * x * (1.0 + jnp.tanh(0.7978845608028654 * (x + 0.044715 * (x * x * x))))


def _rms(x, g):
    return x * lax.rsqrt(jnp.mean(x * x, axis=-1, keepdims=True) + EPS) * g


def _dot(a, b):
    return jnp.dot(a, b, preferred_element_type=F32)


def _dot_nt(a, b):
    return lax.dot_general(a, b, (((1,), (1,)), ((), ())), preferred_element_type=F32)


def _iota(shape, dim):
    return lax.broadcasted_iota(jnp.int32, shape, dim)


def _div_pow2(x, n):
    assert n & (n - 1) == 0
    return lax.shift_right_logical(x, jnp.int32(n.bit_length() - 1))


def _mod_pow2(x, n):
    assert n & (n - 1) == 0
    return x & (n - 1)


def _norm_matmul_kernel(x_ref, g_ref, w_ref, *o_refs, splits):
    hb = _rms(x_ref[...], g_ref[...]).astype(BF16)
    for o_ref, (a, b) in zip(o_refs, splits):
        o_ref[...] = _dot(hb, w_ref[:, a:b]).astype(o_ref.dtype)


def norm_matmul(x, g, w, splits, tm, name):
    t, d = x.shape
    n = w.shape[1]
    return pl.pallas_call(
        functools.partial(_norm_matmul_kernel, splits=splits),
        grid=(t // tm,),
        in_specs=[pl.BlockSpec((tm, d), lambda i: (i, 0)),
                  pl.BlockSpec((1, d), lambda i: (0, 0)),
                  pl.BlockSpec((d, n), lambda i: (0, 0))],
        out_specs=[pl.BlockSpec((tm, b - a), lambda i: (i, 0)) for a, b in splits],
        out_shape=[jax.ShapeDtypeStruct((t, b - a), F32) for a, b in splits],
        compiler_params=_cp(("parallel",)),
        name=name,
    )(x, g.reshape(1, d), w)


def _nsa_prep_kernel(zq_ref, zkv_ref, qn_ref, kn_ref,
                     q_out, kvc_out, ks_out, vs_out, kw_out, vw_out, *, ts):
    zq = zq_ref[0]
    scale = DK ** -0.5 * LOG2E
    for h in range(NSA_HEADS):
        q_out[0, h] = (_rms(zq[:, h * DK:(h + 1) * DK], qn_ref[...]) * scale).astype(BF16)
    zkv = zkv_ref[0]

    def piece(i, g):
        o = i * NSA_GROUPS * DK + g * DK
        return zkv[:, o:o + DK]

    ones_col = jnp.where(_iota((ts, SEL_VR - DK), 1) == 0, 1.0, 0.0)
    for g in range(NSA_GROUPS):
        kvc_out[0, 0, g] = piece(0, g)
        kvc_out[1, 0, g] = piece(1, g)
        ks_out[0, g] = _rms(piece(2, g), kn_ref[1:2, :]).astype(BF16)
        vs_out[0, g] = jnp.concatenate([piece(3, g), ones_col], axis=-1).T.astype(BF16)
        kw_out[0, g] = _rms(piece(4, g), kn_ref[2:3, :]).astype(BF16)
        vw_out[0, g] = jnp.concatenate([piece(5, g), ones_col], axis=-1).T.astype(BF16)


def nsa_prep(zq, zkv, q_norm, k_norm, ts=512):
    b, s, _ = zq.shape
    g = NSA_GROUPS
    hm = lambda bi, si: (bi, 0, si, 0)
    return pl.pallas_call(
        functools.partial(_nsa_prep_kernel, ts=ts),
        grid=(b, s // ts),
        in_specs=[pl.BlockSpec((1, ts, NSA_HEADS * DK), lambda bi, si: (bi, si, 0)),
                  pl.BlockSpec((1, ts, 6 * g * DK), lambda bi, si: (bi, si, 0)),
                  pl.BlockSpec((1, DK), lambda bi, si: (0, 0)),
                  pl.BlockSpec((3, DK), lambda bi, si: (0, 0))],
        out_specs=[pl.BlockSpec((1, NSA_HEADS, ts, DK), hm),
                   pl.BlockSpec((2, 1, g, ts, DK), lambda bi, si: (0, bi, 0, si, 0)),
                   pl.BlockSpec((1, g, ts, DK), hm),
                   pl.BlockSpec((1, g, SEL_VR, ts), lambda bi, si: (bi, 0, 0, si)),
                   pl.BlockSpec((1, g, ts, DK), hm),
                   pl.BlockSpec((1, g, SEL_VR, ts), lambda bi, si: (bi, 0, 0, si))],
        out_shape=[jax.ShapeDtypeStruct((b, NSA_HEADS, s, DK), BF16),
                   jax.ShapeDtypeStruct((2, b, g, s, DK), F32),
                   jax.ShapeDtypeStruct((b, g, s, DK), BF16),
                   jax.ShapeDtypeStruct((b, g, SEL_VR, s), BF16),
                   jax.ShapeDtypeStruct((b, g, s, DK), BF16),
                   jax.ShapeDtypeStruct((b, g, SEL_VR, s), BF16)],
        compiler_params=_cp(("parallel", "parallel")),
        name="nsa_prep",
    )(zq, zkv, q_norm.reshape(1, DK), k_norm)


def _cmp_mlp_kernel(x_ref, pe_ref, w1_ref, w2_ref, kn_ref, k_out, vt_out, *, nc):
    c = pl.program_id(2)
    x = x_ref[0, 0, 0]
    a = _dot((x + pe_ref[0, 0:1, :]).astype(BF16), w1_ref[0, 0])
    b = _dot((x + pe_ref[0, 1:2, :]).astype(BF16), w1_ref[0, 1])
    pre = a + pltpu.roll(b, nc - 1, 0)
    comp = _dot(_gelu(pre).astype(BF16), w2_ref[0])

    @pl.when(c == 0)
    def _():
        k_out[0, 0] = _rms(comp, kn_ref[...]).astype(BF16)

    @pl.when(c == 1)
    def _():
        ones_col = jnp.where(_iota((nc, SEL_VR - DK), 1) == 0, 1.0, 0.0)
        vt_out[0, 0] = jnp.concatenate([comp, ones_col], axis=-1).T.astype(BF16)


def cmp_mlp(xc, pe, w1, w2, k_norm0):
    _, b, g, nc, width = xc.shape
    return pl.pallas_call(
        functools.partial(_cmp_mlp_kernel, nc=nc),
        grid=(b, g, 2),
        in_specs=[pl.BlockSpec((1, 1, 1, nc, width), lambda bi, gi, c: (c, bi, gi, 0, 0)),
                  pl.BlockSpec((1, 2, width), lambda bi, gi, c: (c, 0, 0)),
                  pl.BlockSpec((1, 2, width, CMP_HIDDEN), lambda bi, gi, c: (c, 0, 0, 0)),
                  pl.BlockSpec((1, CMP_HIDDEN, DK), lambda bi, gi, c: (c, 0, 0)),
                  pl.BlockSpec((1, DK), lambda bi, gi, c: (0, 0))],
        out_specs=[pl.BlockSpec((1, 1, nc, DK), lambda bi, gi, c: (bi, gi, 0, 0)),
                   pl.BlockSpec((1, 1, SEL_VR, nc), lambda bi, gi, c: (bi, gi, 0, 0))],
        out_shape=[jax.ShapeDtypeStruct((b, g, nc, DK), BF16),
                   jax.ShapeDtypeStruct((b, g, SEL_VR, nc), BF16)],
        compiler_params=_cp(("parallel", "parallel", "arbitrary")),
        name="cmp_mlp",
    )(xc, pe, w1, w2, k_norm0.reshape(1, DK))


def _cmp_attn_kernel(q_ref, k_ref, vt_ref, oc_ref, sb_ref, s_s, acc_s, ps_s, *, nc, n_slc):
    qb = pl.program_id(2)
    rows = NSA_REP * QB
    ck = CMP_CHUNK
    q = q_ref[0].reshape(rows, DK)
    n_chunk = ((qb * QB + QB) // D_CMP + ck - 1) // ck
    t = qb * QB + _mod_pow2(_iota((ck, rows), 1), QB)

    def pass_a(ci, m):
        off = pl.multiple_of(ci * ck, ck)
        st = _dot_nt(k_ref[0, 0, pl.ds(off, ck), :], q)
        n = off + _iota((ck, rows), 0)
        st = jnp.where(n * D_CMP + (L_CMP - 1) <= t, st, NEG)
        s_s[ci] = st
        return jnp.maximum(m, jnp.max(st, axis=0, keepdims=True))

    m = lax.fori_loop(0, n_chunk, pass_a, jnp.full((1, rows), NEG, F32))
    acc_s[...] = jnp.zeros_like(acc_s)
    ps_s[...] = jnp.zeros_like(ps_s)

    def pass_b(ci, den):
        off = pl.multiple_of(ci * ck, ck)
        e = jnp.exp2(s_s[ci] - m)
        s_s[ci] = e
        acc_s[...] += _dot(vt_ref[0, 0, :, pl.ds(off, ck)], e.astype(BF16))
        return den + jnp.sum(e, axis=0, keepdims=True)

    den = lax.fori_loop(0, n_chunk, pass_b, jnp.zeros((1, rows), F32))
    seen = m > 0.5 * NEG
    inv_o = jnp.where(seen, 1.0 / jnp.maximum(acc_s[DK:DK + 1, :], 1e-30), 0.0)
    inv_p = jnp.where(seen, 1.0 / jnp.maximum(den, 1e-30), 0.0)
    oc_ref[0] = (acc_s[...] * inv_o).T[:, 0:DK].reshape(NSA_REP, QB, DK)

    def pass_c(ci, carry):
        off = pl.multiple_of(ci * ck, ck)
        p = s_s[ci] * inv_p
        ps = p[:, 0:QB]
        for r in range(1, NSA_REP):
            ps = ps + p[:, r * QB:(r + 1) * QB]
        ps_s[pl.ds(CMP_HALO + off, ck), :] = ps
        return carry

    lax.fori_loop(0, n_chunk, pass_c, 0)

    ratio = L_SLC // D_CMP
    imp = jnp.zeros((n_slc, QB), F32)
    for k in range(-(L_CMP // D_CMP - 1), ratio):
        overlap = min(k * D_CMP + L_CMP, L_SLC) - max(k * D_CMP, 0)
        imp = imp + (overlap / D_CMP) * ps_s[pl.ds(CMP_HALO + k, n_slc, stride=ratio), :]

    cur = _div_pow2(qb * QB + _iota((n_slc, QB), 1), L_SLC)
    j = _iota((n_slc, QB), 0)
    forced = (j == 0) | (j == cur) | (j == cur - 1)
    score = jnp.where(j <= cur, imp + jnp.where(forced, FORCE_BONUS, 0.0), NEG)
    work = score
    tau = None
    for _ in range(N_SEL):
        tau = jnp.max(work, axis=0, keepdims=True)
        work = jnp.where(work >= tau, REMOVED, work)
    selected = (score > 0.5 * NEG) & (score >= tau)
    sb_ref[0, 0, 0] = jnp.where(selected, 0.0, NEG)


def cmp_attn_select(qn, kc, vct):
    b, h, s, _ = qn.shape
    g = NSA_GROUPS
    nc = kc.shape[2]
    n_slc = s // L_SLC
    nqb = s // QB
    return pl.pallas_call(
        functools.partial(_cmp_attn_kernel, nc=nc, n_slc=n_slc),
        grid=(b, g, nqb),
        in_specs=[pl.BlockSpec((1, NSA_REP, QB, DK), lambda bi, gi, qi: (bi, gi, qi, 0)),
                  pl.BlockSpec((1, 1, nc, DK), lambda bi, gi, qi: (bi, gi, 0, 0)),
                  pl.BlockSpec((1, 1, SEL_VR, nc), lambda bi, gi, qi: (bi, gi, 0, 0))],
        out_specs=[pl.BlockSpec((1, NSA_REP, QB, DK), lambda bi, gi, qi: (bi, gi, qi, 0)),
                   pl.BlockSpec((1, 1, 1, n_slc, QB), lambda bi, gi, qi: (bi, gi, qi, 0, 0))],
        out_shape=[jax.ShapeDtypeStruct((b, h, s, DK), F32),
                   jax.ShapeDtypeStruct((b, g, nqb, n_slc, QB), F32)],
        scratch_shapes=[pltpu.VMEM((nc // CMP_CHUNK, CMP_CHUNK, NSA_REP * QB), F32),
                        pltpu.VMEM((SEL_VR, NSA_REP * QB), F32),
                        pltpu.VMEM((CMP_HALO + nc, QB), F32)],
        compiler_params=_cp(("parallel", "parallel", "parallel")),
        name="cmp_attn_select",
    )(qn, kc, vct)


def _win_attn_kernel(q_ref, *refs):
    nkb = WINDOW // QB + 1
    nrefs = nkb + WIN_QBLOCKS - 1
    k_refs, v_refs, o_ref = refs[:nrefs], refs[nrefs:2 * nrefs], refs[2 * nrefs]
    rows = NSA_REP * QB
    nk = nkb * QB
    for hf in range(WIN_QBLOCKS):
        qb = pl.program_id(2) * WIN_QBLOCKS + hf
        qs = slice(hf * QB, (hf + 1) * QB)
        q = q_ref[0, :, qs, :].reshape(rows, DK)
        k = jnp.concatenate([r[0, 0] for r in k_refs[hf:hf + nkb]], axis=0)
        vt = jnp.concatenate([r[0, 0] for r in v_refs[hf:hf + nkb]], axis=-1)
        st = _dot_nt(k, q)
        t = qb * QB + _mod_pow2(_iota((nk, rows), 1), QB)
        kpos = (qb - (nkb - 1)) * QB + _iota((nk, rows), 0)
        delta = t - kpos
        valid = (delta >= 0) & (delta < WINDOW) & (kpos >= 0)
        st = jnp.where(valid, st, NEG)
        e = jnp.exp2(st - jnp.max(st, axis=0, keepdims=True)).astype(BF16)
        acc = _dot(vt, e)
        o = (acc / jnp.maximum(acc[DK:DK + 1, :], 1e-30)).T[:, 0:DK]
        o_ref[0, :, qs, :] = o.reshape(NSA_REP, QB, DK)


def win_attn(qn, kw, vw):
    b, h, s, _ = qn.shape
    g = NSA_GROUPS
    nkb = WINDOW // QB + 1
    nrefs = nkb + WIN_QBLOCKS - 1
    nqb = s // QB

    def first_block(qi, i):
        return jnp.maximum(qi * WIN_QBLOCKS - (nkb - 1) + i, 0)

    def k_spec(i):
        return pl.BlockSpec((1, 1, QB, DK), lambda bi, gi, qi: (bi, gi, first_block(qi, i), 0))

    def vt_spec(i):
        return pl.BlockSpec((1, 1, SEL_VR, QB), lambda bi, gi, qi: (bi, gi, 0, first_block(qi, i)))

    qspec = pl.BlockSpec((1, NSA_REP, WIN_QBLOCKS * QB, DK), lambda bi, gi, qi: (bi, gi, qi, 0))
    return pl.pallas_call(
        _win_attn_kernel,
        grid=(b, g, nqb // WIN_QBLOCKS),
        in_specs=[qspec] + [k_spec(i) for i in range(nrefs)] + [vt_spec(i) for i in range(nrefs)],
        out_specs=qspec,
        out_shape=jax.ShapeDtypeStruct((b, h, s, DK), F32),
        compiler_params=_cp(("parallel", "parallel", "parallel")),
        name="win_attn",
    )(qn, *([kw] * nrefs), *([vw] * nrefs))


def _sel_attn_kernel(q_ref, sb_ref, k_ref, vt_ref, oc_ref, ow_ref, gl_ref, o_ref,
                     m_s, acc_s, s0_s, s1_s, p0_s, p1_s, a0_s, a1_s):
    qb = pl.program_id(2)
    rows = NSA_REP * QB
    blocks = SEL_SUB // L_SLC
    qv = q_ref[0].reshape(rows, DK)
    nsub = (qb * QB + QB + SEL_SUB - 1) // SEL_SUB
    m_s[...] = jnp.full_like(m_s, NEG)
    acc_s[...] = jnp.zeros_like(acc_s)

    def scores(i, s_ref):
        off = pl.multiple_of(i * SEL_SUB, SEL_SUB)
        s_ref[...] = _dot_nt(k_ref[0, 0, pl.ds(off, SEL_SUB), :], qv)

    def weights(i, s_ref, p_ref, a_ref, causal):
        bias = sb_ref[0, 0, 0, pl.ds(pl.multiple_of(i * blocks, blocks), blocks), :]
        bias = jnp.broadcast_to(bias[:, None, :], (blocks, L_SLC, QB)).reshape(SEL_SUB, QB)
        st = s_ref[...] + jnp.concatenate([bias] * NSA_REP, axis=1)
        if causal:
            kpos = i * SEL_SUB + _iota((SEL_SUB, rows), 0)
            t = qb * QB + _mod_pow2(_iota((SEL_SUB, rows), 1), QB)
            st = jnp.where(kpos <= t, st, NEG)
        m_old = m_s[...]
        m_new = jnp.maximum(m_old, jnp.max(st, axis=0, keepdims=True))
        p_ref[...] = jnp.exp2(st - m_new).astype(BF16)
        a_ref[...] = jnp.exp2(m_old - m_new)
        m_s[...] = m_new

    def accumulate(i, p_ref, a_ref):
        off = pl.multiple_of(i * SEL_SUB, SEL_SUB)
        acc_s[...] = a_ref[...] * acc_s[...] + _dot(vt_ref[0, 0, :, pl.ds(off, SEL_SUB)], p_ref[...])

    p1_s[...] = jnp.zeros_like(p1_s)
    a1_s[...] = jnp.ones_like(a1_s)
    scores(0, s0_s)

    def two_tiles(i):
        scores(i + 1, s1_s)
        weights(i, s0_s, p0_s, a0_s, False)
        accumulate(jnp.maximum(i - 1, 0), p1_s, a1_s)
        scores(i + 2, s0_s)
        weights(i + 1, s1_s, p1_s, a1_s, False)
        accumulate(i, p0_s, a0_s)

    e = 0
    left = nsub - 1
    group = SEL_UNROLL
    while group >= 2:
        n_group = left // group

        def body(j, carry, group=group, base=e):
            for k in range(0, group, 2):
                two_tiles(base + group * j + k)
            return carry

        lax.fori_loop(0, n_group, body, 0)
        e = e + group * n_group
        left = left - group * n_group
        group //= 2
    odd_tail = left

    @pl.when(odd_tail == 0)
    def _():
        weights(e, s0_s, p0_s, a0_s, True)
        accumulate(jnp.maximum(e - 1, 0), p1_s, a1_s)
        accumulate(e, p0_s, a0_s)

    @pl.when(odd_tail == 1)
    def _():
        scores(e + 1, s1_s)
        weights(e, s0_s, p0_s, a0_s, False)
        accumulate(jnp.maximum(e - 1, 0), p1_s, a1_s)
        weights(e + 1, s1_s, p1_s, a1_s, True)
        accumulate(e, p0_s, a0_s)
        accumulate(e + 1, p1_s, a1_s)

    o_s = (acc_s[...] / jnp.maximum(acc_s[DK:DK + 1, :], 1e-30)).T[:, 0:DK]
    gate = jax.nn.sigmoid(gl_ref[0].reshape(rows, 3))
    out = (gate[:, 0:1] * oc_ref[0].reshape(rows, DK) + gate[:, 1:2] * o_s
           + gate[:, 2:3] * ow_ref[0].reshape(rows, DK))
    o_ref[0] = out.reshape(NSA_REP, QB, DK)


def sel_attn(qn, selb, ks, vs, oc, ow, gl):
    b, h, s, _ = qn.shape
    g = NSA_GROUPS
    nqb = s // QB
    n_slc = s // L_SLC
    qmap = lambda bi, gi, qi: (bi, gi, qi, 0)
    return pl.pallas_call(
        _sel_attn_kernel,
        grid=(b, g, nqb),
        in_specs=[pl.BlockSpec((1, NSA_REP, QB, DK), qmap),
                  pl.BlockSpec((1, 1, 1, n_slc, QB), lambda bi, gi, qi: (bi, gi, qi, 0, 0)),
                  pl.BlockSpec((1, 1, s, DK), lambda bi, gi, qi: (bi, gi, 0, 0)),
                  pl.BlockSpec((1, 1, SEL_VR, s), lambda bi, gi, qi: (bi, gi, 0, 0)),
                  pl.BlockSpec((1, NSA_REP, QB, DK), qmap),
                  pl.BlockSpec((1, NSA_REP, QB, DK), qmap),
                  pl.BlockSpec((1, NSA_REP, QB, 3), qmap)],
        out_specs=pl.BlockSpec((1, NSA_REP, QB, DK), qmap),
        out_shape=jax.ShapeDtypeStruct((b, h, s, DK), F32),
        scratch_shapes=[pltpu.VMEM((1, NSA_REP * QB), F32),
                        pltpu.VMEM((SEL_VR, NSA_REP * QB), F32),
                        pltpu.VMEM((SEL_SUB, NSA_REP * QB), F32),
                        pltpu.VMEM((SEL_SUB, NSA_REP * QB), F32),
                        pltpu.VMEM((SEL_SUB, NSA_REP * QB), BF16),
                        pltpu.VMEM((SEL_SUB, NSA_REP * QB), BF16),
                        pltpu.VMEM((1, NSA_REP * QB), F32),
                        pltpu.VMEM((1, NSA_REP * QB), F32)],
        compiler_params=_cp(("parallel", "parallel", "parallel")),
        name="sel_attn",
    )(qn, selb, ks, vs, oc, ow, gl)


def _sgu_kernel(z_ref, g_ref, b_ref, w_ref, bias_ref, o_ref, *, tm):
    z = _gelu(z_ref[...])
    u = z[:, :SGU_WIDTH]
    v = z[:, SGU_WIDTH:]
    mu = jnp.mean(v, axis=-1, keepdims=True)
    var = jnp.mean(jnp.square(v - mu), axis=-1, keepdims=True)
    vn = ((v - mu) * lax.rsqrt(var + EPS) * g_ref[...] + b_ref[...]).astype(BF16)
    tril = _iota((SGU_CHUNK, SGU_CHUNK), 1) <= _iota((SGU_CHUNK, SGU_CHUNK), 0)
    wm = [jnp.where(tril, w_ref[gi], 0.0).astype(BF16) for gi in range(SGU_GROUPS)]
    grp = _div_pow2(_iota((SGU_CHUNK, SGU_WIDTH), 1), SGU_WIDTH // SGU_GROUPS)
    for ch in range(tm // SGU_CHUNK):
        rs = slice(ch * SGU_CHUNK, (ch + 1) * SGU_CHUNK)
        mixed = bias_ref[...]
        for gi in range(SGU_GROUPS):
            mixed = mixed + jnp.where(grp == gi, _dot(wm[gi], vn[rs]), 0.0)
        o_ref[rs, :] = u[rs] * mixed


def sgu(z, ln_g, ln_b, w, bias_full, tm=512):
    t = z.shape[0]
    return pl.pallas_call(
        functools.partial(_sgu_kernel, tm=tm),
        grid=(t // tm,),
        in_specs=[pl.BlockSpec((tm, 2 * SGU_WIDTH), lambda i: (i, 0)),
                  pl.BlockSpec((1, SGU_WIDTH), lambda i: (0, 0)),
                  pl.BlockSpec((1, SGU_WIDTH), lambda i: (0, 0)),
                  pl.BlockSpec((SGU_GROUPS, SGU_CHUNK, SGU_CHUNK), lambda i: (0, 0, 0)),
                  pl.BlockSpec((SGU_CHUNK, SGU_WIDTH), lambda i: (0, 0))],
        out_specs=pl.BlockSpec((tm, SGU_WIDTH), lambda i: (i, 0)),
        out_shape=jax.ShapeDtypeStruct((t, SGU_WIDTH), F32),
        compiler_params=_cp(("parallel",)),
        name="sgu",
    )(z, ln_g.reshape(1, -1), ln_b.reshape(1, -1), w, bias_full)


def _pool_kernel(p_ref, h_ref, w_ref, sc_ref, o_ref, *, tp):
    i = pl.program_id(1)
    halo_rows = POOL_WINDOWS[-1]
    p = p_ref[0]
    halo = jnp.where(i > 0, h_ref[0], 0.0)
    ext = jnp.concatenate([halo, p], axis=0)
    sums = [ext]
    shift = 1
    for _ in POOL_WINDOWS:
        prev = sums[-1]
        sums.append(prev + pltpu.roll(prev, shift, 0))
        shift *= 2
    pos = i * tp + _iota((tp, POOL_WIDTH), 0)
    grp = _div_pow2(_iota((tp, POOL_WIDTH), 1), POOL_WIDTH // len(POOL_WINDOWS))
    d = jnp.zeros((tp, POOL_WIDTH), F32)
    for gi, w in enumerate(POOL_WINDOWS):
        mean = sums[gi + 1][halo_rows:halo_rows + tp] / jnp.minimum(pos + 1, w).astype(F32)
        d = jnp.where(grp == gi, mean, d)
    d = d - p
    o_ref[0] = _dot(d.astype(BF16), w_ref[...]) * sc_ref[...]


def pool(p, w_bd, scale, tp=512):
    b, s, c = p.shape
    halo_rows = POOL_WINDOWS[-1]
    return pl.pallas_call(
        functools.partial(_pool_kernel, tp=tp),
        grid=(b, s // tp),
        in_specs=[pl.BlockSpec((1, tp, c), lambda bi, i: (bi, i, 0)),
                  pl.BlockSpec((1, halo_rows, c),
                               lambda bi, i: (bi, jnp.maximum(i * (tp // halo_rows) - 1, 0), 0)),
                  pl.BlockSpec((c, c), lambda bi, i: (0, 0)),
                  pl.BlockSpec((1, c), lambda bi, i: (0, 0))],
        out_specs=pl.BlockSpec((1, tp, c), lambda bi, i: (bi, i, 0)),
        out_shape=jax.ShapeDtypeStruct((b, s, c), F32),
        compiler_params=_cp(("parallel", "parallel")),
        name="pool",
    )(p, p, w_bd, scale.reshape(1, c))


def _merge_kernel(x_ref, oa_ref, ob_ref, oc_ref, zm_ref, la_ref, lb_ref, lc_ref, wo_ref, o_ref):
    d = D_MODEL
    oa = jnp.concatenate([oa_ref[0, h].astype(BF16) for h in range(NSA_HEADS)], axis=-1)
    zm = zm_ref[0]
    merged = (jax.nn.sigmoid(zm[:, 0:d]) * _dot(oa, la_ref[...])
              + jax.nn.sigmoid(zm[:, d:2 * d]) * _dot(ob_ref[0].astype(BF16), lb_ref[...])
              + jax.nn.sigmoid(zm[:, 2 * d:3 * d]) * _dot(oc_ref[0].astype(BF16), lc_ref[...]))
    o_ref[0] = x_ref[0] + _dot(merged.astype(BF16), wo_ref[...])


def merge(x, oa, ob, oc, zm, la, lb, lc, wo, tm=512):
    b, s, d = x.shape
    row = lambda bi, i: (bi, i, 0)
    full = lambda bi, i: (0, 0)
    return pl.pallas_call(
        _merge_kernel,
        grid=(b, s // tm),
        in_specs=[pl.BlockSpec((1, tm, d), row),
                  pl.BlockSpec((1, NSA_HEADS, tm, DK), lambda bi, i: (bi, 0, i, 0)),
                  pl.BlockSpec((1, tm, SGU_WIDTH), row),
                  pl.BlockSpec((1, tm, POOL_WIDTH), row),
                  pl.BlockSpec((1, tm, 3 * d), row),
                  pl.BlockSpec(la.shape, full), pl.BlockSpec(lb.shape, full),
                  pl.BlockSpec(lc.shape, full), pl.BlockSpec(wo.shape, full)],
        out_specs=pl.BlockSpec((1, tm, d), row),
        out_shape=jax.ShapeDtypeStruct((b, s, d), F32),
        compiler_params=_cp(("parallel", "parallel")),
        name="merge",
    )(x, oa, ob, oc, zm, la, lb, lc, wo)


def _mem_kv_kernel(m_ref, g_ref, wk_ref, wv_ref, kn_ref, k_out, v_out):
    mh = _rms(m_ref[0], g_ref[...]).astype(BF16)
    k = _dot(mh, wk_ref[...])
    for h in range(XA_HEADS):
        hs = slice(h * XA_HD, (h + 1) * XA_HD)
        k_out[0, :, hs] = _rms(k[:, hs], kn_ref[...]).astype(BF16)
    v_out[0] = _dot(mh, wv_ref[...]).astype(BF16)


def mem_kv(mem, g, wk, wv, k_norm):
    b, m, d = mem.shape
    full = lambda bi: (0, 0)
    return pl.pallas_call(
        _mem_kv_kernel,
        grid=(b,),
        in_specs=[pl.BlockSpec((1, m, d), lambda bi: (bi, 0, 0)),
                  pl.BlockSpec((1, d), full),
                  pl.BlockSpec(wk.shape, full), pl.BlockSpec(wv.shape, full),
                  pl.BlockSpec((1, XA_HD), full)],
        out_specs=[pl.BlockSpec((1, m, XA_WIDTH), lambda bi: (bi, 0, 0))] * 2,
        out_shape=[jax.ShapeDtypeStruct((b, m, XA_WIDTH), BF16)] * 2,
        compiler_params=_cp(("parallel",)),
        name="mem_kv",
    )(mem, g.reshape(1, d), wk, wv, k_norm.reshape(1, XA_HD))


def _xattn_kernel(x_ref, g_ref, wq_ref, qn_ref, k_ref, v_ref, wo_ref, o_ref):
    x = x_ref[0]
    q = _dot(_rms(x, g_ref[...]).astype(BF16), wq_ref[...])
    k = k_ref[0]
    v = v_ref[0]
    outs = []
    for h in range(XA_HEADS):
        hs = slice(h * XA_HD, (h + 1) * XA_HD)
        qh = (_rms(q[:, hs], qn_ref[...])).astype(BF16)
        s = _dot_nt(qh, k[:, hs]) * (XA_HD ** -0.5)
        e = jnp.exp(s - jnp.max(s, axis=-1, keepdims=True))
        p = e / jnp.sum(e, axis=-1, keepdims=True)
        outs.append(_dot(p.astype(BF16), v[:, hs]).astype(BF16))
    o = jnp.concatenate(outs, axis=-1)
    o_ref[0] = x + _dot(o, wo_ref[...])


def xattn(x, g, wq, q_norm, k, v, wo, tm=512):
    b, s, d = x.shape
    m = k.shape[1]
    row = lambda bi, i: (bi, i, 0)
    full = lambda bi, i: (0, 0)
    return pl.pallas_call(
        _xattn_kernel,
        grid=(b, s // tm),
        in_specs=[pl.BlockSpec((1, tm, d), row),
                  pl.BlockSpec((1, d), full),
                  pl.BlockSpec(wq.shape, full),
                  pl.BlockSpec((1, XA_HD), full),
                  pl.BlockSpec((1, m, XA_WIDTH), lambda bi, i: (bi, 0, 0)),
                  pl.BlockSpec((1, m, XA_WIDTH), lambda bi, i: (bi, 0, 0)),
                  pl.BlockSpec(wo.shape, full)],
        out_specs=pl.BlockSpec((1, tm, d), row),
        out_shape=jax.ShapeDtypeStruct((b, s, d), F32),
        compiler_params=_cp(("parallel", "parallel")),
        name="xattn",
    )(x, g.reshape(1, d), wq, q_norm.reshape(1, XA_HD), k, v, wo)


def _top_rows(s, n, want_rank):
    out = []
    work = s
    rank = jnp.full(s.shape, float(s.shape[0] - 1), F32) if want_rank else None
    for i in range(n):
        m = jnp.max(work, axis=0, keepdims=True)
        out.append(m)
        hit = work >= m
        if want_rank:
            rank = jnp.where(hit, float(i), rank)
        work = jnp.where(hit, REMOVED, work)
    return out, rank


def _stack_rows(rows, pad_rows):
    tt = rows[0].shape[1]
    rowi = _iota((pad_rows, tt), 0)
    out = jnp.full((pad_rows, tt), REMOVED, F32)
    for i, r in enumerate(rows[:pad_rows]):
        out = jnp.where(rowi == i, r, out)
    return out


def _pair_sum_candidates(v1, v2, tt):
    row_full = _iota((24, tt), 0)
    row = _iota((8, tt), 0)
    v2_full = _stack_rows(v2, 24)
    v2_8 = _stack_rows(v2, 8)

    def shifted(k):
        return pltpu.roll(v2_8, k, 0)

    def pick(rows_from, first):
        out = v1[rows_from]
        for r in range(first + 1, 8):
            if rows_from + r - first < len(v1):
                out = jnp.where(row == r, v1[rows_from + r - first], out)
        return out

    return jnp.concatenate([
        jnp.where(row_full < 17, v1[0] + v2_full, REMOVED),
        v1[1] + v2_8,
        jnp.where(row < 5, v1[2] + v2_8,
                  jnp.where(row < 7, v1[5] + shifted(5), v1[8] + shifted(7))),
        jnp.where(row < 4, v1[3] + v2_8,
                  jnp.where(row < 7, v1[4] + shifted(4), v1[9] + shifted(7))),
        jnp.where(row < 2, v1[6] + v2_8,
                  jnp.where(row < 4, v1[7] + shifted(2), pick(10, 4) + v2[0])),
        jnp.where(row < 3, pick(14, 0) + v2[0], REMOVED),
    ], axis=0)


def _peer_kernel(x_ref, g_ref, wq_ref, k1_ref, k2_ref, u_ref, vt_ref, o_ref,
                 hbt_s, q_s, r2_s, na_s, e1_s, e2_s, acc_s, *, tt, ch):
    c = pl.program_id(1)
    ntop = PEER_TOPK + 1
    assert ntop == 17

    @pl.when(c == 0)
    def _():
        h = _rms(x_ref[...], g_ref[...])
        hbt_s[...] = h.T.astype(BF16)
        q = _dot(h.astype(BF16), wq_ref[...])
        for hd in range(PEER_HEADS):
            q_s[hd] = q[:, hd * 2 * PEER_HALF:(hd + 1) * 2 * PEER_HALF].astype(BF16)
        acc_s[...] = jnp.zeros_like(acc_s)

        def route(hd, carry):
            qh = q_s[hd]
            s1 = _dot_nt(k1_ref[...], qh)
            s2 = _dot_nt(k2_ref[...], qh)
            v1, _ = _top_rows(s1, ntop, False)
            v2, rank2 = _top_rows(s2, ntop, True)
            cand = _pair_sum_candidates(v1, v2, tt)
            top = v1[0] + v2[0]
            z = jnp.zeros((1, tt), F32)
            c_prev = top
            c_last = top
            work = cand
            for r in range(ntop):
                m = jnp.max(work, axis=0, keepdims=True)
                work = jnp.where(work >= m, REMOVED, work)
                if r < PEER_TOPK:
                    z = z + jnp.exp(m - top)
                c_prev, c_last = c_last, m
            thr = 0.5 * (c_prev + c_last)
            na = jnp.zeros((PEER_KEYS, tt), F32)
            for j in range(ntop):
                na = jnp.where(s1 + v2[j] >= thr, float(j + 1), na)
            r2_s[hd] = rank2.astype(BF16)
            na_s[hd] = na
            e1_s[hd] = jnp.exp(s1 - v1[0])
            e2_s[hd] = (jnp.exp(s2 - v2[0]) / z).astype(BF16)
            return carry

        lax.fori_loop(0, PEER_HEADS, route, 0, unroll=4)

    n_slab = ch // PEER_KEYS
    slabs_per_group = PEER_PROJ_ROWS // PEER_KEYS
    n_group = ch // PEER_PROJ_ROWS

    def project(gi):
        rs = slice(gi * PEER_PROJ_ROWS, (gi + 1) * PEER_PROJ_ROWS)
        return _gelu(_dot(u_ref[rs, :], hbt_s[...]).astype(BF16))

    def gate(al, gl_rows):
        a = c * n_slab + al
        wc = jnp.zeros((PEER_KEYS, tt), BF16)
        for hd in range(PEER_HEADS):
            na = na_s[hd, pl.ds(a, 1), :].astype(BF16)
            e1 = e1_s[hd, pl.ds(a, 1), :].astype(BF16)
            wc = wc + jnp.where(r2_s[hd] < na, e2_s[hd] * e1, jnp.zeros((), BF16))
        return wc * gl_rows

    ys = []
    gl_next = project(0)
    for gi in range(n_group):
        gl = gl_next
        if gi + 1 < n_group:
            gl_next = project(gi + 1)
        for k in range(slabs_per_group):
            ys.append(gate(gi * slabs_per_group + k, gl[k * PEER_KEYS:(k + 1) * PEER_KEYS]))
    y = jnp.concatenate(ys, axis=0)
    acc_s[...] += _dot(vt_ref[0], y)

    @pl.when(c == pl.num_programs(1) - 1)
    def _():
        o_ref[...] = x_ref[...] + acc_s[...].T


def peer(x, g, wq, k1p, k2p, u, vt, tt=PEER_TT, ch=PEER_CH):
    t, d = x.shape
    ne = u.shape[0]
    return pl.pallas_call(
        functools.partial(_peer_kernel, tt=tt, ch=ch),
        grid=(t // tt, ne // ch),
        in_specs=[pl.BlockSpec((tt, d), lambda i, c: (i, 0)),
                  pl.BlockSpec((1, d), lambda i, c: (0, 0)),
                  pl.BlockSpec(wq.shape, lambda i, c: (0, 0)),
                  pl.BlockSpec(k1p.shape, lambda i, c: (0, 0)),
                  pl.BlockSpec(k2p.shape, lambda i, c: (0, 0)),
                  pl.BlockSpec((ch, d), lambda i, c: (c, 0)),
                  pl.BlockSpec((1, d, ch), lambda i, c: (c, 0, 0))],
        out_specs=pl.BlockSpec((tt, d), lambda i, c: (i, 0)),
        out_shape=jax.ShapeDtypeStruct((t, d), F32),
        scratch_shapes=[pltpu.VMEM((d, tt), BF16),
                        pltpu.VMEM((PEER_HEADS, tt, 2 * PEER_HALF), BF16),
                        pltpu.VMEM((PEER_HEADS, PEER_KEYS, tt), BF16),
                        pltpu.VMEM((PEER_HEADS, PEER_KEYS, tt), F32),
                        pltpu.VMEM((PEER_HEADS, PEER_KEYS, tt), F32),
                        pltpu.VMEM((PEER_HEADS, PEER_KEYS, tt), BF16),
                        pltpu.VMEM((d, tt), F32)],
        compiler_params=_cp(("parallel", "arbitrary")),
        name="peer",
    )(x, g.reshape(1, d), wq, k1p, k2p, u, vt)


_IN_WIDTHS = (NSA_HEADS * DK, 6 * NSA_GROUPS * DK, 3 * NSA_HEADS, 2 * SGU_WIDTH, POOL_WIDTH, 3 * D_MODEL)


def _in_proj_layout():
    splits, src = [], []
    o_src = o_dst = 0
    for w in _IN_WIDTHS:
        wp = -(-w // LANES) * LANES
        splits.append((o_dst, o_dst + wp))
        src.append((o_src, o_src + w))
        o_src += w
        o_dst += wp
    return splits, src, o_dst


def _chunked_transpose(v):
    ne, d = v.shape
    return v.astype(BF16).reshape(ne // PEER_CH, PEER_CH, d).transpose(0, 2, 1)


def nsa_layer(zq, zkv, zg, q_norm, k_norm, cmp_pe, cmp_w1, cmp_w2, b, s):
    qn, kvc, ks, vs, kw, vw = nsa_prep(zq.reshape(b, s, -1), zkv.reshape(b, s, -1), q_norm, k_norm)
    nc = s // D_CMP
    xc = kvc.reshape(2, b, NSA_GROUPS, nc, D_CMP * DK)
    pe = cmp_pe.reshape(2, 2, D_CMP * DK)
    w1 = cmp_w1.reshape(2, 2, D_CMP * DK, CMP_HIDDEN).astype(BF16)
    kc, vct = cmp_mlp(xc, pe, w1, cmp_w2.astype(BF16), k_norm[0])
    oc, selb = cmp_attn_select(qn, kc, vct)
    ow = win_attn(qn, kw, vw)
    gl = zg[:, :3 * NSA_HEADS].reshape(b, s, NSA_HEADS, 3).transpose(0, 2, 1, 3)
    return sel_attn(qn, selb, ks, vs, oc, ow, gl)


def kernel(x, mem, mix_norm, w_in, nsa_q_norm, nsa_k_norm, cmp_pe, cmp_w1, cmp_w2, sgu_ln_g, sgu_ln_b, sgu_w, sgu_b, pool_w, pool_scale, lift_a, lift_b, lift_c, w_out, xa_norm, mem_norm, xa_wq, xa_wk, xa_wv, xa_q_norm, xa_k_norm, xa_wo, ffn_norm, peer_wq, peer_keys1, peer_keys2, peer_u, peer_v):
    b, s, d = x.shape
    t = b * s
    depth = w_in.shape[0]
    splits, src, n_pad = _in_proj_layout()
    zeros_half = jnp.zeros((PEER_KEYS, PEER_HALF), BF16)
    for l in range(depth):
        w_parts = []
        for (a0, a1), (d0, d1) in zip(src, splits):
            w_parts.append(jnp.pad(w_in[l][:, a0:a1], ((0, 0), (0, (d1 - d0) - (a1 - a0)))))
        w_pad = jnp.concatenate(w_parts, axis=1).astype(BF16)
        zq, zkv, zg, zs, zp, zm = norm_matmul(x.reshape(t, d), mix_norm[l], w_pad, splits, 256, "in_proj")

        oa = nsa_layer(zq, zkv, zg, nsa_q_norm[l], nsa_k_norm[l], cmp_pe[l], cmp_w1[l], cmp_w2[l], b, s)
        bias_full = jnp.repeat(sgu_b[l].T, SGU_WIDTH // SGU_GROUPS, axis=1)
        ob = sgu(zs, sgu_ln_g[l], sgu_ln_b[l], sgu_w[l], bias_full)
        cg = POOL_WIDTH // len(POOL_WINDOWS)
        w_bd = jnp.zeros((POOL_WIDTH, POOL_WIDTH), F32)
        for gi in range(len(POOL_WINDOWS)):
            w_bd = w_bd.at[gi * cg:(gi + 1) * cg, gi * cg:(gi + 1) * cg].set(pool_w[l, gi])
        oc = pool(zp.reshape(b, s, POOL_WIDTH), w_bd.astype(BF16), pool_scale[l])
        x = merge(x, oa, ob.reshape(b, s, SGU_WIDTH), oc, zm.reshape(b, s, 3 * d),
                  lift_a[l].astype(BF16), lift_b[l].astype(BF16), lift_c[l].astype(BF16),
                  w_out[l].astype(BF16))

        mk, mv = mem_kv(mem, mem_norm[l], xa_wk[l].astype(BF16), xa_wv[l].astype(BF16), xa_k_norm[l])
        x = xattn(x, xa_norm[l], xa_wq[l].astype(BF16), xa_q_norm[l], mk, mv, xa_wo[l].astype(BF16))

        k1p = jnp.concatenate([peer_keys1[l].astype(BF16), zeros_half], axis=1)
        k2p = jnp.concatenate([zeros_half, peer_keys2[l].astype(BF16)], axis=1)
        x = peer(x.reshape(t, d), ffn_norm[l], peer_wq[l].astype(BF16), k1p, k2p,
                 peer_u[l].astype(BF16), _chunked_transpose(peer_v[l])).reshape(b, s, d)
    return x
```

```python
import functools

import jax
import jax.numpy as jnp
from jax import lax
from jax.experimental import pallas as pl
from jax.experimental.pallas import tpu as pltpu

F32 = jnp.float32
BF16 = jnp.bfloat16

EPS = 1e-6
LOG2E = 1.4426950408889634
NEG = -1e30
REMOVED = -3e38

D_MODEL = 1024
DK = 64
NSA_HEADS = 8
NSA_GROUPS = 2
NSA_REP = NSA_HEADS // NSA_GROUPS
L_CMP, D_CMP = 32, 16
CMP_HIDDEN = 128
L_SLC = 64
N_SEL = 16
WINDOW = 512
QB = 128
FORCE_BONUS = 1.0e3
SGU_WIDTH = 256
SGU_GROUPS = 4
SGU_CHUNK = 128
POOL_WIDTH = 256
POOL_WINDOWS = (2, 4, 8, 16)
XA_HEADS, XA_HD = 4, 128
XA_WIDTH = XA_HEADS * XA_HD
PEER_HEADS = 8
PEER_KEYS = 128
PEER_TOPK = 16
PEER_HALF = 64

LANES = 128
SEL_VR = 2 * DK
SEL_SUB = 512
SEL_QBLOCKS = 1
SEL_UNROLL = 8
WIN_QBLOCKS = 4
CMP_CHUNK = 256
CMP_HALO = 8
PEER_TT = 512
PEER_CH = 2048
PEER_PROJ_ROWS = 128
VMEM_LIMIT = 56 * 1024 * 1024


def _cp(sem):
    return pltpu.CompilerParams(dimension_semantics=sem, vmem_limit_bytes=VMEM_LIMIT)


def _gelu(x):
    return 0.5 * x * (1.0 + jnp.tanh(0.7978845608028654 * (x + 0.044715 * (x * x * x))))


def _rms(x, g):
    return x * lax.rsqrt(jnp.mean(x * x, axis=-1, keepdims=True) + EPS) * g


def _dot(a, b):
    return jnp.dot(a, b, preferred_element_type=F32)


def _dot_nt(a, b):
    return lax.dot_general(a, b, (((1,), (1,)), ((), ())), preferred_element_type=F32)


def _iota(shape, dim):
    return lax.broadcasted_iota(jnp.int32, shape, dim)


def _div_pow2(x, n):
    assert n & (n - 1) == 0
    return lax.shift_right_logical(x, jnp.int32(n.bit_length() - 1))


def _mod_pow2(x, n):
    assert n & (n - 1) == 0
    return x & (n - 1)


def _norm_matmul_kernel(x_ref, g_ref, w_ref, *o_refs, splits):
    hb = _rms(x_ref[...], g_ref[...]).astype(BF16)
    for o_ref, (a, b) in zip(o_refs, splits):
        o_ref[...] = _dot(hb, w_ref[:, a:b]).astype(o_ref.dtype)


def norm_matmul(x, g, w, splits, tm, name):
    t, d = x.shape
    n = w.shape[1]
    return pl.pallas_call(
        functools.partial(_norm_matmul_kernel, splits=splits),
        grid=(t // tm,),
        in_specs=[pl.BlockSpec((tm, d), lambda i: (i, 0)),
                  pl.BlockSpec((1, d), lambda i: (0, 0)),
                  pl.BlockSpec((d, n), lambda i: (0, 0))],
        out_specs=[pl.BlockSpec((tm, b - a), lambda i: (i, 0)) for a, b in splits],
        out_shape=[jax.ShapeDtypeStruct((t, b - a), F32) for a, b in splits],
        compiler_params=_cp(("parallel",)),
        name=name,
    )(x, g.reshape(1, d), w)


def _nsa_prep_kernel(zq_ref, zkv_ref, qn_ref, kn_ref,
                     q_out, kvc_out, ks_out, vs_out, kw_out, vw_out, *, ts):
    zq = zq_ref[0]
    scale = DK ** -0.5 * LOG2E
    for h in range(NSA_HEADS):
        q_out[0, h] = (_rms(zq[:, h * DK:(h + 1) * DK], qn_ref[...]) * scale).astype(BF16)
    zkv = zkv_ref[0]

    def piece(i, g):
        o = i * NSA_GROUPS * DK + g * DK
        return zkv[:, o:o + DK]

    ones_col = jnp.where(_iota((ts, SEL_VR - DK), 1) == 0, 1.0, 0.0)
    for g in range(NSA_GROUPS):
        kvc_out[0, 0, g] = piece(0, g)
        kvc_out[1, 0, g] = piece(1, g)
        ks_out[0, g] = _rms(piece(2, g), kn_ref[1:2, :]).astype(BF16)
        vs_out[0, g] = jnp.concatenate([piece(3, g), ones_col], axis=-1).T.astype(BF16)
        kw_out[0, g] = _rms(piece(4, g), kn_ref[2:3, :]).astype(BF16)
        vw_out[0, g] = jnp.concatenate([piece(5, g), ones_col], axis=-1).T.astype(BF16)


def nsa_prep(zq, zkv, q_norm, k_norm, ts=512):
    b, s, _ = zq.shape
    g = NSA_GROUPS
    hm = lambda bi, si: (bi, 0, si, 0)
    return pl.pallas_call(
        functools.partial(_nsa_prep_kernel, ts=ts),
        grid=(b, s // ts),
        in_specs=[pl.BlockSpec((1, ts, NSA_HEADS * DK), lambda bi, si: (bi, si, 0)),
                  pl.BlockSpec((1, ts, 6 * g * DK), lambda bi, si: (bi, si, 0)),
                  pl.BlockSpec((1, DK), lambda bi, si: (0, 0)),
                  pl.BlockSpec((3, DK), lambda bi, si: (0, 0))],
        out_specs=[pl.BlockSpec((1, NSA_HEADS, ts, DK), hm),
                   pl.BlockSpec((2, 1, g, ts, DK), lambda bi, si: (0, bi, 0, si, 0)),
                   pl.BlockSpec((1, g, ts, DK), hm),
                   pl.BlockSpec((1, g, SEL_VR, ts), lambda bi, si: (bi, 0, 0, si)),
                   pl.BlockSpec((1, g, ts, DK), hm),
                   pl.BlockSpec((1, g, SEL_VR, ts), lambda bi, si: (bi, 0, 0, si))],
        out_shape=[jax.ShapeDtypeStruct((b, NSA_HEADS, s, DK), BF16),
                   jax.ShapeDtypeStruct((2, b, g, s, DK), F32),
                   jax.ShapeDtypeStruct((b, g, s, DK), BF16),
                   jax.ShapeDtypeStruct((b, g, SEL_VR, s), BF16),
                   jax.ShapeDtypeStruct((b, g, s, DK), BF16),
                   jax.ShapeDtypeStruct((b, g, SEL_VR, s), BF16)],
        compiler_params=_cp(("parallel", "parallel")),
        name="nsa_prep",
    )(zq, zkv, q_norm.reshape(1, DK), k_norm)


def _cmp_mlp_kernel(x_ref, pe_ref, w1_ref, w2_ref, kn_ref, k_out, vt_out, *, nc):
    c = pl.program_id(2)
    x = x_ref[0, 0, 0]
    a = _dot((x + pe_ref[0, 0:1, :]).astype(BF16), w1_ref[0, 0])
    b = _dot((x + pe_ref[0, 1:2, :]).astype(BF16), w1_ref[0, 1])
    pre = a + pltpu.roll(b, nc - 1, 0)
    comp = _dot(_gelu(pre).astype(BF16), w2_ref[0])

    @pl.when(c == 0)
    def _():
        k_out[0, 0] = _rms(comp, kn_ref[...]).astype(BF16)

    @pl.when(c == 1)
    def _():
        ones_col = jnp.where(_iota((nc, SEL_VR - DK), 1) == 0, 1.0, 0.0)
        vt_out[0, 0] = jnp.concatenate([comp, ones_col], axis=-1).T.astype(BF16)


def cmp_mlp(xc, pe, w1, w2, k_norm0):
    _, b, g, nc, width = xc.shape
    return pl.pallas_call(
        functools.partial(_cmp_mlp_kernel, nc=nc),
        grid=(b, g, 2),
        in_specs=[pl.BlockSpec((1, 1, 1, nc, width), lambda bi, gi, c: (c, bi, gi, 0, 0)),
                  pl.BlockSpec((1, 2, width), lambda bi, gi, c: (c, 0, 0)),
                  pl.BlockSpec((1, 2, width, CMP_HIDDEN), lambda bi, gi, c: (c, 0, 0, 0)),
                  pl.BlockSpec((1, CMP_HIDDEN, DK), lambda bi, gi, c: (c, 0, 0)),
                  pl.BlockSpec((1, DK), lambda bi, gi, c: (0, 0))],
        out_specs=[pl.BlockSpec((1, 1, nc, DK), lambda bi, gi, c: (bi, gi, 0, 0)),
                   pl.BlockSpec((1, 1, SEL_VR, nc), lambda bi, gi, c: (bi, gi, 0, 0))],
        out_shape=[jax.ShapeDtypeStruct((b, g, nc, DK), BF16),
                   jax.ShapeDtypeStruct((b, g, SEL_VR, nc), BF16)],
        compiler_params=_cp(("parallel", "parallel", "arbitrary")),
        name="cmp_mlp",
    )(xc, pe, w1, w2, k_norm0.reshape(1, DK))


def _cmp_attn_kernel(q_ref, k_ref, vt_ref, oc_ref, sb_ref, s_s, acc_s, ps_s, *, nc, n_slc):
    qb = pl.program_id(2)
    rows = NSA_REP * QB
    ck = CMP_CHUNK
    q = q_ref[0].reshape(rows, DK)
    n_chunk = ((qb * QB + QB) // D_CMP + ck - 1) // ck
    t = qb * QB + _mod_pow2(_iota((ck, rows), 1), QB)

    def pass_a(ci, m):
        off = pl.multiple_of(ci * ck, ck)
        st = _dot_nt(k_ref[0, 0, pl.ds(off, ck), :], q)
        n = off + _iota((ck, rows), 0)
        st = jnp.where(n * D_CMP + (L_CMP - 1) <= t, st, NEG)
        s_s[ci] = st
        return jnp.maximum(m, jnp.max(st, axis=0, keepdims=True))

    m = lax.fori_loop(0, n_chunk, pass_a, jnp.full((1, rows), NEG, F32))
    acc_s[...] = jnp.zeros_like(acc_s)
    ps_s[...] = jnp.zeros_like(ps_s)

    def pass_b(ci, den):
        off = pl.multiple_of(ci * ck, ck)
        e = jnp.exp2(s_s[ci] - m)
        s_s[ci] = e
        acc_s[...] += _dot(vt_ref[0, 0, :, pl.ds(off, ck)], e.astype(BF16))
        return den + jnp.sum(e, axis=0, keepdims=True)

    den = lax.fori_loop(0, n_chunk, pass_b, jnp.zeros((1, rows), F32))
    seen = m > 0.5 * NEG
    inv_o = jnp.where(seen, 1.0 / jnp.maximum(acc_s[DK:DK + 1, :], 1e-30), 0.0)
    inv_p = jnp.where(seen, 1.0 / jnp.maximum(den, 1e-30), 0.0)
    oc_ref[0] = (acc_s[...] * inv_o).T[:, 0:DK].reshape(NSA_REP, QB, DK)

    def pass_c(ci, carry):
        off = pl.multiple_of(ci * ck, ck)
        p = s_s[ci] * inv_p
        ps = p[:, 0:QB]
        for r in range(1, NSA_REP):
            ps = ps + p[:, r * QB:(r + 1) * QB]
        ps_s[pl.ds(CMP_HALO + off, ck), :] = ps
        return carry

    lax.fori_loop(0, n_chunk, pass_c, 0)

    ratio = L_SLC // D_CMP
    imp = jnp.zeros((n_slc, QB), F32)
    for k in range(-(L_CMP // D_CMP - 1), ratio):
        overlap = min(k * D_CMP + L_CMP, L_SLC) - max(k * D_CMP, 0)
        imp = imp + (overlap / D_CMP) * ps_s[pl.ds(CMP_HALO + k, n_slc, stride=ratio), :]

    cur = _div_pow2(qb * QB + _iota((n_slc, QB), 1), L_SLC)
    j = _iota((n_slc, QB), 0)
    forced = (j == 0) | (j == cur) | (j == cur - 1)
    score = jnp.where(j <= cur, imp + jnp.where(forced, FORCE_BONUS, 0.0), NEG)
    work = score
    tau = None
    for _ in range(N_SEL):
        tau = jnp.max(work, axis=0, keepdims=True)
        work = jnp.where(work >= tau, REMOVED, work)
    selected = (score > 0.5 * NEG) & (score >= tau)
    sb_ref[0, 0, 0] = jnp.where(selected, 0.0, NEG)


def cmp_attn_select(qn, kc, vct):
    b, h, s, _ = qn.shape
    g = NSA_GROUPS
    nc = kc.shape[2]
    n_slc = s // L_SLC
    nqb = s // QB
    return pl.pallas_call(
        functools.partial(_cmp_attn_kernel, nc=nc, n_slc=n_slc),
        grid=(b, g, nqb),
        in_specs=[pl.BlockSpec((1, NSA_REP, QB, DK), lambda bi, gi, qi: (bi, gi, qi, 0)),
                  pl.BlockSpec((1, 1, nc, DK), lambda bi, gi, qi: (bi, gi, 0, 0)),
                  pl.BlockSpec((1, 1, SEL_VR, nc), lambda bi, gi, qi: (bi, gi, 0, 0))],
        out_specs=[pl.BlockSpec((1, NSA_REP, QB, DK), lambda bi, gi, qi: (bi, gi, qi, 0)),
                   pl.BlockSpec((1, 1, 1, n_slc, QB), lambda bi, gi, qi: (bi, gi, qi, 0, 0))],
        out_shape=[jax.ShapeDtypeStruct((b, h, s, DK), F32),
                   jax.ShapeDtypeStruct((b, g, nqb, n_slc, QB), F32)],
        scratch_shapes=[pltpu.VMEM((nc // CMP_CHUNK, CMP_CHUNK, NSA_REP * QB), F32),
                        pltpu.VMEM((SEL_VR, NSA_REP * QB), F32),
                        pltpu.VMEM((CMP_HALO + nc, QB), F32)],
        compiler_params=_cp(("parallel", "parallel", "parallel")),
        name="cmp_attn_select",
    )(qn, kc, vct)


def _win_attn_kernel(q_ref, *refs):
    nkb = WINDOW // QB + 1
    nrefs = nkb + WIN_QBLOCKS - 1
    k_refs, v_refs, o_ref = refs[:nrefs], refs[nrefs:2 * nrefs], refs[2 * nrefs]
    rows = NSA_REP * QB
    nk = nkb * QB
    for hf in range(WIN_QBLOCKS):
        qb = pl.program_id(2) * WIN_QBLOCKS + hf
        qs = slice(hf * QB, (hf + 1) * QB)
        q = q_ref[0, :, qs, :].reshape(rows, DK)
        k = jnp.concatenate([r[0, 0] for r in k_refs[hf:hf + nkb]], axis=0)
        vt = jnp.concatenate([r[0, 0] for r in v_refs[hf:hf + nkb]], axis=-1)
        st = _dot_nt(k, q)
        t = qb * QB + _mod_pow2(_iota((nk, rows), 1), QB)
        kpos = (qb - (nkb - 1)) * QB + _iota((nk, rows), 0)
        delta = t - kpos
        valid = (delta >= 0) & (delta < WINDOW) & (kpos >= 0)
        st = jnp.where(valid, st, NEG)
        e = jnp.exp2(st - jnp.max(st, axis=0, keepdims=True)).astype(BF16)
        acc = _dot(vt, e)
        o = (acc / jnp.maximum(acc[DK:DK + 1, :], 1e-30)).T[:, 0:DK]
        o_ref[0, :, qs, :] = o.reshape(NSA_REP, QB, DK)


def win_attn(qn, kw, vw):
    b, h, s, _ = qn.shape
    g = NSA_GROUPS
    nkb = WINDOW // QB + 1
    nrefs = nkb + WIN_QBLOCKS - 1
    nqb = s // QB

    def first_block(qi, i):
        return jnp.maximum(qi * WIN_QBLOCKS - (nkb - 1) + i, 0)

    def k_spec(i):
        return pl.BlockSpec((1, 1, QB, DK), lambda bi, gi, qi: (bi, gi, first_block(qi, i), 0))

    def vt_spec(i):
        return pl.BlockSpec((1, 1, SEL_VR, QB), lambda bi, gi, qi: (bi, gi, 0, first_block(qi, i)))

    qspec = pl.BlockSpec((1, NSA_REP, WIN_QBLOCKS * QB, DK), lambda bi, gi, qi: (bi, gi, qi, 0))
    return pl.pallas_call(
        _win_attn_kernel,
        grid=(b, g, nqb // WIN_QBLOCKS),
        in_specs=[qspec] + [k_spec(i) for i in range(nrefs)] + [vt_spec(i) for i in range(nrefs)],
        out_specs=qspec,
        out_shape=jax.ShapeDtypeStruct((b, h, s, DK), F32),
        compiler_params=_cp(("parallel", "parallel", "parallel")),
        name="win_attn",
    )(qn, *([kw] * nrefs), *([vw] * nrefs))


def _sel_attn_kernel(q_ref, sb_ref, k_ref, vt_ref, oc_ref, ow_ref, gl_ref, o_ref,
                     m_s, acc_s, s0_s, s1_s, p0_s, p1_s, a0_s, a1_s):
    qspan = SEL_QBLOCKS * QB
    q0 = pl.program_id(2) * qspan
    rows = NSA_REP * qspan
    blocks = SEL_SUB // L_SLC
    qv = q_ref[0].reshape(rows, DK)
    nsub = (q0 + qspan + SEL_SUB - 1) // SEL_SUB
    m_s[...] = jnp.full_like(m_s, NEG)
    acc_s[...] = jnp.zeros_like(acc_s)

    def scores(i, s_ref):
        off = pl.multiple_of(i * SEL_SUB, SEL_SUB)
        s_ref[...] = _dot_nt(k_ref[0, 0, pl.ds(off, SEL_SUB), :], qv)

    def weights(i, s_ref, p_ref, a_ref, causal):
        first = pl.multiple_of(i * blocks, blocks)
        bias = jnp.concatenate([sb_ref[0, 0, qi, pl.ds(first, blocks), :] for qi in range(SEL_QBLOCKS)],
                               axis=1)
        bias = jnp.broadcast_to(bias[:, None, :], (blocks, L_SLC, qspan)).reshape(SEL_SUB, qspan)
        st = s_ref[...] + jnp.concatenate([bias] * NSA_REP, axis=1)
        if causal:
            kpos = i * SEL_SUB + _iota((SEL_SUB, rows), 0)
            t = q0 + _mod_pow2(_iota((SEL_SUB, rows), 1), qspan)
            st = jnp.where(kpos <= t, st, NEG)
        m_old = m_s[...]
        m_new = jnp.maximum(m_old, jnp.max(st, axis=0, keepdims=True))
        p_ref[...] = jnp.exp2(st - m_new).astype(BF16)
        a_ref[...] = jnp.exp2(m_old - m_new)
        m_s[...] = m_new

    def accumulate(i, p_ref, a_ref):
        off = pl.multiple_of(i * SEL_SUB, SEL_SUB)
        acc_s[...] = a_ref[...] * acc_s[...] + _dot(vt_ref[0, 0, :, pl.ds(off, SEL_SUB)], p_ref[...])

    p1_s[...] = jnp.zeros_like(p1_s)
    a1_s[...] = jnp.ones_like(a1_s)
    scores(0, s0_s)

    def two_tiles(i):
        scores(i + 1, s1_s)
        weights(i, s0_s, p0_s, a0_s, False)
        accumulate(jnp.maximum(i - 1, 0), p1_s, a1_s)
        scores(i + 2, s0_s)
        weights(i + 1, s1_s, p1_s, a1_s, False)
        accumulate(i, p0_s, a0_s)

    e = 0
    left = nsub - 1
    group = SEL_UNROLL
    while group >= 2:
        n_group = left // group

        def body(j, carry, group=group, base=e):
            for k in range(0, group, 2):
                two_tiles(base + group * j + k)
            return carry

        lax.fori_loop(0, n_group, body, 0)
        e = e + group * n_group
        left = left - group * n_group
        group //= 2
    odd_tail = left

    @pl.when(odd_tail == 0)
    def _():
        weights(e, s0_s, p0_s, a0_s, True)
        accumulate(jnp.maximum(e - 1, 0), p1_s, a1_s)
        accumulate(e, p0_s, a0_s)

    @pl.when(odd_tail == 1)
    def _():
        scores(e + 1, s1_s)
        weights(e, s0_s, p0_s, a0_s, False)
        accumulate(jnp.maximum(e - 1, 0), p1_s, a1_s)
        weights(e + 1, s1_s, p1_s, a1_s, True)
        accumulate(e, p0_s, a0_s)
        accumulate(e + 1, p1_s, a1_s)

    o_s = (acc_s[...] / jnp.maximum(acc_s[DK:DK + 1, :], 1e-30)).T[:, 0:DK]
    gate = jax.nn.sigmoid(gl_ref[0].reshape(rows, 3))
    out = (gate[:, 0:1] * oc_ref[0].reshape(rows, DK) + gate[:, 1:2] * o_s
           + gate[:, 2:3] * ow_ref[0].reshape(rows, DK))
    o_ref[0] = out.reshape(NSA_REP, qspan, DK)


def sel_attn(qn, selb, ks, vs, oc, ow, gl):
    b, h, s, _ = qn.shape
    g = NSA_GROUPS
    qspan = SEL_QBLOCKS * QB
    rows = NSA_REP * qspan
    n_slc = s // L_SLC
    qmap = lambda bi, gi, qi: (bi, gi, qi, 0)
    return pl.pallas_call(
        _sel_attn_kernel,
        grid=(b, g, s // qspan),
        in_specs=[pl.BlockSpec((1, NSA_REP, qspan, DK), qmap),
                  pl.BlockSpec((1, 1, SEL_QBLOCKS, n_slc, QB), lambda bi, gi, qi: (bi, gi, qi, 0, 0)),
                  pl.BlockSpec((1, 1, s, DK), lambda bi, gi, qi: (bi, gi, 0, 0)),
                  pl.BlockSpec((1, 1, SEL_VR, s), lambda bi, gi, qi: (bi, gi, 0, 0)),
                  pl.BlockSpec((1, NSA_REP, qspan, DK), qmap),
                  pl.BlockSpec((1, NSA_REP, qspan, DK), qmap),
                  pl.BlockSpec((1, NSA_REP, qspan, 3), qmap)],
        out_specs=pl.BlockSpec((1, NSA_REP, qspan, DK), qmap),
        out_shape=jax.ShapeDtypeStruct((b, h, s, DK), F32),
        scratch_shapes=[pltpu.VMEM((1, rows), F32),
                        pltpu.VMEM((SEL_VR, rows), F32),
                        pltpu.VMEM((SEL_SUB, rows), F32),
                        pltpu.VMEM((SEL_SUB, rows), F32),
                        pltpu.VMEM((SEL_SUB, rows), BF16),
                        pltpu.VMEM((SEL_SUB, rows), BF16),
                        pltpu.VMEM((1, rows), F32),
                        pltpu.VMEM((1, rows), F32)],
        compiler_params=_cp(("parallel", "parallel", "parallel")),
        name="sel_attn",
    )(qn, selb, ks, vs, oc, ow, gl)


def _sgu_kernel(z_ref, g_ref, b_ref, w_ref, bias_ref, o_ref, *, tm):
    z = _gelu(z_ref[...])
    u = z[:, :SGU_WIDTH]
    v = z[:, SGU_WIDTH:]
    mu = jnp.mean(v, axis=-1, keepdims=True)
    var = jnp.mean(jnp.square(v - mu), axis=-1, keepdims=True)
    vn = ((v - mu) * lax.rsqrt(var + EPS) * g_ref[...] + b_ref[...]).astype(BF16)
    tril = _iota((SGU_CHUNK, SGU_CHUNK), 1) <= _iota((SGU_CHUNK, SGU_CHUNK), 0)
    wm = [jnp.where(tril, w_ref[gi], 0.0).astype(BF16) for gi in range(SGU_GROUPS)]
    grp = _div_pow2(_iota((SGU_CHUNK, SGU_WIDTH), 1), SGU_WIDTH // SGU_GROUPS)
    for ch in range(tm // SGU_CHUNK):
        rs = slice(ch * SGU_CHUNK, (ch + 1) * SGU_CHUNK)
        mixed = bias_ref[...]
        for gi in range(SGU_GROUPS):
            mixed = mixed + jnp.where(grp == gi, _dot(wm[gi], vn[rs]), 0.0)
        o_ref[rs, :] = u[rs] * mixed


def sgu(z, ln_g, ln_b, w, bias_full, tm=512):
    t = z.shape[0]
    return pl.pallas_call(
        functools.partial(_sgu_kernel, tm=tm),
        grid=(t // tm,),
        in_specs=[pl.BlockSpec((tm, 2 * SGU_WIDTH), lambda i: (i, 0)),
                  pl.BlockSpec((1, SGU_WIDTH), lambda i: (0, 0)),
                  pl.BlockSpec((1, SGU_WIDTH), lambda i: (0, 0)),
                  pl.BlockSpec((SGU_GROUPS, SGU_CHUNK, SGU_CHUNK), lambda i: (0, 0, 0)),
                  pl.BlockSpec((SGU_CHUNK, SGU_WIDTH), lambda i: (0, 0))],
        out_specs=pl.BlockSpec((tm, SGU_WIDTH), lambda i: (i, 0)),
        out_shape=jax.ShapeDtypeStruct((t, SGU_WIDTH), F32),
        compiler_params=_cp(("parallel",)),
        name="sgu",
    )(z, ln_g.reshape(1, -1), ln_b.reshape(1, -1), w, bias_full)


def _pool_kernel(p_ref, h_ref, w_ref, sc_ref, o_ref, *, tp):
    i = pl.program_id(1)
    halo_rows = POOL_WINDOWS[-1]
    p = p_ref[0]
    halo = jnp.where(i > 0, h_ref[0], 0.0)
    ext = jnp.concatenate([halo, p], axis=0)
    sums = [ext]
    shift = 1
    for _ in POOL_WINDOWS:
        prev = sums[-1]
        sums.append(prev + pltpu.roll(prev, shift, 0))
        shift *= 2
    pos = i * tp + _iota((tp, POOL_WIDTH), 0)
    grp = _div_pow2(_iota((tp, POOL_WIDTH), 1), POOL_WIDTH // len(POOL_WINDOWS))
    d = jnp.zeros((tp, POOL_WIDTH), F32)
    for gi, w in enumerate(POOL_WINDOWS):
        mean = sums[gi + 1][halo_rows:halo_rows + tp] / jnp.minimum(pos + 1, w).astype(F32)
        d = jnp.where(grp == gi, mean, d)
    d = d - p
    o_ref[0] = _dot(d.astype(BF16), w_ref[...]) * sc_ref[...]


def pool(p, w_bd, scale, tp=512):
    b, s, c = p.shape
    halo_rows = POOL_WINDOWS[-1]
    return pl.pallas_call(
        functools.partial(_pool_kernel, tp=tp),
        grid=(b, s // tp),
        in_specs=[pl.BlockSpec((1, tp, c), lambda bi, i: (bi, i, 0)),
                  pl.BlockSpec((1, halo_rows, c),
                               lambda bi, i: (bi, jnp.maximum(i * (tp // halo_rows) - 1, 0), 0)),
                  pl.BlockSpec((c, c), lambda bi, i: (0, 0)),
                  pl.BlockSpec((1, c), lambda bi, i: (0, 0))],
        out_specs=pl.BlockSpec((1, tp, c), lambda bi, i: (bi, i, 0)),
        out_shape=jax.ShapeDtypeStruct((b, s, c), F32),
        compiler_params=_cp(("parallel", "parallel")),
        name="pool",
    )(p, p, w_bd, scale.reshape(1, c))


def _merge_kernel(x_ref, oa_ref, ob_ref, oc_ref, zm_ref, la_ref, lb_ref, lc_ref, wo_ref, o_ref):
    d = D_MODEL
    oa = jnp.concatenate([oa_ref[0, h].astype(BF16) for h in range(NSA_HEADS)], axis=-1)
    zm = zm_ref[0]
    merged = (jax.nn.sigmoid(zm[:, 0:d]) * _dot(oa, la_ref[...])
              + jax.nn.sigmoid(zm[:, d:2 * d]) * _dot(ob_ref[0].astype(BF16), lb_ref[...])
              + jax.nn.sigmoid(zm[:, 2 * d:3 * d]) * _dot(oc_ref[0].astype(BF16), lc_ref[...]))
    o_ref[0] = x_ref[0] + _dot(merged.astype(BF16), wo_ref[...])


def merge(x, oa, ob, oc, zm, la, lb, lc, wo, tm=512):
    b, s, d = x.shape
    row = lambda bi, i: (bi, i, 0)
    full = lambda bi, i: (0, 0)
    return pl.pallas_call(
        _merge_kernel,
        grid=(b, s // tm),
        in_specs=[pl.BlockSpec((1, tm, d), row),
                  pl.BlockSpec((1, NSA_HEADS, tm, DK), lambda bi, i: (bi, 0, i, 0)),
                  pl.BlockSpec((1, tm, SGU_WIDTH), row),
                  pl.BlockSpec((1, tm, POOL_WIDTH), row),
                  pl.BlockSpec((1, tm, 3 * d), row),
                  pl.BlockSpec(la.shape, full), pl.BlockSpec(lb.shape, full),
                  pl.BlockSpec(lc.shape, full), pl.BlockSpec(wo.shape, full)],
        out_specs=pl.BlockSpec((1, tm, d), row),
        out_shape=jax.ShapeDtypeStruct((b, s, d), F32),
        compiler_params=_cp(("parallel", "parallel")),
        name="merge",
    )(x, oa, ob, oc, zm, la, lb, lc, wo)


def _mem_kv_kernel(m_ref, g_ref, wk_ref, wv_ref, kn_ref, k_out, v_out):
    mh = _rms(m_ref[0], g_ref[...]).astype(BF16)
    k = _dot(mh, wk_ref[...])
    for h in range(XA_HEADS):
        hs = slice(h * XA_HD, (h + 1) * XA_HD)
        k_out[0, :, hs] = _rms(k[:, hs], kn_ref[...]).astype(BF16)
    v_out[0] = _dot(mh, wv_ref[...]).astype(BF16)


def mem_kv(mem, g, wk, wv, k_norm):
    b, m, d = mem.shape
    full = lambda bi: (0, 0)
    return pl.pallas_call(
        _mem_kv_kernel,
        grid=(b,),
        in_specs=[pl.BlockSpec((1, m, d), lambda bi: (bi, 0, 0)),
                  pl.BlockSpec((1, d), full),
                  pl.BlockSpec(wk.shape, full), pl.BlockSpec(wv.shape, full),
                  pl.BlockSpec((1, XA_HD), full)],
        out_specs=[pl.BlockSpec((1, m, XA_WIDTH), lambda bi: (bi, 0, 0))] * 2,
        out_shape=[jax.ShapeDtypeStruct((b, m, XA_WIDTH), BF16)] * 2,
        compiler_params=_cp(("parallel",)),
        name="mem_kv",
    )(mem, g.reshape(1, d), wk, wv, k_norm.reshape(1, XA_HD))


def _xattn_kernel(x_ref, g_ref, wq_ref, qn_ref, k_ref, v_ref, wo_ref, o_ref):
    x = x_ref[0]
    q = _dot(_rms(x, g_ref[...]).astype(BF16), wq_ref[...])
    k = k_ref[0]
    v = v_ref[0]
    outs = []
    for h in range(XA_HEADS):
        hs = slice(h * XA_HD, (h + 1) * XA_HD)
        qh = (_rms(q[:, hs], qn_ref[...])).astype(BF16)
        s = _dot_nt(qh, k[:, hs]) * (XA_HD ** -0.5)
        e = jnp.exp(s - jnp.max(s, axis=-1, keepdims=True))
        p = e / jnp.sum(e, axis=-1, keepdims=True)
        outs.append(_dot(p.astype(BF16), v[:, hs]).astype(BF16))
    o = jnp.concatenate(outs, axis=-1)
    o_ref[0] = x + _dot(o, wo_ref[...])


def xattn(x, g, wq, q_norm, k, v, wo, tm=512):
    b, s, d = x.shape
    m = k.shape[1]
    row = lambda bi, i: (bi, i, 0)
    full = lambda bi, i: (0, 0)
    return pl.pallas_call(
        _xattn_kernel,
        grid=(b, s // tm),
        in_specs=[pl.BlockSpec((1, tm, d), row),
                  pl.BlockSpec((1, d), full),
                  pl.BlockSpec(wq.shape, full),
                  pl.BlockSpec((1, XA_HD), full),
                  pl.BlockSpec((1, m, XA_WIDTH), lambda bi, i: (bi, 0, 0)),
                  pl.BlockSpec((1, m, XA_WIDTH), lambda bi, i: (bi, 0, 0)),
                  pl.BlockSpec(wo.shape, full)],
        out_specs=pl.BlockSpec((1, tm, d), row),
        out_shape=jax.ShapeDtypeStruct((b, s, d), F32),
        compiler_params=_cp(("parallel", "parallel")),
        name="xattn",
    )(x, g.reshape(1, d), wq, q_norm.reshape(1, XA_HD), k, v, wo)


def _top_rows(s, n, want_rank):
    out = []
    work = s
    rank = jnp.full(s.shape, float(s.shape[0] - 1), F32) if want_rank else None
    for i in range(n):
        m = jnp.max(work, axis=0, keepdims=True)
        out.append(m)
        hit = work >= m
        if want_rank:
            rank = jnp.where(hit, float(i), rank)
        work = jnp.where(hit, REMOVED, work)
    return out, rank


def _stack_rows(rows, pad_rows):
    tt = rows[0].shape[1]
    rowi = _iota((pad_rows, tt), 0)
    out = jnp.full((pad_rows, tt), REMOVED, F32)
    for i, r in enumerate(rows[:pad_rows]):
        out = jnp.where(rowi == i, r, out)
    return out


def _pair_sum_candidates(v1, v2, tt):
    row_full = _iota((24, tt), 0)
    row = _iota((8, tt), 0)
    v2_full = _stack_rows(v2, 24)
    v2_8 = _stack_rows(v2, 8)

    def shifted(k):
        return pltpu.roll(v2_8, k, 0)

    def pick(rows_from, first):
        out = v1[rows_from]
        for r in range(first + 1, 8):
            if rows_from + r - first < len(v1):
                out = jnp.where(row == r, v1[rows_from + r - first], out)
        return out

    return jnp.concatenate([
        jnp.where(row_full < 17, v1[0] + v2_full, REMOVED),
        v1[1] + v2_8,
        jnp.where(row < 5, v1[2] + v2_8,
                  jnp.where(row < 7, v1[5] + shifted(5), v1[8] + shifted(7))),
        jnp.where(row < 4, v1[3] + v2_8,
                  jnp.where(row < 7, v1[4] + shifted(4), v1[9] + shifted(7))),
        jnp.where(row < 2, v1[6] + v2_8,
                  jnp.where(row < 4, v1[7] + shifted(2), pick(10, 4) + v2[0])),
        jnp.where(row < 3, pick(14, 0) + v2[0], REMOVED),
    ], axis=0)


def _peer_kernel(x_ref, g_ref, wq_ref, k1_ref, k2_ref, u_ref, vt_ref, o_ref,
                 hbt_s, q_s, r2_s, na_s, e1_s, e2_s, acc_s, *, tt, ch):
    c = pl.program_id(1)
    ntop = PEER_TOPK + 1
    assert ntop == 17

    @pl.when(c == 0)
    def _():
        h = _rms(x_ref[...], g_ref[...])
        hbt_s[...] = h.T.astype(BF16)
        q = _dot(h.astype(BF16), wq_ref[...])
        for hd in range(PEER_HEADS):
            q_s[hd] = q[:, hd * 2 * PEER_HALF:(hd + 1) * 2 * PEER_HALF].astype(BF16)
        acc_s[...] = jnp.zeros_like(acc_s)

        def route(hd, carry):
            qh = q_s[hd]
            s1 = _dot_nt(k1_ref[...], qh)
            s2 = _dot_nt(k2_ref[...], qh)
            v1, _ = _top_rows(s1, ntop, False)
            v2, rank2 = _top_rows(s2, ntop, True)
            cand = _pair_sum_candidates(v1, v2, tt)
            top = v1[0] + v2[0]
            z = jnp.zeros((1, tt), F32)
            c_prev = top
            c_last = top
            work = cand
            for r in range(ntop):
                m = jnp.max(work, axis=0, keepdims=True)
                work = jnp.where(work >= m, REMOVED, work)
                if r < PEER_TOPK:
                    z = z + jnp.exp(m - top)
                c_prev, c_last = c_last, m
            thr = 0.5 * (c_prev + c_last)
            na = jnp.zeros((PEER_KEYS, tt), F32)
            for j in range(ntop):
                na = jnp.where(s1 + v2[j] >= thr, float(j + 1), na)
            r2_s[hd] = rank2.astype(BF16)
            na_s[hd] = na
            e1_s[hd] = jnp.exp(s1 - v1[0])
            e2_s[hd] = (jnp.exp(s2 - v2[0]) / z).astype(BF16)
            return carry

        lax.fori_loop(0, PEER_HEADS, route, 0, unroll=4)

    n_slab = ch // PEER_KEYS
    slabs_per_group = PEER_PROJ_ROWS // PEER_KEYS
    n_group = ch // PEER_PROJ_ROWS

    def project(gi):
        rs = slice(gi * PEER_PROJ_ROWS, (gi + 1) * PEER_PROJ_ROWS)
        return _gelu(_dot(u_ref[rs, :], hbt_s[...]).astype(BF16))

    def gate(al, gl_rows):
        a = c * n_slab + al
        wc = jnp.zeros((PEER_KEYS, tt), BF16)
        for hd in range(PEER_HEADS):
            na = na_s[hd, pl.ds(a, 1), :].astype(BF16)
            e1 = e1_s[hd, pl.ds(a, 1), :].astype(BF16)
            wc = wc + jnp.where(r2_s[hd] < na, e2_s[hd] * e1, jnp.zeros((), BF16))
        return wc * gl_rows

    ys = []
    gl_next = project(0)
    for gi in range(n_group):
        gl = gl_next
        if gi + 1 < n_group:
            gl_next = project(gi + 1)
        for k in range(slabs_per_group):
            ys.append(gate(gi * slabs_per_group + k, gl[k * PEER_KEYS:(k + 1) * PEER_KEYS]))
    y = jnp.concatenate(ys, axis=0)
    acc_s[...] += _dot(vt_ref[0], y)

    @pl.when(c == pl.num_programs(1) - 1)
    def _():
        o_ref[...] = x_ref[...] + acc_s[...].T


def peer(x, g, wq, k1p, k2p, u, vt, tt=PEER_TT, ch=PEER_CH):
    t, d = x.shape
    ne = u.shape[0]
    return pl.pallas_call(
        functools.partial(_peer_kernel, tt=tt, ch=ch),
        grid=(t // tt, ne // ch),
        in_specs=[pl.BlockSpec((tt, d), lambda i, c: (i, 0)),
                  pl.BlockSpec((1, d), lambda i, c: (0, 0)),
                  pl.BlockSpec(wq.shape, lambda i, c: (0, 0)),
                  pl.BlockSpec(k1p.shape, lambda i, c: (0, 0)),
                  pl.BlockSpec(k2p.shape, lambda i, c: (0, 0)),
                  pl.BlockSpec((ch, d), lambda i, c: (c, 0)),
                  pl.BlockSpec((1, d, ch), lambda i, c: (c, 0, 0))],
        out_specs=pl.BlockSpec((tt, d), lambda i, c: (i, 0)),
        out_shape=jax.ShapeDtypeStruct((t, d), F32),
        scratch_shapes=[pltpu.VMEM((d, tt), BF16),
                        pltpu.VMEM((PEER_HEADS, tt, 2 * PEER_HALF), BF16),
                        pltpu.VMEM((PEER_HEADS, PEER_KEYS, tt), BF16),
                        pltpu.VMEM((PEER_HEADS, PEER_KEYS, tt), F32),
                        pltpu.VMEM((PEER_HEADS, PEER_KEYS, tt), F32),
                        pltpu.VMEM((PEER_HEADS, PEER_KEYS, tt), BF16),
                        pltpu.VMEM((d, tt), F32)],
        compiler_params=_cp(("parallel", "arbitrary")),
        name="peer",
    )(x, g.reshape(1, d), wq, k1p, k2p, u, vt)


_IN_WIDTHS = (NSA_HEADS * DK, 6 * NSA_GROUPS * DK, 3 * NSA_HEADS, 2 * SGU_WIDTH, POOL_WIDTH, 3 * D_MODEL)


def _in_proj_layout():
    splits, src = [], []
    o_src = o_dst = 0
    for w in _IN_WIDTHS:
        wp = -(-w // LANES) * LANES
        splits.append((o_dst, o_dst + wp))
        src.append((o_src, o_src + w))
        o_src += w
        o_dst += wp
    return splits, src, o_dst


def _chunked_transpose(v):
    ne, d = v.shape
    return v.astype(BF16).reshape(ne // PEER_CH, PEER_CH, d).transpose(0, 2, 1)


def nsa_layer(zq, zkv, zg, q_norm, k_norm, cmp_pe, cmp_w1, cmp_w2, b, s):
    qn, kvc, ks, vs, kw, vw = nsa_prep(zq.reshape(b, s, -1), zkv.reshape(b, s, -1), q_norm, k_norm)
    nc = s // D_CMP
    xc = kvc.reshape(2, b, NSA_GROUPS, nc, D_CMP * DK)
    pe = cmp_pe.reshape(2, 2, D_CMP * DK)
    w1 = cmp_w1.reshape(2, 2, D_CMP * DK, CMP_HIDDEN).astype(BF16)
    kc, vct = cmp_mlp(xc, pe, w1, cmp_w2.astype(BF16), k_norm[0])
    oc, selb = cmp_attn_select(qn, kc, vct)
    ow = win_attn(qn, kw, vw)
    gl = zg[:, :3 * NSA_HEADS].reshape(b, s, NSA_HEADS, 3).transpose(0, 2, 1, 3)
    return sel_attn(qn, selb, ks, vs, oc, ow, gl)


def kernel(x, mem, mix_norm, w_in, nsa_q_norm, nsa_k_norm, cmp_pe, cmp_w1, cmp_w2, sgu_ln_g, sgu_ln_b, sgu_w, sgu_b, pool_w, pool_scale, lift_a, lift_b, lift_c, w_out, xa_norm, mem_norm, xa_wq, xa_wk, xa_wv, xa_q_norm, xa_k_norm, xa_wo, ffn_norm, peer_wq, peer_keys1, peer_keys2, peer_u, peer_v):
    b, s, d = x.shape
    t = b * s
    depth = w_in.shape[0]
    splits, src, n_pad = _in_proj_layout()
    zeros_half = jnp.zeros((PEER_KEYS, PEER_HALF), BF16)
    for l in range(depth):
        w_parts = []
        for (a0, a1), (d0, d1) in zip(src, splits):
            w_parts.append(jnp.pad(w_in[l][:, a0:a1], ((0, 0), (0, (d1 - d0) - (a1 - a0)))))
        w_pad = jnp.concatenate(w_parts, axis=1).astype(BF16)
        zq, zkv, zg, zs, zp, zm = norm_matmul(x.reshape(t, d), mix_norm[l], w_pad, splits, 256, "in_proj")

        oa = nsa_layer(zq, zkv, zg, nsa_q_norm[l], nsa_k_norm[l], cmp_pe[l], cmp_w1[l], cmp_w2[l], b, s)
        bias_full = jnp.repeat(sgu_b[l].T, SGU_WIDTH // SGU_GROUPS, axis=1)
        ob = sgu(zs, sgu_ln_g[l], sgu_ln_b[l], sgu_w[l], bias_full)
        cg = POOL_WIDTH // len(POOL_WINDOWS)
        w_bd = jnp.zeros((POOL_WIDTH, POOL_WIDTH), F32)
        for gi in range(len(POOL_WINDOWS)):
            w_bd = w_bd.at[gi * cg:(gi + 1) * cg, gi * cg:(gi + 1) * cg].set(pool_w[l, gi])
        oc = pool(zp.reshape(b, s, POOL_WIDTH), w_bd.astype(BF16), pool_scale[l])
        x = merge(x, oa, ob.reshape(b, s, SGU_WIDTH), oc, zm.reshape(b, s, 3 * d),
                  lift_a[l].astype(BF16), lift_b[l].astype(BF16), lift_c[l].astype(BF16),
                  w_out[l].astype(BF16))

        mk, mv = mem_kv(mem, mem_norm[l], xa_wk[l].astype(BF16), xa_wv[l].astype(BF16), xa_k_norm[l])
        x = xattn(x, xa_norm[l], xa_wq[l].astype(BF16), xa_q_norm[l], mk, mv, xa_wo[l].astype(BF16))

        k1p = jnp.concatenate([peer_keys1[l].astype(BF16), zeros_half], axis=1)
        k2p = jnp.concatenate([zeros_half, peer_keys2[l].astype(BF16)], axis=1)
        x = peer(x.reshape(t, d), ffn_norm[l], peer_wq[l].astype(BF16), k1p, k2p,
                 peer_u[l].astype(BF16), _chunked_transpose(peer_v[l])).reshape(b, s, d)
    return x
```

```python
import functools

import jax
import jax.numpy as jnp
from jax import lax
from jax.experimental import pallas as pl
from jax.experimental.pallas import tpu as pltpu

F32 = jnp.float32
BF16 = jnp.bfloat16

EPS = 1e-6
LOG2E = 1.4426950408889634
NEG = -1e30
REMOVED = -3e38

D_MODEL = 1024
DK = 64
NSA_HEADS = 8
NSA_GROUPS = 2
NSA_REP = NSA_HEADS // NSA_GROUPS
L_CMP, D_CMP = 32, 16
CMP_HIDDEN = 128
L_SLC = 64
N_SEL = 16
WINDOW = 512
QB = 128
FORCE_BONUS = 1.0e3
SGU_WIDTH = 256
SGU_GROUPS = 4
SGU_CHUNK = 128
POOL_WIDTH = 256
POOL_WINDOWS = (2, 4, 8, 16)
XA_HEADS, XA_HD = 4, 128
XA_WIDTH = XA_HEADS * XA_HD
PEER_HEADS = 8
PEER_KEYS = 128
PEER_TOPK = 16
PEER_HALF = 64

LANES = 128
SEL_VR = 2 * DK
SEL_SUB = 512
SEL_QBLOCKS = 1
SEL_UNROLL = 8
WIN_QBLOCKS = 4
CMP_QBLOCKS = 4
CMP_CHUNK = 256
CMP_HALO = 8
PEER_TT = 512
PEER_CH = 2048
PEER_PROJ_ROWS = 128
VMEM_LIMIT = 56 * 1024 * 1024


def _cp(sem):
    return pltpu.CompilerParams(dimension_semantics=sem, vmem_limit_bytes=VMEM_LIMIT)


def _gelu(x):
    return 0.5 * x * (1.0 + jnp.tanh(0.7978845608028654 * (x + 0.044715 * (x * x * x))))


def _rms(x, g):
    return x * lax.rsqrt(jnp.mean(x * x, axis=-1, keepdims=True) + EPS) * g


def _dot(a, b):
    return jnp.dot(a, b, preferred_element_type=F32)


def _dot_nt(a, b):
    return lax.dot_general(a, b, (((1,), (1,)), ((), ())), preferred_element_type=F32)


def _iota(shape, dim):
    return lax.broadcasted_iota(jnp.int32, shape, dim)


def _div_pow2(x, n):
    assert n & (n - 1) == 0
    return lax.shift_right_logical(x, jnp.int32(n.bit_length() - 1))


def _mod_pow2(x, n):
    assert n & (n - 1) == 0
    return x & (n - 1)


def _norm_matmul_kernel(x_ref, g_ref, w_ref, *o_refs, splits):
    hb = _rms(x_ref[...], g_ref[...]).astype(BF16)
    for o_ref, (a, b) in zip(o_refs, splits):
        o_ref[...] = _dot(hb, w_ref[:, a:b]).astype(o_ref.dtype)


def norm_matmul(x, g, w, splits, tm, name):
    t, d = x.shape
    n = w.shape[1]
    return pl.pallas_call(
        functools.partial(_norm_matmul_kernel, splits=splits),
        grid=(t // tm,),
        in_specs=[pl.BlockSpec((tm, d), lambda i: (i, 0)),
                  pl.BlockSpec((1, d), lambda i: (0, 0)),
                  pl.BlockSpec((d, n), lambda i: (0, 0))],
        out_specs=[pl.BlockSpec((tm, b - a), lambda i: (i, 0)) for a, b in splits],
        out_shape=[jax.ShapeDtypeStruct((t, b - a), F32) for a, b in splits],
        compiler_params=_cp(("parallel",)),
        name=name,
    )(x, g.reshape(1, d), w)


def _nsa_prep_kernel(zq_ref, zkv_ref, qn_ref, kn_ref,
                     q_out, kvc_out, ks_out, vs_out, kw_out, vw_out, *, ts):
    zq = zq_ref[0]
    scale = DK ** -0.5 * LOG2E
    for h in range(NSA_HEADS):
        q_out[0, h] = (_rms(zq[:, h * DK:(h + 1) * DK], qn_ref[...]) * scale).astype(BF16)
    zkv = zkv_ref[0]

    def piece(i, g):
        o = i * NSA_GROUPS * DK + g * DK
        return zkv[:, o:o + DK]

    ones_col = jnp.where(_iota((ts, SEL_VR - DK), 1) == 0, 1.0, 0.0)
    for g in range(NSA_GROUPS):
        kvc_out[0, 0, g] = piece(0, g)
        kvc_out[1, 0, g] = piece(1, g)
        ks_out[0, g] = _rms(piece(2, g), kn_ref[1:2, :]).astype(BF16)
        vs_out[0, g] = jnp.concatenate([piece(3, g), ones_col], axis=-1).T.astype(BF16)
        kw_out[0, g] = _rms(piece(4, g), kn_ref[2:3, :]).astype(BF16)
        vw_out[0, g] = jnp.concatenate([piece(5, g), ones_col], axis=-1).T.astype(BF16)


def nsa_prep(zq, zkv, q_norm, k_norm, ts=512):
    b, s, _ = zq.shape
    g = NSA_GROUPS
    hm = lambda bi, si: (bi, 0, si, 0)
    return pl.pallas_call(
        functools.partial(_nsa_prep_kernel, ts=ts),
        grid=(b, s // ts),
        in_specs=[pl.BlockSpec((1, ts, NSA_HEADS * DK), lambda bi, si: (bi, si, 0)),
                  pl.BlockSpec((1, ts, 6 * g * DK), lambda bi, si: (bi, si, 0)),
                  pl.BlockSpec((1, DK), lambda bi, si: (0, 0)),
                  pl.BlockSpec((3, DK), lambda bi, si: (0, 0))],
        out_specs=[pl.BlockSpec((1, NSA_HEADS, ts, DK), hm),
                   pl.BlockSpec((2, 1, g, ts, DK), lambda bi, si: (0, bi, 0, si, 0)),
                   pl.BlockSpec((1, g, ts, DK), hm),
                   pl.BlockSpec((1, g, SEL_VR, ts), lambda bi, si: (bi, 0, 0, si)),
                   pl.BlockSpec((1, g, ts, DK), hm),
                   pl.BlockSpec((1, g, SEL_VR, ts), lambda bi, si: (bi, 0, 0, si))],
        out_shape=[jax.ShapeDtypeStruct((b, NSA_HEADS, s, DK), BF16),
                   jax.ShapeDtypeStruct((2, b, g, s, DK), F32),
                   jax.ShapeDtypeStruct((b, g, s, DK), BF16),
                   jax.ShapeDtypeStruct((b, g, SEL_VR, s), BF16),
                   jax.ShapeDtypeStruct((b, g, s, DK), BF16),
                   jax.ShapeDtypeStruct((b, g, SEL_VR, s), BF16)],
        compiler_params=_cp(("parallel", "parallel")),
        name="nsa_prep",
    )(zq, zkv, q_norm.reshape(1, DK), k_norm)


def _cmp_mlp_kernel(x_ref, pe_ref, w1_ref, w2_ref, kn_ref, k_out, vt_out, *, nc):
    c = pl.program_id(2)
    x = x_ref[0, 0, 0]
    a = _dot((x + pe_ref[0, 0:1, :]).astype(BF16), w1_ref[0, 0])
    b = _dot((x + pe_ref[0, 1:2, :]).astype(BF16), w1_ref[0, 1])
    pre = a + pltpu.roll(b, nc - 1, 0)
    comp = _dot(_gelu(pre).astype(BF16), w2_ref[0])

    @pl.when(c == 0)
    def _():
        k_out[0, 0] = _rms(comp, kn_ref[...]).astype(BF16)

    @pl.when(c == 1)
    def _():
        ones_col = jnp.where(_iota((nc, SEL_VR - DK), 1) == 0, 1.0, 0.0)
        vt_out[0, 0] = jnp.concatenate([comp, ones_col], axis=-1).T.astype(BF16)


def cmp_mlp(xc, pe, w1, w2, k_norm0):
    _, b, g, nc, width = xc.shape
    return pl.pallas_call(
        functools.partial(_cmp_mlp_kernel, nc=nc),
        grid=(b, g, 2),
        in_specs=[pl.BlockSpec((1, 1, 1, nc, width), lambda bi, gi, c: (c, bi, gi, 0, 0)),
                  pl.BlockSpec((1, 2, width), lambda bi, gi, c: (c, 0, 0)),
                  pl.BlockSpec((1, 2, width, CMP_HIDDEN), lambda bi, gi, c: (c, 0, 0, 0)),
                  pl.BlockSpec((1, CMP_HIDDEN, DK), lambda bi, gi, c: (c, 0, 0)),
                  pl.BlockSpec((1, DK), lambda bi, gi, c: (0, 0))],
        out_specs=[pl.BlockSpec((1, 1, nc, DK), lambda bi, gi, c: (bi, gi, 0, 0)),
                   pl.BlockSpec((1, 1, SEL_VR, nc), lambda bi, gi, c: (bi, gi, 0, 0))],
        out_shape=[jax.ShapeDtypeStruct((b, g, nc, DK), BF16),
                   jax.ShapeDtypeStruct((b, g, SEL_VR, nc), BF16)],
        compiler_params=_cp(("parallel", "parallel", "arbitrary")),
        name="cmp_mlp",
    )(xc, pe, w1, w2, k_norm0.reshape(1, DK))


def _cmp_attn_kernel(q_ref, k_ref, vt_ref, oc_ref, sb_ref, s_s, acc_s, ps_s, *, nc, n_slc):
    qspan = CMP_QBLOCKS * QB
    q0 = pl.program_id(2) * qspan
    rows = NSA_REP * qspan
    ck = CMP_CHUNK
    q = q_ref[0].reshape(rows, DK)
    n_chunk = ((q0 + qspan) // D_CMP + ck - 1) // ck
    t = q0 + _mod_pow2(_iota((ck, rows), 1), qspan)

    def pass_a(ci, m):
        off = pl.multiple_of(ci * ck, ck)
        st = _dot_nt(k_ref[0, 0, pl.ds(off, ck), :], q)
        n = off + _iota((ck, rows), 0)
        st = jnp.where(n * D_CMP + (L_CMP - 1) <= t, st, NEG)
        s_s[ci] = st
        return jnp.maximum(m, jnp.max(st, axis=0, keepdims=True))

    m = lax.fori_loop(0, n_chunk, pass_a, jnp.full((1, rows), NEG, F32))
    acc_s[...] = jnp.zeros_like(acc_s)
    ps_s[...] = jnp.zeros_like(ps_s)

    def pass_b(ci, den):
        off = pl.multiple_of(ci * ck, ck)
        e = jnp.exp2(s_s[ci] - m)
        s_s[ci] = e
        acc_s[...] += _dot(vt_ref[0, 0, :, pl.ds(off, ck)], e.astype(BF16))
        return den + jnp.sum(e, axis=0, keepdims=True)

    den = lax.fori_loop(0, n_chunk, pass_b, jnp.zeros((1, rows), F32))
    seen = m > 0.5 * NEG
    inv_o = jnp.where(seen, 1.0 / jnp.maximum(acc_s[DK:DK + 1, :], 1e-30), 0.0)
    inv_p = jnp.where(seen, 1.0 / jnp.maximum(den, 1e-30), 0.0)
    oc_ref[0] = (acc_s[...] * inv_o).T[:, 0:DK].reshape(NSA_REP, qspan, DK)

    def pass_c(ci, carry):
        off = pl.multiple_of(ci * ck, ck)
        p = s_s[ci] * inv_p
        ps = p[:, 0:qspan]
        for r in range(1, NSA_REP):
            ps = ps + p[:, r * qspan:(r + 1) * qspan]
        for qi in range(CMP_QBLOCKS):
            ps_s[qi, pl.ds(CMP_HALO + off, ck), :] = ps[:, qi * QB:(qi + 1) * QB]
        return carry

    lax.fori_loop(0, n_chunk, pass_c, 0)

    ratio = L_SLC // D_CMP
    imp = jnp.zeros((n_slc, qspan), F32)
    for k in range(-(L_CMP // D_CMP - 1), ratio):
        overlap = min(k * D_CMP + L_CMP, L_SLC) - max(k * D_CMP, 0)
        taps = [ps_s[qi, pl.ds(CMP_HALO + k, n_slc, stride=ratio), :] for qi in range(CMP_QBLOCKS)]
        imp = imp + (overlap / D_CMP) * jnp.concatenate(taps, axis=1)

    cur = _div_pow2(q0 + _iota((n_slc, qspan), 1), L_SLC)
    j = _iota((n_slc, qspan), 0)
    forced = (j == 0) | (j == cur) | (j == cur - 1)
    score = jnp.where(j <= cur, imp + jnp.where(forced, FORCE_BONUS, 0.0), NEG)
    work = score
    tau = None
    for _ in range(N_SEL):
        tau = jnp.max(work, axis=0, keepdims=True)
        work = jnp.where(work >= tau, REMOVED, work)
    selected = (score > 0.5 * NEG) & (score >= tau)
    bias = jnp.where(selected, 0.0, NEG)
    for qi in range(CMP_QBLOCKS):
        sb_ref[0, 0, qi] = bias[:, qi * QB:(qi + 1) * QB]


def cmp_attn_select(qn, kc, vct):
    b, h, s, _ = qn.shape
    g = NSA_GROUPS
    nc = kc.shape[2]
    n_slc = s // L_SLC
    nqb = s // QB
    qspan = CMP_QBLOCKS * QB
    rows = NSA_REP * qspan
    return pl.pallas_call(
        functools.partial(_cmp_attn_kernel, nc=nc, n_slc=n_slc),
        grid=(b, g, s // qspan),
        in_specs=[pl.BlockSpec((1, NSA_REP, qspan, DK), lambda bi, gi, qi: (bi, gi, qi, 0)),
                  pl.BlockSpec((1, 1, nc, DK), lambda bi, gi, qi: (bi, gi, 0, 0)),
                  pl.BlockSpec((1, 1, SEL_VR, nc), lambda bi, gi, qi: (bi, gi, 0, 0))],
        out_specs=[pl.BlockSpec((1, NSA_REP, qspan, DK), lambda bi, gi, qi: (bi, gi, qi, 0)),
                   pl.BlockSpec((1, 1, CMP_QBLOCKS, n_slc, QB), lambda bi, gi, qi: (bi, gi, qi, 0, 0))],
        out_shape=[jax.ShapeDtypeStruct((b, h, s, DK), F32),
                   jax.ShapeDtypeStruct((b, g, nqb, n_slc, QB), F32)],
        scratch_shapes=[pltpu.VMEM((nc // CMP_CHUNK, CMP_CHUNK, rows), F32),
                        pltpu.VMEM((SEL_VR, rows), F32),
                        pltpu.VMEM((CMP_QBLOCKS, CMP_HALO + nc, QB), F32)],
        compiler_params=_cp(("parallel", "parallel", "parallel")),
        name="cmp_attn_select",
    )(qn, kc, vct)


def _win_attn_kernel(q_ref, *refs):
    nkb = WINDOW // QB + 1
    nrefs = nkb + WIN_QBLOCKS - 1
    k_refs, v_refs, o_ref = refs[:nrefs], refs[nrefs:2 * nrefs], refs[2 * nrefs]
    rows = NSA_REP * QB
    nk = nkb * QB
    for hf in range(WIN_QBLOCKS):
        qb = pl.program_id(2) * WIN_QBLOCKS + hf
        qs = slice(hf * QB, (hf + 1) * QB)
        q = q_ref[0, :, qs, :].reshape(rows, DK)
        k = jnp.concatenate([r[0, 0] for r in k_refs[hf:hf + nkb]], axis=0)
        vt = jnp.concatenate([r[0, 0] for r in v_refs[hf:hf + nkb]], axis=-1)
        st = _dot_nt(k, q)
        t = qb * QB + _mod_pow2(_iota((nk, rows), 1), QB)
        kpos = (qb - (nkb - 1)) * QB + _iota((nk, rows), 0)
        delta = t - kpos
        valid = (delta >= 0) & (delta < WINDOW) & (kpos >= 0)
        st = jnp.where(valid, st, NEG)
        e = jnp.exp2(st - jnp.max(st, axis=0, keepdims=True)).astype(BF16)
        acc = _dot(vt, e)
        o = (acc / jnp.maximum(acc[DK:DK + 1, :], 1e-30)).T[:, 0:DK]
        o_ref[0, :, qs, :] = o.reshape(NSA_REP, QB, DK)


def win_attn(qn, kw, vw):
    b, h, s, _ = qn.shape
    g = NSA_GROUPS
    nkb = WINDOW // QB + 1
    nrefs = nkb + WIN_QBLOCKS - 1
    nqb = s // QB

    def first_block(qi, i):
        return jnp.maximum(qi * WIN_QBLOCKS - (nkb - 1) + i, 0)

    def k_spec(i):
        return pl.BlockSpec((1, 1, QB, DK), lambda bi, gi, qi: (bi, gi, first_block(qi, i), 0))

    def vt_spec(i):
        return pl.BlockSpec((1, 1, SEL_VR, QB), lambda bi, gi, qi: (bi, gi, 0, first_block(qi, i)))

    qspec = pl.BlockSpec((1, NSA_REP, WIN_QBLOCKS * QB, DK), lambda bi, gi, qi: (bi, gi, qi, 0))
    return pl.pallas_call(
        _win_attn_kernel,
        grid=(b, g, nqb // WIN_QBLOCKS),
        in_specs=[qspec] + [k_spec(i) for i in range(nrefs)] + [vt_spec(i) for i in range(nrefs)],
        out_specs=qspec,
        out_shape=jax.ShapeDtypeStruct((b, h, s, DK), F32),
        compiler_params=_cp(("parallel", "parallel", "parallel")),
        name="win_attn",
    )(qn, *([kw] * nrefs), *([vw] * nrefs))


def _sel_attn_kernel(q_ref, sb_ref, k_ref, vt_ref, oc_ref, ow_ref, gl_ref, o_ref,
                     m_s, acc_s, s0_s, s1_s, p0_s, p1_s, a0_s, a1_s):
    qspan = SEL_QBLOCKS * QB
    q0 = pl.program_id(2) * qspan
    rows = NSA_REP * qspan
    blocks = SEL_SUB // L_SLC
    qv = q_ref[0].reshape(rows, DK)
    nsub = (q0 + qspan + SEL_SUB - 1) // SEL_SUB
    m_s[...] = jnp.full_like(m_s, NEG)
    acc_s[...] = jnp.zeros_like(acc_s)

    def scores(i, s_ref):
        off = pl.multiple_of(i * SEL_SUB, SEL_SUB)
        s_ref[...] = _dot_nt(k_ref[0, 0, pl.ds(off, SEL_SUB), :], qv)

    def weights(i, s_ref, p_ref, a_ref, causal):
        first = pl.multiple_of(i * blocks, blocks)
        bias = jnp.concatenate([sb_ref[0, 0, qi, pl.ds(first, blocks), :] for qi in range(SEL_QBLOCKS)],
                               axis=1)
        bias = jnp.broadcast_to(bias[:, None, :], (blocks, L_SLC, qspan)).reshape(SEL_SUB, qspan)
        st = s_ref[...] + jnp.concatenate([bias] * NSA_REP, axis=1)
        if causal:
            kpos = i * SEL_SUB + _iota((SEL_SUB, rows), 0)
            t = q0 + _mod_pow2(_iota((SEL_SUB, rows), 1), qspan)
            st = jnp.where(kpos <= t, st, NEG)
        m_old = m_s[...]
        m_new = jnp.maximum(m_old, jnp.max(st, axis=0, keepdims=True))
        p_ref[...] = jnp.exp2(st - m_new).astype(BF16)
        a_ref[...] = jnp.exp2(m_old - m_new)
        m_s[...] = m_new

    def accumulate(i, p_ref, a_ref):
        off = pl.multiple_of(i * SEL_SUB, SEL_SUB)
        acc_s[...] = a_ref[...] * acc_s[...] + _dot(vt_ref[0, 0, :, pl.ds(off, SEL_SUB)], p_ref[...])

    p1_s[...] = jnp.zeros_like(p1_s)
    a1_s[...] = jnp.ones_like(a1_s)
    scores(0, s0_s)

    def two_tiles(i):
        scores(i + 1, s1_s)
        weights(i, s0_s, p0_s, a0_s, False)
        accumulate(jnp.maximum(i - 1, 0), p1_s, a1_s)
        scores(i + 2, s0_s)
        weights(i + 1, s1_s, p1_s, a1_s, False)
        accumulate(i, p0_s, a0_s)

    e = 0
    left = nsub - 1
    group = SEL_UNROLL
    while group >= 2:
        n_group = left // group

        def body(j, carry, group=group, base=e):
            for k in range(0, group, 2):
                two_tiles(base + group * j + k)
            return carry

        lax.fori_loop(0, n_group, body, 0)
        e = e + group * n_group
        left = left - group * n_group
        group //= 2
    odd_tail = left

    @pl.when(odd_tail == 0)
    def _():
        weights(e, s0_s, p0_s, a0_s, True)
        accumulate(jnp.maximum(e - 1, 0), p1_s, a1_s)
        accumulate(e, p0_s, a0_s)

    @pl.when(odd_tail == 1)
    def _():
        scores(e + 1, s1_s)
        weights(e, s0_s, p0_s, a0_s, False)
        accumulate(jnp.maximum(e - 1, 0), p1_s, a1_s)
        weights(e + 1, s1_s, p1_s, a1_s, True)
        accumulate(e, p0_s, a0_s)
        accumulate(e + 1, p1_s, a1_s)

    o_s = (acc_s[...] / jnp.maximum(acc_s[DK:DK + 1, :], 1e-30)).T[:, 0:DK]
    gate = jax.nn.sigmoid(gl_ref[0].reshape(rows, 3))
    out = (gate[:, 0:1] * oc_ref[0].reshape(rows, DK) + gate[:, 1:2] * o_s
           + gate[:, 2:3] * ow_ref[0].reshape(rows, DK))
    o_ref[0] = out.reshape(NSA_REP, qspan, DK)


def sel_attn(qn, selb, ks, vs, oc, ow, gl):
    b, h, s, _ = qn.shape
    g = NSA_GROUPS
    qspan = SEL_QBLOCKS * QB
    rows = NSA_REP * qspan
    n_slc = s // L_SLC
    qmap = lambda bi, gi, qi: (bi, gi, qi, 0)
    return pl.pallas_call(
        _sel_attn_kernel,
        grid=(b, g, s // qspan),
        in_specs=[pl.BlockSpec((1, NSA_REP, qspan, DK), qmap),
                  pl.BlockSpec((1, 1, SEL_QBLOCKS, n_slc, QB), lambda bi, gi, qi: (bi, gi, qi, 0, 0)),
                  pl.BlockSpec((1, 1, s, DK), lambda bi, gi, qi: (bi, gi, 0, 0)),
                  pl.BlockSpec((1, 1, SEL_VR, s), lambda bi, gi, qi: (bi, gi, 0, 0)),
                  pl.BlockSpec((1, NSA_REP, qspan, DK), qmap),
                  pl.BlockSpec((1, NSA_REP, qspan, DK), qmap),
                  pl.BlockSpec((1, NSA_REP, qspan, 3), qmap)],
        out_specs=pl.BlockSpec((1, NSA_REP, qspan, DK), qmap),
        out_shape=jax.ShapeDtypeStruct((b, h, s, DK), F32),
        scratch_shapes=[pltpu.VMEM((1, rows), F32),
                        pltpu.VMEM((SEL_VR, rows), F32),
                        pltpu.VMEM((SEL_SUB, rows), F32),
                        pltpu.VMEM((SEL_SUB, rows), F32),
                        pltpu.VMEM((SEL_SUB, rows), BF16),
                        pltpu.VMEM((SEL_SUB, rows), BF16),
                        pltpu.VMEM((1, rows), F32),
                        pltpu.VMEM((1, rows), F32)],
        compiler_params=_cp(("parallel", "parallel", "parallel")),
        name="sel_attn",
    )(qn, selb, ks, vs, oc, ow, gl)


def _sgu_kernel(z_ref, g_ref, b_ref, w_ref, bias_ref, o_ref, *, tm):
    z = _gelu(z_ref[...])
    u = z[:, :SGU_WIDTH]
    v = z[:, SGU_WIDTH:]
    mu = jnp.mean(v, axis=-1, keepdims=True)
    var = jnp.mean(jnp.square(v - mu), axis=-1, keepdims=True)
    vn = ((v - mu) * lax.rsqrt(var + EPS) * g_ref[...] + b_ref[...]).astype(BF16)
    tril = _iota((SGU_CHUNK, SGU_CHUNK), 1) <= _iota((SGU_CHUNK, SGU_CHUNK), 0)
    wm = [jnp.where(tril, w_ref[gi], 0.0).astype(BF16) for gi in range(SGU_GROUPS)]
    grp = _div_pow2(_iota((SGU_CHUNK, SGU_WIDTH), 1), SGU_WIDTH // SGU_GROUPS)
    for ch in range(tm // SGU_CHUNK):
        rs = slice(ch * SGU_CHUNK, (ch + 1) * SGU_CHUNK)
        mixed = bias_ref[...]
        for gi in range(SGU_GROUPS):
            mixed = mixed + jnp.where(grp == gi, _dot(wm[gi], vn[rs]), 0.0)
        o_ref[rs, :] = u[rs] * mixed


def sgu(z, ln_g, ln_b, w, bias_full, tm=512):
    t = z.shape[0]
    return pl.pallas_call(
        functools.partial(_sgu_kernel, tm=tm),
        grid=(t // tm,),
        in_specs=[pl.BlockSpec((tm, 2 * SGU_WIDTH), lambda i: (i, 0)),
                  pl.BlockSpec((1, SGU_WIDTH), lambda i: (0, 0)),
                  pl.BlockSpec((1, SGU_WIDTH), lambda i: (0, 0)),
                  pl.BlockSpec((SGU_GROUPS, SGU_CHUNK, SGU_CHUNK), lambda i: (0, 0, 0)),
                  pl.BlockSpec((SGU_CHUNK, SGU_WIDTH), lambda i: (0, 0))],
        out_specs=pl.BlockSpec((tm, SGU_WIDTH), lambda i: (i, 0)),
        out_shape=jax.ShapeDtypeStruct((t, SGU_WIDTH), F32),
        compiler_params=_cp(("parallel",)),
        name="sgu",
    )(z, ln_g.reshape(1, -1), ln_b.reshape(1, -1), w, bias_full)


def _pool_kernel(p_ref, h_ref, w_ref, sc_ref, o_ref, *, tp):
    i = pl.program_id(1)
    halo_rows = POOL_WINDOWS[-1]
    p = p_ref[0]
    halo = jnp.where(i > 0, h_ref[0], 0.0)
    ext = jnp.concatenate([halo, p], axis=0)
    sums = [ext]
    shift = 1
    for _ in POOL_WINDOWS:
        prev = sums[-1]
        sums.append(prev + pltpu.roll(prev, shift, 0))
        shift *= 2
    pos = i * tp + _iota((tp, POOL_WIDTH), 0)
    grp = _div_pow2(_iota((tp, POOL_WIDTH), 1), POOL_WIDTH // len(POOL_WINDOWS))
    d = jnp.zeros((tp, POOL_WIDTH), F32)
    for gi, w in enumerate(POOL_WINDOWS):
        mean = sums[gi + 1][halo_rows:halo_rows + tp] / jnp.minimum(pos + 1, w).astype(F32)
        d = jnp.where(grp == gi, mean, d)
    d = d - p
    o_ref[0] = _dot(d.astype(BF16), w_ref[...]) * sc_ref[...]


def pool(p, w_bd, scale, tp=512):
    b, s, c = p.shape
    halo_rows = POOL_WINDOWS[-1]
    return pl.pallas_call(
        functools.partial(_pool_kernel, tp=tp),
        grid=(b, s // tp),
        in_specs=[pl.BlockSpec((1, tp, c), lambda bi, i: (bi, i, 0)),
                  pl.BlockSpec((1, halo_rows, c),
                               lambda bi, i: (bi, jnp.maximum(i * (tp // halo_rows) - 1, 0), 0)),
                  pl.BlockSpec((c, c), lambda bi, i: (0, 0)),
                  pl.BlockSpec((1, c), lambda bi, i: (0, 0))],
        out_specs=pl.BlockSpec((1, tp, c), lambda bi, i: (bi, i, 0)),
        out_shape=jax.ShapeDtypeStruct((b, s, c), F32),
        compiler_params=_cp(("parallel", "parallel")),
        name="pool",
    )(p, p, w_bd, scale.reshape(1, c))


def _merge_kernel(x_ref, oa_ref, ob_ref, oc_ref, zm_ref, la_ref, lb_ref, lc_ref, wo_ref, o_ref):
    d = D_MODEL
    oa = jnp.concatenate([oa_ref[0, h].astype(BF16) for h in range(NSA_HEADS)], axis=-1)
    zm = zm_ref[0]
    merged = (jax.nn.sigmoid(zm[:, 0:d]) * _dot(oa, la_ref[...])
              + jax.nn.sigmoid(zm[:, d:2 * d]) * _dot(ob_ref[0].astype(BF16), lb_ref[...])
              + jax.nn.sigmoid(zm[:, 2 * d:3 * d]) * _dot(oc_ref[0].astype(BF16), lc_ref[...]))
    o_ref[0] = x_ref[0] + _dot(merged.astype(BF16), wo_ref[...])


def merge(x, oa, ob, oc, zm, la, lb, lc, wo, tm=512):
    b, s, d = x.shape
    row = lambda bi, i: (bi, i, 0)
    full = lambda bi, i: (0, 0)
    return pl.pallas_call(
        _merge_kernel,
        grid=(b, s // tm),
        in_specs=[pl.BlockSpec((1, tm, d), row),
                  pl.BlockSpec((1, NSA_HEADS, tm, DK), lambda bi, i: (bi, 0, i, 0)),
                  pl.BlockSpec((1, tm, SGU_WIDTH), row),
                  pl.BlockSpec((1, tm, POOL_WIDTH), row),
                  pl.BlockSpec((1, tm, 3 * d), row),
                  pl.BlockSpec(la.shape, full), pl.BlockSpec(lb.shape, full),
                  pl.BlockSpec(lc.shape, full), pl.BlockSpec(wo.shape, full)],
        out_specs=pl.BlockSpec((1, tm, d), row),
        out_shape=jax.ShapeDtypeStruct((b, s, d), F32),
        compiler_params=_cp(("parallel", "parallel")),
        name="merge",
    )(x, oa, ob, oc, zm, la, lb, lc, wo)


def _mem_kv_kernel(m_ref, g_ref, wk_ref, wv_ref, kn_ref, k_out, v_out):
    mh = _rms(m_ref[0], g_ref[...]).astype(BF16)
    k = _dot(mh, wk_ref[...])
    for h in range(XA_HEADS):
        hs = slice(h * XA_HD, (h + 1) * XA_HD)
        k_out[0, :, hs] = _rms(k[:, hs], kn_ref[...]).astype(BF16)
    v_out[0] = _dot(mh, wv_ref[...]).astype(BF16)


def mem_kv(mem, g, wk, wv, k_norm):
    b, m, d = mem.shape
    full = lambda bi: (0, 0)
    return pl.pallas_call(
        _mem_kv_kernel,
        grid=(b,),
        in_specs=[pl.BlockSpec((1, m, d), lambda bi: (bi, 0, 0)),
                  pl.BlockSpec((1, d), full),
                  pl.BlockSpec(wk.shape, full), pl.BlockSpec(wv.shape, full),
                  pl.BlockSpec((1, XA_HD), full)],
        out_specs=[pl.BlockSpec((1, m, XA_WIDTH), lambda bi: (bi, 0, 0))] * 2,
        out_shape=[jax.ShapeDtypeStruct((b, m, XA_WIDTH), BF16)] * 2,
        compiler_params=_cp(("parallel",)),
        name="mem_kv",
    )(mem, g.reshape(1, d), wk, wv, k_norm.reshape(1, XA_HD))


def _xattn_kernel(x_ref, g_ref, wq_ref, qn_ref, k_ref, v_ref, wo_ref, o_ref):
    x = x_ref[0]
    q = _dot(_rms(x, g_ref[...]).astype(BF16), wq_ref[...])
    k = k_ref[0]
    v = v_ref[0]
    outs = []
    for h in range(XA_HEADS):
        hs = slice(h * XA_HD, (h + 1) * XA_HD)
        qh = (_rms(q[:, hs], qn_ref[...])).astype(BF16)
        s = _dot_nt(qh, k[:, hs]) * (XA_HD ** -0.5)
        e = jnp.exp(s - jnp.max(s, axis=-1, keepdims=True))
        p = e / jnp.sum(e, axis=-1, keepdims=True)
        outs.append(_dot(p.astype(BF16), v[:, hs]).astype(BF16))
    o = jnp.concatenate(outs, axis=-1)
    o_ref[0] = x + _dot(o, wo_ref[...])


def xattn(x, g, wq, q_norm, k, v, wo, tm=512):
    b, s, d = x.shape
    m = k.shape[1]
    row = lambda bi, i: (bi, i, 0)
    full = lambda bi, i: (0, 0)
    return pl.pallas_call(
        _xattn_kernel,
        grid=(b, s // tm),
        in_specs=[pl.BlockSpec((1, tm, d), row),
                  pl.BlockSpec((1, d), full),
                  pl.BlockSpec(wq.shape, full),
                  pl.BlockSpec((1, XA_HD), full),
                  pl.BlockSpec((1, m, XA_WIDTH), lambda bi, i: (bi, 0, 0)),
                  pl.BlockSpec((1, m, XA_WIDTH), lambda bi, i: (bi, 0, 0)),
                  pl.BlockSpec(wo.shape, full)],
        out_specs=pl.BlockSpec((1, tm, d), row),
        out_shape=jax.ShapeDtypeStruct((b, s, d), F32),
        compiler_params=_cp(("parallel", "parallel")),
        name="xattn",
    )(x, g.reshape(1, d), wq, q_norm.reshape(1, XA_HD), k, v, wo)


def _top_rows(s, n, want_rank):
    out = []
    work = s
    rank = jnp.full(s.shape, float(s.shape[0] - 1), F32) if want_rank else None
    for i in range(n):
        m = jnp.max(work, axis=0, keepdims=True)
        out.append(m)
        hit = work >= m
        if want_rank:
            rank = jnp.where(hit, float(i), rank)
        work = jnp.where(hit, REMOVED, work)
    return out, rank


def _stack_rows(rows, pad_rows):
    tt = rows[0].shape[1]
    rowi = _iota((pad_rows, tt), 0)
    out = jnp.full((pad_rows, tt), REMOVED, F32)
    for i, r in enumerate(rows[:pad_rows]):
        out = jnp.where(rowi == i, r, out)
    return out


def _pair_sum_candidates(v1, v2, tt):
    row_full = _iota((24, tt), 0)
    row = _iota((8, tt), 0)
    v2_full = _stack_rows(v2, 24)
    v2_8 = _stack_rows(v2, 8)

    def shifted(k):
        return pltpu.roll(v2_8, k, 0)

    def pick(rows_from, first):
        out = v1[rows_from]
        for r in range(first + 1, 8):
            if rows_from + r - first < len(v1):
                out = jnp.where(row == r, v1[rows_from + r - first], out)
        return out

    return jnp.concatenate([
        jnp.where(row_full < 17, v1[0] + v2_full, REMOVED),
        v1[1] + v2_8,
        jnp.where(row < 5, v1[2] + v2_8,
                  jnp.where(row < 7, v1[5] + shifted(5), v1[8] + shifted(7))),
        jnp.where(row < 4, v1[3] + v2_8,
                  jnp.where(row < 7, v1[4] + shifted(4), v1[9] + shifted(7))),
        jnp.where(row < 2, v1[6] + v2_8,
                  jnp.where(row < 4, v1[7] + shifted(2), pick(10, 4) + v2[0])),
        jnp.where(row < 3, pick(14, 0) + v2[0], REMOVED),
    ], axis=0)


def _peer_kernel(x_ref, g_ref, wq_ref, k1_ref, k2_ref, u_ref, vt_ref, o_ref,
                 hbt_s, q_s, r2_s, na_s, e1_s, e2_s, acc_s, *, tt, ch):
    c = pl.program_id(1)
    ntop = PEER_TOPK + 1
    assert ntop == 17

    @pl.when(c == 0)
    def _():
        h = _rms(x_ref[...], g_ref[...])
        hbt_s[...] = h.T.astype(BF16)
        q = _dot(h.astype(BF16), wq_ref[...])
        for hd in range(PEER_HEADS):
            q_s[hd] = q[:, hd * 2 * PEER_HALF:(hd + 1) * 2 * PEER_HALF].astype(BF16)
        acc_s[...] = jnp.zeros_like(acc_s)

        def route(hd, carry):
            qh = q_s[hd]
            s1 = _dot_nt(k1_ref[...], qh)
            s2 = _dot_nt(k2_ref[...], qh)
            v1, _ = _top_rows(s1, ntop, False)
            v2, rank2 = _top_rows(s2, ntop, True)
            cand = _pair_sum_candidates(v1, v2, tt)
            top = v1[0] + v2[0]
            z = jnp.zeros((1, tt), F32)
            c_prev = top
            c_last = top
            work = cand
            for r in range(ntop):
                m = jnp.max(work, axis=0, keepdims=True)
                work = jnp.where(work >= m, REMOVED, work)
                if r < PEER_TOPK:
                    z = z + jnp.exp(m - top)
                c_prev, c_last = c_last, m
            thr = 0.5 * (c_prev + c_last)
            na = jnp.zeros((PEER_KEYS, tt), F32)
            for j in range(ntop):
                na = jnp.where(s1 + v2[j] >= thr, float(j + 1), na)
            r2_s[hd] = rank2.astype(BF16)
            na_s[hd] = na
            e1_s[hd] = jnp.exp(s1 - v1[0])
            e2_s[hd] = (jnp.exp(s2 - v2[0]) / z).astype(BF16)
            return carry

        lax.fori_loop(0, PEER_HEADS, route, 0, unroll=4)

    n_slab = ch // PEER_KEYS
    slabs_per_group = PEER_PROJ_ROWS // PEER_KEYS
    n_group = ch // PEER_PROJ_ROWS

    def project(gi):
        rs = slice(gi * PEER_PROJ_ROWS, (gi + 1) * PEER_PROJ_ROWS)
        return _gelu(_dot(u_ref[rs, :], hbt_s[...]).astype(BF16))

    def gate(al, gl_rows):
        a = c * n_slab + al
        wc = jnp.zeros((PEER_KEYS, tt), BF16)
        for hd in range(PEER_HEADS):
            na = na_s[hd, pl.ds(a, 1), :].astype(BF16)
            e1 = e1_s[hd, pl.ds(a, 1), :].astype(BF16)
            wc = wc + jnp.where(r2_s[hd] < na, e2_s[hd] * e1, jnp.zeros((), BF16))
        return wc * gl_rows

    ys = []
    gl_next = project(0)
    for gi in range(n_group):
        gl = gl_next
        if gi + 1 < n_group:
            gl_next = project(gi + 1)
        for k in range(slabs_per_group):
            ys.append(gate(gi * slabs_per_group + k, gl[k * PEER_KEYS:(k + 1) * PEER_KEYS]))
    y = jnp.concatenate(ys, axis=0)
    acc_s[...] += _dot(vt_ref[0], y)

    @pl.when(c == pl.num_programs(1) - 1)
    def _():
        o_ref[...] = x_ref[...] + acc_s[...].T


def peer(x, g, wq, k1p, k2p, u, vt, tt=PEER_TT, ch=PEER_CH):
    t, d = x.shape
    ne = u.shape[0]
    return pl.pallas_call(
        functools.partial(_peer_kernel, tt=tt, ch=ch),
        grid=(t // tt, ne // ch),
        in_specs=[pl.BlockSpec((tt, d), lambda i, c: (i, 0)),
                  pl.BlockSpec((1, d), lambda i, c: (0, 0)),
                  pl.BlockSpec(wq.shape, lambda i, c: (0, 0)),
                  pl.BlockSpec(k1p.shape, lambda i, c: (0, 0)),
                  pl.BlockSpec(k2p.shape, lambda i, c: (0, 0)),
                  pl.BlockSpec((ch, d), lambda i, c: (c, 0)),
                  pl.BlockSpec((1, d, ch), lambda i, c: (c, 0, 0))],
        out_specs=pl.BlockSpec((tt, d), lambda i, c: (i, 0)),
        out_shape=jax.ShapeDtypeStruct((t, d), F32),
        scratch_shapes=[pltpu.VMEM((d, tt), BF16),
                        pltpu.VMEM((PEER_HEADS, tt, 2 * PEER_HALF), BF16),
                        pltpu.VMEM((PEER_HEADS, PEER_KEYS, tt), BF16),
                        pltpu.VMEM((PEER_HEADS, PEER_KEYS, tt), F32),
                        pltpu.VMEM((PEER_HEADS, PEER_KEYS, tt), F32),
                        pltpu.VMEM((PEER_HEADS, PEER_KEYS, tt), BF16),
                        pltpu.VMEM((d, tt), F32)],
        compiler_params=_cp(("parallel", "arbitrary")),
        name="peer",
    )(x, g.reshape(1, d), wq, k1p, k2p, u, vt)


_IN_WIDTHS = (NSA_HEADS * DK, 6 * NSA_GROUPS * DK, 3 * NSA_HEADS, 2 * SGU_WIDTH, POOL_WIDTH, 3 * D_MODEL)


def _in_proj_layout():
    splits, src = [], []
    o_src = o_dst = 0
    for w in _IN_WIDTHS:
        wp = -(-w // LANES) * LANES
        splits.append((o_dst, o_dst + wp))
        src.append((o_src, o_src + w))
        o_src += w
        o_dst += wp
    return splits, src, o_dst


def _chunked_transpose(v):
    ne, d = v.shape
    return v.astype(BF16).reshape(ne // PEER_CH, PEER_CH, d).transpose(0, 2, 1)


def nsa_layer(zq, zkv, zg, q_norm, k_norm, cmp_pe, cmp_w1, cmp_w2, b, s):
    qn, kvc, ks, vs, kw, vw = nsa_prep(zq.reshape(b, s, -1), zkv.reshape(b, s, -1), q_norm, k_norm)
    nc = s // D_CMP
    xc = kvc.reshape(2, b, NSA_GROUPS, nc, D_CMP * DK)
    pe = cmp_pe.reshape(2, 2, D_CMP * DK)
    w1 = cmp_w1.reshape(2, 2, D_CMP * DK, CMP_HIDDEN).astype(BF16)
    kc, vct = cmp_mlp(xc, pe, w1, cmp_w2.astype(BF16), k_norm[0])
    oc, selb = cmp_attn_select(qn, kc, vct)
    ow = win_attn(qn, kw, vw)
    gl = zg[:, :3 * NSA_HEADS].reshape(b, s, NSA_HEADS, 3).transpose(0, 2, 1, 3)
    return sel_attn(qn, selb, ks, vs, oc, ow, gl)


def kernel(x, mem, mix_norm, w_in, nsa_q_norm, nsa_k_norm, cmp_pe, cmp_w1, cmp_w2, sgu_ln_g, sgu_ln_b, sgu_w, sgu_b, pool_w, pool_scale, lift_a, lift_b, lift_c, w_out, xa_norm, mem_norm, xa_wq, xa_wk, xa_wv, xa_q_norm, xa_k_norm, xa_wo, ffn_norm, peer_wq, peer_keys1, peer_keys2, peer_u, peer_v):
    b, s, d = x.shape
    t = b * s
    depth = w_in.shape[0]
    splits, src, n_pad = _in_proj_layout()
    zeros_half = jnp.zeros((PEER_KEYS, PEER_HALF), BF16)
    for l in range(depth):
        w_parts = []
        for (a0, a1), (d0, d1) in zip(src, splits):
            w_parts.append(jnp.pad(w_in[l][:, a0:a1], ((0, 0), (0, (d1 - d0) - (a1 - a0)))))
        w_pad = jnp.concatenate(w_parts, axis=1).astype(BF16)
        zq, zkv, zg, zs, zp, zm = norm_matmul(x.reshape(t, d), mix_norm[l], w_pad, splits, 256, "in_proj")

        oa = nsa_layer(zq, zkv, zg, nsa_q_norm[l], nsa_k_norm[l], cmp_pe[l], cmp_w1[l], cmp_w2[l], b, s)
        bias_full = jnp.repeat(sgu_b[l].T, SGU_WIDTH // SGU_GROUPS, axis=1)
        ob = sgu(zs, sgu_ln_g[l], sgu_ln_b[l], sgu_w[l], bias_full)
        cg = POOL_WIDTH // len(POOL_WINDOWS)
        w_bd = jnp.zeros((POOL_WIDTH, POOL_WIDTH), F32)
        for gi in range(len(POOL_WINDOWS)):
            w_bd = w_bd.at[gi * cg:(gi + 1) * cg, gi * cg:(gi + 1) * cg].set(pool_w[l, gi])
        oc = pool(zp.reshape(b, s, POOL_WIDTH), w_bd.astype(BF16), pool_scale[l])
        x = merge(x, oa, ob.reshape(b, s, SGU_WIDTH), oc, zm.reshape(b, s, 3 * d),
                  lift_a[l].astype(BF16), lift_b[l].astype(BF16), lift_c[l].astype(BF16),
                  w_out[l].astype(BF16))

        mk, mv = mem_kv(mem, mem_norm[l], xa_wk[l].astype(BF16), xa_wv[l].astype(BF16), xa_k_norm[l])
        x = xattn(x, xa_norm[l], xa_wq[l].astype(BF16), xa_q_norm[l], mk, mv, xa_wo[l].astype(BF16))

        k1p = jnp.concatenate([peer_keys1[l].astype(BF16), zeros_half], axis=1)
        k2p = jnp.concatenate([zeros_half, peer_keys2[l].astype(BF16)], axis=1)
        x = peer(x.reshape(t, d), ffn_norm[l], peer_wq[l].astype(BF16), k1p, k2p,
                 peer_u[l].astype(BF16), _chunked_transpose(peer_v[l])).reshape(b, s, d)
    return x
```

```python
import functools

import jax
import jax.numpy as jnp
from jax import lax
from jax.experimental import pallas as pl
from jax.experimental.pallas import tpu as pltpu

F32 = jnp.float32
BF16 = jnp.bfloat16

EPS = 1e-6
LOG2E = 1.4426950408889634
NEG = -1e30
REMOVED = -3e38

D_MODEL = 1024
DK = 64
NSA_HEADS = 8
NSA_GROUPS = 2
NSA_REP = NSA_HEADS // NSA_GROUPS
L_CMP, D_CMP = 32, 16
CMP_HIDDEN = 128
L_SLC = 64
N_SEL = 16
WINDOW = 512
QB = 128
FORCE_BONUS = 1.0e3
SGU_WIDTH = 256
SGU_GROUPS = 4
SGU_CHUNK = 128
POOL_WIDTH = 256
POOL_WINDOWS = (2, 4, 8, 16)
XA_HEADS, XA_HD = 4, 128
XA_WIDTH = XA_HEADS * XA_HD
PEER_HEADS = 8
PEER_KEYS = 128
PEER_TOPK = 16
PEER_HALF = 64

LANES = 128
SEL_VR = 2 * DK
SEL_SUB = 512
SEL_QBLOCKS = 1
SEL_UNROLL = 8
WIN_QBLOCKS = 8
CMP_QBLOCKS = 8
CMP_CHUNK = 256
CMP_HALO = 8
PEER_TT = 512
PEER_CH = 2048
PEER_PROJ_ROWS = 128
VMEM_LIMIT = 56 * 1024 * 1024


def _cp(sem):
    return pltpu.CompilerParams(dimension_semantics=sem, vmem_limit_bytes=VMEM_LIMIT)


def _gelu(x):
    return 0.5 * x * (1.0 + jnp.tanh(0.7978845608028654 * (x + 0.044715 * (x * x * x))))


def _rms(x, g):
    return x * lax.rsqrt(jnp.mean(x * x, axis=-1, keepdims=True) + EPS) * g


def _dot(a, b):
    return jnp.dot(a, b, preferred_element_type=F32)


def _dot_nt(a, b):
    return lax.dot_general(a, b, (((1,), (1,)), ((), ())), preferred_element_type=F32)


def _iota(shape, dim):
    return lax.broadcasted_iota(jnp.int32, shape, dim)


def _div_pow2(x, n):
    assert n & (n - 1) == 0
    return lax.shift_right_logical(x, jnp.int32(n.bit_length() - 1))


def _mod_pow2(x, n):
    assert n & (n - 1) == 0
    return x & (n - 1)


def _norm_matmul_kernel(x_ref, g_ref, w_ref, *o_refs, splits):
    hb = _rms(x_ref[...], g_ref[...]).astype(BF16)
    for o_ref, (a, b) in zip(o_refs, splits):
        o_ref[...] = _dot(hb, w_ref[:, a:b]).astype(o_ref.dtype)


def norm_matmul(x, g, w, splits, tm, name):
    t, d = x.shape
    n = w.shape[1]
    return pl.pallas_call(
        functools.partial(_norm_matmul_kernel, splits=splits),
        grid=(t // tm,),
        in_specs=[pl.BlockSpec((tm, d), lambda i: (i, 0)),
                  pl.BlockSpec((1, d), lambda i: (0, 0)),
                  pl.BlockSpec((d, n), lambda i: (0, 0))],
        out_specs=[pl.BlockSpec((tm, b - a), lambda i: (i, 0)) for a, b in splits],
        out_shape=[jax.ShapeDtypeStruct((t, b - a), F32) for a, b in splits],
        compiler_params=_cp(("parallel",)),
        name=name,
    )(x, g.reshape(1, d), w)


def _nsa_prep_kernel(zq_ref, zkv_ref, qn_ref, kn_ref,
                     q_out, kvc_out, ks_out, vs_out, kw_out, vw_out, *, ts):
    zq = zq_ref[0]
    scale = DK ** -0.5 * LOG2E
    for h in range(NSA_HEADS):
        q_out[0, h] = (_rms(zq[:, h * DK:(h + 1) * DK], qn_ref[...]) * scale).astype(BF16)
    zkv = zkv_ref[0]

    def piece(i, g):
        o = i * NSA_GROUPS * DK + g * DK
        return zkv[:, o:o + DK]

    ones_col = jnp.where(_iota((ts, SEL_VR - DK), 1) == 0, 1.0, 0.0)
    for g in range(NSA_GROUPS):
        kvc_out[0, 0, g] = piece(0, g)
        kvc_out[1, 0, g] = piece(1, g)
        ks_out[0, g] = _rms(piece(2, g), kn_ref[1:2, :]).astype(BF16)
        vs_out[0, g] = jnp.concatenate([piece(3, g), ones_col], axis=-1).T.astype(BF16)
        kw_out[0, g] = _rms(piece(4, g), kn_ref[2:3, :]).astype(BF16)
        vw_out[0, g] = jnp.concatenate([piece(5, g), ones_col], axis=-1).T.astype(BF16)


def nsa_prep(zq, zkv, q_norm, k_norm, ts=512):
    b, s, _ = zq.shape
    g = NSA_GROUPS
    hm = lambda bi, si: (bi, 0, si, 0)
    return pl.pallas_call(
        functools.partial(_nsa_prep_kernel, ts=ts),
        grid=(b, s // ts),
        in_specs=[pl.BlockSpec((1, ts, NSA_HEADS * DK), lambda bi, si: (bi, si, 0)),
                  pl.BlockSpec((1, ts, 6 * g * DK), lambda bi, si: (bi, si, 0)),
                  pl.BlockSpec((1, DK), lambda bi, si: (0, 0)),
                  pl.BlockSpec((3, DK), lambda bi, si: (0, 0))],
        out_specs=[pl.BlockSpec((1, NSA_HEADS, ts, DK), hm),
                   pl.BlockSpec((2, 1, g, ts, DK), lambda bi, si: (0, bi, 0, si, 0)),
                   pl.BlockSpec((1, g, ts, DK), hm),
                   pl.BlockSpec((1, g, SEL_VR, ts), lambda bi, si: (bi, 0, 0, si)),
                   pl.BlockSpec((1, g, ts, DK), hm),
                   pl.BlockSpec((1, g, SEL_VR, ts), lambda bi, si: (bi, 0, 0, si))],
        out_shape=[jax.ShapeDtypeStruct((b, NSA_HEADS, s, DK), BF16),
                   jax.ShapeDtypeStruct((2, b, g, s, DK), F32),
                   jax.ShapeDtypeStruct((b, g, s, DK), BF16),
                   jax.ShapeDtypeStruct((b, g, SEL_VR, s), BF16),
                   jax.ShapeDtypeStruct((b, g, s, DK), BF16),
                   jax.ShapeDtypeStruct((b, g, SEL_VR, s), BF16)],
        compiler_params=_cp(("parallel", "parallel")),
        name="nsa_prep",
    )(zq, zkv, q_norm.reshape(1, DK), k_norm)


def _cmp_mlp_kernel(x_ref, pe_ref, w1_ref, w2_ref, kn_ref, k_out, vt_out, *, nc):
    c = pl.program_id(2)
    x = x_ref[0, 0, 0]
    a = _dot((x + pe_ref[0, 0:1, :]).astype(BF16), w1_ref[0, 0])
    b = _dot((x + pe_ref[0, 1:2, :]).astype(BF16), w1_ref[0, 1])
    pre = a + pltpu.roll(b, nc - 1, 0)
    comp = _dot(_gelu(pre).astype(BF16), w2_ref[0])

    @pl.when(c == 0)
    def _():
        k_out[0, 0] = _rms(comp, kn_ref[...]).astype(BF16)

    @pl.when(c == 1)
    def _():
        ones_col = jnp.where(_iota((nc, SEL_VR - DK), 1) == 0, 1.0, 0.0)
        vt_out[0, 0] = jnp.concatenate([comp, ones_col], axis=-1).T.astype(BF16)


def cmp_mlp(xc, pe, w1, w2, k_norm0):
    _, b, g, nc, width = xc.shape
    return pl.pallas_call(
        functools.partial(_cmp_mlp_kernel, nc=nc),
        grid=(b, g, 2),
        in_specs=[pl.BlockSpec((1, 1, 1, nc, width), lambda bi, gi, c: (c, bi, gi, 0, 0)),
                  pl.BlockSpec((1, 2, width), lambda bi, gi, c: (c, 0, 0)),
                  pl.BlockSpec((1, 2, width, CMP_HIDDEN), lambda bi, gi, c: (c, 0, 0, 0)),
                  pl.BlockSpec((1, CMP_HIDDEN, DK), lambda bi, gi, c: (c, 0, 0)),
                  pl.BlockSpec((1, DK), lambda bi, gi, c: (0, 0))],
        out_specs=[pl.BlockSpec((1, 1, nc, DK), lambda bi, gi, c: (bi, gi, 0, 0)),
                   pl.BlockSpec((1, 1, SEL_VR, nc), lambda bi, gi, c: (bi, gi, 0, 0))],
        out_shape=[jax.ShapeDtypeStruct((b, g, nc, DK), BF16),
                   jax.ShapeDtypeStruct((b, g, SEL_VR, nc), BF16)],
        compiler_params=_cp(("parallel", "parallel", "arbitrary")),
        name="cmp_mlp",
    )(xc, pe, w1, w2, k_norm0.reshape(1, DK))


def _cmp_attn_kernel(q_ref, k_ref, vt_ref, oc_ref, sb_ref, s_s, acc_s, ps_s, *, nc, n_slc):
    qspan = CMP_QBLOCKS * QB
    q0 = pl.program_id(2) * qspan
    rows = NSA_REP * qspan
    ck = CMP_CHUNK
    q = q_ref[0].reshape(rows, DK)
    n_chunk = ((q0 + qspan) // D_CMP + ck - 1) // ck
    t = q0 + _mod_pow2(_iota((ck, rows), 1), qspan)

    def pass_a(ci, m):
        off = pl.multiple_of(ci * ck, ck)
        st = _dot_nt(k_ref[0, 0, pl.ds(off, ck), :], q)
        n = off + _iota((ck, rows), 0)
        st = jnp.where(n * D_CMP + (L_CMP - 1) <= t, st, NEG)
        s_s[ci] = st
        return jnp.maximum(m, jnp.max(st, axis=0, keepdims=True))

    m = lax.fori_loop(0, n_chunk, pass_a, jnp.full((1, rows), NEG, F32))
    acc_s[...] = jnp.zeros_like(acc_s)
    ps_s[...] = jnp.zeros_like(ps_s)

    def pass_b(ci, den):
        off = pl.multiple_of(ci * ck, ck)
        e = jnp.exp2(s_s[ci] - m)
        s_s[ci] = e
        acc_s[...] += _dot(vt_ref[0, 0, :, pl.ds(off, ck)], e.astype(BF16))
        return den + jnp.sum(e, axis=0, keepdims=True)

    den = lax.fori_loop(0, n_chunk, pass_b, jnp.zeros((1, rows), F32))
    seen = m > 0.5 * NEG
    inv_o = jnp.where(seen, 1.0 / jnp.maximum(acc_s[DK:DK + 1, :], 1e-30), 0.0)
    inv_p = jnp.where(seen, 1.0 / jnp.maximum(den, 1e-30), 0.0)
    oc_ref[0] = (acc_s[...] * inv_o).T[:, 0:DK].reshape(NSA_REP, qspan, DK)

    def pass_c(ci, carry):
        off = pl.multiple_of(ci * ck, ck)
        p = s_s[ci] * inv_p
        ps = p[:, 0:qspan]
        for r in range(1, NSA_REP):
            ps = ps + p[:, r * qspan:(r + 1) * qspan]
        for qi in range(CMP_QBLOCKS):
            ps_s[qi, pl.ds(CMP_HALO + off, ck), :] = ps[:, qi * QB:(qi + 1) * QB]
        return carry

    lax.fori_loop(0, n_chunk, pass_c, 0)

    ratio = L_SLC // D_CMP
    imp = jnp.zeros((n_slc, qspan), F32)
    for k in range(-(L_CMP // D_CMP - 1), ratio):
        overlap = min(k * D_CMP + L_CMP, L_SLC) - max(k * D_CMP, 0)
        taps = [ps_s[qi, pl.ds(CMP_HALO + k, n_slc, stride=ratio), :] for qi in range(CMP_QBLOCKS)]
        imp = imp + (overlap / D_CMP) * jnp.concatenate(taps, axis=1)

    cur = _div_pow2(q0 + _iota((n_slc, qspan), 1), L_SLC)
    j = _iota((n_slc, qspan), 0)
    forced = (j == 0) | (j == cur) | (j == cur - 1)
    score = jnp.where(j <= cur, imp + jnp.where(forced, FORCE_BONUS, 0.0), NEG)
    work = score
    tau = None
    for _ in range(N_SEL):
        tau = jnp.max(work, axis=0, keepdims=True)
        work = jnp.where(work >= tau, REMOVED, work)
    selected = (score > 0.5 * NEG) & (score >= tau)
    bias = jnp.where(selected, 0.0, NEG)
    for qi in range(CMP_QBLOCKS):
        sb_ref[0, 0, qi] = bias[:, qi * QB:(qi + 1) * QB]


def cmp_attn_select(qn, kc, vct):
    b, h, s, _ = qn.shape
    g = NSA_GROUPS
    nc = kc.shape[2]
    n_slc = s // L_SLC
    nqb = s // QB
    qspan = CMP_QBLOCKS * QB
    rows = NSA_REP * qspan
    return pl.pallas_call(
        functools.partial(_cmp_attn_kernel, nc=nc, n_slc=n_slc),
        grid=(b, g, s // qspan),
        in_specs=[pl.BlockSpec((1, NSA_REP, qspan, DK), lambda bi, gi, qi: (bi, gi, qi, 0)),
                  pl.BlockSpec((1, 1, nc, DK), lambda bi, gi, qi: (bi, gi, 0, 0)),
                  pl.BlockSpec((1, 1, SEL_VR, nc), lambda bi, gi, qi: (bi, gi, 0, 0))],
        out_specs=[pl.BlockSpec((1, NSA_REP, qspan, DK), lambda bi, gi, qi: (bi, gi, qi, 0)),
                   pl.BlockSpec((1, 1, CMP_QBLOCKS, n_slc, QB), lambda bi, gi, qi: (bi, gi, qi, 0, 0))],
        out_shape=[jax.ShapeDtypeStruct((b, h, s, DK), F32),
                   jax.ShapeDtypeStruct((b, g, nqb, n_slc, QB), F32)],
        scratch_shapes=[pltpu.VMEM((nc // CMP_CHUNK, CMP_CHUNK, rows), F32),
                        pltpu.VMEM((SEL_VR, rows), F32),
                        pltpu.VMEM((CMP_QBLOCKS, CMP_HALO + nc, QB), F32)],
        compiler_params=_cp(("parallel", "parallel", "parallel")),
        name="cmp_attn_select",
    )(qn, kc, vct)


def _win_attn_kernel(q_ref, *refs):
    nkb = WINDOW // QB + 1
    nrefs = nkb + WIN_QBLOCKS - 1
    k_refs, v_refs, o_ref = refs[:nrefs], refs[nrefs:2 * nrefs], refs[2 * nrefs]
    rows = NSA_REP * QB
    nk = nkb * QB
    for hf in range(WIN_QBLOCKS):
        qb = pl.program_id(2) * WIN_QBLOCKS + hf
        qs = slice(hf * QB, (hf + 1) * QB)
        q = q_ref[0, :, qs, :].reshape(rows, DK)
        k = jnp.concatenate([r[0, 0] for r in k_refs[hf:hf + nkb]], axis=0)
        vt = jnp.concatenate([r[0, 0] for r in v_refs[hf:hf + nkb]], axis=-1)
        st = _dot_nt(k, q)
        t = qb * QB + _mod_pow2(_iota((nk, rows), 1), QB)
        kpos = (qb - (nkb - 1)) * QB + _iota((nk, rows), 0)
        delta = t - kpos
        valid = (delta >= 0) & (delta < WINDOW) & (kpos >= 0)
        st = jnp.where(valid, st, NEG)
        e = jnp.exp2(st - jnp.max(st, axis=0, keepdims=True)).astype(BF16)
        acc = _dot(vt, e)
        o = (acc / jnp.maximum(acc[DK:DK + 1, :], 1e-30)).T[:, 0:DK]
        o_ref[0, :, qs, :] = o.reshape(NSA_REP, QB, DK)


def win_attn(qn, kw, vw):
    b, h, s, _ = qn.shape
    g = NSA_GROUPS
    nkb = WINDOW // QB + 1
    nrefs = nkb + WIN_QBLOCKS - 1
    nqb = s // QB

    def first_block(qi, i):
        return jnp.maximum(qi * WIN_QBLOCKS - (nkb - 1) + i, 0)

    def k_spec(i):
        return pl.BlockSpec((1, 1, QB, DK), lambda bi, gi, qi: (bi, gi, first_block(qi, i), 0))

    def vt_spec(i):
        return pl.BlockSpec((1, 1, SEL_VR, QB), lambda bi, gi, qi: (bi, gi, 0, first_block(qi, i)))

    qspec = pl.BlockSpec((1, NSA_REP, WIN_QBLOCKS * QB, DK), lambda bi, gi, qi: (bi, gi, qi, 0))
    return pl.pallas_call(
        _win_attn_kernel,
        grid=(b, g, nqb // WIN_QBLOCKS),
        in_specs=[qspec] + [k_spec(i) for i in range(nrefs)] + [vt_spec(i) for i in range(nrefs)],
        out_specs=qspec,
        out_shape=jax.ShapeDtypeStruct((b, h, s, DK), F32),
        compiler_params=_cp(("parallel", "parallel", "parallel")),
        name="win_attn",
    )(qn, *([kw] * nrefs), *([vw] * nrefs))


def _sel_attn_kernel(q_ref, sb_ref, k_ref, vt_ref, oc_ref, ow_ref, gl_ref, o_ref,
                     m_s, acc_s, s0_s, s1_s, p0_s, p1_s, a0_s, a1_s):
    qspan = SEL_QBLOCKS * QB
    q0 = pl.program_id(2) * qspan
    rows = NSA_REP * qspan
    blocks = SEL_SUB // L_SLC
    qv = q_ref[0].reshape(rows, DK)
    nsub = (q0 + qspan + SEL_SUB - 1) // SEL_SUB
    m_s[...] = jnp.full_like(m_s, NEG)
    acc_s[...] = jnp.zeros_like(acc_s)

    def scores(i, s_ref):
        off = pl.multiple_of(i * SEL_SUB, SEL_SUB)
        s_ref[...] = _dot_nt(k_ref[0, 0, pl.ds(off, SEL_SUB), :], qv)

    def weights(i, s_ref, p_ref, a_ref, causal):
        first = pl.multiple_of(i * blocks, blocks)
        bias = jnp.concatenate([sb_ref[0, 0, qi, pl.ds(first, blocks), :] for qi in range(SEL_QBLOCKS)],
                               axis=1)
        bias = jnp.broadcast_to(bias[:, None, :], (blocks, L_SLC, qspan)).reshape(SEL_SUB, qspan)
        st = s_ref[...] + jnp.concatenate([bias] * NSA_REP, axis=1)
        if causal:
            kpos = i * SEL_SUB + _iota((SEL_SUB, rows), 0)
            t = q0 + _mod_pow2(_iota((SEL_SUB, rows), 1), qspan)
            st = jnp.where(kpos <= t, st, NEG)
        m_old = m_s[...]
        m_new = jnp.maximum(m_old, jnp.max(st, axis=0, keepdims=True))
        p_ref[...] = jnp.exp2(st - m_new).astype(BF16)
        a_ref[...] = jnp.exp2(m_old - m_new)
        m_s[...] = m_new

    def accumulate(i, p_ref, a_ref):
        off = pl.multiple_of(i * SEL_SUB, SEL_SUB)
        acc_s[...] = a_ref[...] * acc_s[...] + _dot(vt_ref[0, 0, :, pl.ds(off, SEL_SUB)], p_ref[...])

    p1_s[...] = jnp.zeros_like(p1_s)
    a1_s[...] = jnp.ones_like(a1_s)
    scores(0, s0_s)

    def two_tiles(i):
        scores(i + 1, s1_s)
        weights(i, s0_s, p0_s, a0_s, False)
        accumulate(jnp.maximum(i - 1, 0), p1_s, a1_s)
        scores(i + 2, s0_s)
        weights(i + 1, s1_s, p1_s, a1_s, False)
        accumulate(i, p0_s, a0_s)

    e = 0
    left = nsub - 1
    group = SEL_UNROLL
    while group >= 2:
        n_group = left // group

        def body(j, carry, group=group, base=e):
            for k in range(0, group, 2):
                two_tiles(base + group * j + k)
            return carry

        lax.fori_loop(0, n_group, body, 0)
        e = e + group * n_group
        left = left - group * n_group
        group //= 2
    odd_tail = left

    @pl.when(odd_tail == 0)
    def _():
        weights(e, s0_s, p0_s, a0_s, True)
        accumulate(jnp.maximum(e - 1, 0), p1_s, a1_s)
        accumulate(e, p0_s, a0_s)

    @pl.when(odd_tail == 1)
    def _():
        scores(e + 1, s1_s)
        weights(e, s0_s, p0_s, a0_s, False)
        accumulate(jnp.maximum(e - 1, 0), p1_s, a1_s)
        weights(e + 1, s1_s, p1_s, a1_s, True)
        accumulate(e, p0_s, a0_s)
        accumulate(e + 1, p1_s, a1_s)

    o_s = (acc_s[...] / jnp.maximum(acc_s[DK:DK + 1, :], 1e-30)).T[:, 0:DK]
    gate = jax.nn.sigmoid(gl_ref[0].reshape(rows, 3))
    out = (gate[:, 0:1] * oc_ref[0].reshape(rows, DK) + gate[:, 1:2] * o_s
           + gate[:, 2:3] * ow_ref[0].reshape(rows, DK))
    o_ref[0] = out.reshape(NSA_REP, qspan, DK)


def sel_attn(qn, selb, ks, vs, oc, ow, gl):
    b, h, s, _ = qn.shape
    g = NSA_GROUPS
    qspan = SEL_QBLOCKS * QB
    rows = NSA_REP * qspan
    n_slc = s // L_SLC
    qmap = lambda bi, gi, qi: (bi, gi, qi, 0)
    return pl.pallas_call(
        _sel_attn_kernel,
        grid=(b, g, s // qspan),
        in_specs=[pl.BlockSpec((1, NSA_REP, qspan, DK), qmap),
                  pl.BlockSpec((1, 1, SEL_QBLOCKS, n_slc, QB), lambda bi, gi, qi: (bi, gi, qi, 0, 0)),
                  pl.BlockSpec((1, 1, s, DK), lambda bi, gi, qi: (bi, gi, 0, 0)),
                  pl.BlockSpec((1, 1, SEL_VR, s), lambda bi, gi, qi: (bi, gi, 0, 0)),
                  pl.BlockSpec((1, NSA_REP, qspan, DK), qmap),
                  pl.BlockSpec((1, NSA_REP, qspan, DK), qmap),
                  pl.BlockSpec((1, NSA_REP, qspan, 3), qmap)],
        out_specs=pl.BlockSpec((1, NSA_REP, qspan, DK), qmap),
        out_shape=jax.ShapeDtypeStruct((b, h, s, DK), F32),
        scratch_shapes=[pltpu.VMEM((1, rows), F32),
                        pltpu.VMEM((SEL_VR, rows), F32),
                        pltpu.VMEM((SEL_SUB, rows), F32),
                        pltpu.VMEM((SEL_SUB, rows), F32),
                        pltpu.VMEM((SEL_SUB, rows), BF16),
                        pltpu.VMEM((SEL_SUB, rows), BF16),
                        pltpu.VMEM((1, rows), F32),
                        pltpu.VMEM((1, rows), F32)],
        compiler_params=_cp(("parallel", "parallel", "parallel")),
        name="sel_attn",
    )(qn, selb, ks, vs, oc, ow, gl)


def _sgu_kernel(z_ref, g_ref, b_ref, w_ref, bias_ref, o_ref, *, tm):
    z = _gelu(z_ref[...])
    u = z[:, :SGU_WIDTH]
    v = z[:, SGU_WIDTH:]
    mu = jnp.mean(v, axis=-1, keepdims=True)
    var = jnp.mean(jnp.square(v - mu), axis=-1, keepdims=True)
    vn = ((v - mu) * lax.rsqrt(var + EPS) * g_ref[...] + b_ref[...]).astype(BF16)
    tril = _iota((SGU_CHUNK, SGU_CHUNK), 1) <= _iota((SGU_CHUNK, SGU_CHUNK), 0)
    wm = [jnp.where(tril, w_ref[gi], 0.0).astype(BF16) for gi in range(SGU_GROUPS)]
    grp = _div_pow2(_iota((SGU_CHUNK, SGU_WIDTH), 1), SGU_WIDTH // SGU_GROUPS)
    for ch in range(tm // SGU_CHUNK):
        rs = slice(ch * SGU_CHUNK, (ch + 1) * SGU_CHUNK)
        mixed = bias_ref[...]
        for gi in range(SGU_GROUPS):
            mixed = mixed + jnp.where(grp == gi, _dot(wm[gi], vn[rs]), 0.0)
        o_ref[rs, :] = u[rs] * mixed


def sgu(z, ln_g, ln_b, w, bias_full, tm=512):
    t = z.shape[0]
    return pl.pallas_call(
        functools.partial(_sgu_kernel, tm=tm),
        grid=(t // tm,),
        in_specs=[pl.BlockSpec((tm, 2 * SGU_WIDTH), lambda i: (i, 0)),
                  pl.BlockSpec((1, SGU_WIDTH), lambda i: (0, 0)),
                  pl.BlockSpec((1, SGU_WIDTH), lambda i: (0, 0)),
                  pl.BlockSpec((SGU_GROUPS, SGU_CHUNK, SGU_CHUNK), lambda i: (0, 0, 0)),
                  pl.BlockSpec((SGU_CHUNK, SGU_WIDTH), lambda i: (0, 0))],
        out_specs=pl.BlockSpec((tm, SGU_WIDTH), lambda i: (i, 0)),
        out_shape=jax.ShapeDtypeStruct((t, SGU_WIDTH), F32),
        compiler_params=_cp(("parallel",)),
        name="sgu",
    )(z, ln_g.reshape(1, -1), ln_b.reshape(1, -1), w, bias_full)


def _pool_kernel(p_ref, h_ref, w_ref, sc_ref, o_ref, *, tp):
    i = pl.program_id(1)
    halo_rows = POOL_WINDOWS[-1]
    p = p_ref[0]
    halo = jnp.where(i > 0, h_ref[0], 0.0)
    ext = jnp.concatenate([halo, p], axis=0)
    sums = [ext]
    shift = 1
    for _ in POOL_WINDOWS:
        prev = sums[-1]
        sums.append(prev + pltpu.roll(prev, shift, 0))
        shift *= 2
    pos = i * tp + _iota((tp, POOL_WIDTH), 0)
    grp = _div_pow2(_iota((tp, POOL_WIDTH), 1), POOL_WIDTH // len(POOL_WINDOWS))
    d = jnp.zeros((tp, POOL_WIDTH), F32)
    for gi, w in enumerate(POOL_WINDOWS):
        mean = sums[gi + 1][halo_rows:halo_rows + tp] / jnp.minimum(pos + 1, w).astype(F32)
        d = jnp.where(grp == gi, mean, d)
    d = d - p
    o_ref[0] = _dot(d.astype(BF16), w_ref[...]) * sc_ref[...]


def pool(p, w_bd, scale, tp=512):
    b, s, c = p.shape
    halo_rows = POOL_WINDOWS[-1]
    return pl.pallas_call(
        functools.partial(_pool_kernel, tp=tp),
        grid=(b, s // tp),
        in_specs=[pl.BlockSpec((1, tp, c), lambda bi, i: (bi, i, 0)),
                  pl.BlockSpec((1, halo_rows, c),
                               lambda bi, i: (bi, jnp.maximum(i * (tp // halo_rows) - 1, 0), 0)),
                  pl.BlockSpec((c, c), lambda bi, i: (0, 0)),
                  pl.BlockSpec((1, c), lambda bi, i: (0, 0))],
        out_specs=pl.BlockSpec((1, tp, c), lambda bi, i: (bi, i, 0)),
        out_shape=jax.ShapeDtypeStruct((b, s, c), F32),
        compiler_params=_cp(("parallel", "parallel")),
        name="pool",
    )(p, p, w_bd, scale.reshape(1, c))


def _merge_kernel(x_ref, oa_ref, ob_ref, oc_ref, zm_ref, la_ref, lb_ref, lc_ref, wo_ref, o_ref):
    d = D_MODEL
    oa = jnp.concatenate([oa_ref[0, h].astype(BF16) for h in range(NSA_HEADS)], axis=-1)
    zm = zm_ref[0]
    merged = (jax.nn.sigmoid(zm[:, 0:d]) * _dot(oa, la_ref[...])
              + jax.nn.sigmoid(zm[:, d:2 * d]) * _dot(ob_ref[0].astype(BF16), lb_ref[...])
              + jax.nn.sigmoid(zm[:, 2 * d:3 * d]) * _dot(oc_ref[0].astype(BF16), lc_ref[...]))
    o_ref[0] = x_ref[0] + _dot(merged.astype(BF16), wo_ref[...])


def merge(x, oa, ob, oc, zm, la, lb, lc, wo, tm=512):
    b, s, d = x.shape
    row = lambda bi, i: (bi, i, 0)
    full = lambda bi, i: (0, 0)
    return pl.pallas_call(
        _merge_kernel,
        grid=(b, s // tm),
        in_specs=[pl.BlockSpec((1, tm, d), row),
                  pl.BlockSpec((1, NSA_HEADS, tm, DK), lambda bi, i: (bi, 0, i, 0)),
                  pl.BlockSpec((1, tm, SGU_WIDTH), row),
                  pl.BlockSpec((1, tm, POOL_WIDTH), row),
                  pl.BlockSpec((1, tm, 3 * d), row),
                  pl.BlockSpec(la.shape, full), pl.BlockSpec(lb.shape, full),
                  pl.BlockSpec(lc.shape, full), pl.BlockSpec(wo.shape, full)],
        out_specs=pl.BlockSpec((1, tm, d), row),
        out_shape=jax.ShapeDtypeStruct((b, s, d), F32),
        compiler_params=_cp(("parallel", "parallel")),
        name="merge",
    )(x, oa, ob, oc, zm, la, lb, lc, wo)


def _mem_kv_kernel(m_ref, g_ref, wk_ref, wv_ref, kn_ref, k_out, v_out):
    mh = _rms(m_ref[0], g_ref[...]).astype(BF16)
    k = _dot(mh, wk_ref[...])
    for h in range(XA_HEADS):
        hs = slice(h * XA_HD, (h + 1) * XA_HD)
        k_out[0, :, hs] = _rms(k[:, hs], kn_ref[...]).astype(BF16)
    v_out[0] = _dot(mh, wv_ref[...]).astype(BF16)


def mem_kv(mem, g, wk, wv, k_norm):
    b, m, d = mem.shape
    full = lambda bi: (0, 0)
    return pl.pallas_call(
        _mem_kv_kernel,
        grid=(b,),
        in_specs=[pl.BlockSpec((1, m, d), lambda bi: (bi, 0, 0)),
                  pl.BlockSpec((1, d), full),
                  pl.BlockSpec(wk.shape, full), pl.BlockSpec(wv.shape, full),
                  pl.BlockSpec((1, XA_HD), full)],
        out_specs=[pl.BlockSpec((1, m, XA_WIDTH), lambda bi: (bi, 0, 0))] * 2,
        out_shape=[jax.ShapeDtypeStruct((b, m, XA_WIDTH), BF16)] * 2,
        compiler_params=_cp(("parallel",)),
        name="mem_kv",
    )(mem, g.reshape(1, d), wk, wv, k_norm.reshape(1, XA_HD))


def _xattn_kernel(x_ref, g_ref, wq_ref, qn_ref, k_ref, v_ref, wo_ref, o_ref):
    x = x_ref[0]
    q = _dot(_rms(x, g_ref[...]).astype(BF16), wq_ref[...])
    k = k_ref[0]
    v = v_ref[0]
    outs = []
    for h in range(XA_HEADS):
        hs = slice(h * XA_HD, (h + 1) * XA_HD)
        qh = (_rms(q[:, hs], qn_ref[...])).astype(BF16)
        s = _dot_nt(qh, k[:, hs]) * (XA_HD ** -0.5)
        e = jnp.exp(s - jnp.max(s, axis=-1, keepdims=True))
        p = e / jnp.sum(e, axis=-1, keepdims=True)
        outs.append(_dot(p.astype(BF16), v[:, hs]).astype(BF16))
    o = jnp.concatenate(outs, axis=-1)
    o_ref[0] = x + _dot(o, wo_ref[...])


def xattn(x, g, wq, q_norm, k, v, wo, tm=512):
    b, s, d = x.shape
    m = k.shape[1]
    row = lambda bi, i: (bi, i, 0)
    full = lambda bi, i: (0, 0)
    return pl.pallas_call(
        _xattn_kernel,
        grid=(b, s // tm),
        in_specs=[pl.BlockSpec((1, tm, d), row),
                  pl.BlockSpec((1, d), full),
                  pl.BlockSpec(wq.shape, full),
                  pl.BlockSpec((1, XA_HD), full),
                  pl.BlockSpec((1, m, XA_WIDTH), lambda bi, i: (bi, 0, 0)),
                  pl.BlockSpec((1, m, XA_WIDTH), lambda bi, i: (bi, 0, 0)),
                  pl.BlockSpec(wo.shape, full)],
        out_specs=pl.BlockSpec((1, tm, d), row),
        out_shape=jax.ShapeDtypeStruct((b, s, d), F32),
        compiler_params=_cp(("parallel", "parallel")),
        name="xattn",
    )(x, g.reshape(1, d), wq, q_norm.reshape(1, XA_HD), k, v, wo)


def _top_rows(s, n, want_rank):
    out = []
    work = s
    rank = jnp.full(s.shape, float(s.shape[0] - 1), F32) if want_rank else None
    for i in range(n):
        m = jnp.max(work, axis=0, keepdims=True)
        out.append(m)
        hit = work >= m
        if want_rank:
            rank = jnp.where(hit, float(i), rank)
        work = jnp.where(hit, REMOVED, work)
    return out, rank


def _stack_rows(rows, pad_rows):
    tt = rows[0].shape[1]
    rowi = _iota((pad_rows, tt), 0)
    out = jnp.full((pad_rows, tt), REMOVED, F32)
    for i, r in enumerate(rows[:pad_rows]):
        out = jnp.where(rowi == i, r, out)
    return out


def _pair_sum_candidates(v1, v2, tt):
    row_full = _iota((24, tt), 0)
    row = _iota((8, tt), 0)
    v2_full = _stack_rows(v2, 24)
    v2_8 = _stack_rows(v2, 8)

    def shifted(k):
        return pltpu.roll(v2_8, k, 0)

    def pick(rows_from, first):
        out = v1[rows_from]
        for r in range(first + 1, 8):
            if rows_from + r - first < len(v1):
                out = jnp.where(row == r, v1[rows_from + r - first], out)
        return out

    return jnp.concatenate([
        jnp.where(row_full < 17, v1[0] + v2_full, REMOVED),
        v1[1] + v2_8,
        jnp.where(row < 5, v1[2] + v2_8,
                  jnp.where(row < 7, v1[5] + shifted(5), v1[8] + shifted(7))),
        jnp.where(row < 4, v1[3] + v2_8,
                  jnp.where(row < 7, v1[4] + shifted(4), v1[9] + shifted(7))),
        jnp.where(row < 2, v1[6] + v2_8,
                  jnp.where(row < 4, v1[7] + shifted(2), pick(10, 4) + v2[0])),
        jnp.where(row < 3, pick(14, 0) + v2[0], REMOVED),
    ], axis=0)


def _peer_kernel(x_ref, g_ref, wq_ref, k1_ref, k2_ref, u_ref, vt_ref, o_ref,
                 hbt_s, q_s, r2_s, na_s, e1_s, e2_s, acc_s, *, tt, ch):
    c = pl.program_id(1)
    ntop = PEER_TOPK + 1
    assert ntop == 17

    @pl.when(c == 0)
    def _():
        h = _rms(x_ref[...], g_ref[...])
        hbt_s[...] = h.T.astype(BF16)
        q = _dot(h.astype(BF16), wq_ref[...])
        for hd in range(PEER_HEADS):
            q_s[hd] = q[:, hd * 2 * PEER_HALF:(hd + 1) * 2 * PEER_HALF].astype(BF16)
        acc_s[...] = jnp.zeros_like(acc_s)

        def route(hd, carry):
            qh = q_s[hd]
            s1 = _dot_nt(k1_ref[...], qh)
            s2 = _dot_nt(k2_ref[...], qh)
            v1, _ = _top_rows(s1, ntop, False)
            v2, rank2 = _top_rows(s2, ntop, True)
            cand = _pair_sum_candidates(v1, v2, tt)
            top = v1[0] + v2[0]
            z = jnp.zeros((1, tt), F32)
            c_prev = top
            c_last = top
            work = cand
            for r in range(ntop):
                m = jnp.max(work, axis=0, keepdims=True)
                work = jnp.where(work >= m, REMOVED, work)
                if r < PEER_TOPK:
                    z = z + jnp.exp(m - top)
                c_prev, c_last = c_last, m
            thr = 0.5 * (c_prev + c_last)
            na = jnp.zeros((PEER_KEYS, tt), F32)
            for j in range(ntop):
                na = jnp.where(s1 + v2[j] >= thr, float(j + 1), na)
            r2_s[hd] = rank2.astype(BF16)
            na_s[hd] = na
            e1_s[hd] = jnp.exp(s1 - v1[0])
            e2_s[hd] = (jnp.exp(s2 - v2[0]) / z).astype(BF16)
            return carry

        lax.fori_loop(0, PEER_HEADS, route, 0, unroll=4)

    n_slab = ch // PEER_KEYS
    slabs_per_group = PEER_PROJ_ROWS // PEER_KEYS
    n_group = ch // PEER_PROJ_ROWS

    def project(gi):
        rs = slice(gi * PEER_PROJ_ROWS, (gi + 1) * PEER_PROJ_ROWS)
        return _gelu(_dot(u_ref[rs, :], hbt_s[...]).astype(BF16))

    def gate(al, gl_rows):
        a = c * n_slab + al
        wc = jnp.zeros((PEER_KEYS, tt), BF16)
        for hd in range(PEER_HEADS):
            na = na_s[hd, pl.ds(a, 1), :].astype(BF16)
            e1 = e1_s[hd, pl.ds(a, 1), :].astype(BF16)
            wc = wc + jnp.where(r2_s[hd] < na, e2_s[hd] * e1, jnp.zeros((), BF16))
        return wc * gl_rows

    ys = []
    gl_next = project(0)
    for gi in range(n_group):
        gl = gl_next
        if gi + 1 < n_group:
            gl_next = project(gi + 1)
        for k in range(slabs_per_group):
            ys.append(gate(gi * slabs_per_group + k, gl[k * PEER_KEYS:(k + 1) * PEER_KEYS]))
    y = jnp.concatenate(ys, axis=0)
    acc_s[...] += _dot(vt_ref[0], y)

    @pl.when(c == pl.num_programs(1) - 1)
    def _():
        o_ref[...] = x_ref[...] + acc_s[...].T


def peer(x, g, wq, k1p, k2p, u, vt, tt=PEER_TT, ch=PEER_CH):
    t, d = x.shape
    ne = u.shape[0]
    return pl.pallas_call(
        functools.partial(_peer_kernel, tt=tt, ch=ch),
        grid=(t // tt, ne // ch),
        in_specs=[pl.BlockSpec((tt, d), lambda i, c: (i, 0)),
                  pl.BlockSpec((1, d), lambda i, c: (0, 0)),
                  pl.BlockSpec(wq.shape, lambda i, c: (0, 0)),
                  pl.BlockSpec(k1p.shape, lambda i, c: (0, 0)),
                  pl.BlockSpec(k2p.shape, lambda i, c: (0, 0)),
                  pl.BlockSpec((ch, d), lambda i, c: (c, 0)),
                  pl.BlockSpec((1, d, ch), lambda i, c: (c, 0, 0))],
        out_specs=pl.BlockSpec((tt, d), lambda i, c: (i, 0)),
        out_shape=jax.ShapeDtypeStruct((t, d), F32),
        scratch_shapes=[pltpu.VMEM((d, tt), BF16),
                        pltpu.VMEM((PEER_HEADS, tt, 2 * PEER_HALF), BF16),
                        pltpu.VMEM((PEER_HEADS, PEER_KEYS, tt), BF16),
                        pltpu.VMEM((PEER_HEADS, PEER_KEYS, tt), F32),
                        pltpu.VMEM((PEER_HEADS, PEER_KEYS, tt), F32),
                        pltpu.VMEM((PEER_HEADS, PEER_KEYS, tt), BF16),
                        pltpu.VMEM((d, tt), F32)],
        compiler_params=_cp(("parallel", "arbitrary")),
        name="peer",
    )(x, g.reshape(1, d), wq, k1p, k2p, u, vt)


_IN_WIDTHS = (NSA_HEADS * DK, 6 * NSA_GROUPS * DK, 3 * NSA_HEADS, 2 * SGU_WIDTH, POOL_WIDTH, 3 * D_MODEL)


def _in_proj_layout():
    splits, src = [], []
    o_src = o_dst = 0
    for w in _IN_WIDTHS:
        wp = -(-w // LANES) * LANES
        splits.append((o_dst, o_dst + wp))
        src.append((o_src, o_src + w))
        o_src += w
        o_dst += wp
    return splits, src, o_dst


def _chunked_transpose(v):
    ne, d = v.shape
    return v.astype(BF16).reshape(ne // PEER_CH, PEER_CH, d).transpose(0, 2, 1)


def nsa_layer(zq, zkv, zg, q_norm, k_norm, cmp_pe, cmp_w1, cmp_w2, b, s):
    qn, kvc, ks, vs, kw, vw = nsa_prep(zq.reshape(b, s, -1), zkv.reshape(b, s, -1), q_norm, k_norm)
    nc = s // D_CMP
    xc = kvc.reshape(2, b, NSA_GROUPS, nc, D_CMP * DK)
    pe = cmp_pe.reshape(2, 2, D_CMP * DK)
    w1 = cmp_w1.reshape(2, 2, D_CMP * DK, CMP_HIDDEN).astype(BF16)
    kc, vct = cmp_mlp(xc, pe, w1, cmp_w2.astype(BF16), k_norm[0])
    oc, selb = cmp_attn_select(qn, kc, vct)
    ow = win_attn(qn, kw, vw)
    gl = zg[:, :3 * NSA_HEADS].reshape(b, s, NSA_HEADS, 3).transpose(0, 2, 1, 3)
    return sel_attn(qn, selb, ks, vs, oc, ow, gl)


def kernel(x, mem, mix_norm, w_in, nsa_q_norm, nsa_k_norm, cmp_pe, cmp_w1, cmp_w2, sgu_ln_g, sgu_ln_b, sgu_w, sgu_b, pool_w, pool_scale, lift_a, lift_b, lift_c, w_out, xa_norm, mem_norm, xa_wq, xa_wk, xa_wv, xa_q_norm, xa_k_norm, xa_wo, ffn_norm, peer_wq, peer_keys1, peer_keys2, peer_u, peer_v):
    b, s, d = x.shape
    t = b * s
    depth = w_in.shape[0]
    splits, src, n_pad = _in_proj_layout()
    zeros_half = jnp.zeros((PEER_KEYS, PEER_HALF), BF16)
    for l in range(depth):
        w_parts = []
        for (a0, a1), (d0, d1) in zip(src, splits):
            w_parts.append(jnp.pad(w_in[l][:, a0:a1], ((0, 0), (0, (d1 - d0) - (a1 - a0)))))
        w_pad = jnp.concatenate(w_parts, axis=1).astype(BF16)
        zq, zkv, zg, zs, zp, zm = norm_matmul(x.reshape(t, d), mix_norm[l], w_pad, splits, 256, "in_proj")

        oa = nsa_layer(zq, zkv, zg, nsa_q_norm[l], nsa_k_norm[l], cmp_pe[l], cmp_w1[l], cmp_w2[l], b, s)
        bias_full = jnp.repeat(sgu_b[l].T, SGU_WIDTH // SGU_GROUPS, axis=1)
        ob = sgu(zs, sgu_ln_g[l], sgu_ln_b[l], sgu_w[l], bias_full)
        cg = POOL_WIDTH // len(POOL_WINDOWS)
        w_bd = jnp.zeros((POOL_WIDTH, POOL_WIDTH), F32)
        for gi in range(len(POOL_WINDOWS)):
            w_bd = w_bd.at[gi * cg:(gi + 1) * cg, gi * cg:(gi + 1) * cg].set(pool_w[l, gi])
        oc = pool(zp.reshape(b, s, POOL_WIDTH), w_bd.astype(BF16), pool_scale[l])
        x = merge(x, oa, ob.reshape(b, s, SGU_WIDTH), oc, zm.reshape(b, s, 3 * d),
                  lift_a[l].astype(BF16), lift_b[l].astype(BF16), lift_c[l].astype(BF16),
                  w_out[l].astype(BF16))

        mk, mv = mem_kv(mem, mem_norm[l], xa_wk[l].astype(BF16), xa_wv[l].astype(BF16), xa_k_norm[l])
        x = xattn(x, xa_norm[l], xa_wq[l].astype(BF16), xa_q_norm[l], mk, mv, xa_wo[l].astype(BF16))

        k1p = jnp.concatenate([peer_keys1[l].astype(BF16), zeros_half], axis=1)
        k2p = jnp.concatenate([zeros_half, peer_keys2[l].astype(BF16)], axis=1)
        x = peer(x.reshape(t, d), ffn_norm[l], peer_wq[l].astype(BF16), k1p, k2p,
                 peer_u[l].astype(BF16), _chunked_transpose(peer_v[l])).reshape(b, s, d)
    return x
```

```python
import functools

import jax
import jax.numpy as jnp
from jax import lax
from jax.experimental import pallas as pl
from jax.experimental.pallas import tpu as pltpu

F32 = jnp.float32
BF16 = jnp.bfloat16

EPS = 1e-6
LOG2E = 1.4426950408889634
NEG = -1e30
REMOVED = -3e38

D_MODEL = 1024
DK = 64
NSA_HEADS = 8
NSA_GROUPS = 2
NSA_REP = NSA_HEADS // NSA_GROUPS
L_CMP, D_CMP = 32, 16
CMP_HIDDEN = 128
L_SLC = 64
N_SEL = 16
WINDOW = 512
QB = 128
FORCE_BONUS = 1.0e3
SGU_WIDTH = 256
SGU_GROUPS = 4
SGU_CHUNK = 128
POOL_WIDTH = 256
POOL_WINDOWS = (2, 4, 8, 16)
XA_HEADS, XA_HD = 4, 128
XA_WIDTH = XA_HEADS * XA_HD
PEER_HEADS = 8
PEER_KEYS = 128
PEER_TOPK = 16
PEER_HALF = 64

LANES = 128
SEL_VR = 2 * DK
SEL_SUB = 256
SEL_QBLOCKS = 1
SEL_UNROLL = 16
WIN_QBLOCKS = 8
CMP_QBLOCKS = 8
CMP_CHUNK = 256
CMP_HALO = 8
PEER_TT = 512
PEER_CH = 2048
PEER_PROJ_ROWS = 128
VMEM_LIMIT = 56 * 1024 * 1024


def _cp(sem):
    return pltpu.CompilerParams(dimension_semantics=sem, vmem_limit_bytes=VMEM_LIMIT)


def _gelu(x):
    return 0.5 * x * (1.0 + jnp.tanh(0.7978845608028654 * (x + 0.044715 * (x * x * x))))


def _rms(x, g):
    return x * lax.rsqrt(jnp.mean(x * x, axis=-1, keepdims=True) + EPS) * g


def _dot(a, b):
    return jnp.dot(a, b, preferred_element_type=F32)


def _dot_nt(a, b):
    return lax.dot_general(a, b, (((1,), (1,)), ((), ())), preferred_element_type=F32)


def _iota(shape, dim):
    return lax.broadcasted_iota(jnp.int32, shape, dim)


def _div_pow2(x, n):
    assert n & (n - 1) == 0
    return lax.shift_right_logical(x, jnp.int32(n.bit_length() - 1))


def _mod_pow2(x, n):
    assert n & (n - 1) == 0
    return x & (n - 1)


def _norm_matmul_kernel(x_ref, g_ref, w_ref, *o_refs, splits):
    hb = _rms(x_ref[...], g_ref[...]).astype(BF16)
    for o_ref, (a, b) in zip(o_refs, splits):
        o_ref[...] = _dot(hb, w_ref[:, a:b]).astype(o_ref.dtype)


def norm_matmul(x, g, w, splits, tm, name):
    t, d = x.shape
    n = w.shape[1]
    return pl.pallas_call(
        functools.partial(_norm_matmul_kernel, splits=splits),
        grid=(t // tm,),
        in_specs=[pl.BlockSpec((tm, d), lambda i: (i, 0)),
                  pl.BlockSpec((1, d), lambda i: (0, 0)),
                  pl.BlockSpec((d, n), lambda i: (0, 0))],
        out_specs=[pl.BlockSpec((tm, b - a), lambda i: (i, 0)) for a, b in splits],
        out_shape=[jax.ShapeDtypeStruct((t, b - a), F32) for a, b in splits],
        compiler_params=_cp(("parallel",)),
        name=name,
    )(x, g.reshape(1, d), w)


def _nsa_prep_kernel(zq_ref, zkv_ref, qn_ref, kn_ref,
                     q_out, kvc_out, ks_out, vs_out, kw_out, vw_out, *, ts):
    zq = zq_ref[0]
    scale = DK ** -0.5 * LOG2E
    for h in range(NSA_HEADS):
        q_out[0, h] = (_rms(zq[:, h * DK:(h + 1) * DK], qn_ref[...]) * scale).astype(BF16)
    zkv = zkv_ref[0]

    def piece(i, g):
        o = i * NSA_GROUPS * DK + g * DK
        return zkv[:, o:o + DK]

    ones_col = jnp.where(_iota((ts, SEL_VR - DK), 1) == 0, 1.0, 0.0)
    for g in range(NSA_GROUPS):
        kvc_out[0, 0, g] = piece(0, g)
        kvc_out[1, 0, g] = piece(1, g)
        ks_out[0, g] = _rms(piece(2, g), kn_ref[1:2, :]).astype(BF16)
        vs_out[0, g] = jnp.concatenate([piece(3, g), ones_col], axis=-1).T.astype(BF16)
        kw_out[0, g] = _rms(piece(4, g), kn_ref[2:3, :]).astype(BF16)
        vw_out[0, g] = jnp.concatenate([piece(5, g), ones_col], axis=-1).T.astype(BF16)


def nsa_prep(zq, zkv, q_norm, k_norm, ts=512):
    b, s, _ = zq.shape
    g = NSA_GROUPS
    hm = lambda bi, si: (bi, 0, si, 0)
    return pl.pallas_call(
        functools.partial(_nsa_prep_kernel, ts=ts),
        grid=(b, s // ts),
        in_specs=[pl.BlockSpec((1, ts, NSA_HEADS * DK), lambda bi, si: (bi, si, 0)),
                  pl.BlockSpec((1, ts, 6 * g * DK), lambda bi, si: (bi, si, 0)),
                  pl.BlockSpec((1, DK), lambda bi, si: (0, 0)),
                  pl.BlockSpec((3, DK), lambda bi, si: (0, 0))],
        out_specs=[pl.BlockSpec((1, NSA_HEADS, ts, DK), hm),
                   pl.BlockSpec((2, 1, g, ts, DK), lambda bi, si: (0, bi, 0, si, 0)),
                   pl.BlockSpec((1, g, ts, DK), hm),
                   pl.BlockSpec((1, g, SEL_VR, ts), lambda bi, si: (bi, 0, 0, si)),
                   pl.BlockSpec((1, g, ts, DK), hm),
                   pl.BlockSpec((1, g, SEL_VR, ts), lambda bi, si: (bi, 0, 0, si))],
        out_shape=[jax.ShapeDtypeStruct((b, NSA_HEADS, s, DK), BF16),
                   jax.ShapeDtypeStruct((2, b, g, s, DK), F32),
                   jax.ShapeDtypeStruct((b, g, s, DK), BF16),
                   jax.ShapeDtypeStruct((b, g, SEL_VR, s), BF16),
                   jax.ShapeDtypeStruct((b, g, s, DK), BF16),
                   jax.ShapeDtypeStruct((b, g, SEL_VR, s), BF16)],
        compiler_params=_cp(("parallel", "parallel")),
        name="nsa_prep",
    )(zq, zkv, q_norm.reshape(1, DK), k_norm)


def _cmp_mlp_kernel(x_ref, pe_ref, w1_ref, w2_ref, kn_ref, k_out, vt_out, *, nc):
    c = pl.program_id(2)
    x = x_ref[0, 0, 0]
    a = _dot((x + pe_ref[0, 0:1, :]).astype(BF16), w1_ref[0, 0])
    b = _dot((x + pe_ref[0, 1:2, :]).astype(BF16), w1_ref[0, 1])
    pre = a + pltpu.roll(b, nc - 1, 0)
    comp = _dot(_gelu(pre).astype(BF16), w2_ref[0])

    @pl.when(c == 0)
    def _():
        k_out[0, 0] = _rms(comp, kn_ref[...]).astype(BF16)

    @pl.when(c == 1)
    def _():
        ones_col = jnp.where(_iota((nc, SEL_VR - DK), 1) == 0, 1.0, 0.0)
        vt_out[0, 0] = jnp.concatenate([comp, ones_col], axis=-1).T.astype(BF16)


def cmp_mlp(xc, pe, w1, w2, k_norm0):
    _, b, g, nc, width = xc.shape
    return pl.pallas_call(
        functools.partial(_cmp_mlp_kernel, nc=nc),
        grid=(b, g, 2),
        in_specs=[pl.BlockSpec((1, 1, 1, nc, width), lambda bi, gi, c: (c, bi, gi, 0, 0)),
                  pl.BlockSpec((1, 2, width), lambda bi, gi, c: (c, 0, 0)),
                  pl.BlockSpec((1, 2, width, CMP_HIDDEN), lambda bi, gi, c: (c, 0, 0, 0)),
                  pl.BlockSpec((1, CMP_HIDDEN, DK), lambda bi, gi, c: (c, 0, 0)),
                  pl.BlockSpec((1, DK), lambda bi, gi, c: (0, 0))],
        out_specs=[pl.BlockSpec((1, 1, nc, DK), lambda bi, gi, c: (bi, gi, 0, 0)),
                   pl.BlockSpec((1, 1, SEL_VR, nc), lambda bi, gi, c: (bi, gi, 0, 0))],
        out_shape=[jax.ShapeDtypeStruct((b, g, nc, DK), BF16),
                   jax.ShapeDtypeStruct((b, g, SEL_VR, nc), BF16)],
        compiler_params=_cp(("parallel", "parallel", "arbitrary")),
        name="cmp_mlp",
    )(xc, pe, w1, w2, k_norm0.reshape(1, DK))


def _cmp_attn_kernel(q_ref, k_ref, vt_ref, oc_ref, sb_ref, s_s, acc_s, ps_s, *, nc, n_slc):
    qspan = CMP_QBLOCKS * QB
    q0 = pl.program_id(2) * qspan
    rows = NSA_REP * qspan
    ck = CMP_CHUNK
    q = q_ref[0].reshape(rows, DK)
    n_chunk = ((q0 + qspan) // D_CMP + ck - 1) // ck
    t = q0 + _mod_pow2(_iota((ck, rows), 1), qspan)

    def pass_a(ci, m):
        off = pl.multiple_of(ci * ck, ck)
        st = _dot_nt(k_ref[0, 0, pl.ds(off, ck), :], q)
        n = off + _iota((ck, rows), 0)
        st = jnp.where(n * D_CMP + (L_CMP - 1) <= t, st, NEG)
        s_s[ci] = st
        return jnp.maximum(m, jnp.max(st, axis=0, keepdims=True))

    m = lax.fori_loop(0, n_chunk, pass_a, jnp.full((1, rows), NEG, F32))
    acc_s[...] = jnp.zeros_like(acc_s)
    ps_s[...] = jnp.zeros_like(ps_s)

    def pass_b(ci, den):
        off = pl.multiple_of(ci * ck, ck)
        e = jnp.exp2(s_s[ci] - m)
        s_s[ci] = e
        acc_s[...] += _dot(vt_ref[0, 0, :, pl.ds(off, ck)], e.astype(BF16))
        return den + jnp.sum(e, axis=0, keepdims=True)

    den = lax.fori_loop(0, n_chunk, pass_b, jnp.zeros((1, rows), F32))
    seen = m > 0.5 * NEG
    inv_o = jnp.where(seen, 1.0 / jnp.maximum(acc_s[DK:DK + 1, :], 1e-30), 0.0)
    inv_p = jnp.where(seen, 1.0 / jnp.maximum(den, 1e-30), 0.0)
    oc_ref[0] = (acc_s[...] * inv_o).T[:, 0:DK].reshape(NSA_REP, qspan, DK)

    def pass_c(ci, carry):
        off = pl.multiple_of(ci * ck, ck)
        p = s_s[ci] * inv_p
        ps = p[:, 0:qspan]
        for r in range(1, NSA_REP):
            ps = ps + p[:, r * qspan:(r + 1) * qspan]
        for qi in range(CMP_QBLOCKS):
            ps_s[qi, pl.ds(CMP_HALO + off, ck), :] = ps[:, qi * QB:(qi + 1) * QB]
        return carry

    lax.fori_loop(0, n_chunk, pass_c, 0)

    ratio = L_SLC // D_CMP
    imp = jnp.zeros((n_slc, qspan), F32)
    for k in range(-(L_CMP // D_CMP - 1), ratio):
        overlap = min(k * D_CMP + L_CMP, L_SLC) - max(k * D_CMP, 0)
        taps = [ps_s[qi, pl.ds(CMP_HALO + k, n_slc, stride=ratio), :] for qi in range(CMP_QBLOCKS)]
        imp = imp + (overlap / D_CMP) * jnp.concatenate(taps, axis=1)

    cur = _div_pow2(q0 + _iota((n_slc, qspan), 1), L_SLC)
    j = _iota((n_slc, qspan), 0)
    forced = (j == 0) | (j == cur) | (j == cur - 1)
    score = jnp.where(j <= cur, imp + jnp.where(forced, FORCE_BONUS, 0.0), NEG)
    work = score
    tau = None
    for _ in range(N_SEL):
        tau = jnp.max(work, axis=0, keepdims=True)
        work = jnp.where(work >= tau, REMOVED, work)
    selected = (score > 0.5 * NEG) & (score >= tau)
    bias = jnp.where(selected, 0.0, NEG)
    for qi in range(CMP_QBLOCKS):
        sb_ref[0, 0, qi] = bias[:, qi * QB:(qi + 1) * QB]


def cmp_attn_select(qn, kc, vct):
    b, h, s, _ = qn.shape
    g = NSA_GROUPS
    nc = kc.shape[2]
    n_slc = s // L_SLC
    nqb = s // QB
    qspan = CMP_QBLOCKS * QB
    rows = NSA_REP * qspan
    return pl.pallas_call(
        functools.partial(_cmp_attn_kernel, nc=nc, n_slc=n_slc),
        grid=(b, g, s // qspan),
        in_specs=[pl.BlockSpec((1, NSA_REP, qspan, DK), lambda bi, gi, qi: (bi, gi, qi, 0)),
                  pl.BlockSpec((1, 1, nc, DK), lambda bi, gi, qi: (bi, gi, 0, 0)),
                  pl.BlockSpec((1, 1, SEL_VR, nc), lambda bi, gi, qi: (bi, gi, 0, 0))],
        out_specs=[pl.BlockSpec((1, NSA_REP, qspan, DK), lambda bi, gi, qi: (bi, gi, qi, 0)),
                   pl.BlockSpec((1, 1, CMP_QBLOCKS, n_slc, QB), lambda bi, gi, qi: (bi, gi, qi, 0, 0))],
        out_shape=[jax.ShapeDtypeStruct((b, h, s, DK), F32),
                   jax.ShapeDtypeStruct((b, g, nqb, n_slc, QB), F32)],
        scratch_shapes=[pltpu.VMEM((nc // CMP_CHUNK, CMP_CHUNK, rows), F32),
                        pltpu.VMEM((SEL_VR, rows), F32),
                        pltpu.VMEM((CMP_QBLOCKS, CMP_HALO + nc, QB), F32)],
        compiler_params=_cp(("parallel", "parallel", "parallel")),
        name="cmp_attn_select",
    )(qn, kc, vct)


def _win_attn_kernel(q_ref, *refs):
    nkb = WINDOW // QB + 1
    nrefs = nkb + WIN_QBLOCKS - 1
    k_refs, v_refs, o_ref = refs[:nrefs], refs[nrefs:2 * nrefs], refs[2 * nrefs]
    rows = NSA_REP * QB
    nk = nkb * QB
    for hf in range(WIN_QBLOCKS):
        qb = pl.program_id(2) * WIN_QBLOCKS + hf
        qs = slice(hf * QB, (hf + 1) * QB)
        q = q_ref[0, :, qs, :].reshape(rows, DK)
        k = jnp.concatenate([r[0, 0] for r in k_refs[hf:hf + nkb]], axis=0)
        vt = jnp.concatenate([r[0, 0] for r in v_refs[hf:hf + nkb]], axis=-1)
        st = _dot_nt(k, q)
        t = qb * QB + _mod_pow2(_iota((nk, rows), 1), QB)
        kpos = (qb - (nkb - 1)) * QB + _iota((nk, rows), 0)
        delta = t - kpos
        valid = (delta >= 0) & (delta < WINDOW) & (kpos >= 0)
        st = jnp.where(valid, st, NEG)
        e = jnp.exp2(st - jnp.max(st, axis=0, keepdims=True)).astype(BF16)
        acc = _dot(vt, e)
        o = (acc / jnp.maximum(acc[DK:DK + 1, :], 1e-30)).T[:, 0:DK]
        o_ref[0, :, qs, :] = o.reshape(NSA_REP, QB, DK)


def win_attn(qn, kw, vw):
    b, h, s, _ = qn.shape
    g = NSA_GROUPS
    nkb = WINDOW // QB + 1
    nrefs = nkb + WIN_QBLOCKS - 1
    nqb = s // QB

    def first_block(qi, i):
        return jnp.maximum(qi * WIN_QBLOCKS - (nkb - 1) + i, 0)

    def k_spec(i):
        return pl.BlockSpec((1, 1, QB, DK), lambda bi, gi, qi: (bi, gi, first_block(qi, i), 0))

    def vt_spec(i):
        return pl.BlockSpec((1, 1, SEL_VR, QB), lambda bi, gi, qi: (bi, gi, 0, first_block(qi, i)))

    qspec = pl.BlockSpec((1, NSA_REP, WIN_QBLOCKS * QB, DK), lambda bi, gi, qi: (bi, gi, qi, 0))
    return pl.pallas_call(
        _win_attn_kernel,
        grid=(b, g, nqb // WIN_QBLOCKS),
        in_specs=[qspec] + [k_spec(i) for i in range(nrefs)] + [vt_spec(i) for i in range(nrefs)],
        out_specs=qspec,
        out_shape=jax.ShapeDtypeStruct((b, h, s, DK), F32),
        compiler_params=_cp(("parallel", "parallel", "parallel")),
        name="win_attn",
    )(qn, *([kw] * nrefs), *([vw] * nrefs))


def _sel_attn_kernel(q_ref, sb_ref, k_ref, vt_ref, oc_ref, ow_ref, gl_ref, o_ref,
                     m_s, acc_s, s0_s, s1_s, p0_s, p1_s, a0_s, a1_s):
    qspan = SEL_QBLOCKS * QB
    q0 = pl.program_id(2) * qspan
    rows = NSA_REP * qspan
    blocks = SEL_SUB // L_SLC
    qv = q_ref[0].reshape(rows, DK)
    nsub = (q0 + qspan + SEL_SUB - 1) // SEL_SUB
    m_s[...] = jnp.full_like(m_s, NEG)
    acc_s[...] = jnp.zeros_like(acc_s)

    def scores(i, s_ref):
        off = pl.multiple_of(i * SEL_SUB, SEL_SUB)
        s_ref[...] = _dot_nt(k_ref[0, 0, pl.ds(off, SEL_SUB), :], qv)

    def weights(i, s_ref, p_ref, a_ref, causal):
        first = pl.multiple_of(i * blocks, blocks)
        bias = jnp.concatenate([sb_ref[0, 0, qi, pl.ds(first, blocks), :] for qi in range(SEL_QBLOCKS)],
                               axis=1)
        bias = jnp.broadcast_to(bias[:, None, :], (blocks, L_SLC, qspan)).reshape(SEL_SUB, qspan)
        st = s_ref[...] + jnp.concatenate([bias] * NSA_REP, axis=1)
        if causal:
            kpos = i * SEL_SUB + _iota((SEL_SUB, rows), 0)
            t = q0 + _mod_pow2(_iota((SEL_SUB, rows), 1), qspan)
            st = jnp.where(kpos <= t, st, NEG)
        m_old = m_s[...]
        m_new = jnp.maximum(m_old, jnp.max(st, axis=0, keepdims=True))
        p_ref[...] = jnp.exp2(st - m_new).astype(BF16)
        a_ref[...] = jnp.exp2(m_old - m_new)
        m_s[...] = m_new

    def accumulate(i, p_ref, a_ref):
        off = pl.multiple_of(i * SEL_SUB, SEL_SUB)
        acc_s[...] = a_ref[...] * acc_s[...] + _dot(vt_ref[0, 0, :, pl.ds(off, SEL_SUB)], p_ref[...])

    p1_s[...] = jnp.zeros_like(p1_s)
    a1_s[...] = jnp.ones_like(a1_s)
    scores(0, s0_s)

    def two_tiles(i):
        scores(i + 1, s1_s)
        weights(i, s0_s, p0_s, a0_s, False)
        accumulate(jnp.maximum(i - 1, 0), p1_s, a1_s)
        scores(i + 2, s0_s)
        weights(i + 1, s1_s, p1_s, a1_s, False)
        accumulate(i, p0_s, a0_s)

    e = 0
    left = nsub - 1
    group = SEL_UNROLL
    while group >= 2:
        n_group = left // group

        def body(j, carry, group=group, base=e):
            for k in range(0, group, 2):
                two_tiles(base + group * j + k)
            return carry

        lax.fori_loop(0, n_group, body, 0)
        e = e + group * n_group
        left = left - group * n_group
        group //= 2
    odd_tail = left

    @pl.when(odd_tail == 0)
    def _():
        weights(e, s0_s, p0_s, a0_s, True)
        accumulate(jnp.maximum(e - 1, 0), p1_s, a1_s)
        accumulate(e, p0_s, a0_s)

    @pl.when(odd_tail == 1)
    def _():
        scores(e + 1, s1_s)
        weights(e, s0_s, p0_s, a0_s, False)
        accumulate(jnp.maximum(e - 1, 0), p1_s, a1_s)
        weights(e + 1, s1_s, p1_s, a1_s, True)
        accumulate(e, p0_s, a0_s)
        accumulate(e + 1, p1_s, a1_s)

    o_s = (acc_s[...] / jnp.maximum(acc_s[DK:DK + 1, :], 1e-30)).T[:, 0:DK]
    gate = jax.nn.sigmoid(gl_ref[0].reshape(rows, 3))
    out = (gate[:, 0:1] * oc_ref[0].reshape(rows, DK) + gate[:, 1:2] * o_s
           + gate[:, 2:3] * ow_ref[0].reshape(rows, DK))
    o_ref[0] = out.reshape(NSA_REP, qspan, DK)


def sel_attn(qn, selb, ks, vs, oc, ow, gl):
    b, h, s, _ = qn.shape
    g = NSA_GROUPS
    qspan = SEL_QBLOCKS * QB
    rows = NSA_REP * qspan
    n_slc = s // L_SLC
    qmap = lambda bi, gi, qi: (bi, gi, qi, 0)
    return pl.pallas_call(
        _sel_attn_kernel,
        grid=(b, g, s // qspan),
        in_specs=[pl.BlockSpec((1, NSA_REP, qspan, DK), qmap),
                  pl.BlockSpec((1, 1, SEL_QBLOCKS, n_slc, QB), lambda bi, gi, qi: (bi, gi, qi, 0, 0)),
                  pl.BlockSpec((1, 1, s, DK), lambda bi, gi, qi: (bi, gi, 0, 0)),
                  pl.BlockSpec((1, 1, SEL_VR, s), lambda bi, gi, qi: (bi, gi, 0, 0)),
                  pl.BlockSpec((1, NSA_REP, qspan, DK), qmap),
                  pl.BlockSpec((1, NSA_REP, qspan, DK), qmap),
                  pl.BlockSpec((1, NSA_REP, qspan, 3), qmap)],
        out_specs=pl.BlockSpec((1, NSA_REP, qspan, DK), qmap),
        out_shape=jax.ShapeDtypeStruct((b, h, s, DK), F32),
        scratch_shapes=[pltpu.VMEM((1, rows), F32),
                        pltpu.VMEM((SEL_VR, rows), F32),
                        pltpu.VMEM((SEL_SUB, rows), F32),
                        pltpu.VMEM((SEL_SUB, rows), F32),
                        pltpu.VMEM((SEL_SUB, rows), BF16),
                        pltpu.VMEM((SEL_SUB, rows), BF16),
                        pltpu.VMEM((1, rows), F32),
                        pltpu.VMEM((1, rows), F32)],
        compiler_params=_cp(("parallel", "parallel", "parallel")),
        name="sel_attn",
    )(qn, selb, ks, vs, oc, ow, gl)


def _sgu_kernel(z_ref, g_ref, b_ref, w_ref, bias_ref, o_ref, *, tm):
    z = _gelu(z_ref[...])
    u = z[:, :SGU_WIDTH]
    v = z[:, SGU_WIDTH:]
    mu = jnp.mean(v, axis=-1, keepdims=True)
    var = jnp.mean(jnp.square(v - mu), axis=-1, keepdims=True)
    vn = ((v - mu) * lax.rsqrt(var + EPS) * g_ref[...] + b_ref[...]).astype(BF16)
    tril = _iota((SGU_CHUNK, SGU_CHUNK), 1) <= _iota((SGU_CHUNK, SGU_CHUNK), 0)
    wm = [jnp.where(tril, w_ref[gi], 0.0).astype(BF16) for gi in range(SGU_GROUPS)]
    grp = _div_pow2(_iota((SGU_CHUNK, SGU_WIDTH), 1), SGU_WIDTH // SGU_GROUPS)
    for ch in range(tm // SGU_CHUNK):
        rs = slice(ch * SGU_CHUNK, (ch + 1) * SGU_CHUNK)
        mixed = bias_ref[...]
        for gi in range(SGU_GROUPS):
            mixed = mixed + jnp.where(grp == gi, _dot(wm[gi], vn[rs]), 0.0)
        o_ref[rs, :] = u[rs] * mixed


def sgu(z, ln_g, ln_b, w, bias_full, tm=512):
    t = z.shape[0]
    return pl.pallas_call(
        functools.partial(_sgu_kernel, tm=tm),
        grid=(t // tm,),
        in_specs=[pl.BlockSpec((tm, 2 * SGU_WIDTH), lambda i: (i, 0)),
                  pl.BlockSpec((1, SGU_WIDTH), lambda i: (0, 0)),
                  pl.BlockSpec((1, SGU_WIDTH), lambda i: (0, 0)),
                  pl.BlockSpec((SGU_GROUPS, SGU_CHUNK, SGU_CHUNK), lambda i: (0, 0, 0)),
                  pl.BlockSpec((SGU_CHUNK, SGU_WIDTH), lambda i: (0, 0))],
        out_specs=pl.BlockSpec((tm, SGU_WIDTH), lambda i: (i, 0)),
        out_shape=jax.ShapeDtypeStruct((t, SGU_WIDTH), F32),
        compiler_params=_cp(("parallel",)),
        name="sgu",
    )(z, ln_g.reshape(1, -1), ln_b.reshape(1, -1), w, bias_full)


def _pool_kernel(p_ref, h_ref, w_ref, sc_ref, o_ref, *, tp):
    i = pl.program_id(1)
    halo_rows = POOL_WINDOWS[-1]
    p = p_ref[0]
    halo = jnp.where(i > 0, h_ref[0], 0.0)
    ext = jnp.concatenate([halo, p], axis=0)
    sums = [ext]
    shift = 1
    for _ in POOL_WINDOWS:
        prev = sums[-1]
        sums.append(prev + pltpu.roll(prev, shift, 0))
        shift *= 2
    pos = i * tp + _iota((tp, POOL_WIDTH), 0)
    grp = _div_pow2(_iota((tp, POOL_WIDTH), 1), POOL_WIDTH // len(POOL_WINDOWS))
    d = jnp.zeros((tp, POOL_WIDTH), F32)
    for gi, w in enumerate(POOL_WINDOWS):
        mean = sums[gi + 1][halo_rows:halo_rows + tp] / jnp.minimum(pos + 1, w).astype(F32)
        d = jnp.where(grp == gi, mean, d)
    d = d - p
    o_ref[0] = _dot(d.astype(BF16), w_ref[...]) * sc_ref[...]


def pool(p, w_bd, scale, tp=512):
    b, s, c = p.shape
    halo_rows = POOL_WINDOWS[-1]
    return pl.pallas_call(
        functools.partial(_pool_kernel, tp=tp),
        grid=(b, s // tp),
        in_specs=[pl.BlockSpec((1, tp, c), lambda bi, i: (bi, i, 0)),
                  pl.BlockSpec((1, halo_rows, c),
                               lambda bi, i: (bi, jnp.maximum(i * (tp // halo_rows) - 1, 0), 0)),
                  pl.BlockSpec((c, c), lambda bi, i: (0, 0)),
                  pl.BlockSpec((1, c), lambda bi, i: (0, 0))],
        out_specs=pl.BlockSpec((1, tp, c), lambda bi, i: (bi, i, 0)),
        out_shape=jax.ShapeDtypeStruct((b, s, c), F32),
        compiler_params=_cp(("parallel", "parallel")),
        name="pool",
    )(p, p, w_bd, scale.reshape(1, c))


def _merge_kernel(x_ref, oa_ref, ob_ref, oc_ref, zm_ref, la_ref, lb_ref, lc_ref, wo_ref, o_ref):
    d = D_MODEL
    oa = jnp.concatenate([oa_ref[0, h].astype(BF16) for h in range(NSA_HEADS)], axis=-1)
    zm = zm_ref[0]
    merged = (jax.nn.sigmoid(zm[:, 0:d]) * _dot(oa, la_ref[...])
              + jax.nn.sigmoid(zm[:, d:2 * d]) * _dot(ob_ref[0].astype(BF16), lb_ref[...])
              + jax.nn.sigmoid(zm[:, 2 * d:3 * d]) * _dot(oc_ref[0].astype(BF16), lc_ref[...]))
    o_ref[0] = x_ref[0] + _dot(merged.astype(BF16), wo_ref[...])


def merge(x, oa, ob, oc, zm, la, lb, lc, wo, tm=512):
    b, s, d = x.shape
    row = lambda bi, i: (bi, i, 0)
    full = lambda bi, i: (0, 0)
    return pl.pallas_call(
        _merge_kernel,
        grid=(b, s // tm),
        in_specs=[pl.BlockSpec((1, tm, d), row),
                  pl.BlockSpec((1, NSA_HEADS, tm, DK), lambda bi, i: (bi, 0, i, 0)),
                  pl.BlockSpec((1, tm, SGU_WIDTH), row),
                  pl.BlockSpec((1, tm, POOL_WIDTH), row),
                  pl.BlockSpec((1, tm, 3 * d), row),
                  pl.BlockSpec(la.shape, full), pl.BlockSpec(lb.shape, full),
                  pl.BlockSpec(lc.shape, full), pl.BlockSpec(wo.shape, full)],
        out_specs=pl.BlockSpec((1, tm, d), row),
        out_shape=jax.ShapeDtypeStruct((b, s, d), F32),
        compiler_params=_cp(("parallel", "parallel")),
        name="merge",
    )(x, oa, ob, oc, zm, la, lb, lc, wo)


def _mem_kv_kernel(m_ref, g_ref, wk_ref, wv_ref, kn_ref, k_out, v_out):
    mh = _rms(m_ref[0], g_ref[...]).astype(BF16)
    k = _dot(mh, wk_ref[...])
    for h in range(XA_HEADS):
        hs = slice(h * XA_HD, (h + 1) * XA_HD)
        k_out[0, :, hs] = _rms(k[:, hs], kn_ref[...]).astype(BF16)
    v_out[0] = _dot(mh, wv_ref[...]).astype(BF16)


def mem_kv(mem, g, wk, wv, k_norm):
    b, m, d = mem.shape
    full = lambda bi: (0, 0)
    return pl.pallas_call(
        _mem_kv_kernel,
        grid=(b,),
        in_specs=[pl.BlockSpec((1, m, d), lambda bi: (bi, 0, 0)),
                  pl.BlockSpec((1, d), full),
                  pl.BlockSpec(wk.shape, full), pl.BlockSpec(wv.shape, full),
                  pl.BlockSpec((1, XA_HD), full)],
        out_specs=[pl.BlockSpec((1, m, XA_WIDTH), lambda bi: (bi, 0, 0))] * 2,
        out_shape=[jax.ShapeDtypeStruct((b, m, XA_WIDTH), BF16)] * 2,
        compiler_params=_cp(("parallel",)),
        name="mem_kv",
    )(mem, g.reshape(1, d), wk, wv, k_norm.reshape(1, XA_HD))


def _xattn_kernel(x_ref, g_ref, wq_ref, qn_ref, k_ref, v_ref, wo_ref, o_ref):
    x = x_ref[0]
    q = _dot(_rms(x, g_ref[...]).astype(BF16), wq_ref[...])
    k = k_ref[0]
    v = v_ref[0]
    outs = []
    for h in range(XA_HEADS):
        hs = slice(h * XA_HD, (h + 1) * XA_HD)
        qh = (_rms(q[:, hs], qn_ref[...])).astype(BF16)
        s = _dot_nt(qh, k[:, hs]) * (XA_HD ** -0.5)
        e = jnp.exp(s - jnp.max(s, axis=-1, keepdims=True))
        p = e / jnp.sum(e, axis=-1, keepdims=True)
        outs.append(_dot(p.astype(BF16), v[:, hs]).astype(BF16))
    o = jnp.concatenate(outs, axis=-1)
    o_ref[0] = x + _dot(o, wo_ref[...])


def xattn(x, g, wq, q_norm, k, v, wo, tm=512):
    b, s, d = x.shape
    m = k.shape[1]
    row = lambda bi, i: (bi, i, 0)
    full = lambda bi, i: (0, 0)
    return pl.pallas_call(
        _xattn_kernel,
        grid=(b, s // tm),
        in_specs=[pl.BlockSpec((1, tm, d), row),
                  pl.BlockSpec((1, d), full),
                  pl.BlockSpec(wq.shape, full),
                  pl.BlockSpec((1, XA_HD), full),
                  pl.BlockSpec((1, m, XA_WIDTH), lambda bi, i: (bi, 0, 0)),
                  pl.BlockSpec((1, m, XA_WIDTH), lambda bi, i: (bi, 0, 0)),
                  pl.BlockSpec(wo.shape, full)],
        out_specs=pl.BlockSpec((1, tm, d), row),
        out_shape=jax.ShapeDtypeStruct((b, s, d), F32),
        compiler_params=_cp(("parallel", "parallel")),
        name="xattn",
    )(x, g.reshape(1, d), wq, q_norm.reshape(1, XA_HD), k, v, wo)


def _top_rows(s, n, want_rank):
    out = []
    work = s
    rank = jnp.full(s.shape, float(s.shape[0] - 1), F32) if want_rank else None
    for i in range(n):
        m = jnp.max(work, axis=0, keepdims=True)
        out.append(m)
        hit = work >= m
        if want_rank:
            rank = jnp.where(hit, float(i), rank)
        work = jnp.where(hit, REMOVED, work)
    return out, rank


def _stack_rows(rows, pad_rows):
    tt = rows[0].shape[1]
    rowi = _iota((pad_rows, tt), 0)
    out = jnp.full((pad_rows, tt), REMOVED, F32)
    for i, r in enumerate(rows[:pad_rows]):
        out = jnp.where(rowi == i, r, out)
    return out


def _pair_sum_candidates(v1, v2, tt):
    row_full = _iota((24, tt), 0)
    row = _iota((8, tt), 0)
    v2_full = _stack_rows(v2, 24)
    v2_8 = _stack_rows(v2, 8)

    def shifted(k):
        return pltpu.roll(v2_8, k, 0)

    def pick(rows_from, first):
        out = v1[rows_from]
        for r in range(first + 1, 8):
            if rows_from + r - first < len(v1):
                out = jnp.where(row == r, v1[rows_from + r - first], out)
        return out

    return jnp.concatenate([
        jnp.where(row_full < 17, v1[0] + v2_full, REMOVED),
        v1[1] + v2_8,
        jnp.where(row < 5, v1[2] + v2_8,
                  jnp.where(row < 7, v1[5] + shifted(5), v1[8] + shifted(7))),
        jnp.where(row < 4, v1[3] + v2_8,
                  jnp.where(row < 7, v1[4] + shifted(4), v1[9] + shifted(7))),
        jnp.where(row < 2, v1[6] + v2_8,
                  jnp.where(row < 4, v1[7] + shifted(2), pick(10, 4) + v2[0])),
        jnp.where(row < 3, pick(14, 0) + v2[0], REMOVED),
    ], axis=0)


def _peer_kernel(x_ref, g_ref, wq_ref, k1_ref, k2_ref, u_ref, vt_ref, o_ref,
                 hbt_s, q_s, r2_s, na_s, e1_s, e2_s, acc_s, *, tt, ch):
    c = pl.program_id(1)
    ntop = PEER_TOPK + 1
    assert ntop == 17

    @pl.when(c == 0)
    def _():
        h = _rms(x_ref[...], g_ref[...])
        hbt_s[...] = h.T.astype(BF16)
        q = _dot(h.astype(BF16), wq_ref[...])
        for hd in range(PEER_HEADS):
            q_s[hd] = q[:, hd * 2 * PEER_HALF:(hd + 1) * 2 * PEER_HALF].astype(BF16)
        acc_s[...] = jnp.zeros_like(acc_s)

        def route(hd, carry):
            qh = q_s[hd]
            s1 = _dot_nt(k1_ref[...], qh)
            s2 = _dot_nt(k2_ref[...], qh)
            v1, _ = _top_rows(s1, ntop, False)
            v2, rank2 = _top_rows(s2, ntop, True)
            cand = _pair_sum_candidates(v1, v2, tt)
            top = v1[0] + v2[0]
            z = jnp.zeros((1, tt), F32)
            c_prev = top
            c_last = top
            work = cand
            for r in range(ntop):
                m = jnp.max(work, axis=0, keepdims=True)
                work = jnp.where(work >= m, REMOVED, work)
                if r < PEER_TOPK:
                    z = z + jnp.exp(m - top)
                c_prev, c_last = c_last, m
            thr = 0.5 * (c_prev + c_last)
            na = jnp.zeros((PEER_KEYS, tt), F32)
            for j in range(ntop):
                na = jnp.where(s1 + v2[j] >= thr, float(j + 1), na)
            r2_s[hd] = rank2.astype(BF16)
            na_s[hd] = na
            e1_s[hd] = jnp.exp(s1 - v1[0])
            e2_s[hd] = (jnp.exp(s2 - v2[0]) / z).astype(BF16)
            return carry

        lax.fori_loop(0, PEER_HEADS, route, 0, unroll=4)

    n_slab = ch // PEER_KEYS
    slabs_per_group = PEER_PROJ_ROWS // PEER_KEYS
    n_group = ch // PEER_PROJ_ROWS

    def project(gi):
        rs = slice(gi * PEER_PROJ_ROWS, (gi + 1) * PEER_PROJ_ROWS)
        return _gelu(_dot(u_ref[rs, :], hbt_s[...]).astype(BF16))

    def gate(al, gl_rows):
        a = c * n_slab + al
        wc = jnp.zeros((PEER_KEYS, tt), BF16)
        for hd in range(PEER_HEADS):
            na = na_s[hd, pl.ds(a, 1), :].astype(BF16)
            e1 = e1_s[hd, pl.ds(a, 1), :].astype(BF16)
            wc = wc + jnp.where(r2_s[hd] < na, e2_s[hd] * e1, jnp.zeros((), BF16))
        return wc * gl_rows

    ys = []
    gl_next = project(0)
    for gi in range(n_group):
        gl = gl_next
        if gi + 1 < n_group:
            gl_next = project(gi + 1)
        for k in range(slabs_per_group):
            ys.append(gate(gi * slabs_per_group + k, gl[k * PEER_KEYS:(k + 1) * PEER_KEYS]))
    y = jnp.concatenate(ys, axis=0)
    acc_s[...] += _dot(vt_ref[0], y)

    @pl.when(c == pl.num_programs(1) - 1)
    def _():
        o_ref[...] = x_ref[...] + acc_s[...].T


def peer(x, g, wq, k1p, k2p, u, vt, tt=PEER_TT, ch=PEER_CH):
    t, d = x.shape
    ne = u.shape[0]
    return pl.pallas_call(
        functools.partial(_peer_kernel, tt=tt, ch=ch),
        grid=(t // tt, ne // ch),
        in_specs=[pl.BlockSpec((tt, d), lambda i, c: (i, 0)),
                  pl.BlockSpec((1, d), lambda i, c: (0, 0)),
                  pl.BlockSpec(wq.shape, lambda i, c: (0, 0)),
                  pl.BlockSpec(k1p.shape, lambda i, c: (0, 0)),
                  pl.BlockSpec(k2p.shape, lambda i, c: (0, 0)),
                  pl.BlockSpec((ch, d), lambda i, c: (c, 0)),
                  pl.BlockSpec((1, d, ch), lambda i, c: (c, 0, 0))],
        out_specs=pl.BlockSpec((tt, d), lambda i, c: (i, 0)),
        out_shape=jax.ShapeDtypeStruct((t, d), F32),
        scratch_shapes=[pltpu.VMEM((d, tt), BF16),
                        pltpu.VMEM((PEER_HEADS, tt, 2 * PEER_HALF), BF16),
                        pltpu.VMEM((PEER_HEADS, PEER_KEYS, tt), BF16),
                        pltpu.VMEM((PEER_HEADS, PEER_KEYS, tt), F32),
                        pltpu.VMEM((PEER_HEADS, PEER_KEYS, tt), F32),
                        pltpu.VMEM((PEER_HEADS, PEER_KEYS, tt), BF16),
                        pltpu.VMEM((d, tt), F32)],
        compiler_params=_cp(("parallel", "arbitrary")),
        name="peer",
    )(x, g.reshape(1, d), wq, k1p, k2p, u, vt)


_IN_WIDTHS = (NSA_HEADS * DK, 6 * NSA_GROUPS * DK, 3 * NSA_HEADS, 2 * SGU_WIDTH, POOL_WIDTH, 3 * D_MODEL)


def _in_proj_layout():
    splits, src = [], []
    o_src = o_dst = 0
    for w in _IN_WIDTHS:
        wp = -(-w // LANES) * LANES
        splits.append((o_dst, o_dst + wp))
        src.append((o_src, o_src + w))
        o_src += w
        o_dst += wp
    return splits, src, o_dst


def _chunked_transpose(v):
    ne, d = v.shape
    return v.astype(BF16).reshape(ne // PEER_CH, PEER_CH, d).transpose(0, 2, 1)


def nsa_layer(zq, zkv, zg, q_norm, k_norm, cmp_pe, cmp_w1, cmp_w2, b, s):
    qn, kvc, ks, vs, kw, vw = nsa_prep(zq.reshape(b, s, -1), zkv.reshape(b, s, -1), q_norm, k_norm)
    nc = s // D_CMP
    xc = kvc.reshape(2, b, NSA_GROUPS, nc, D_CMP * DK)
    pe = cmp_pe.reshape(2, 2, D_CMP * DK)
    w1 = cmp_w1.reshape(2, 2, D_CMP * DK, CMP_HIDDEN).astype(BF16)
    kc, vct = cmp_mlp(xc, pe, w1, cmp_w2.astype(BF16), k_norm[0])
    oc, selb = cmp_attn_select(qn, kc, vct)
    ow = win_attn(qn, kw, vw)
    gl = zg[:, :3 * NSA_HEADS].reshape(b, s, NSA_HEADS, 3).transpose(0, 2, 1, 3)
    return sel_attn(qn, selb, ks, vs, oc, ow, gl)


def kernel(x, mem, mix_norm, w_in, nsa_q_norm, nsa_k_norm, cmp_pe, cmp_w1, cmp_w2, sgu_ln_g, sgu_ln_b, sgu_w, sgu_b, pool_w, pool_scale, lift_a, lift_b, lift_c, w_out, xa_norm, mem_norm, xa_wq, xa_wk, xa_wv, xa_q_norm, xa_k_norm, xa_wo, ffn_norm, peer_wq, peer_keys1, peer_keys2, peer_u, peer_v):
    b, s, d = x.shape
    t = b * s
    depth = w_in.shape[0]
    splits, src, n_pad = _in_proj_layout()
    zeros_half = jnp.zeros((PEER_KEYS, PEER_HALF), BF16)
    for l in range(depth):
        w_parts = []
        for (a0, a1), (d0, d1) in zip(src, splits):
            w_parts.append(jnp.pad(w_in[l][:, a0:a1], ((0, 0), (0, (d1 - d0) - (a1 - a0)))))
        w_pad = jnp.concatenate(w_parts, axis=1).astype(BF16)
        zq, zkv, zg, zs, zp, zm = norm_matmul(x.reshape(t, d), mix_norm[l], w_pad, splits, 256, "in_proj")

        oa = nsa_layer(zq, zkv, zg, nsa_q_norm[l], nsa_k_norm[l], cmp_pe[l], cmp_w1[l], cmp_w2[l], b, s)
        bias_full = jnp.repeat(sgu_b[l].T, SGU_WIDTH // SGU_GROUPS, axis=1)
        ob = sgu(zs, sgu_ln_g[l], sgu_ln_b[l], sgu_w[l], bias_full)
        cg = POOL_WIDTH // len(POOL_WINDOWS)
        w_bd = jnp.zeros((POOL_WIDTH, POOL_WIDTH), F32)
        for gi in range(len(POOL_WINDOWS)):
            w_bd = w_bd.at[gi * cg:(gi + 1) * cg, gi * cg:(gi + 1) * cg].set(pool_w[l, gi])
        oc = pool(zp.reshape(b, s, POOL_WIDTH), w_bd.astype(BF16), pool_scale[l])
        x = merge(x, oa, ob.reshape(b, s, SGU_WIDTH), oc, zm.reshape(b, s, 3 * d),
                  lift_a[l].astype(BF16), lift_b[l].astype(BF16), lift_c[l].astype(BF16),
                  w_out[l].astype(BF16))

        mk, mv = mem_kv(mem, mem_norm[l], xa_wk[l].astype(BF16), xa_wv[l].astype(BF16), xa_k_norm[l])
        x = xattn(x, xa_norm[l], xa_wq[l].astype(BF16), xa_q_norm[l], mk, mv, xa_wo[l].astype(BF16))

        k1p = jnp.concatenate([peer_keys1[l].astype(BF16), zeros_half], axis=1)
        k2p = jnp.concatenate([zeros_half, peer_keys2[l].astype(BF16)], axis=1)
        x = peer(x.reshape(t, d), ffn_norm[l], peer_wq[l].astype(BF16), k1p, k2p,
                 peer_u[l].astype(BF16), _chunked_transpose(peer_v[l])).reshape(b, s, d)
    return x
```

```python
import functools

import jax
import jax.numpy as jnp
from jax import lax
from jax.experimental import pallas as pl
from jax.experimental.pallas import tpu as pltpu

F32 = jnp.float32
BF16 = jnp.bfloat16

EPS = 1e-6
LOG2E = 1.4426950408889634
NEG = -1e30
REMOVED = -3e38

D_MODEL = 1024
DK = 64
NSA_HEADS = 8
NSA_GROUPS = 2
NSA_REP = NSA_HEADS // NSA_GROUPS
L_CMP, D_CMP = 32, 16
CMP_HIDDEN = 128
L_SLC = 64
N_SEL = 16
WINDOW = 512
QB = 128
FORCE_BONUS = 1.0e3
SGU_WIDTH = 256
SGU_GROUPS = 4
SGU_CHUNK = 128
POOL_WIDTH = 256
POOL_WINDOWS = (2, 4, 8, 16)
XA_HEADS, XA_HD = 4, 128
XA_WIDTH = XA_HEADS * XA_HD
PEER_HEADS = 8
PEER_KEYS = 128
PEER_TOPK = 16
PEER_HALF = 64

LANES = 128
SEL_VR = 2 * DK
SEL_SUB = 256
SEL_QBLOCKS = 1
SEL_UNROLL = 16
WIN_QBLOCKS = 8
CMP_QBLOCKS = 8
CMP_CHUNK = 256
CMP_HALO = 8
PEER_TT = 512
PEER_CH = 2048
PEER_PROJ_ROWS = 128
VMEM_LIMIT = 56 * 1024 * 1024


def _cp(sem):
    return pltpu.CompilerParams(dimension_semantics=sem, vmem_limit_bytes=VMEM_LIMIT)


def _gelu(x):
    return 0.5 * x * (1.0 + jnp.tanh(0.7978845608028654 * (x + 0.044715 * (x * x * x))))


def _rms(x, g):
    return x * lax.rsqrt(jnp.mean(x * x, axis=-1, keepdims=True) + EPS) * g


def _dot(a, b):
    return jnp.dot(a, b, preferred_element_type=F32)


def _dot_nt(a, b):
    return lax.dot_general(a, b, (((1,), (1,)), ((), ())), preferred_element_type=F32)


def _iota(shape, dim):
    return lax.broadcasted_iota(jnp.int32, shape, dim)


def _div_pow2(x, n):
    assert n & (n - 1) == 0
    return lax.shift_right_logical(x, jnp.int32(n.bit_length() - 1))


def _mod_pow2(x, n):
    assert n & (n - 1) == 0
    return x & (n - 1)


def _norm_matmul_kernel(x_ref, g_ref, w_ref, *o_refs, splits):
    hb = _rms(x_ref[...], g_ref[...]).astype(BF16)
    for o_ref, (a, b) in zip(o_refs, splits):
        o_ref[...] = _dot(hb, w_ref[:, a:b]).astype(o_ref.dtype)


def norm_matmul(x, g, w, splits, out_dtypes, tm, name):
    t, d = x.shape
    n = w.shape[1]
    return pl.pallas_call(
        functools.partial(_norm_matmul_kernel, splits=splits),
        grid=(t // tm,),
        in_specs=[pl.BlockSpec((tm, d), lambda i: (i, 0)),
                  pl.BlockSpec((1, d), lambda i: (0, 0)),
                  pl.BlockSpec((d, n), lambda i: (0, 0))],
        out_specs=[pl.BlockSpec((tm, b - a), lambda i: (i, 0)) for a, b in splits],
        out_shape=[jax.ShapeDtypeStruct((t, b - a), dt) for (a, b), dt in zip(splits, out_dtypes)],
        compiler_params=_cp(("parallel",)),
        name=name,
    )(x, g.reshape(1, d), w)


def _nsa_prep_kernel(zq_ref, zkv_ref, qn_ref, kn_ref,
                     q_out, kvc_out, ks_out, vs_out, kw_out, vw_out, *, ts):
    zq = zq_ref[0]
    scale = DK ** -0.5 * LOG2E
    for h in range(NSA_HEADS):
        q_out[0, h] = (_rms(zq[:, h * DK:(h + 1) * DK], qn_ref[...]) * scale).astype(BF16)
    zkv = zkv_ref[0]

    def piece(i, g):
        o = i * NSA_GROUPS * DK + g * DK
        return zkv[:, o:o + DK]

    ones_col = jnp.where(_iota((ts, SEL_VR - DK), 1) == 0, 1.0, 0.0)
    for g in range(NSA_GROUPS):
        kvc_out[0, 0, g] = piece(0, g)
        kvc_out[1, 0, g] = piece(1, g)
        ks_out[0, g] = _rms(piece(2, g), kn_ref[1:2, :]).astype(BF16)
        vs_out[0, g] = jnp.concatenate([piece(3, g), ones_col], axis=-1).T.astype(BF16)
        kw_out[0, g] = _rms(piece(4, g), kn_ref[2:3, :]).astype(BF16)
        vw_out[0, g] = jnp.concatenate([piece(5, g), ones_col], axis=-1).T.astype(BF16)


def nsa_prep(zq, zkv, q_norm, k_norm, ts=512):
    b, s, _ = zq.shape
    g = NSA_GROUPS
    hm = lambda bi, si: (bi, 0, si, 0)
    return pl.pallas_call(
        functools.partial(_nsa_prep_kernel, ts=ts),
        grid=(b, s // ts),
        in_specs=[pl.BlockSpec((1, ts, NSA_HEADS * DK), lambda bi, si: (bi, si, 0)),
                  pl.BlockSpec((1, ts, 6 * g * DK), lambda bi, si: (bi, si, 0)),
                  pl.BlockSpec((1, DK), lambda bi, si: (0, 0)),
                  pl.BlockSpec((3, DK), lambda bi, si: (0, 0))],
        out_specs=[pl.BlockSpec((1, NSA_HEADS, ts, DK), hm),
                   pl.BlockSpec((2, 1, g, ts, DK), lambda bi, si: (0, bi, 0, si, 0)),
                   pl.BlockSpec((1, g, ts, DK), hm),
                   pl.BlockSpec((1, g, SEL_VR, ts), lambda bi, si: (bi, 0, 0, si)),
                   pl.BlockSpec((1, g, ts, DK), hm),
                   pl.BlockSpec((1, g, SEL_VR, ts), lambda bi, si: (bi, 0, 0, si))],
        out_shape=[jax.ShapeDtypeStruct((b, NSA_HEADS, s, DK), BF16),
                   jax.ShapeDtypeStruct((2, b, g, s, DK), F32),
                   jax.ShapeDtypeStruct((b, g, s, DK), BF16),
                   jax.ShapeDtypeStruct((b, g, SEL_VR, s), BF16),
                   jax.ShapeDtypeStruct((b, g, s, DK), BF16),
                   jax.ShapeDtypeStruct((b, g, SEL_VR, s), BF16)],
        compiler_params=_cp(("parallel", "parallel")),
        name="nsa_prep",
    )(zq, zkv, q_norm.reshape(1, DK), k_norm)


def _cmp_mlp_kernel(x_ref, pe_ref, w1_ref, w2_ref, kn_ref, k_out, vt_out, *, nc):
    c = pl.program_id(2)
    x = x_ref[0, 0, 0]
    a = _dot((x + pe_ref[0, 0:1, :]).astype(BF16), w1_ref[0, 0])
    b = _dot((x + pe_ref[0, 1:2, :]).astype(BF16), w1_ref[0, 1])
    pre = a + pltpu.roll(b, nc - 1, 0)
    comp = _dot(_gelu(pre).astype(BF16), w2_ref[0])

    @pl.when(c == 0)
    def _():
        k_out[0, 0] = _rms(comp, kn_ref[...]).astype(BF16)

    @pl.when(c == 1)
    def _():
        ones_col = jnp.where(_iota((nc, SEL_VR - DK), 1) == 0, 1.0, 0.0)
        vt_out[0, 0] = jnp.concatenate([comp, ones_col], axis=-1).T.astype(BF16)


def cmp_mlp(xc, pe, w1, w2, k_norm0):
    _, b, g, nc, width = xc.shape
    return pl.pallas_call(
        functools.partial(_cmp_mlp_kernel, nc=nc),
        grid=(b, g, 2),
        in_specs=[pl.BlockSpec((1, 1, 1, nc, width), lambda bi, gi, c: (c, bi, gi, 0, 0)),
                  pl.BlockSpec((1, 2, width), lambda bi, gi, c: (c, 0, 0)),
                  pl.BlockSpec((1, 2, width, CMP_HIDDEN), lambda bi, gi, c: (c, 0, 0, 0)),
                  pl.BlockSpec((1, CMP_HIDDEN, DK), lambda bi, gi, c: (c, 0, 0)),
                  pl.BlockSpec((1, DK), lambda bi, gi, c: (0, 0))],
        out_specs=[pl.BlockSpec((1, 1, nc, DK), lambda bi, gi, c: (bi, gi, 0, 0)),
                   pl.BlockSpec((1, 1, SEL_VR, nc), lambda bi, gi, c: (bi, gi, 0, 0))],
        out_shape=[jax.ShapeDtypeStruct((b, g, nc, DK), BF16),
                   jax.ShapeDtypeStruct((b, g, SEL_VR, nc), BF16)],
        compiler_params=_cp(("parallel", "parallel", "arbitrary")),
        name="cmp_mlp",
    )(xc, pe, w1, w2, k_norm0.reshape(1, DK))


def _cmp_attn_kernel(q_ref, k_ref, vt_ref, oc_ref, sb_ref, s_s, acc_s, ps_s, *, nc, n_slc):
    qspan = CMP_QBLOCKS * QB
    q0 = pl.program_id(2) * qspan
    rows = NSA_REP * qspan
    ck = CMP_CHUNK
    q = q_ref[0].reshape(rows, DK)
    n_chunk = ((q0 + qspan) // D_CMP + ck - 1) // ck
    t = q0 + _mod_pow2(_iota((ck, rows), 1), qspan)

    def pass_a(ci, m):
        off = pl.multiple_of(ci * ck, ck)
        st = _dot_nt(k_ref[0, 0, pl.ds(off, ck), :], q)
        n = off + _iota((ck, rows), 0)
        st = jnp.where(n * D_CMP + (L_CMP - 1) <= t, st, NEG)
        s_s[ci] = st
        return jnp.maximum(m, jnp.max(st, axis=0, keepdims=True))

    m = lax.fori_loop(0, n_chunk, pass_a, jnp.full((1, rows), NEG, F32))
    acc_s[...] = jnp.zeros_like(acc_s)
    ps_s[...] = jnp.zeros_like(ps_s)

    def pass_b(ci, den):
        off = pl.multiple_of(ci * ck, ck)
        e = jnp.exp2(s_s[ci] - m)
        s_s[ci] = e
        acc_s[...] += _dot(vt_ref[0, 0, :, pl.ds(off, ck)], e.astype(BF16))
        return den + jnp.sum(e, axis=0, keepdims=True)

    den = lax.fori_loop(0, n_chunk, pass_b, jnp.zeros((1, rows), F32))
    seen = m > 0.5 * NEG
    inv_o = jnp.where(seen, 1.0 / jnp.maximum(acc_s[DK:DK + 1, :], 1e-30), 0.0)
    inv_p = jnp.where(seen, 1.0 / jnp.maximum(den, 1e-30), 0.0)
    oc_ref[0] = (acc_s[...] * inv_o).T[:, 0:DK].reshape(NSA_REP, qspan, DK)

    def pass_c(ci, carry):
        off = pl.multiple_of(ci * ck, ck)
        p = s_s[ci] * inv_p
        ps = p[:, 0:qspan]
        for r in range(1, NSA_REP):
            ps = ps + p[:, r * qspan:(r + 1) * qspan]
        for qi in range(CMP_QBLOCKS):
            ps_s[qi, pl.ds(CMP_HALO + off, ck), :] = ps[:, qi * QB:(qi + 1) * QB]
        return carry

    lax.fori_loop(0, n_chunk, pass_c, 0)

    ratio = L_SLC // D_CMP
    imp = jnp.zeros((n_slc, qspan), F32)
    for k in range(-(L_CMP // D_CMP - 1), ratio):
        overlap = min(k * D_CMP + L_CMP, L_SLC) - max(k * D_CMP, 0)
        taps = [ps_s[qi, pl.ds(CMP_HALO + k, n_slc, stride=ratio), :] for qi in range(CMP_QBLOCKS)]
        imp = imp + (overlap / D_CMP) * jnp.concatenate(taps, axis=1)

    cur = _div_pow2(q0 + _iota((n_slc, qspan), 1), L_SLC)
    j = _iota((n_slc, qspan), 0)
    forced = (j == 0) | (j == cur) | (j == cur - 1)
    score = jnp.where(j <= cur, imp + jnp.where(forced, FORCE_BONUS, 0.0), NEG)
    work = score
    tau = None
    for _ in range(N_SEL):
        tau = jnp.max(work, axis=0, keepdims=True)
        work = jnp.where(work >= tau, REMOVED, work)
    selected = (score > 0.5 * NEG) & (score >= tau)
    bias = jnp.where(selected, 0.0, NEG)
    for qi in range(CMP_QBLOCKS):
        sb_ref[0, 0, qi] = bias[:, qi * QB:(qi + 1) * QB]


def cmp_attn_select(qn, kc, vct):
    b, h, s, _ = qn.shape
    g = NSA_GROUPS
    nc = kc.shape[2]
    n_slc = s // L_SLC
    nqb = s // QB
    qspan = CMP_QBLOCKS * QB
    rows = NSA_REP * qspan
    return pl.pallas_call(
        functools.partial(_cmp_attn_kernel, nc=nc, n_slc=n_slc),
        grid=(b, g, s // qspan),
        in_specs=[pl.BlockSpec((1, NSA_REP, qspan, DK), lambda bi, gi, qi: (bi, gi, qi, 0)),
                  pl.BlockSpec((1, 1, nc, DK), lambda bi, gi, qi: (bi, gi, 0, 0)),
                  pl.BlockSpec((1, 1, SEL_VR, nc), lambda bi, gi, qi: (bi, gi, 0, 0))],
        out_specs=[pl.BlockSpec((1, NSA_REP, qspan, DK), lambda bi, gi, qi: (bi, gi, qi, 0)),
                   pl.BlockSpec((1, 1, CMP_QBLOCKS, n_slc, QB), lambda bi, gi, qi: (bi, gi, qi, 0, 0))],
        out_shape=[jax.ShapeDtypeStruct((b, h, s, DK), F32),
                   jax.ShapeDtypeStruct((b, g, nqb, n_slc, QB), F32)],
        scratch_shapes=[pltpu.VMEM((nc // CMP_CHUNK, CMP_CHUNK, rows), F32),
                        pltpu.VMEM((SEL_VR, rows), F32),
                        pltpu.VMEM((CMP_QBLOCKS, CMP_HALO + nc, QB), F32)],
        compiler_params=_cp(("parallel", "parallel", "parallel")),
        name="cmp_attn_select",
    )(qn, kc, vct)


def _win_attn_kernel(q_ref, *refs):
    nkb = WINDOW // QB + 1
    nrefs = nkb + WIN_QBLOCKS - 1
    k_refs, v_refs, o_ref = refs[:nrefs], refs[nrefs:2 * nrefs], refs[2 * nrefs]
    rows = NSA_REP * QB
    nk = nkb * QB
    for hf in range(WIN_QBLOCKS):
        qb = pl.program_id(2) * WIN_QBLOCKS + hf
        qs = slice(hf * QB, (hf + 1) * QB)
        q = q_ref[0, :, qs, :].reshape(rows, DK)
        k = jnp.concatenate([r[0, 0] for r in k_refs[hf:hf + nkb]], axis=0)
        vt = jnp.concatenate([r[0, 0] for r in v_refs[hf:hf + nkb]], axis=-1)
        st = _dot_nt(k, q)
        t = qb * QB + _mod_pow2(_iota((nk, rows), 1), QB)
        kpos = (qb - (nkb - 1)) * QB + _iota((nk, rows), 0)
        delta = t - kpos
        valid = (delta >= 0) & (delta < WINDOW) & (kpos >= 0)
        st = jnp.where(valid, st, NEG)
        e = jnp.exp2(st - jnp.max(st, axis=0, keepdims=True)).astype(BF16)
        acc = _dot(vt, e)
        o = (acc / jnp.maximum(acc[DK:DK + 1, :], 1e-30)).T[:, 0:DK]
        o_ref[0, :, qs, :] = o.reshape(NSA_REP, QB, DK)


def win_attn(qn, kw, vw):
    b, h, s, _ = qn.shape
    g = NSA_GROUPS
    nkb = WINDOW // QB + 1
    nrefs = nkb + WIN_QBLOCKS - 1
    nqb = s // QB

    def first_block(qi, i):
        return jnp.maximum(qi * WIN_QBLOCKS - (nkb - 1) + i, 0)

    def k_spec(i):
        return pl.BlockSpec((1, 1, QB, DK), lambda bi, gi, qi: (bi, gi, first_block(qi, i), 0))

    def vt_spec(i):
        return pl.BlockSpec((1, 1, SEL_VR, QB), lambda bi, gi, qi: (bi, gi, 0, first_block(qi, i)))

    qspec = pl.BlockSpec((1, NSA_REP, WIN_QBLOCKS * QB, DK), lambda bi, gi, qi: (bi, gi, qi, 0))
    return pl.pallas_call(
        _win_attn_kernel,
        grid=(b, g, nqb // WIN_QBLOCKS),
        in_specs=[qspec] + [k_spec(i) for i in range(nrefs)] + [vt_spec(i) for i in range(nrefs)],
        out_specs=qspec,
        out_shape=jax.ShapeDtypeStruct((b, h, s, DK), F32),
        compiler_params=_cp(("parallel", "parallel", "parallel")),
        name="win_attn",
    )(qn, *([kw] * nrefs), *([vw] * nrefs))


def _sel_attn_kernel(q_ref, sb_ref, k_ref, vt_ref, oc_ref, ow_ref, gl_ref, o_ref,
                     m_s, acc_s, s0_s, s1_s, p0_s, p1_s, a0_s, a1_s):
    qspan = SEL_QBLOCKS * QB
    q0 = pl.program_id(2) * qspan
    rows = NSA_REP * qspan
    blocks = SEL_SUB // L_SLC
    qv = q_ref[0].reshape(rows, DK)
    nsub = (q0 + qspan + SEL_SUB - 1) // SEL_SUB
    m_s[...] = jnp.full_like(m_s, NEG)
    acc_s[...] = jnp.zeros_like(acc_s)

    def scores(i, s_ref):
        off = pl.multiple_of(i * SEL_SUB, SEL_SUB)
        s_ref[...] = _dot_nt(k_ref[0, 0, pl.ds(off, SEL_SUB), :], qv)

    def weights(i, s_ref, p_ref, a_ref, causal):
        first = pl.multiple_of(i * blocks, blocks)
        bias = jnp.concatenate([sb_ref[0, 0, qi, pl.ds(first, blocks), :] for qi in range(SEL_QBLOCKS)],
                               axis=1)
        bias = jnp.broadcast_to(bias[:, None, :], (blocks, L_SLC, qspan)).reshape(SEL_SUB, qspan)
        st = s_ref[...] + jnp.concatenate([bias] * NSA_REP, axis=1)
        if causal:
            kpos = i * SEL_SUB + _iota((SEL_SUB, rows), 0)
            t = q0 + _mod_pow2(_iota((SEL_SUB, rows), 1), qspan)
            st = jnp.where(kpos <= t, st, NEG)
        m_old = m_s[...]
        m_new = jnp.maximum(m_old, jnp.max(st, axis=0, keepdims=True))
        p_ref[...] = jnp.exp2(st - m_new).astype(BF16)
        a_ref[...] = jnp.exp2(m_old - m_new)
        m_s[...] = m_new

    def accumulate(i, p_ref, a_ref):
        off = pl.multiple_of(i * SEL_SUB, SEL_SUB)
        acc_s[...] = a_ref[...] * acc_s[...] + _dot(vt_ref[0, 0, :, pl.ds(off, SEL_SUB)], p_ref[...])

    p1_s[...] = jnp.zeros_like(p1_s)
    a1_s[...] = jnp.ones_like(a1_s)
    scores(0, s0_s)

    def two_tiles(i):
        scores(i + 1, s1_s)
        weights(i, s0_s, p0_s, a0_s, False)
        accumulate(jnp.maximum(i - 1, 0), p1_s, a1_s)
        scores(i + 2, s0_s)
        weights(i + 1, s1_s, p1_s, a1_s, False)
        accumulate(i, p0_s, a0_s)

    e = 0
    left = nsub - 1
    group = SEL_UNROLL
    while group >= 2:
        n_group = left // group

        def body(j, carry, group=group, base=e):
            for k in range(0, group, 2):
                two_tiles(base + group * j + k)
            return carry

        lax.fori_loop(0, n_group, body, 0)
        e = e + group * n_group
        left = left - group * n_group
        group //= 2
    odd_tail = left

    @pl.when(odd_tail == 0)
    def _():
        weights(e, s0_s, p0_s, a0_s, True)
        accumulate(jnp.maximum(e - 1, 0), p1_s, a1_s)
        accumulate(e, p0_s, a0_s)

    @pl.when(odd_tail == 1)
    def _():
        scores(e + 1, s1_s)
        weights(e, s0_s, p0_s, a0_s, False)
        accumulate(jnp.maximum(e - 1, 0), p1_s, a1_s)
        weights(e + 1, s1_s, p1_s, a1_s, True)
        accumulate(e, p0_s, a0_s)
        accumulate(e + 1, p1_s, a1_s)

    o_s = (acc_s[...] / jnp.maximum(acc_s[DK:DK + 1, :], 1e-30)).T[:, 0:DK]
    gate = jax.nn.sigmoid(gl_ref[0].reshape(rows, 3))
    out = (gate[:, 0:1] * oc_ref[0].reshape(rows, DK) + gate[:, 1:2] * o_s
           + gate[:, 2:3] * ow_ref[0].reshape(rows, DK))
    o_ref[0] = out.reshape(NSA_REP, qspan, DK)


def sel_attn(qn, selb, ks, vs, oc, ow, gl):
    b, h, s, _ = qn.shape
    g = NSA_GROUPS
    qspan = SEL_QBLOCKS * QB
    rows = NSA_REP * qspan
    n_slc = s // L_SLC
    qmap = lambda bi, gi, qi: (bi, gi, qi, 0)
    return pl.pallas_call(
        _sel_attn_kernel,
        grid=(b, g, s // qspan),
        in_specs=[pl.BlockSpec((1, NSA_REP, qspan, DK), qmap),
                  pl.BlockSpec((1, 1, SEL_QBLOCKS, n_slc, QB), lambda bi, gi, qi: (bi, gi, qi, 0, 0)),
                  pl.BlockSpec((1, 1, s, DK), lambda bi, gi, qi: (bi, gi, 0, 0)),
                  pl.BlockSpec((1, 1, SEL_VR, s), lambda bi, gi, qi: (bi, gi, 0, 0)),
                  pl.BlockSpec((1, NSA_REP, qspan, DK), qmap),
                  pl.BlockSpec((1, NSA_REP, qspan, DK), qmap),
                  pl.BlockSpec((1, NSA_REP, qspan, 3), qmap)],
        out_specs=pl.BlockSpec((1, NSA_REP, qspan, DK), qmap),
        out_shape=jax.ShapeDtypeStruct((b, h, s, DK), F32),
        scratch_shapes=[pltpu.VMEM((1, rows), F32),
                        pltpu.VMEM((SEL_VR, rows), F32),
                        pltpu.VMEM((SEL_SUB, rows), F32),
                        pltpu.VMEM((SEL_SUB, rows), F32),
                        pltpu.VMEM((SEL_SUB, rows), BF16),
                        pltpu.VMEM((SEL_SUB, rows), BF16),
                        pltpu.VMEM((1, rows), F32),
                        pltpu.VMEM((1, rows), F32)],
        compiler_params=_cp(("parallel", "parallel", "parallel")),
        name="sel_attn",
    )(qn, selb, ks, vs, oc, ow, gl)


def _sgu_kernel(z_ref, g_ref, b_ref, w_ref, bias_ref, o_ref, *, tm):
    z = _gelu(z_ref[...])
    u = z[:, :SGU_WIDTH]
    v = z[:, SGU_WIDTH:]
    mu = jnp.mean(v, axis=-1, keepdims=True)
    var = jnp.mean(jnp.square(v - mu), axis=-1, keepdims=True)
    vn = ((v - mu) * lax.rsqrt(var + EPS) * g_ref[...] + b_ref[...]).astype(BF16)
    tril = _iota((SGU_CHUNK, SGU_CHUNK), 1) <= _iota((SGU_CHUNK, SGU_CHUNK), 0)
    wm = [jnp.where(tril, w_ref[gi], 0.0).astype(BF16) for gi in range(SGU_GROUPS)]
    grp = _div_pow2(_iota((SGU_CHUNK, SGU_WIDTH), 1), SGU_WIDTH // SGU_GROUPS)
    for ch in range(tm // SGU_CHUNK):
        rs = slice(ch * SGU_CHUNK, (ch + 1) * SGU_CHUNK)
        mixed = bias_ref[...]
        for gi in range(SGU_GROUPS):
            mixed = mixed + jnp.where(grp == gi, _dot(wm[gi], vn[rs]), 0.0)
        o_ref[rs, :] = u[rs] * mixed


def sgu(z, ln_g, ln_b, w, bias_full, tm=512):
    t = z.shape[0]
    return pl.pallas_call(
        functools.partial(_sgu_kernel, tm=tm),
        grid=(t // tm,),
        in_specs=[pl.BlockSpec((tm, 2 * SGU_WIDTH), lambda i: (i, 0)),
                  pl.BlockSpec((1, SGU_WIDTH), lambda i: (0, 0)),
                  pl.BlockSpec((1, SGU_WIDTH), lambda i: (0, 0)),
                  pl.BlockSpec((SGU_GROUPS, SGU_CHUNK, SGU_CHUNK), lambda i: (0, 0, 0)),
                  pl.BlockSpec((SGU_CHUNK, SGU_WIDTH), lambda i: (0, 0))],
        out_specs=pl.BlockSpec((tm, SGU_WIDTH), lambda i: (i, 0)),
        out_shape=jax.ShapeDtypeStruct((t, SGU_WIDTH), F32),
        compiler_params=_cp(("parallel",)),
        name="sgu",
    )(z, ln_g.reshape(1, -1), ln_b.reshape(1, -1), w, bias_full)


def _pool_kernel(p_ref, h_ref, w_ref, sc_ref, o_ref, *, tp):
    i = pl.program_id(1)
    halo_rows = POOL_WINDOWS[-1]
    p = p_ref[0]
    halo = jnp.where(i > 0, h_ref[0], 0.0)
    ext = jnp.concatenate([halo, p], axis=0)
    sums = [ext]
    shift = 1
    for _ in POOL_WINDOWS:
        prev = sums[-1]
        sums.append(prev + pltpu.roll(prev, shift, 0))
        shift *= 2
    pos = i * tp + _iota((tp, POOL_WIDTH), 0)
    grp = _div_pow2(_iota((tp, POOL_WIDTH), 1), POOL_WIDTH // len(POOL_WINDOWS))
    d = jnp.zeros((tp, POOL_WIDTH), F32)
    for gi, w in enumerate(POOL_WINDOWS):
        mean = sums[gi + 1][halo_rows:halo_rows + tp] / jnp.minimum(pos + 1, w).astype(F32)
        d = jnp.where(grp == gi, mean, d)
    d = d - p
    o_ref[0] = _dot(d.astype(BF16), w_ref[...]) * sc_ref[...]


def pool(p, w_bd, scale, tp=512):
    b, s, c = p.shape
    halo_rows = POOL_WINDOWS[-1]
    return pl.pallas_call(
        functools.partial(_pool_kernel, tp=tp),
        grid=(b, s // tp),
        in_specs=[pl.BlockSpec((1, tp, c), lambda bi, i: (bi, i, 0)),
                  pl.BlockSpec((1, halo_rows, c),
                               lambda bi, i: (bi, jnp.maximum(i * (tp // halo_rows) - 1, 0), 0)),
                  pl.BlockSpec((c, c), lambda bi, i: (0, 0)),
                  pl.BlockSpec((1, c), lambda bi, i: (0, 0))],
        out_specs=pl.BlockSpec((1, tp, c), lambda bi, i: (bi, i, 0)),
        out_shape=jax.ShapeDtypeStruct((b, s, c), F32),
        compiler_params=_cp(("parallel", "parallel")),
        name="pool",
    )(p, p, w_bd, scale.reshape(1, c))


def _merge_kernel(x_ref, oa_ref, ob_ref, oc_ref, zm_ref, la_ref, lb_ref, lc_ref, wo_ref, o_ref):
    d = D_MODEL
    oa = jnp.concatenate([oa_ref[0, h].astype(BF16) for h in range(NSA_HEADS)], axis=-1)
    zm = zm_ref[0].astype(F32)
    merged = (jax.nn.sigmoid(zm[:, 0:d]) * _dot(oa, la_ref[...])
              + jax.nn.sigmoid(zm[:, d:2 * d]) * _dot(ob_ref[0].astype(BF16), lb_ref[...])
              + jax.nn.sigmoid(zm[:, 2 * d:3 * d]) * _dot(oc_ref[0].astype(BF16), lc_ref[...]))
    o_ref[0] = x_ref[0] + _dot(merged.astype(BF16), wo_ref[...])


def merge(x, oa, ob, oc, zm, la, lb, lc, wo, tm=512):
    b, s, d = x.shape
    row = lambda bi, i: (bi, i, 0)
    full = lambda bi, i: (0, 0)
    return pl.pallas_call(
        _merge_kernel,
        grid=(b, s // tm),
        in_specs=[pl.BlockSpec((1, tm, d), row),
                  pl.BlockSpec((1, NSA_HEADS, tm, DK), lambda bi, i: (bi, 0, i, 0)),
                  pl.BlockSpec((1, tm, SGU_WIDTH), row),
                  pl.BlockSpec((1, tm, POOL_WIDTH), row),
                  pl.BlockSpec((1, tm, 3 * d), row),
                  pl.BlockSpec(la.shape, full), pl.BlockSpec(lb.shape, full),
                  pl.BlockSpec(lc.shape, full), pl.BlockSpec(wo.shape, full)],
        out_specs=pl.BlockSpec((1, tm, d), row),
        out_shape=jax.ShapeDtypeStruct((b, s, d), F32),
        compiler_params=_cp(("parallel", "parallel")),
        name="merge",
    )(x, oa, ob, oc, zm, la, lb, lc, wo)


def _mem_kv_kernel(m_ref, g_ref, wk_ref, wv_ref, kn_ref, k_out, v_out):
    mh = _rms(m_ref[0], g_ref[...]).astype(BF16)
    k = _dot(mh, wk_ref[...])
    for h in range(XA_HEADS):
        hs = slice(h * XA_HD, (h + 1) * XA_HD)
        k_out[0, :, hs] = _rms(k[:, hs], kn_ref[...]).astype(BF16)
    v_out[0] = _dot(mh, wv_ref[...]).astype(BF16)


def mem_kv(mem, g, wk, wv, k_norm):
    b, m, d = mem.shape
    full = lambda bi: (0, 0)
    return pl.pallas_call(
        _mem_kv_kernel,
        grid=(b,),
        in_specs=[pl.BlockSpec((1, m, d), lambda bi: (bi, 0, 0)),
                  pl.BlockSpec((1, d), full),
                  pl.BlockSpec(wk.shape, full), pl.BlockSpec(wv.shape, full),
                  pl.BlockSpec((1, XA_HD), full)],
        out_specs=[pl.BlockSpec((1, m, XA_WIDTH), lambda bi: (bi, 0, 0))] * 2,
        out_shape=[jax.ShapeDtypeStruct((b, m, XA_WIDTH), BF16)] * 2,
        compiler_params=_cp(("parallel",)),
        name="mem_kv",
    )(mem, g.reshape(1, d), wk, wv, k_norm.reshape(1, XA_HD))


def _xattn_kernel(x_ref, g_ref, wq_ref, qn_ref, k_ref, v_ref, wo_ref, o_ref):
    x = x_ref[0]
    q = _dot(_rms(x, g_ref[...]).astype(BF16), wq_ref[...])
    k = k_ref[0]
    v = v_ref[0]
    outs = []
    for h in range(XA_HEADS):
        hs = slice(h * XA_HD, (h + 1) * XA_HD)
        qh = (_rms(q[:, hs], qn_ref[...])).astype(BF16)
        s = _dot_nt(qh, k[:, hs]) * (XA_HD ** -0.5)
        e = jnp.exp(s - jnp.max(s, axis=-1, keepdims=True))
        p = e / jnp.sum(e, axis=-1, keepdims=True)
        outs.append(_dot(p.astype(BF16), v[:, hs]).astype(BF16))
    o = jnp.concatenate(outs, axis=-1)
    o_ref[0] = x + _dot(o, wo_ref[...])


def xattn(x, g, wq, q_norm, k, v, wo, tm=512):
    b, s, d = x.shape
    m = k.shape[1]
    row = lambda bi, i: (bi, i, 0)
    full = lambda bi, i: (0, 0)
    return pl.pallas_call(
        _xattn_kernel,
        grid=(b, s // tm),
        in_specs=[pl.BlockSpec((1, tm, d), row),
                  pl.BlockSpec((1, d), full),
                  pl.BlockSpec(wq.shape, full),
                  pl.BlockSpec((1, XA_HD), full),
                  pl.BlockSpec((1, m, XA_WIDTH), lambda bi, i: (bi, 0, 0)),
                  pl.BlockSpec((1, m, XA_WIDTH), lambda bi, i: (bi, 0, 0)),
                  pl.BlockSpec(wo.shape, full)],
        out_specs=pl.BlockSpec((1, tm, d), row),
        out_shape=jax.ShapeDtypeStruct((b, s, d), F32),
        compiler_params=_cp(("parallel", "parallel")),
        name="xattn",
    )(x, g.reshape(1, d), wq, q_norm.reshape(1, XA_HD), k, v, wo)


def _top_rows(s, n, want_rank):
    out = []
    work = s
    rank = jnp.full(s.shape, float(s.shape[0] - 1), F32) if want_rank else None
    for i in range(n):
        m = jnp.max(work, axis=0, keepdims=True)
        out.append(m)
        hit = work >= m
        if want_rank:
            rank = jnp.where(hit, float(i), rank)
        work = jnp.where(hit, REMOVED, work)
    return out, rank


def _stack_rows(rows, pad_rows):
    tt = rows[0].shape[1]
    rowi = _iota((pad_rows, tt), 0)
    out = jnp.full((pad_rows, tt), REMOVED, F32)
    for i, r in enumerate(rows[:pad_rows]):
        out = jnp.where(rowi == i, r, out)
    return out


def _pair_sum_candidates(v1, v2, tt):
    row_full = _iota((24, tt), 0)
    row = _iota((8, tt), 0)
    v2_full = _stack_rows(v2, 24)
    v2_8 = _stack_rows(v2, 8)

    def shifted(k):
        return pltpu.roll(v2_8, k, 0)

    def pick(rows_from, first):
        out = v1[rows_from]
        for r in range(first + 1, 8):
            if rows_from + r - first < len(v1):
                out = jnp.where(row == r, v1[rows_from + r - first], out)
        return out

    return jnp.concatenate([
        jnp.where(row_full < 17, v1[0] + v2_full, REMOVED),
        v1[1] + v2_8,
        jnp.where(row < 5, v1[2] + v2_8,
                  jnp.where(row < 7, v1[5] + shifted(5), v1[8] + shifted(7))),
        jnp.where(row < 4, v1[3] + v2_8,
                  jnp.where(row < 7, v1[4] + shifted(4), v1[9] + shifted(7))),
        jnp.where(row < 2, v1[6] + v2_8,
                  jnp.where(row < 4, v1[7] + shifted(2), pick(10, 4) + v2[0])),
        jnp.where(row < 3, pick(14, 0) + v2[0], REMOVED),
    ], axis=0)


def _peer_kernel(x_ref, g_ref, wq_ref, k1_ref, k2_ref, u_ref, vt_ref, o_ref,
                 hbt_s, q_s, r2_s, na_s, e1_s, e2_s, acc_s, *, tt, ch):
    c = pl.program_id(1)
    ntop = PEER_TOPK + 1
    assert ntop == 17

    @pl.when(c == 0)
    def _():
        h = _rms(x_ref[...], g_ref[...])
        hbt_s[...] = h.T.astype(BF16)
        q = _dot(h.astype(BF16), wq_ref[...])
        for hd in range(PEER_HEADS):
            q_s[hd] = q[:, hd * 2 * PEER_HALF:(hd + 1) * 2 * PEER_HALF].astype(BF16)
        acc_s[...] = jnp.zeros_like(acc_s)

        def route(hd, carry):
            qh = q_s[hd]
            s1 = _dot_nt(k1_ref[...], qh)
            s2 = _dot_nt(k2_ref[...], qh)
            v1, _ = _top_rows(s1, ntop, False)
            v2, rank2 = _top_rows(s2, ntop, True)
            cand = _pair_sum_candidates(v1, v2, tt)
            top = v1[0] + v2[0]
            z = jnp.zeros((1, tt), F32)
            c_prev = top
            c_last = top
            work = cand
            for r in range(ntop):
                m = jnp.max(work, axis=0, keepdims=True)
                work = jnp.where(work >= m, REMOVED, work)
                if r < PEER_TOPK:
                    z = z + jnp.exp(m - top)
                c_prev, c_last = c_last, m
            thr = 0.5 * (c_prev + c_last)
            na = jnp.zeros((PEER_KEYS, tt), F32)
            for j in range(ntop):
                na = jnp.where(s1 + v2[j] >= thr, float(j + 1), na)
            r2_s[hd] = rank2.astype(BF16)
            na_s[hd] = na
            e1_s[hd] = jnp.exp(s1 - v1[0])
            e2_s[hd] = (jnp.exp(s2 - v2[0]) / z).astype(BF16)
            return carry

        lax.fori_loop(0, PEER_HEADS, route, 0, unroll=4)

    n_slab = ch // PEER_KEYS
    slabs_per_group = PEER_PROJ_ROWS // PEER_KEYS
    n_group = ch // PEER_PROJ_ROWS

    def project(gi):
        rs = slice(gi * PEER_PROJ_ROWS, (gi + 1) * PEER_PROJ_ROWS)
        return _gelu(_dot(u_ref[rs, :], hbt_s[...]).astype(BF16))

    def gate(al, gl_rows):
        a = c * n_slab + al
        wc = jnp.zeros((PEER_KEYS, tt), BF16)
        for hd in range(PEER_HEADS):
            na = na_s[hd, pl.ds(a, 1), :].astype(BF16)
            e1 = e1_s[hd, pl.ds(a, 1), :].astype(BF16)
            wc = wc + jnp.where(r2_s[hd] < na, e2_s[hd] * e1, jnp.zeros((), BF16))
        return wc * gl_rows

    ys = []
    gl_next = project(0)
    for gi in range(n_group):
        gl = gl_next
        if gi + 1 < n_group:
            gl_next = project(gi + 1)
        for k in range(slabs_per_group):
            ys.append(gate(gi * slabs_per_group + k, gl[k * PEER_KEYS:(k + 1) * PEER_KEYS]))
    y = jnp.concatenate(ys, axis=0)
    acc_s[...] += _dot(vt_ref[0], y)

    @pl.when(c == pl.num_programs(1) - 1)
    def _():
        o_ref[...] = x_ref[...] + acc_s[...].T


def peer(x, g, wq, k1p, k2p, u, vt, tt=PEER_TT, ch=PEER_CH):
    t, d = x.shape
    ne = u.shape[0]
    return pl.pallas_call(
        functools.partial(_peer_kernel, tt=tt, ch=ch),
        grid=(t // tt, ne // ch),
        in_specs=[pl.BlockSpec((tt, d), lambda i, c: (i, 0)),
                  pl.BlockSpec((1, d), lambda i, c: (0, 0)),
                  pl.BlockSpec(wq.shape, lambda i, c: (0, 0)),
                  pl.BlockSpec(k1p.shape, lambda i, c: (0, 0)),
                  pl.BlockSpec(k2p.shape, lambda i, c: (0, 0)),
                  pl.BlockSpec((ch, d), lambda i, c: (c, 0)),
                  pl.BlockSpec((1, d, ch), lambda i, c: (c, 0, 0))],
        out_specs=pl.BlockSpec((tt, d), lambda i, c: (i, 0)),
        out_shape=jax.ShapeDtypeStruct((t, d), F32),
        scratch_shapes=[pltpu.VMEM((d, tt), BF16),
                        pltpu.VMEM((PEER_HEADS, tt, 2 * PEER_HALF), BF16),
                        pltpu.VMEM((PEER_HEADS, PEER_KEYS, tt), BF16),
                        pltpu.VMEM((PEER_HEADS, PEER_KEYS, tt), F32),
                        pltpu.VMEM((PEER_HEADS, PEER_KEYS, tt), F32),
                        pltpu.VMEM((PEER_HEADS, PEER_KEYS, tt), BF16),
                        pltpu.VMEM((d, tt), F32)],
        compiler_params=_cp(("parallel", "arbitrary")),
        name="peer",
    )(x, g.reshape(1, d), wq, k1p, k2p, u, vt)


_IN_WIDTHS = (NSA_HEADS * DK, 6 * NSA_GROUPS * DK, 3 * NSA_HEADS, 2 * SGU_WIDTH, POOL_WIDTH, 3 * D_MODEL)


def _in_proj_layout():
    splits, src = [], []
    o_src = o_dst = 0
    for w in _IN_WIDTHS:
        wp = -(-w // LANES) * LANES
        splits.append((o_dst, o_dst + wp))
        src.append((o_src, o_src + w))
        o_src += w
        o_dst += wp
    return splits, src, o_dst


def _chunked_transpose(v):
    ne, d = v.shape
    return v.astype(BF16).reshape(ne // PEER_CH, PEER_CH, d).transpose(0, 2, 1)


def nsa_layer(zq, zkv, zg, q_norm, k_norm, cmp_pe, cmp_w1, cmp_w2, b, s):
    qn, kvc, ks, vs, kw, vw = nsa_prep(zq.reshape(b, s, -1), zkv.reshape(b, s, -1), q_norm, k_norm)
    nc = s // D_CMP
    xc = kvc.reshape(2, b, NSA_GROUPS, nc, D_CMP * DK)
    pe = cmp_pe.reshape(2, 2, D_CMP * DK)
    w1 = cmp_w1.reshape(2, 2, D_CMP * DK, CMP_HIDDEN).astype(BF16)
    kc, vct = cmp_mlp(xc, pe, w1, cmp_w2.astype(BF16), k_norm[0])
    oc, selb = cmp_attn_select(qn, kc, vct)
    ow = win_attn(qn, kw, vw)
    gl = zg[:, :3 * NSA_HEADS].reshape(b, s, NSA_HEADS, 3).transpose(0, 2, 1, 3)
    return sel_attn(qn, selb, ks, vs, oc, ow, gl)


def kernel(x, mem, mix_norm, w_in, nsa_q_norm, nsa_k_norm, cmp_pe, cmp_w1, cmp_w2, sgu_ln_g, sgu_ln_b, sgu_w, sgu_b, pool_w, pool_scale, lift_a, lift_b, lift_c, w_out, xa_norm, mem_norm, xa_wq, xa_wk, xa_wv, xa_q_norm, xa_k_norm, xa_wo, ffn_norm, peer_wq, peer_keys1, peer_keys2, peer_u, peer_v):
    b, s, d = x.shape
    t = b * s
    depth = w_in.shape[0]
    splits, src, n_pad = _in_proj_layout()
    zeros_half = jnp.zeros((PEER_KEYS, PEER_HALF), BF16)
    for l in range(depth):
        w_parts = []
        for (a0, a1), (d0, d1) in zip(src, splits):
            w_parts.append(jnp.pad(w_in[l][:, a0:a1], ((0, 0), (0, (d1 - d0) - (a1 - a0)))))
        w_pad = jnp.concatenate(w_parts, axis=1).astype(BF16)
        zq, zkv, zg, zs, zp, zm = norm_matmul(x.reshape(t, d), mix_norm[l], w_pad, splits,
                                              (F32, F32, F32, F32, F32, BF16), 256, "in_proj")

        oa = nsa_layer(zq, zkv, zg, nsa_q_norm[l], nsa_k_norm[l], cmp_pe[l], cmp_w1[l], cmp_w2[l], b, s)
        bias_full = jnp.repeat(sgu_b[l].T, SGU_WIDTH // SGU_GROUPS, axis=1)
        ob = sgu(zs, sgu_ln_g[l], sgu_ln_b[l], sgu_w[l], bias_full)
        cg = POOL_WIDTH // len(POOL_WINDOWS)
        w_bd = jnp.zeros((POOL_WIDTH, POOL_WIDTH), F32)
        for gi in range(len(POOL_WINDOWS)):
            w_bd = w_bd.at[gi * cg:(gi + 1) * cg, gi * cg:(gi + 1) * cg].set(pool_w[l, gi])
        oc = pool(zp.reshape(b, s, POOL_WIDTH), w_bd.astype(BF16), pool_scale[l])
        x = merge(x, oa, ob.reshape(b, s, SGU_WIDTH), oc, zm.reshape(b, s, 3 * d),
                  lift_a[l].astype(BF16), lift_b[l].astype(BF16), lift_c[l].astype(BF16),
                  w_out[l].astype(BF16))

        mk, mv = mem_kv(mem, mem_norm[l], xa_wk[l].astype(BF16), xa_wv[l].astype(BF16), xa_k_norm[l])
        x = xattn(x, xa_norm[l], xa_wq[l].astype(BF16), xa_q_norm[l], mk, mv, xa_wo[l].astype(BF16))

        k1p = jnp.concatenate([peer_keys1[l].astype(BF16), zeros_half], axis=1)
        k2p = jnp.concatenate([zeros_half, peer_keys2[l].astype(BF16)], axis=1)
        x = peer(x.reshape(t, d), ffn_norm[l], peer_wq[l].astype(BF16), k1p, k2p,
                 peer_u[l].astype(BF16), _chunked_transpose(peer_v[l])).reshape(b, s, d)
    return x
```

```python
import functools

import jax
import jax.numpy as jnp
from jax import lax
from jax.experimental import pallas as pl
from jax.experimental.pallas import tpu as pltpu

F32 = jnp.float32
BF16 = jnp.bfloat16

EPS = 1e-6
LOG2E = 1.4426950408889634
NEG = -1e30
REMOVED = -3e38

D_MODEL = 1024
DK = 64
NSA_HEADS = 8
NSA_GROUPS = 2
NSA_REP = NSA_HEADS // NSA_GROUPS
L_CMP, D_CMP = 32, 16
CMP_HIDDEN = 128
L_SLC = 64
N_SEL = 16
WINDOW = 512
QB = 128
FORCE_BONUS = 1.0e3
SGU_WIDTH = 256
SGU_GROUPS = 4
SGU_CHUNK = 128
POOL_WIDTH = 256
POOL_WINDOWS = (2, 4, 8, 16)
XA_HEADS, XA_HD = 4, 128
XA_WIDTH = XA_HEADS * XA_HD
PEER_HEADS = 8
PEER_KEYS = 128
PEER_TOPK = 16
PEER_HALF = 64

LANES = 128
SEL_VR = 2 * DK
SEL_SUB = 256
SEL_QBLOCKS = 1
SEL_UNROLL = 16
WIN_QBLOCKS = 8
CMP_QBLOCKS = 8
CMP_CHUNK = 256
CMP_HALO = 8
PEER_TT = 512
PEER_CH = 2048
PEER_PROJ_ROWS = 128
VMEM_LIMIT = 56 * 1024 * 1024


def _cp(sem):
    return pltpu.CompilerParams(dimension_semantics=sem, vmem_limit_bytes=VMEM_LIMIT)


def _gelu(x):
    return 0.5 * x * (1.0 + jnp.tanh(0.7978845608028654 * (x + 0.044715 * (x * x * x))))


def _rms(x, g):
    return x * lax.rsqrt(jnp.mean(x * x, axis=-1, keepdims=True) + EPS) * g


def _dot(a, b):
    return jnp.dot(a, b, preferred_element_type=F32)


def _dot_nt(a, b):
    return lax.dot_general(a, b, (((1,), (1,)), ((), ())), preferred_element_type=F32)


def _iota(shape, dim):
    return lax.broadcasted_iota(jnp.int32, shape, dim)


def _div_pow2(x, n):
    assert n & (n - 1) == 0
    return lax.shift_right_logical(x, jnp.int32(n.bit_length() - 1))


def _mod_pow2(x, n):
    assert n & (n - 1) == 0
    return x & (n - 1)


def _norm_matmul_kernel(x_ref, g_ref, w_ref, *o_refs, splits):
    hb = _rms(x_ref[...], g_ref[...]).astype(BF16)
    for o_ref, (a, b) in zip(o_refs, splits):
        o_ref[...] = _dot(hb, w_ref[:, a:b]).astype(o_ref.dtype)


def norm_matmul(x, g, w, splits, out_dtypes, tm, name):
    t, d = x.shape
    n = w.shape[1]
    return pl.pallas_call(
        functools.partial(_norm_matmul_kernel, splits=splits),
        grid=(t // tm,),
        in_specs=[pl.BlockSpec((tm, d), lambda i: (i, 0)),
                  pl.BlockSpec((1, d), lambda i: (0, 0)),
                  pl.BlockSpec((d, n), lambda i: (0, 0))],
        out_specs=[pl.BlockSpec((tm, b - a), lambda i: (i, 0)) for a, b in splits],
        out_shape=[jax.ShapeDtypeStruct((t, b - a), dt) for (a, b), dt in zip(splits, out_dtypes)],
        compiler_params=_cp(("parallel",)),
        name=name,
    )(x, g.reshape(1, d), w)


def _nsa_prep_kernel(zq_ref, zkv_ref, qn_ref, kn_ref,
                     q_out, kvc_out, ks_out, vs_out, kw_out, vw_out, *, ts):
    zq = zq_ref[0]
    scale = DK ** -0.5 * LOG2E
    for h in range(NSA_HEADS):
        q_out[0, h] = (_rms(zq[:, h * DK:(h + 1) * DK], qn_ref[...]) * scale).astype(BF16)
    zkv = zkv_ref[0]

    def piece(i, g):
        o = i * NSA_GROUPS * DK + g * DK
        return zkv[:, o:o + DK]

    ones_col = jnp.where(_iota((ts, SEL_VR - DK), 1) == 0, 1.0, 0.0)
    for g in range(NSA_GROUPS):
        kvc_out[0, 0, g] = piece(0, g)
        kvc_out[1, 0, g] = piece(1, g)
        ks_out[0, g] = _rms(piece(2, g), kn_ref[1:2, :]).astype(BF16)
        vs_out[0, g] = jnp.concatenate([piece(3, g), ones_col], axis=-1).T.astype(BF16)
        kw_out[0, g] = _rms(piece(4, g), kn_ref[2:3, :]).astype(BF16)
        vw_out[0, g] = jnp.concatenate([piece(5, g), ones_col], axis=-1).T.astype(BF16)


def nsa_prep(zq, zkv, q_norm, k_norm, ts=512):
    b, s, _ = zq.shape
    g = NSA_GROUPS
    hm = lambda bi, si: (bi, 0, si, 0)
    return pl.pallas_call(
        functools.partial(_nsa_prep_kernel, ts=ts),
        grid=(b, s // ts),
        in_specs=[pl.BlockSpec((1, ts, NSA_HEADS * DK), lambda bi, si: (bi, si, 0)),
                  pl.BlockSpec((1, ts, 6 * g * DK), lambda bi, si: (bi, si, 0)),
                  pl.BlockSpec((1, DK), lambda bi, si: (0, 0)),
                  pl.BlockSpec((3, DK), lambda bi, si: (0, 0))],
        out_specs=[pl.BlockSpec((1, NSA_HEADS, ts, DK), hm),
                   pl.BlockSpec((2, 1, g, ts, DK), lambda bi, si: (0, bi, 0, si, 0)),
                   pl.BlockSpec((1, g, ts, DK), hm),
                   pl.BlockSpec((1, g, SEL_VR, ts), lambda bi, si: (bi, 0, 0, si)),
                   pl.BlockSpec((1, g, ts, DK), hm),
                   pl.BlockSpec((1, g, SEL_VR, ts), lambda bi, si: (bi, 0, 0, si))],
        out_shape=[jax.ShapeDtypeStruct((b, NSA_HEADS, s, DK), BF16),
                   jax.ShapeDtypeStruct((2, b, g, s, DK), F32),
                   jax.ShapeDtypeStruct((b, g, s, DK), BF16),
                   jax.ShapeDtypeStruct((b, g, SEL_VR, s), BF16),
                   jax.ShapeDtypeStruct((b, g, s, DK), BF16),
                   jax.ShapeDtypeStruct((b, g, SEL_VR, s), BF16)],
        compiler_params=_cp(("parallel", "parallel")),
        name="nsa_prep",
    )(zq, zkv, q_norm.reshape(1, DK), k_norm)


def _cmp_mlp_kernel(x_ref, pe_ref, w1_ref, w2_ref, kn_ref, k_out, vt_out, *, nc):
    c = pl.program_id(2)
    x = x_ref[0, 0, 0]
    a = _dot((x + pe_ref[0, 0:1, :]).astype(BF16), w1_ref[0, 0])
    b = _dot((x + pe_ref[0, 1:2, :]).astype(BF16), w1_ref[0, 1])
    pre = a + pltpu.roll(b, nc - 1, 0)
    comp = _dot(_gelu(pre).astype(BF16), w2_ref[0])

    @pl.when(c == 0)
    def _():
        k_out[0, 0] = _rms(comp, kn_ref[...]).astype(BF16)

    @pl.when(c == 1)
    def _():
        ones_col = jnp.where(_iota((nc, SEL_VR - DK), 1) == 0, 1.0, 0.0)
        vt_out[0, 0] = jnp.concatenate([comp, ones_col], axis=-1).T.astype(BF16)


def cmp_mlp(xc, pe, w1, w2, k_norm0):
    _, b, g, nc, width = xc.shape
    return pl.pallas_call(
        functools.partial(_cmp_mlp_kernel, nc=nc),
        grid=(b, g, 2),
        in_specs=[pl.BlockSpec((1, 1, 1, nc, width), lambda bi, gi, c: (c, bi, gi, 0, 0)),
                  pl.BlockSpec((1, 2, width), lambda bi, gi, c: (c, 0, 0)),
                  pl.BlockSpec((1, 2, width, CMP_HIDDEN), lambda bi, gi, c: (c, 0, 0, 0)),
                  pl.BlockSpec((1, CMP_HIDDEN, DK), lambda bi, gi, c: (c, 0, 0)),
                  pl.BlockSpec((1, DK), lambda bi, gi, c: (0, 0))],
        out_specs=[pl.BlockSpec((1, 1, nc, DK), lambda bi, gi, c: (bi, gi, 0, 0)),
                   pl.BlockSpec((1, 1, SEL_VR, nc), lambda bi, gi, c: (bi, gi, 0, 0))],
        out_shape=[jax.ShapeDtypeStruct((b, g, nc, DK), BF16),
                   jax.ShapeDtypeStruct((b, g, SEL_VR, nc), BF16)],
        compiler_params=_cp(("parallel", "parallel", "arbitrary")),
        name="cmp_mlp",
    )(xc, pe, w1, w2, k_norm0.reshape(1, DK))


def _cmp_attn_kernel(q_ref, k_ref, vt_ref, oc_ref, sb_ref, s_s, acc_s, ps_s, *, nc, n_slc):
    qspan = CMP_QBLOCKS * QB
    q0 = pl.program_id(2) * qspan
    rows = NSA_REP * qspan
    ck = CMP_CHUNK
    q = q_ref[0].reshape(rows, DK)
    n_chunk = ((q0 + qspan) // D_CMP + ck - 1) // ck
    t = q0 + _mod_pow2(_iota((ck, rows), 1), qspan)

    def pass_a(ci, m):
        off = pl.multiple_of(ci * ck, ck)
        st = _dot_nt(k_ref[0, 0, pl.ds(off, ck), :], q)
        n = off + _iota((ck, rows), 0)
        st = jnp.where(n * D_CMP + (L_CMP - 1) <= t, st, NEG)
        s_s[ci] = st
        return jnp.maximum(m, jnp.max(st, axis=0, keepdims=True))

    m = lax.fori_loop(0, n_chunk, pass_a, jnp.full((1, rows), NEG, F32))
    acc_s[...] = jnp.zeros_like(acc_s)
    ps_s[...] = jnp.zeros_like(ps_s)

    def pass_b(ci, den):
        off = pl.multiple_of(ci * ck, ck)
        e = jnp.exp2(s_s[ci] - m)
        s_s[ci] = e
        acc_s[...] += _dot(vt_ref[0, 0, :, pl.ds(off, ck)], e.astype(BF16))
        return den + jnp.sum(e, axis=0, keepdims=True)

    den = lax.fori_loop(0, n_chunk, pass_b, jnp.zeros((1, rows), F32))
    seen = m > 0.5 * NEG
    inv_o = jnp.where(seen, 1.0 / jnp.maximum(acc_s[DK:DK + 1, :], 1e-30), 0.0)
    inv_p = jnp.where(seen, 1.0 / jnp.maximum(den, 1e-30), 0.0)
    oc_ref[0] = (acc_s[...] * inv_o).T[:, 0:DK].reshape(NSA_REP, qspan, DK)

    def pass_c(ci, carry):
        off = pl.multiple_of(ci * ck, ck)
        p = s_s[ci] * inv_p
        ps = p[:, 0:qspan]
        for r in range(1, NSA_REP):
            ps = ps + p[:, r * qspan:(r + 1) * qspan]
        for qi in range(CMP_QBLOCKS):
            ps_s[qi, pl.ds(CMP_HALO + off, ck), :] = ps[:, qi * QB:(qi + 1) * QB]
        return carry

    lax.fori_loop(0, n_chunk, pass_c, 0)

    ratio = L_SLC // D_CMP
    imp = jnp.zeros((n_slc, qspan), F32)
    for k in range(-(L_CMP // D_CMP - 1), ratio):
        overlap = min(k * D_CMP + L_CMP, L_SLC) - max(k * D_CMP, 0)
        taps = [ps_s[qi, pl.ds(CMP_HALO + k, n_slc, stride=ratio), :] for qi in range(CMP_QBLOCKS)]
        imp = imp + (overlap / D_CMP) * jnp.concatenate(taps, axis=1)

    cur = _div_pow2(q0 + _iota((n_slc, qspan), 1), L_SLC)
    j = _iota((n_slc, qspan), 0)
    forced = (j == 0) | (j == cur) | (j == cur - 1)
    score = jnp.where(j <= cur, imp + jnp.where(forced, FORCE_BONUS, 0.0), NEG)
    work = score
    tau = None
    for _ in range(N_SEL):
        tau = jnp.max(work, axis=0, keepdims=True)
        work = jnp.where(work >= tau, REMOVED, work)
    selected = (score > 0.5 * NEG) & (score >= tau)
    bias = jnp.where(selected, 0.0, NEG)
    for qi in range(CMP_QBLOCKS):
        sb_ref[0, 0, qi] = bias[:, qi * QB:(qi + 1) * QB]


def cmp_attn_select(qn, kc, vct):
    b, h, s, _ = qn.shape
    g = NSA_GROUPS
    nc = kc.shape[2]
    n_slc = s // L_SLC
    nqb = s // QB
    qspan = CMP_QBLOCKS * QB
    rows = NSA_REP * qspan
    return pl.pallas_call(
        functools.partial(_cmp_attn_kernel, nc=nc, n_slc=n_slc),
        grid=(b, g, s // qspan),
        in_specs=[pl.BlockSpec((1, NSA_REP, qspan, DK), lambda bi, gi, qi: (bi, gi, qi, 0)),
                  pl.BlockSpec((1, 1, nc, DK), lambda bi, gi, qi: (bi, gi, 0, 0)),
                  pl.BlockSpec((1, 1, SEL_VR, nc), lambda bi, gi, qi: (bi, gi, 0, 0))],
        out_specs=[pl.BlockSpec((1, NSA_REP, qspan, DK), lambda bi, gi, qi: (bi, gi, qi, 0)),
                   pl.BlockSpec((1, 1, CMP_QBLOCKS, n_slc, QB), lambda bi, gi, qi: (bi, gi, qi, 0, 0))],
        out_shape=[jax.ShapeDtypeStruct((b, h, s, DK), F32),
                   jax.ShapeDtypeStruct((b, g, nqb, n_slc, QB), F32)],
        scratch_shapes=[pltpu.VMEM((nc // CMP_CHUNK, CMP_CHUNK, rows), F32),
                        pltpu.VMEM((SEL_VR, rows), F32),
                        pltpu.VMEM((CMP_QBLOCKS, CMP_HALO + nc, QB), F32)],
        compiler_params=_cp(("parallel", "parallel", "parallel")),
        name="cmp_attn_select",
    )(qn, kc, vct)


def _win_attn_kernel(q_ref, *refs):
    nkb = WINDOW // QB + 1
    nrefs = nkb + WIN_QBLOCKS - 1
    k_refs, v_refs, o_ref = refs[:nrefs], refs[nrefs:2 * nrefs], refs[2 * nrefs]
    rows = NSA_REP * QB
    nk = nkb * QB
    for hf in range(WIN_QBLOCKS):
        qb = pl.program_id(2) * WIN_QBLOCKS + hf
        qs = slice(hf * QB, (hf + 1) * QB)
        q = q_ref[0, :, qs, :].reshape(rows, DK)
        k = jnp.concatenate([r[0, 0] for r in k_refs[hf:hf + nkb]], axis=0)
        vt = jnp.concatenate([r[0, 0] for r in v_refs[hf:hf + nkb]], axis=-1)
        st = _dot_nt(k, q)
        t = qb * QB + _mod_pow2(_iota((nk, rows), 1), QB)
        kpos = (qb - (nkb - 1)) * QB + _iota((nk, rows), 0)
        delta = t - kpos
        valid = (delta >= 0) & (delta < WINDOW) & (kpos >= 0)
        st = jnp.where(valid, st, NEG)
        e = jnp.exp2(st - jnp.max(st, axis=0, keepdims=True)).astype(BF16)
        acc = _dot(vt, e)
        o = (acc / jnp.maximum(acc[DK:DK + 1, :], 1e-30)).T[:, 0:DK]
        o_ref[0, :, qs, :] = o.reshape(NSA_REP, QB, DK)


def win_attn(qn, kw, vw):
    b, h, s, _ = qn.shape
    g = NSA_GROUPS
    nkb = WINDOW // QB + 1
    nrefs = nkb + WIN_QBLOCKS - 1
    nqb = s // QB

    def first_block(qi, i):
        return jnp.maximum(qi * WIN_QBLOCKS - (nkb - 1) + i, 0)

    def k_spec(i):
        return pl.BlockSpec((1, 1, QB, DK), lambda bi, gi, qi: (bi, gi, first_block(qi, i), 0))

    def vt_spec(i):
        return pl.BlockSpec((1, 1, SEL_VR, QB), lambda bi, gi, qi: (bi, gi, 0, first_block(qi, i)))

    qspec = pl.BlockSpec((1, NSA_REP, WIN_QBLOCKS * QB, DK), lambda bi, gi, qi: (bi, gi, qi, 0))
    return pl.pallas_call(
        _win_attn_kernel,
        grid=(b, g, nqb // WIN_QBLOCKS),
        in_specs=[qspec] + [k_spec(i) for i in range(nrefs)] + [vt_spec(i) for i in range(nrefs)],
        out_specs=qspec,
        out_shape=jax.ShapeDtypeStruct((b, h, s, DK), F32),
        compiler_params=_cp(("parallel", "parallel", "parallel")),
        name="win_attn",
    )(qn, *([kw] * nrefs), *([vw] * nrefs))


def _sel_attn_kernel(q_ref, sb_ref, k_ref, vt_ref, oc_ref, ow_ref, gl_ref, o_ref,
                     m_s, acc_s, s0_s, s1_s, p0_s, p1_s, a0_s, a1_s):
    qspan = SEL_QBLOCKS * QB
    q0 = pl.program_id(2) * qspan
    rows = NSA_REP * qspan
    blocks = SEL_SUB // L_SLC
    qv = q_ref[0].reshape(rows, DK)
    nsub = (q0 + qspan + SEL_SUB - 1) // SEL_SUB
    m_s[...] = jnp.full_like(m_s, NEG)
    acc_s[...] = jnp.zeros_like(acc_s)

    def scores(i, s_ref):
        off = pl.multiple_of(i * SEL_SUB, SEL_SUB)
        s_ref[...] = _dot_nt(k_ref[0, 0, pl.ds(off, SEL_SUB), :], qv)

    def weights(i, s_ref, p_ref, a_ref, causal):
        first = pl.multiple_of(i * blocks, blocks)
        bias = jnp.concatenate([sb_ref[0, 0, qi, pl.ds(first, blocks), :] for qi in range(SEL_QBLOCKS)],
                               axis=1)
        bias = jnp.concatenate([bias] * NSA_REP, axis=1)
        st = s_ref[...]
        if causal:
            kpos = i * SEL_SUB + _iota((SEL_SUB, rows), 0)
            t = q0 + _mod_pow2(_iota((SEL_SUB, rows), 1), qspan)
            st = jnp.where(kpos <= t, st, NEG)
        st = st.reshape(blocks, L_SLC, rows)
        m_old = m_s[...]
        m_new = jnp.maximum(m_old, jnp.max(jnp.max(st, axis=1) + bias, axis=0, keepdims=True))
        shift = bias - m_new
        p_ref[...] = jnp.exp2(st + shift[:, None, :]).reshape(SEL_SUB, rows).astype(BF16)
        a_ref[...] = jnp.exp2(m_old - m_new)
        m_s[...] = m_new

    def accumulate(i, p_ref, a_ref):
        off = pl.multiple_of(i * SEL_SUB, SEL_SUB)
        acc_s[...] = a_ref[...] * acc_s[...] + _dot(vt_ref[0, 0, :, pl.ds(off, SEL_SUB)], p_ref[...])

    p1_s[...] = jnp.zeros_like(p1_s)
    a1_s[...] = jnp.ones_like(a1_s)
    scores(0, s0_s)

    def two_tiles(i):
        scores(i + 1, s1_s)
        weights(i, s0_s, p0_s, a0_s, False)
        accumulate(jnp.maximum(i - 1, 0), p1_s, a1_s)
        scores(i + 2, s0_s)
        weights(i + 1, s1_s, p1_s, a1_s, False)
        accumulate(i, p0_s, a0_s)

    e = 0
    left = nsub - 1
    group = SEL_UNROLL
    while group >= 2:
        n_group = left // group

        def body(j, carry, group=group, base=e):
            for k in range(0, group, 2):
                two_tiles(base + group * j + k)
            return carry

        lax.fori_loop(0, n_group, body, 0)
        e = e + group * n_group
        left = left - group * n_group
        group //= 2
    odd_tail = left

    @pl.when(odd_tail == 0)
    def _():
        weights(e, s0_s, p0_s, a0_s, True)
        accumulate(jnp.maximum(e - 1, 0), p1_s, a1_s)
        accumulate(e, p0_s, a0_s)

    @pl.when(odd_tail == 1)
    def _():
        scores(e + 1, s1_s)
        weights(e, s0_s, p0_s, a0_s, False)
        accumulate(jnp.maximum(e - 1, 0), p1_s, a1_s)
        weights(e + 1, s1_s, p1_s, a1_s, True)
        accumulate(e, p0_s, a0_s)
        accumulate(e + 1, p1_s, a1_s)

    o_s = (acc_s[...] / jnp.maximum(acc_s[DK:DK + 1, :], 1e-30)).T[:, 0:DK]
    gate = jax.nn.sigmoid(gl_ref[0].reshape(rows, 3))
    out = (gate[:, 0:1] * oc_ref[0].reshape(rows, DK) + gate[:, 1:2] * o_s
           + gate[:, 2:3] * ow_ref[0].reshape(rows, DK))
    o_ref[0] = out.reshape(NSA_REP, qspan, DK)


def sel_attn(qn, selb, ks, vs, oc, ow, gl):
    b, h, s, _ = qn.shape
    g = NSA_GROUPS
    qspan = SEL_QBLOCKS * QB
    rows = NSA_REP * qspan
    n_slc = s // L_SLC
    qmap = lambda bi, gi, qi: (bi, gi, qi, 0)
    return pl.pallas_call(
        _sel_attn_kernel,
        grid=(b, g, s // qspan),
        in_specs=[pl.BlockSpec((1, NSA_REP, qspan, DK), qmap),
                  pl.BlockSpec((1, 1, SEL_QBLOCKS, n_slc, QB), lambda bi, gi, qi: (bi, gi, qi, 0, 0)),
                  pl.BlockSpec((1, 1, s, DK), lambda bi, gi, qi: (bi, gi, 0, 0)),
                  pl.BlockSpec((1, 1, SEL_VR, s), lambda bi, gi, qi: (bi, gi, 0, 0)),
                  pl.BlockSpec((1, NSA_REP, qspan, DK), qmap),
                  pl.BlockSpec((1, NSA_REP, qspan, DK), qmap),
                  pl.BlockSpec((1, NSA_REP, qspan, 3), qmap)],
        out_specs=pl.BlockSpec((1, NSA_REP, qspan, DK), qmap),
        out_shape=jax.ShapeDtypeStruct((b, h, s, DK), F32),
        scratch_shapes=[pltpu.VMEM((1, rows), F32),
                        pltpu.VMEM((SEL_VR, rows), F32),
                        pltpu.VMEM((SEL_SUB, rows), F32),
                        pltpu.VMEM((SEL_SUB, rows), F32),
                        pltpu.VMEM((SEL_SUB, rows), BF16),
                        pltpu.VMEM((SEL_SUB, rows), BF16),
                        pltpu.VMEM((1, rows), F32),
                        pltpu.VMEM((1, rows), F32)],
        compiler_params=_cp(("parallel", "parallel", "parallel")),
        name="sel_attn",
    )(qn, selb, ks, vs, oc, ow, gl)


def _sgu_kernel(z_ref, g_ref, b_ref, w_ref, bias_ref, o_ref, *, tm):
    z = _gelu(z_ref[...])
    u = z[:, :SGU_WIDTH]
    v = z[:, SGU_WIDTH:]
    mu = jnp.mean(v, axis=-1, keepdims=True)
    var = jnp.mean(jnp.square(v - mu), axis=-1, keepdims=True)
    vn = ((v - mu) * lax.rsqrt(var + EPS) * g_ref[...] + b_ref[...]).astype(BF16)
    tril = _iota((SGU_CHUNK, SGU_CHUNK), 1) <= _iota((SGU_CHUNK, SGU_CHUNK), 0)
    wm = [jnp.where(tril, w_ref[gi], 0.0).astype(BF16) for gi in range(SGU_GROUPS)]
    grp = _div_pow2(_iota((SGU_CHUNK, SGU_WIDTH), 1), SGU_WIDTH // SGU_GROUPS)
    for ch in range(tm // SGU_CHUNK):
        rs = slice(ch * SGU_CHUNK, (ch + 1) * SGU_CHUNK)
        mixed = bias_ref[...]
        for gi in range(SGU_GROUPS):
            mixed = mixed + jnp.where(grp == gi, _dot(wm[gi], vn[rs]), 0.0)
        o_ref[rs, :] = u[rs] * mixed


def sgu(z, ln_g, ln_b, w, bias_full, tm=512):
    t = z.shape[0]
    return pl.pallas_call(
        functools.partial(_sgu_kernel, tm=tm),
        grid=(t // tm,),
        in_specs=[pl.BlockSpec((tm, 2 * SGU_WIDTH), lambda i: (i, 0)),
                  pl.BlockSpec((1, SGU_WIDTH), lambda i: (0, 0)),
                  pl.BlockSpec((1, SGU_WIDTH), lambda i: (0, 0)),
                  pl.BlockSpec((SGU_GROUPS, SGU_CHUNK, SGU_CHUNK), lambda i: (0, 0, 0)),
                  pl.BlockSpec((SGU_CHUNK, SGU_WIDTH), lambda i: (0, 0))],
        out_specs=pl.BlockSpec((tm, SGU_WIDTH), lambda i: (i, 0)),
        out_shape=jax.ShapeDtypeStruct((t, SGU_WIDTH), F32),
        compiler_params=_cp(("parallel",)),
        name="sgu",
    )(z, ln_g.reshape(1, -1), ln_b.reshape(1, -1), w, bias_full)


def _pool_kernel(p_ref, h_ref, w_ref, sc_ref, o_ref, *, tp):
    i = pl.program_id(1)
    halo_rows = POOL_WINDOWS[-1]
    p = p_ref[0]
    halo = jnp.where(i > 0, h_ref[0], 0.0)
    ext = jnp.concatenate([halo, p], axis=0)
    sums = [ext]
    shift = 1
    for _ in POOL_WINDOWS:
        prev = sums[-1]
        sums.append(prev + pltpu.roll(prev, shift, 0))
        shift *= 2
    pos = i * tp + _iota((tp, POOL_WIDTH), 0)
    grp = _div_pow2(_iota((tp, POOL_WIDTH), 1), POOL_WIDTH // len(POOL_WINDOWS))
    d = jnp.zeros((tp, POOL_WIDTH), F32)
    for gi, w in enumerate(POOL_WINDOWS):
        mean = sums[gi + 1][halo_rows:halo_rows + tp] / jnp.minimum(pos + 1, w).astype(F32)
        d = jnp.where(grp == gi, mean, d)
    d = d - p
    o_ref[0] = _dot(d.astype(BF16), w_ref[...]) * sc_ref[...]


def pool(p, w_bd, scale, tp=512):
    b, s, c = p.shape
    halo_rows = POOL_WINDOWS[-1]
    return pl.pallas_call(
        functools.partial(_pool_kernel, tp=tp),
        grid=(b, s // tp),
        in_specs=[pl.BlockSpec((1, tp, c), lambda bi, i: (bi, i, 0)),
                  pl.BlockSpec((1, halo_rows, c),
                               lambda bi, i: (bi, jnp.maximum(i * (tp // halo_rows) - 1, 0), 0)),
                  pl.BlockSpec((c, c), lambda bi, i: (0, 0)),
                  pl.BlockSpec((1, c), lambda bi, i: (0, 0))],
        out_specs=pl.BlockSpec((1, tp, c), lambda bi, i: (bi, i, 0)),
        out_shape=jax.ShapeDtypeStruct((b, s, c), F32),
        compiler_params=_cp(("parallel", "parallel")),
        name="pool",
    )(p, p, w_bd, scale.reshape(1, c))


def _merge_kernel(x_ref, oa_ref, ob_ref, oc_ref, zm_ref, la_ref, lb_ref, lc_ref, wo_ref, o_ref):
    d = D_MODEL
    oa = jnp.concatenate([oa_ref[0, h].astype(BF16) for h in range(NSA_HEADS)], axis=-1)
    zm = zm_ref[0].astype(F32)
    merged = (jax.nn.sigmoid(zm[:, 0:d]) * _dot(oa, la_ref[...])
              + jax.nn.sigmoid(zm[:, d:2 * d]) * _dot(ob_ref[0].astype(BF16), lb_ref[...])
              + jax.nn.sigmoid(zm[:, 2 * d:3 * d]) * _dot(oc_ref[0].astype(BF16), lc_ref[...]))
    o_ref[0] = x_ref[0] + _dot(merged.astype(BF16), wo_ref[...])


def merge(x, oa, ob, oc, zm, la, lb, lc, wo, tm=512):
    b, s, d = x.shape
    row = lambda bi, i: (bi, i, 0)
    full = lambda bi, i: (0, 0)
    return pl.pallas_call(
        _merge_kernel,
        grid=(b, s // tm),
        in_specs=[pl.BlockSpec((1, tm, d), row),
                  pl.BlockSpec((1, NSA_HEADS, tm, DK), lambda bi, i: (bi, 0, i, 0)),
                  pl.BlockSpec((1, tm, SGU_WIDTH), row),
                  pl.BlockSpec((1, tm, POOL_WIDTH), row),
                  pl.BlockSpec((1, tm, 3 * d), row),
                  pl.BlockSpec(la.shape, full), pl.BlockSpec(lb.shape, full),
                  pl.BlockSpec(lc.shape, full), pl.BlockSpec(wo.shape, full)],
        out_specs=pl.BlockSpec((1, tm, d), row),
        out_shape=jax.ShapeDtypeStruct((b, s, d), F32),
        compiler_params=_cp(("parallel", "parallel")),
        name="merge",
    )(x, oa, ob, oc, zm, la, lb, lc, wo)


def _mem_kv_kernel(m_ref, g_ref, wk_ref, wv_ref, kn_ref, k_out, v_out):
    mh = _rms(m_ref[0], g_ref[...]).astype(BF16)
    k = _dot(mh, wk_ref[...])
    for h in range(XA_HEADS):
        hs = slice(h * XA_HD, (h + 1) * XA_HD)
        k_out[0, :, hs] = _rms(k[:, hs], kn_ref[...]).astype(BF16)
    v_out[0] = _dot(mh, wv_ref[...]).astype(BF16)


def mem_kv(mem, g, wk, wv, k_norm):
    b, m, d = mem.shape
    full = lambda bi: (0, 0)
    return pl.pallas_call(
        _mem_kv_kernel,
        grid=(b,),
        in_specs=[pl.BlockSpec((1, m, d), lambda bi: (bi, 0, 0)),
                  pl.BlockSpec((1, d), full),
                  pl.BlockSpec(wk.shape, full), pl.BlockSpec(wv.shape, full),
                  pl.BlockSpec((1, XA_HD), full)],
        out_specs=[pl.BlockSpec((1, m, XA_WIDTH), lambda bi: (bi, 0, 0))] * 2,
        out_shape=[jax.ShapeDtypeStruct((b, m, XA_WIDTH), BF16)] * 2,
        compiler_params=_cp(("parallel",)),
        name="mem_kv",
    )(mem, g.reshape(1, d), wk, wv, k_norm.reshape(1, XA_HD))


def _xattn_kernel(x_ref, g_ref, wq_ref, qn_ref, k_ref, v_ref, wo_ref, o_ref):
    x = x_ref[0]
    q = _dot(_rms(x, g_ref[...]).astype(BF16), wq_ref[...])
    k = k_ref[0]
    v = v_ref[0]
    outs = []
    for h in range(XA_HEADS):
        hs = slice(h * XA_HD, (h + 1) * XA_HD)
        qh = (_rms(q[:, hs], qn_ref[...])).astype(BF16)
        s = _dot_nt(qh, k[:, hs]) * (XA_HD ** -0.5)
        e = jnp.exp(s - jnp.max(s, axis=-1, keepdims=True))
        p = e / jnp.sum(e, axis=-1, keepdims=True)
        outs.append(_dot(p.astype(BF16), v[:, hs]).astype(BF16))
    o = jnp.concatenate(outs, axis=-1)
    o_ref[0] = x + _dot(o, wo_ref[...])


def xattn(x, g, wq, q_norm, k, v, wo, tm=512):
    b, s, d = x.shape
    m = k.shape[1]
    row = lambda bi, i: (bi, i, 0)
    full = lambda bi, i: (0, 0)
    return pl.pallas_call(
        _xattn_kernel,
        grid=(b, s // tm),
        in_specs=[pl.BlockSpec((1, tm, d), row),
                  pl.BlockSpec((1, d), full),
                  pl.BlockSpec(wq.shape, full),
                  pl.BlockSpec((1, XA_HD), full),
                  pl.BlockSpec((1, m, XA_WIDTH), lambda bi, i: (bi, 0, 0)),
                  pl.BlockSpec((1, m, XA_WIDTH), lambda bi, i: (bi, 0, 0)),
                  pl.BlockSpec(wo.shape, full)],
        out_specs=pl.BlockSpec((1, tm, d), row),
        out_shape=jax.ShapeDtypeStruct((b, s, d), F32),
        compiler_params=_cp(("parallel", "parallel")),
        name="xattn",
    )(x, g.reshape(1, d), wq, q_norm.reshape(1, XA_HD), k, v, wo)


def _top_rows(s, n, want_rank):
    out = []
    work = s
    rank = jnp.full(s.shape, float(s.shape[0] - 1), F32) if want_rank else None
    for i in range(n):
        m = jnp.max(work, axis=0, keepdims=True)
        out.append(m)
        hit = work >= m
        if want_rank:
            rank = jnp.where(hit, float(i), rank)
        work = jnp.where(hit, REMOVED, work)
    return out, rank


def _stack_rows(rows, pad_rows):
    tt = rows[0].shape[1]
    rowi = _iota((pad_rows, tt), 0)
    out = jnp.full((pad_rows, tt), REMOVED, F32)
    for i, r in enumerate(rows[:pad_rows]):
        out = jnp.where(rowi == i, r, out)
    return out


def _pair_sum_candidates(v1, v2, tt):
    row_full = _iota((24, tt), 0)
    row = _iota((8, tt), 0)
    v2_full = _stack_rows(v2, 24)
    v2_8 = _stack_rows(v2, 8)

    def shifted(k):
        return pltpu.roll(v2_8, k, 0)

    def pick(rows_from, first):
        out = v1[rows_from]
        for r in range(first + 1, 8):
            if rows_from + r - first < len(v1):
                out = jnp.where(row == r, v1[rows_from + r - first], out)
        return out

    return jnp.concatenate([
        jnp.where(row_full < 17, v1[0] + v2_full, REMOVED),
        v1[1] + v2_8,
        jnp.where(row < 5, v1[2] + v2_8,
                  jnp.where(row < 7, v1[5] + shifted(5), v1[8] + shifted(7))),
        jnp.where(row < 4, v1[3] + v2_8,
                  jnp.where(row < 7, v1[4] + shifted(4), v1[9] + shifted(7))),
        jnp.where(row < 2, v1[6] + v2_8,
                  jnp.where(row < 4, v1[7] + shifted(2), pick(10, 4) + v2[0])),
        jnp.where(row < 3, pick(14, 0) + v2[0], REMOVED),
    ], axis=0)


def _peer_kernel(x_ref, g_ref, wq_ref, k1_ref, k2_ref, u_ref, vt_ref, o_ref,
                 hbt_s, q_s, r2_s, na_s, e1_s, e2_s, acc_s, *, tt, ch):
    c = pl.program_id(1)
    ntop = PEER_TOPK + 1
    assert ntop == 17

    @pl.when(c == 0)
    def _():
        h = _rms(x_ref[...], g_ref[...])
        hbt_s[...] = h.T.astype(BF16)
        q = _dot(h.astype(BF16), wq_ref[...])
        for hd in range(PEER_HEADS):
            q_s[hd] = q[:, hd * 2 * PEER_HALF:(hd + 1) * 2 * PEER_HALF].astype(BF16)
        acc_s[...] = jnp.zeros_like(acc_s)

        def route(hd, carry):
            qh = q_s[hd]
            s1 = _dot_nt(k1_ref[...], qh)
            s2 = _dot_nt(k2_ref[...], qh)
            v1, _ = _top_rows(s1, ntop, False)
            v2, rank2 = _top_rows(s2, ntop, True)
            cand = _pair_sum_candidates(v1, v2, tt)
            top = v1[0] + v2[0]
            z = jnp.zeros((1, tt), F32)
            c_prev = top
            c_last = top
            work = cand
            for r in range(ntop):
                m = jnp.max(work, axis=0, keepdims=True)
                work = jnp.where(work >= m, REMOVED, work)
                if r < PEER_TOPK:
                    z = z + jnp.exp(m - top)
                c_prev, c_last = c_last, m
            thr = 0.5 * (c_prev + c_last)
            na = jnp.zeros((PEER_KEYS, tt), F32)
            for j in range(ntop):
                na = jnp.where(s1 + v2[j] >= thr, float(j + 1), na)
            r2_s[hd] = rank2.astype(BF16)
            na_s[hd] = na
            e1_s[hd] = jnp.exp(s1 - v1[0])
            e2_s[hd] = (jnp.exp(s2 - v2[0]) / z).astype(BF16)
            return carry

        lax.fori_loop(0, PEER_HEADS, route, 0, unroll=4)

    n_slab = ch // PEER_KEYS
    slabs_per_group = PEER_PROJ_ROWS // PEER_KEYS
    n_group = ch // PEER_PROJ_ROWS

    def project(gi):
        rs = slice(gi * PEER_PROJ_ROWS, (gi + 1) * PEER_PROJ_ROWS)
        return _gelu(_dot(u_ref[rs, :], hbt_s[...]).astype(BF16))

    def gate(al, gl_rows):
        a = c * n_slab + al
        wc = jnp.zeros((PEER_KEYS, tt), BF16)
        for hd in range(PEER_HEADS):
            na = na_s[hd, pl.ds(a, 1), :].astype(BF16)
            e1 = e1_s[hd, pl.ds(a, 1), :].astype(BF16)
            wc = wc + jnp.where(r2_s[hd] < na, e2_s[hd] * e1, jnp.zeros((), BF16))
        return wc * gl_rows

    ys = []
    gl_next = project(0)
    for gi in range(n_group):
        gl = gl_next
        if gi + 1 < n_group:
            gl_next = project(gi + 1)
        for k in range(slabs_per_group):
            ys.append(gate(gi * slabs_per_group + k, gl[k * PEER_KEYS:(k + 1) * PEER_KEYS]))
    y = jnp.concatenate(ys, axis=0)
    acc_s[...] += _dot(vt_ref[0], y)

    @pl.when(c == pl.num_programs(1) - 1)
    def _():
        o_ref[...] = x_ref[...] + acc_s[...].T


def peer(x, g, wq, k1p, k2p, u, vt, tt=PEER_TT, ch=PEER_CH):
    t, d = x.shape
    ne = u.shape[0]
    return pl.pallas_call(
        functools.partial(_peer_kernel, tt=tt, ch=ch),
        grid=(t // tt, ne // ch),
        in_specs=[pl.BlockSpec((tt, d), lambda i, c: (i, 0)),
                  pl.BlockSpec((1, d), lambda i, c: (0, 0)),
                  pl.BlockSpec(wq.shape, lambda i, c: (0, 0)),
                  pl.BlockSpec(k1p.shape, lambda i, c: (0, 0)),
                  pl.BlockSpec(k2p.shape, lambda i, c: (0, 0)),
                  pl.BlockSpec((ch, d), lambda i, c: (c, 0)),
                  pl.BlockSpec((1, d, ch), lambda i, c: (c, 0, 0))],
        out_specs=pl.BlockSpec((tt, d), lambda i, c: (i, 0)),
        out_shape=jax.ShapeDtypeStruct((t, d), F32),
        scratch_shapes=[pltpu.VMEM((d, tt), BF16),
                        pltpu.VMEM((PEER_HEADS, tt, 2 * PEER_HALF), BF16),
                        pltpu.VMEM((PEER_HEADS, PEER_KEYS, tt), BF16),
                        pltpu.VMEM((PEER_HEADS, PEER_KEYS, tt), F32),
                        pltpu.VMEM((PEER_HEADS, PEER_KEYS, tt), F32),
                        pltpu.VMEM((PEER_HEADS, PEER_KEYS, tt), BF16),
                        pltpu.VMEM((d, tt), F32)],
        compiler_params=_cp(("parallel", "arbitrary")),
        name="peer",
    )(x, g.reshape(1, d), wq, k1p, k2p, u, vt)


_IN_WIDTHS = (NSA_HEADS * DK, 6 * NSA_GROUPS * DK, 3 * NSA_HEADS, 2 * SGU_WIDTH, POOL_WIDTH, 3 * D_MODEL)


def _in_proj_layout():
    splits, src = [], []
    o_src = o_dst = 0
    for w in _IN_WIDTHS:
        wp = -(-w // LANES) * LANES
        splits.append((o_dst, o_dst + wp))
        src.append((o_src, o_src + w))
        o_src += w
        o_dst += wp
    return splits, src, o_dst


def _chunked_transpose(v):
    ne, d = v.shape
    return v.astype(BF16).reshape(ne // PEER_CH, PEER_CH, d).transpose(0, 2, 1)


def nsa_layer(zq, zkv, zg, q_norm, k_norm, cmp_pe, cmp_w1, cmp_w2, b, s):
    qn, kvc, ks, vs, kw, vw = nsa_prep(zq.reshape(b, s, -1), zkv.reshape(b, s, -1), q_norm, k_norm)
    nc = s // D_CMP
    xc = kvc.reshape(2, b, NSA_GROUPS, nc, D_CMP * DK)
    pe = cmp_pe.reshape(2, 2, D_CMP * DK)
    w1 = cmp_w1.reshape(2, 2, D_CMP * DK, CMP_HIDDEN).astype(BF16)
    kc, vct = cmp_mlp(xc, pe, w1, cmp_w2.astype(BF16), k_norm[0])
    oc, selb = cmp_attn_select(qn, kc, vct)
    ow = win_attn(qn, kw, vw)
    gl = zg[:, :3 * NSA_HEADS].reshape(b, s, NSA_HEADS, 3).transpose(0, 2, 1, 3)
    return sel_attn(qn, selb, ks, vs, oc, ow, gl)


def kernel(x, mem, mix_norm, w_in, nsa_q_norm, nsa_k_norm, cmp_pe, cmp_w1, cmp_w2, sgu_ln_g, sgu_ln_b, sgu_w, sgu_b, pool_w, pool_scale, lift_a, lift_b, lift_c, w_out, xa_norm, mem_norm, xa_wq, xa_wk, xa_wv, xa_q_norm, xa_k_norm, xa_wo, ffn_norm, peer_wq, peer_keys1, peer_keys2, peer_u, peer_v):
    b, s, d = x.shape
    t = b * s
    depth = w_in.shape[0]
    splits, src, n_pad = _in_proj_layout()
    zeros_half = jnp.zeros((PEER_KEYS, PEER_HALF), BF16)
    for l in range(depth):
        w_parts = []
        for (a0, a1), (d0, d1) in zip(src, splits):
            w_parts.append(jnp.pad(w_in[l][:, a0:a1], ((0, 0), (0, (d1 - d0) - (a1 - a0)))))
        w_pad = jnp.concatenate(w_parts, axis=1).astype(BF16)
        zq, zkv, zg, zs, zp, zm = norm_matmul(x.reshape(t, d), mix_norm[l], w_pad, splits,
                                              (F32, F32, F32, F32, F32, BF16), 256, "in_proj")

        oa = nsa_layer(zq, zkv, zg, nsa_q_norm[l], nsa_k_norm[l], cmp_pe[l], cmp_w1[l], cmp_w2[l], b, s)
        bias_full = jnp.repeat(sgu_b[l].T, SGU_WIDTH // SGU_GROUPS, axis=1)
        ob = sgu(zs, sgu_ln_g[l], sgu_ln_b[l], sgu_w[l], bias_full)
        cg = POOL_WIDTH // len(POOL_WINDOWS)
        w_bd = jnp.zeros((POOL_WIDTH, POOL_WIDTH), F32)
        for gi in range(len(POOL_WINDOWS)):
            w_bd = w_bd.at[gi * cg:(gi + 1) * cg, gi * cg:(gi + 1) * cg].set(pool_w[l, gi])
        oc = pool(zp.reshape(b, s, POOL_WIDTH), w_bd.astype(BF16), pool_scale[l])
        x = merge(x, oa, ob.reshape(b, s, SGU_WIDTH), oc, zm.reshape(b, s, 3 * d),
                  lift_a[l].astype(BF16), lift_b[l].astype(BF16), lift_c[l].astype(BF16),
                  w_out[l].astype(BF16))

        mk, mv = mem_kv(mem, mem_norm[l], xa_wk[l].astype(BF16), xa_wv[l].astype(BF16), xa_k_norm[l])
        x = xattn(x, xa_norm[l], xa_wq[l].astype(BF16), xa_q_norm[l], mk, mv, xa_wo[l].astype(BF16))

        k1p = jnp.concatenate([peer_keys1[l].astype(BF16), zeros_half], axis=1)
        k2p = jnp.concatenate([zeros_half, peer_keys2[l].astype(BF16)], axis=1)
        x = peer(x.reshape(t, d), ffn_norm[l], peer_wq[l].astype(BF16), k1p, k2p,
                 peer_u[l].astype(BF16), _chunked_transpose(peer_v[l])).reshape(b, s, d)
    return x
```

```python
import functools

import jax
import jax.numpy as jnp
from jax import lax
from jax.experimental import pallas as pl
from jax.experimental.pallas import tpu as pltpu

F32 = jnp.float32
BF16 = jnp.bfloat16

EPS = 1e-6
LOG2E = 1.4426950408889634
NEG = -1e30
REMOVED = -3e38

D_MODEL = 1024
DK = 64
NSA_HEADS = 8
NSA_GROUPS = 2
NSA_REP = NSA_HEADS // NSA_GROUPS
L_CMP, D_CMP = 32, 16
CMP_HIDDEN = 128
L_SLC = 64
N_SEL = 16
WINDOW = 512
QB = 128
FORCE_BONUS = 1.0e3
SGU_WIDTH = 256
SGU_GROUPS = 4
SGU_CHUNK = 128
POOL_WIDTH = 256
POOL_WINDOWS = (2, 4, 8, 16)
XA_HEADS, XA_HD = 4, 128
XA_WIDTH = XA_HEADS * XA_HD
PEER_HEADS = 8
PEER_KEYS = 128
PEER_TOPK = 16
PEER_HALF = 64

LANES = 128
SEL_VR = 2 * DK
SEL_SUB = 256
SEL_QBLOCKS = 1
SEL_UNROLL = 16
WIN_QBLOCKS = 8
CMP_QBLOCKS = 8
CMP_CHUNK = 256
CMP_HALO = 8
PEER_TT = 512
PEER_CH = 2048
PEER_PROJ_ROWS = 128
PEER_ROUTE_W = 128
VMEM_LIMIT = 56 * 1024 * 1024


def _cp(sem):
    return pltpu.CompilerParams(dimension_semantics=sem, vmem_limit_bytes=VMEM_LIMIT)


def _gelu(x):
    return 0.5 * x * (1.0 + jnp.tanh(0.7978845608028654 * (x + 0.044715 * (x * x * x))))


def _rms(x, g):
    return x * lax.rsqrt(jnp.mean(x * x, axis=-1, keepdims=True) + EPS) * g


def _dot(a, b):
    return jnp.dot(a, b, preferred_element_type=F32)


def _dot_nt(a, b):
    return lax.dot_general(a, b, (((1,), (1,)), ((), ())), preferred_element_type=F32)


def _iota(shape, dim):
    return lax.broadcasted_iota(jnp.int32, shape, dim)


def _div_pow2(x, n):
    assert n & (n - 1) == 0
    return lax.shift_right_logical(x, jnp.int32(n.bit_length() - 1))


def _mod_pow2(x, n):
    assert n & (n - 1) == 0
    return x & (n - 1)


def _norm_matmul_kernel(x_ref, g_ref, w_ref, *o_refs, splits):
    hb = _rms(x_ref[...], g_ref[...]).astype(BF16)
    for o_ref, (a, b) in zip(o_refs, splits):
        o_ref[...] = _dot(hb, w_ref[:, a:b]).astype(o_ref.dtype)


def norm_matmul(x, g, w, splits, out_dtypes, tm, name):
    t, d = x.shape
    n = w.shape[1]
    return pl.pallas_call(
        functools.partial(_norm_matmul_kernel, splits=splits),
        grid=(t // tm,),
        in_specs=[pl.BlockSpec((tm, d), lambda i: (i, 0)),
                  pl.BlockSpec((1, d), lambda i: (0, 0)),
                  pl.BlockSpec((d, n), lambda i: (0, 0))],
        out_specs=[pl.BlockSpec((tm, b - a), lambda i: (i, 0)) for a, b in splits],
        out_shape=[jax.ShapeDtypeStruct((t, b - a), dt) for (a, b), dt in zip(splits, out_dtypes)],
        compiler_params=_cp(("parallel",)),
        name=name,
    )(x, g.reshape(1, d), w)


def _nsa_prep_kernel(zq_ref, zkv_ref, qn_ref, kn_ref,
                     q_out, kvc_out, ks_out, vs_out, kw_out, vw_out, *, ts):
    zq = zq_ref[0]
    scale = DK ** -0.5 * LOG2E
    for h in range(NSA_HEADS):
        q_out[0, h] = (_rms(zq[:, h * DK:(h + 1) * DK], qn_ref[...]) * scale).astype(BF16)
    zkv = zkv_ref[0]

    def piece(i, g):
        o = i * NSA_GROUPS * DK + g * DK
        return zkv[:, o:o + DK]

    ones_col = jnp.where(_iota((ts, SEL_VR - DK), 1) == 0, 1.0, 0.0)
    for g in range(NSA_GROUPS):
        kvc_out[0, 0, g] = piece(0, g)
        kvc_out[1, 0, g] = piece(1, g)
        ks_out[0, g] = _rms(piece(2, g), kn_ref[1:2, :]).astype(BF16)
        vs_out[0, g] = jnp.concatenate([piece(3, g), ones_col], axis=-1).T.astype(BF16)
        kw_out[0, g] = _rms(piece(4, g), kn_ref[2:3, :]).astype(BF16)
        vw_out[0, g] = jnp.concatenate([piece(5, g), ones_col], axis=-1).T.astype(BF16)


def nsa_prep(zq, zkv, q_norm, k_norm, ts=512):
    b, s, _ = zq.shape
    g = NSA_GROUPS
    hm = lambda bi, si: (bi, 0, si, 0)
    return pl.pallas_call(
        functools.partial(_nsa_prep_kernel, ts=ts),
        grid=(b, s // ts),
        in_specs=[pl.BlockSpec((1, ts, NSA_HEADS * DK), lambda bi, si: (bi, si, 0)),
                  pl.BlockSpec((1, ts, 6 * g * DK), lambda bi, si: (bi, si, 0)),
                  pl.BlockSpec((1, DK), lambda bi, si: (0, 0)),
                  pl.BlockSpec((3, DK), lambda bi, si: (0, 0))],
        out_specs=[pl.BlockSpec((1, NSA_HEADS, ts, DK), hm),
                   pl.BlockSpec((2, 1, g, ts, DK), lambda bi, si: (0, bi, 0, si, 0)),
                   pl.BlockSpec((1, g, ts, DK), hm),
                   pl.BlockSpec((1, g, SEL_VR, ts), lambda bi, si: (bi, 0, 0, si)),
                   pl.BlockSpec((1, g, ts, DK), hm),
                   pl.BlockSpec((1, g, SEL_VR, ts), lambda bi, si: (bi, 0, 0, si))],
        out_shape=[jax.ShapeDtypeStruct((b, NSA_HEADS, s, DK), BF16),
                   jax.ShapeDtypeStruct((2, b, g, s, DK), F32),
                   jax.ShapeDtypeStruct((b, g, s, DK), BF16),
                   jax.ShapeDtypeStruct((b, g, SEL_VR, s), BF16),
                   jax.ShapeDtypeStruct((b, g, s, DK), BF16),
                   jax.ShapeDtypeStruct((b, g, SEL_VR, s), BF16)],
        compiler_params=_cp(("parallel", "parallel")),
        name="nsa_prep",
    )(zq, zkv, q_norm.reshape(1, DK), k_norm)


def _cmp_mlp_kernel(x_ref, pe_ref, w1_ref, w2_ref, kn_ref, k_out, vt_out, *, nc):
    c = pl.program_id(2)
    x = x_ref[0, 0, 0]
    a = _dot((x + pe_ref[0, 0:1, :]).astype(BF16), w1_ref[0, 0])
    b = _dot((x + pe_ref[0, 1:2, :]).astype(BF16), w1_ref[0, 1])
    pre = a + pltpu.roll(b, nc - 1, 0)
    comp = _dot(_gelu(pre).astype(BF16), w2_ref[0])

    @pl.when(c == 0)
    def _():
        k_out[0, 0] = _rms(comp, kn_ref[...]).astype(BF16)

    @pl.when(c == 1)
    def _():
        ones_col = jnp.where(_iota((nc, SEL_VR - DK), 1) == 0, 1.0, 0.0)
        vt_out[0, 0] = jnp.concatenate([comp, ones_col], axis=-1).T.astype(BF16)


def cmp_mlp(xc, pe, w1, w2, k_norm0):
    _, b, g, nc, width = xc.shape
    return pl.pallas_call(
        functools.partial(_cmp_mlp_kernel, nc=nc),
        grid=(b, g, 2),
        in_specs=[pl.BlockSpec((1, 1, 1, nc, width), lambda bi, gi, c: (c, bi, gi, 0, 0)),
                  pl.BlockSpec((1, 2, width), lambda bi, gi, c: (c, 0, 0)),
                  pl.BlockSpec((1, 2, width, CMP_HIDDEN), lambda bi, gi, c: (c, 0, 0, 0)),
                  pl.BlockSpec((1, CMP_HIDDEN, DK), lambda bi, gi, c: (c, 0, 0)),
                  pl.BlockSpec((1, DK), lambda bi, gi, c: (0, 0))],
        out_specs=[pl.BlockSpec((1, 1, nc, DK), lambda bi, gi, c: (bi, gi, 0, 0)),
                   pl.BlockSpec((1, 1, SEL_VR, nc), lambda bi, gi, c: (bi, gi, 0, 0))],
        out_shape=[jax.ShapeDtypeStruct((b, g, nc, DK), BF16),
                   jax.ShapeDtypeStruct((b, g, SEL_VR, nc), BF16)],
        compiler_params=_cp(("parallel", "parallel", "arbitrary")),
        name="cmp_mlp",
    )(xc, pe, w1, w2, k_norm0.reshape(1, DK))


def _cmp_attn_kernel(q_ref, k_ref, vt_ref, oc_ref, sb_ref, s_s, acc_s, ps_s, *, nc, n_slc):
    qspan = CMP_QBLOCKS * QB
    q0 = pl.program_id(2) * qspan
    rows = NSA_REP * qspan
    ck = CMP_CHUNK
    q = q_ref[0].reshape(rows, DK)
    n_chunk = ((q0 + qspan) // D_CMP + ck - 1) // ck
    t = q0 + _mod_pow2(_iota((ck, rows), 1), qspan)

    def pass_a(ci, m):
        off = pl.multiple_of(ci * ck, ck)
        st = _dot_nt(k_ref[0, 0, pl.ds(off, ck), :], q)
        n = off + _iota((ck, rows), 0)
        st = jnp.where(n * D_CMP + (L_CMP - 1) <= t, st, NEG)
        s_s[ci] = st
        return jnp.maximum(m, jnp.max(st, axis=0, keepdims=True))

    m = lax.fori_loop(0, n_chunk, pass_a, jnp.full((1, rows), NEG, F32))
    acc_s[...] = jnp.zeros_like(acc_s)
    ps_s[...] = jnp.zeros_like(ps_s)

    def pass_b(ci, den):
        off = pl.multiple_of(ci * ck, ck)
        e = jnp.exp2(s_s[ci] - m)
        s_s[ci] = e
        acc_s[...] += _dot(vt_ref[0, 0, :, pl.ds(off, ck)], e.astype(BF16))
        return den + jnp.sum(e, axis=0, keepdims=True)

    den = lax.fori_loop(0, n_chunk, pass_b, jnp.zeros((1, rows), F32))
    seen = m > 0.5 * NEG
    inv_o = jnp.where(seen, 1.0 / jnp.maximum(acc_s[DK:DK + 1, :], 1e-30), 0.0)
    inv_p = jnp.where(seen, 1.0 / jnp.maximum(den, 1e-30), 0.0)
    oc_ref[0] = (acc_s[...] * inv_o).T[:, 0:DK].reshape(NSA_REP, qspan, DK)

    def pass_c(ci, carry):
        off = pl.multiple_of(ci * ck, ck)
        p = s_s[ci] * inv_p
        ps = p[:, 0:qspan]
        for r in range(1, NSA_REP):
            ps = ps + p[:, r * qspan:(r + 1) * qspan]
        for qi in range(CMP_QBLOCKS):
            ps_s[qi, pl.ds(CMP_HALO + off, ck), :] = ps[:, qi * QB:(qi + 1) * QB]
        return carry

    lax.fori_loop(0, n_chunk, pass_c, 0)

    ratio = L_SLC // D_CMP
    imp = jnp.zeros((n_slc, qspan), F32)
    for k in range(-(L_CMP // D_CMP - 1), ratio):
        overlap = min(k * D_CMP + L_CMP, L_SLC) - max(k * D_CMP, 0)
        taps = [ps_s[qi, pl.ds(CMP_HALO + k, n_slc, stride=ratio), :] for qi in range(CMP_QBLOCKS)]
        imp = imp + (overlap / D_CMP) * jnp.concatenate(taps, axis=1)

    cur = _div_pow2(q0 + _iota((n_slc, qspan), 1), L_SLC)
    j = _iota((n_slc, qspan), 0)
    forced = (j == 0) | (j == cur) | (j == cur - 1)
    score = jnp.where(j <= cur, imp + jnp.where(forced, FORCE_BONUS, 0.0), NEG)
    work = score
    tau = None
    for _ in range(N_SEL):
        tau = jnp.max(work, axis=0, keepdims=True)
        work = jnp.where(work >= tau, REMOVED, work)
    selected = (score > 0.5 * NEG) & (score >= tau)
    bias = jnp.where(selected, 0.0, NEG)
    for qi in range(CMP_QBLOCKS):
        sb_ref[0, 0, qi] = bias[:, qi * QB:(qi + 1) * QB]


def cmp_attn_select(qn, kc, vct):
    b, h, s, _ = qn.shape
    g = NSA_GROUPS
    nc = kc.shape[2]
    n_slc = s // L_SLC
    nqb = s // QB
    qspan = CMP_QBLOCKS * QB
    rows = NSA_REP * qspan
    return pl.pallas_call(
        functools.partial(_cmp_attn_kernel, nc=nc, n_slc=n_slc),
        grid=(b, g, s // qspan),
        in_specs=[pl.BlockSpec((1, NSA_REP, qspan, DK), lambda bi, gi, qi: (bi, gi, qi, 0)),
                  pl.BlockSpec((1, 1, nc, DK), lambda bi, gi, qi: (bi, gi, 0, 0)),
                  pl.BlockSpec((1, 1, SEL_VR, nc), lambda bi, gi, qi: (bi, gi, 0, 0))],
        out_specs=[pl.BlockSpec((1, NSA_REP, qspan, DK), lambda bi, gi, qi: (bi, gi, qi, 0)),
                   pl.BlockSpec((1, 1, CMP_QBLOCKS, n_slc, QB), lambda bi, gi, qi: (bi, gi, qi, 0, 0))],
        out_shape=[jax.ShapeDtypeStruct((b, h, s, DK), F32),
                   jax.ShapeDtypeStruct((b, g, nqb, n_slc, QB), F32)],
        scratch_shapes=[pltpu.VMEM((nc // CMP_CHUNK, CMP_CHUNK, rows), F32),
                        pltpu.VMEM((SEL_VR, rows), F32),
                        pltpu.VMEM((CMP_QBLOCKS, CMP_HALO + nc, QB), F32)],
        compiler_params=_cp(("parallel", "parallel", "parallel")),
        name="cmp_attn_select",
    )(qn, kc, vct)


def _win_attn_kernel(q_ref, *refs):
    nkb = WINDOW // QB + 1
    nrefs = nkb + WIN_QBLOCKS - 1
    k_refs, v_refs, o_ref = refs[:nrefs], refs[nrefs:2 * nrefs], refs[2 * nrefs]
    rows = NSA_REP * QB
    nk = nkb * QB
    for hf in range(WIN_QBLOCKS):
        qb = pl.program_id(2) * WIN_QBLOCKS + hf
        qs = slice(hf * QB, (hf + 1) * QB)
        q = q_ref[0, :, qs, :].reshape(rows, DK)
        k = jnp.concatenate([r[0, 0] for r in k_refs[hf:hf + nkb]], axis=0)
        vt = jnp.concatenate([r[0, 0] for r in v_refs[hf:hf + nkb]], axis=-1)
        st = _dot_nt(k, q)
        t = qb * QB + _mod_pow2(_iota((nk, rows), 1), QB)
        kpos = (qb - (nkb - 1)) * QB + _iota((nk, rows), 0)
        delta = t - kpos
        valid = (delta >= 0) & (delta < WINDOW) & (kpos >= 0)
        st = jnp.where(valid, st, NEG)
        e = jnp.exp2(st - jnp.max(st, axis=0, keepdims=True)).astype(BF16)
        acc = _dot(vt, e)
        o = (acc / jnp.maximum(acc[DK:DK + 1, :], 1e-30)).T[:, 0:DK]
        o_ref[0, :, qs, :] = o.reshape(NSA_REP, QB, DK)


def win_attn(qn, kw, vw):
    b, h, s, _ = qn.shape
    g = NSA_GROUPS
    nkb = WINDOW // QB + 1
    nrefs = nkb + WIN_QBLOCKS - 1
    nqb = s // QB

    def first_block(qi, i):
        return jnp.maximum(qi * WIN_QBLOCKS - (nkb - 1) + i, 0)

    def k_spec(i):
        return pl.BlockSpec((1, 1, QB, DK), lambda bi, gi, qi: (bi, gi, first_block(qi, i), 0))

    def vt_spec(i):
        return pl.BlockSpec((1, 1, SEL_VR, QB), lambda bi, gi, qi: (bi, gi, 0, first_block(qi, i)))

    qspec = pl.BlockSpec((1, NSA_REP, WIN_QBLOCKS * QB, DK), lambda bi, gi, qi: (bi, gi, qi, 0))
    return pl.pallas_call(
        _win_attn_kernel,
        grid=(b, g, nqb // WIN_QBLOCKS),
        in_specs=[qspec] + [k_spec(i) for i in range(nrefs)] + [vt_spec(i) for i in range(nrefs)],
        out_specs=qspec,
        out_shape=jax.ShapeDtypeStruct((b, h, s, DK), F32),
        compiler_params=_cp(("parallel", "parallel", "parallel")),
        name="win_attn",
    )(qn, *([kw] * nrefs), *([vw] * nrefs))


def _sel_attn_kernel(q_ref, sb_ref, k_ref, vt_ref, oc_ref, ow_ref, gl_ref, o_ref,
                     m_s, acc_s, s0_s, s1_s, p0_s, p1_s, a0_s, a1_s):
    qspan = SEL_QBLOCKS * QB
    q0 = pl.program_id(2) * qspan
    rows = NSA_REP * qspan
    blocks = SEL_SUB // L_SLC
    qv = q_ref[0].reshape(rows, DK)
    nsub = (q0 + qspan + SEL_SUB - 1) // SEL_SUB
    m_s[...] = jnp.full_like(m_s, NEG)
    acc_s[...] = jnp.zeros_like(acc_s)

    def scores(i, s_ref):
        off = pl.multiple_of(i * SEL_SUB, SEL_SUB)
        s_ref[...] = _dot_nt(k_ref[0, 0, pl.ds(off, SEL_SUB), :], qv)

    def weights(i, s_ref, p_ref, a_ref, causal):
        first = pl.multiple_of(i * blocks, blocks)
        bias = jnp.concatenate([sb_ref[0, 0, qi, pl.ds(first, blocks), :] for qi in range(SEL_QBLOCKS)],
                               axis=1)
        bias = jnp.concatenate([bias] * NSA_REP, axis=1)
        st = s_ref[...]
        if causal:
            kpos = i * SEL_SUB + _iota((SEL_SUB, rows), 0)
            t = q0 + _mod_pow2(_iota((SEL_SUB, rows), 1), qspan)
            st = jnp.where(kpos <= t, st, NEG)
        st = st.reshape(blocks, L_SLC, rows)
        m_old = m_s[...]
        m_new = jnp.maximum(m_old, jnp.max(jnp.max(st, axis=1) + bias, axis=0, keepdims=True))
        shift = bias - m_new
        p_ref[...] = jnp.exp2(st + shift[:, None, :]).reshape(SEL_SUB, rows).astype(BF16)
        a_ref[...] = jnp.exp2(m_old - m_new)
        m_s[...] = m_new

    def accumulate(i, p_ref, a_ref):
        off = pl.multiple_of(i * SEL_SUB, SEL_SUB)
        acc_s[...] = a_ref[...] * acc_s[...] + _dot(vt_ref[0, 0, :, pl.ds(off, SEL_SUB)], p_ref[...])

    p1_s[...] = jnp.zeros_like(p1_s)
    a1_s[...] = jnp.ones_like(a1_s)
    scores(0, s0_s)

    def two_tiles(i):
        scores(i + 1, s1_s)
        weights(i, s0_s, p0_s, a0_s, False)
        accumulate(jnp.maximum(i - 1, 0), p1_s, a1_s)
        scores(i + 2, s0_s)
        weights(i + 1, s1_s, p1_s, a1_s, False)
        accumulate(i, p0_s, a0_s)

    e = 0
    left = nsub - 1
    group = SEL_UNROLL
    while group >= 2:
        n_group = left // group

        def body(j, carry, group=group, base=e):
            for k in range(0, group, 2):
                two_tiles(base + group * j + k)
            return carry

        lax.fori_loop(0, n_group, body, 0)
        e = e + group * n_group
        left = left - group * n_group
        group //= 2
    odd_tail = left

    @pl.when(odd_tail == 0)
    def _():
        weights(e, s0_s, p0_s, a0_s, True)
        accumulate(jnp.maximum(e - 1, 0), p1_s, a1_s)
        accumulate(e, p0_s, a0_s)

    @pl.when(odd_tail == 1)
    def _():
        scores(e + 1, s1_s)
        weights(e, s0_s, p0_s, a0_s, False)
        accumulate(jnp.maximum(e - 1, 0), p1_s, a1_s)
        weights(e + 1, s1_s, p1_s, a1_s, True)
        accumulate(e, p0_s, a0_s)
        accumulate(e + 1, p1_s, a1_s)

    o_s = (acc_s[...] / jnp.maximum(acc_s[DK:DK + 1, :], 1e-30)).T[:, 0:DK]
    gate = jax.nn.sigmoid(gl_ref[0].reshape(rows, 3))
    out = (gate[:, 0:1] * oc_ref[0].reshape(rows, DK) + gate[:, 1:2] * o_s
           + gate[:, 2:3] * ow_ref[0].reshape(rows, DK))
    o_ref[0] = out.reshape(NSA_REP, qspan, DK)


def sel_attn(qn, selb, ks, vs, oc, ow, gl):
    b, h, s, _ = qn.shape
    g = NSA_GROUPS
    qspan = SEL_QBLOCKS * QB
    rows = NSA_REP * qspan
    n_slc = s // L_SLC
    qmap = lambda bi, gi, qi: (bi, gi, qi, 0)
    return pl.pallas_call(
        _sel_attn_kernel,
        grid=(b, g, s // qspan),
        in_specs=[pl.BlockSpec((1, NSA_REP, qspan, DK), qmap),
                  pl.BlockSpec((1, 1, SEL_QBLOCKS, n_slc, QB), lambda bi, gi, qi: (bi, gi, qi, 0, 0)),
                  pl.BlockSpec((1, 1, s, DK), lambda bi, gi, qi: (bi, gi, 0, 0)),
                  pl.BlockSpec((1, 1, SEL_VR, s), lambda bi, gi, qi: (bi, gi, 0, 0)),
                  pl.BlockSpec((1, NSA_REP, qspan, DK), qmap),
                  pl.BlockSpec((1, NSA_REP, qspan, DK), qmap),
                  pl.BlockSpec((1, NSA_REP, qspan, 3), qmap)],
        out_specs=pl.BlockSpec((1, NSA_REP, qspan, DK), qmap),
        out_shape=jax.ShapeDtypeStruct((b, h, s, DK), F32),
        scratch_shapes=[pltpu.VMEM((1, rows), F32),
                        pltpu.VMEM((SEL_VR, rows), F32),
                        pltpu.VMEM((SEL_SUB, rows), F32),
                        pltpu.VMEM((SEL_SUB, rows), F32),
                        pltpu.VMEM((SEL_SUB, rows), BF16),
                        pltpu.VMEM((SEL_SUB, rows), BF16),
                        pltpu.VMEM((1, rows), F32),
                        pltpu.VMEM((1, rows), F32)],
        compiler_params=_cp(("parallel", "parallel", "parallel")),
        name="sel_attn",
    )(qn, selb, ks, vs, oc, ow, gl)


def _sgu_kernel(z_ref, g_ref, b_ref, w_ref, bias_ref, o_ref, *, tm):
    z = _gelu(z_ref[...])
    u = z[:, :SGU_WIDTH]
    v = z[:, SGU_WIDTH:]
    mu = jnp.mean(v, axis=-1, keepdims=True)
    var = jnp.mean(jnp.square(v - mu), axis=-1, keepdims=True)
    vn = ((v - mu) * lax.rsqrt(var + EPS) * g_ref[...] + b_ref[...]).astype(BF16)
    tril = _iota((SGU_CHUNK, SGU_CHUNK), 1) <= _iota((SGU_CHUNK, SGU_CHUNK), 0)
    wm = [jnp.where(tril, w_ref[gi], 0.0).astype(BF16) for gi in range(SGU_GROUPS)]
    grp = _div_pow2(_iota((SGU_CHUNK, SGU_WIDTH), 1), SGU_WIDTH // SGU_GROUPS)
    for ch in range(tm // SGU_CHUNK):
        rs = slice(ch * SGU_CHUNK, (ch + 1) * SGU_CHUNK)
        mixed = bias_ref[...]
        for gi in range(SGU_GROUPS):
            mixed = mixed + jnp.where(grp == gi, _dot(wm[gi], vn[rs]), 0.0)
        o_ref[rs, :] = u[rs] * mixed


def sgu(z, ln_g, ln_b, w, bias_full, tm=512):
    t = z.shape[0]
    return pl.pallas_call(
        functools.partial(_sgu_kernel, tm=tm),
        grid=(t // tm,),
        in_specs=[pl.BlockSpec((tm, 2 * SGU_WIDTH), lambda i: (i, 0)),
                  pl.BlockSpec((1, SGU_WIDTH), lambda i: (0, 0)),
                  pl.BlockSpec((1, SGU_WIDTH), lambda i: (0, 0)),
                  pl.BlockSpec((SGU_GROUPS, SGU_CHUNK, SGU_CHUNK), lambda i: (0, 0, 0)),
                  pl.BlockSpec((SGU_CHUNK, SGU_WIDTH), lambda i: (0, 0))],
        out_specs=pl.BlockSpec((tm, SGU_WIDTH), lambda i: (i, 0)),
        out_shape=jax.ShapeDtypeStruct((t, SGU_WIDTH), F32),
        compiler_params=_cp(("parallel",)),
        name="sgu",
    )(z, ln_g.reshape(1, -1), ln_b.reshape(1, -1), w, bias_full)


def _pool_kernel(p_ref, h_ref, w_ref, sc_ref, o_ref, *, tp):
    i = pl.program_id(1)
    halo_rows = POOL_WINDOWS[-1]
    p = p_ref[0]
    halo = jnp.where(i > 0, h_ref[0], 0.0)
    ext = jnp.concatenate([halo, p], axis=0)
    sums = [ext]
    shift = 1
    for _ in POOL_WINDOWS:
        prev = sums[-1]
        sums.append(prev + pltpu.roll(prev, shift, 0))
        shift *= 2
    pos = i * tp + _iota((tp, POOL_WIDTH), 0)
    grp = _div_pow2(_iota((tp, POOL_WIDTH), 1), POOL_WIDTH // len(POOL_WINDOWS))
    d = jnp.zeros((tp, POOL_WIDTH), F32)
    for gi, w in enumerate(POOL_WINDOWS):
        mean = sums[gi + 1][halo_rows:halo_rows + tp] / jnp.minimum(pos + 1, w).astype(F32)
        d = jnp.where(grp == gi, mean, d)
    d = d - p
    o_ref[0] = _dot(d.astype(BF16), w_ref[...]) * sc_ref[...]


def pool(p, w_bd, scale, tp=512):
    b, s, c = p.shape
    halo_rows = POOL_WINDOWS[-1]
    return pl.pallas_call(
        functools.partial(_pool_kernel, tp=tp),
        grid=(b, s // tp),
        in_specs=[pl.BlockSpec((1, tp, c), lambda bi, i: (bi, i, 0)),
                  pl.BlockSpec((1, halo_rows, c),
                               lambda bi, i: (bi, jnp.maximum(i * (tp // halo_rows) - 1, 0), 0)),
                  pl.BlockSpec((c, c), lambda bi, i: (0, 0)),
                  pl.BlockSpec((1, c), lambda bi, i: (0, 0))],
        out_specs=pl.BlockSpec((1, tp, c), lambda bi, i: (bi, i, 0)),
        out_shape=jax.ShapeDtypeStruct((b, s, c), F32),
        compiler_params=_cp(("parallel", "parallel")),
        name="pool",
    )(p, p, w_bd, scale.reshape(1, c))


def _merge_kernel(x_ref, oa_ref, ob_ref, oc_ref, zm_ref, la_ref, lb_ref, lc_ref, wo_ref, o_ref):
    d = D_MODEL
    oa = jnp.concatenate([oa_ref[0, h].astype(BF16) for h in range(NSA_HEADS)], axis=-1)
    zm = zm_ref[0].astype(F32)
    merged = (jax.nn.sigmoid(zm[:, 0:d]) * _dot(oa, la_ref[...])
              + jax.nn.sigmoid(zm[:, d:2 * d]) * _dot(ob_ref[0].astype(BF16), lb_ref[...])
              + jax.nn.sigmoid(zm[:, 2 * d:3 * d]) * _dot(oc_ref[0].astype(BF16), lc_ref[...]))
    o_ref[0] = x_ref[0] + _dot(merged.astype(BF16), wo_ref[...])


def merge(x, oa, ob, oc, zm, la, lb, lc, wo, tm=512):
    b, s, d = x.shape
    row = lambda bi, i: (bi, i, 0)
    full = lambda bi, i: (0, 0)
    return pl.pallas_call(
        _merge_kernel,
        grid=(b, s // tm),
        in_specs=[pl.BlockSpec((1, tm, d), row),
                  pl.BlockSpec((1, NSA_HEADS, tm, DK), lambda bi, i: (bi, 0, i, 0)),
                  pl.BlockSpec((1, tm, SGU_WIDTH), row),
                  pl.BlockSpec((1, tm, POOL_WIDTH), row),
                  pl.BlockSpec((1, tm, 3 * d), row),
                  pl.BlockSpec(la.shape, full), pl.BlockSpec(lb.shape, full),
                  pl.BlockSpec(lc.shape, full), pl.BlockSpec(wo.shape, full)],
        out_specs=pl.BlockSpec((1, tm, d), row),
        out_shape=jax.ShapeDtypeStruct((b, s, d), F32),
        compiler_params=_cp(("parallel", "parallel")),
        name="merge",
    )(x, oa, ob, oc, zm, la, lb, lc, wo)


def _mem_kv_kernel(m_ref, g_ref, wk_ref, wv_ref, kn_ref, k_out, v_out):
    mh = _rms(m_ref[0], g_ref[...]).astype(BF16)
    k = _dot(mh, wk_ref[...])
    for h in range(XA_HEADS):
        hs = slice(h * XA_HD, (h + 1) * XA_HD)
        k_out[0, :, hs] = _rms(k[:, hs], kn_ref[...]).astype(BF16)
    v_out[0] = _dot(mh, wv_ref[...]).astype(BF16)


def mem_kv(mem, g, wk, wv, k_norm):
    b, m, d = mem.shape
    full = lambda bi: (0, 0)
    return pl.pallas_call(
        _mem_kv_kernel,
        grid=(b,),
        in_specs=[pl.BlockSpec((1, m, d), lambda bi: (bi, 0, 0)),
                  pl.BlockSpec((1, d), full),
                  pl.BlockSpec(wk.shape, full), pl.BlockSpec(wv.shape, full),
                  pl.BlockSpec((1, XA_HD), full)],
        out_specs=[pl.BlockSpec((1, m, XA_WIDTH), lambda bi: (bi, 0, 0))] * 2,
        out_shape=[jax.ShapeDtypeStruct((b, m, XA_WIDTH), BF16)] * 2,
        compiler_params=_cp(("parallel",)),
        name="mem_kv",
    )(mem, g.reshape(1, d), wk, wv, k_norm.reshape(1, XA_HD))


def _xattn_kernel(x_ref, g_ref, wq_ref, qn_ref, k_ref, v_ref, wo_ref, o_ref):
    x = x_ref[0]
    q = _dot(_rms(x, g_ref[...]).astype(BF16), wq_ref[...])
    k = k_ref[0]
    v = v_ref[0]
    outs = []
    for h in range(XA_HEADS):
        hs = slice(h * XA_HD, (h + 1) * XA_HD)
        qh = (_rms(q[:, hs], qn_ref[...])).astype(BF16)
        s = _dot_nt(qh, k[:, hs]) * (XA_HD ** -0.5)
        e = jnp.exp(s - jnp.max(s, axis=-1, keepdims=True))
        p = e / jnp.sum(e, axis=-1, keepdims=True)
        outs.append(_dot(p.astype(BF16), v[:, hs]).astype(BF16))
    o = jnp.concatenate(outs, axis=-1)
    o_ref[0] = x + _dot(o, wo_ref[...])


def xattn(x, g, wq, q_norm, k, v, wo, tm=512):
    b, s, d = x.shape
    m = k.shape[1]
    row = lambda bi, i: (bi, i, 0)
    full = lambda bi, i: (0, 0)
    return pl.pallas_call(
        _xattn_kernel,
        grid=(b, s // tm),
        in_specs=[pl.BlockSpec((1, tm, d), row),
                  pl.BlockSpec((1, d), full),
                  pl.BlockSpec(wq.shape, full),
                  pl.BlockSpec((1, XA_HD), full),
                  pl.BlockSpec((1, m, XA_WIDTH), lambda bi, i: (bi, 0, 0)),
                  pl.BlockSpec((1, m, XA_WIDTH), lambda bi, i: (bi, 0, 0)),
                  pl.BlockSpec(wo.shape, full)],
        out_specs=pl.BlockSpec((1, tm, d), row),
        out_shape=jax.ShapeDtypeStruct((b, s, d), F32),
        compiler_params=_cp(("parallel", "parallel")),
        name="xattn",
    )(x, g.reshape(1, d), wq, q_norm.reshape(1, XA_HD), k, v, wo)


def _top_rows(s, n, want_rank):
    out = []
    work = s
    rank = jnp.full(s.shape, float(s.shape[0] - 1), F32) if want_rank else None
    for i in range(n):
        m = jnp.max(work, axis=0, keepdims=True)
        out.append(m)
        hit = work >= m
        if want_rank:
            rank = jnp.where(hit, float(i), rank)
        work = jnp.where(hit, REMOVED, work)
    return out, rank


def _stack_rows(rows, pad_rows):
    tt = rows[0].shape[1]
    rowi = _iota((pad_rows, tt), 0)
    out = jnp.full((pad_rows, tt), REMOVED, F32)
    for i, r in enumerate(rows[:pad_rows]):
        out = jnp.where(rowi == i, r, out)
    return out


def _pair_sum_candidates(v1, v2, tt):
    row_full = _iota((24, tt), 0)
    row = _iota((8, tt), 0)
    v2_full = _stack_rows(v2, 24)
    v2_8 = _stack_rows(v2, 8)

    def shifted(k):
        return pltpu.roll(v2_8, k, 0)

    def pick(rows_from, first):
        out = v1[rows_from]
        for r in range(first + 1, 8):
            if rows_from + r - first < len(v1):
                out = jnp.where(row == r, v1[rows_from + r - first], out)
        return out

    return jnp.concatenate([
        jnp.where(row_full < 17, v1[0] + v2_full, REMOVED),
        v1[1] + v2_8,
        jnp.where(row < 5, v1[2] + v2_8,
                  jnp.where(row < 7, v1[5] + shifted(5), v1[8] + shifted(7))),
        jnp.where(row < 4, v1[3] + v2_8,
                  jnp.where(row < 7, v1[4] + shifted(4), v1[9] + shifted(7))),
        jnp.where(row < 2, v1[6] + v2_8,
                  jnp.where(row < 4, v1[7] + shifted(2), pick(10, 4) + v2[0])),
        jnp.where(row < 3, pick(14, 0) + v2[0], REMOVED),
    ], axis=0)


def _peer_kernel(x_ref, g_ref, wq_ref, k1_ref, k2_ref, u_ref, vt_ref, o_ref,
                 hbt_s, q_s, r2_s, na_s, e1_s, e2_s, acc_s, *, tt, ch):
    c = pl.program_id(1)
    ntop = PEER_TOPK + 1
    assert ntop == 17

    @pl.when(c == 0)
    def _():
        h = _rms(x_ref[...], g_ref[...])
        hbt_s[...] = h.T.astype(BF16)
        q = _dot(h.astype(BF16), wq_ref[...])
        for hd in range(PEER_HEADS):
            q_s[hd] = q[:, hd * 2 * PEER_HALF:(hd + 1) * 2 * PEER_HALF].astype(BF16)
        acc_s[...] = jnp.zeros_like(acc_s)

        def route(hd, carry):
            qh = q_s[hd]
            s1_all = _dot_nt(k1_ref[...], qh)
            s2_all = _dot_nt(k2_ref[...], qh)
            w = PEER_ROUTE_W
            for ti in range(tt // w):
                cs = slice(ti * w, (ti + 1) * w)
                s1 = s1_all[:, cs]
                s2 = s2_all[:, cs]
                v1, _ = _top_rows(s1, ntop, False)
                v2, rank2 = _top_rows(s2, ntop, True)
                cand = _pair_sum_candidates(v1, v2, w)
                top = v1[0] + v2[0]
                z = jnp.zeros((1, w), F32)
                c_prev = top
                c_last = top
                work = cand
                for r in range(ntop):
                    m = jnp.max(work, axis=0, keepdims=True)
                    work = jnp.where(work >= m, REMOVED, work)
                    if r < PEER_TOPK:
                        z = z + jnp.exp(m - top)
                    c_prev, c_last = c_last, m
                thr = 0.5 * (c_prev + c_last)
                na = jnp.zeros((PEER_KEYS, w), F32)
                for j in range(ntop):
                    na = jnp.where(s1 + v2[j] >= thr, float(j + 1), na)
                r2_s[hd, :, cs] = rank2.astype(BF16)
                na_s[hd, :, cs] = na
                e1_s[hd, :, cs] = jnp.exp(s1 - v1[0])
                e2_s[hd, :, cs] = (jnp.exp(s2 - v2[0]) / z).astype(BF16)
            return carry

        lax.fori_loop(0, PEER_HEADS, route, 0, unroll=2)

    n_slab = ch // PEER_KEYS
    slabs_per_group = PEER_PROJ_ROWS // PEER_KEYS
    n_group = ch // PEER_PROJ_ROWS

    def project(gi):
        rs = slice(gi * PEER_PROJ_ROWS, (gi + 1) * PEER_PROJ_ROWS)
        return _gelu(_dot(u_ref[rs, :], hbt_s[...]).astype(BF16))

    def gate(al, gl_rows):
        a = c * n_slab + al
        wc = jnp.zeros((PEER_KEYS, tt), BF16)
        for hd in range(PEER_HEADS):
            na = na_s[hd, pl.ds(a, 1), :].astype(BF16)
            e1 = e1_s[hd, pl.ds(a, 1), :].astype(BF16)
            wc = wc + jnp.where(r2_s[hd] < na, e2_s[hd] * e1, jnp.zeros((), BF16))
        return wc * gl_rows

    ys = []
    gl_next = project(0)
    for gi in range(n_group):
        gl = gl_next
        if gi + 1 < n_group:
            gl_next = project(gi + 1)
        for k in range(slabs_per_group):
            ys.append(gate(gi * slabs_per_group + k, gl[k * PEER_KEYS:(k + 1) * PEER_KEYS]))
    y = jnp.concatenate(ys, axis=0)
    acc_s[...] += _dot(vt_ref[0], y)

    @pl.when(c == pl.num_programs(1) - 1)
    def _():
        o_ref[...] = x_ref[...] + acc_s[...].T


def peer(x, g, wq, k1p, k2p, u, vt, tt=PEER_TT, ch=PEER_CH):
    t, d = x.shape
    ne = u.shape[0]
    return pl.pallas_call(
        functools.partial(_peer_kernel, tt=tt, ch=ch),
        grid=(t // tt, ne // ch),
        in_specs=[pl.BlockSpec((tt, d), lambda i, c: (i, 0)),
                  pl.BlockSpec((1, d), lambda i, c: (0, 0)),
                  pl.BlockSpec(wq.shape, lambda i, c: (0, 0)),
                  pl.BlockSpec(k1p.shape, lambda i, c: (0, 0)),
                  pl.BlockSpec(k2p.shape, lambda i, c: (0, 0)),
                  pl.BlockSpec((ch, d), lambda i, c: (c, 0)),
                  pl.BlockSpec((1, d, ch), lambda i, c: (c, 0, 0))],
        out_specs=pl.BlockSpec((tt, d), lambda i, c: (i, 0)),
        out_shape=jax.ShapeDtypeStruct((t, d), F32),
        scratch_shapes=[pltpu.VMEM((d, tt), BF16),
                        pltpu.VMEM((PEER_HEADS, tt, 2 * PEER_HALF), BF16),
                        pltpu.VMEM((PEER_HEADS, PEER_KEYS, tt), BF16),
                        pltpu.VMEM((PEER_HEADS, PEER_KEYS, tt), F32),
                        pltpu.VMEM((PEER_HEADS, PEER_KEYS, tt), F32),
                        pltpu.VMEM((PEER_HEADS, PEER_KEYS, tt), BF16),
                        pltpu.VMEM((d, tt), F32)],
        compiler_params=_cp(("parallel", "arbitrary")),
        name="peer",
    )(x, g.reshape(1, d), wq, k1p, k2p, u, vt)


_IN_WIDTHS = (NSA_HEADS * DK, 6 * NSA_GROUPS * DK, 3 * NSA_HEADS, 2 * SGU_WIDTH, POOL_WIDTH, 3 * D_MODEL)


def _in_proj_layout():
    splits, src = [], []
    o_src = o_dst = 0
    for w in _IN_WIDTHS:
        wp = -(-w // LANES) * LANES
        splits.append((o_dst, o_dst + wp))
        src.append((o_src, o_src + w))
        o_src += w
        o_dst += wp
    return splits, src, o_dst


def _chunked_transpose(v):
    ne, d = v.shape
    return v.astype(BF16).reshape(ne // PEER_CH, PEER_CH, d).transpose(0, 2, 1)


def nsa_layer(zq, zkv, zg, q_norm, k_norm, cmp_pe, cmp_w1, cmp_w2, b, s):
    qn, kvc, ks, vs, kw, vw = nsa_prep(zq.reshape(b, s, -1), zkv.reshape(b, s, -1), q_norm, k_norm)
    nc = s // D_CMP
    xc = kvc.reshape(2, b, NSA_GROUPS, nc, D_CMP * DK)
    pe = cmp_pe.reshape(2, 2, D_CMP * DK)
    w1 = cmp_w1.reshape(2, 2, D_CMP * DK, CMP_HIDDEN).astype(BF16)
    kc, vct = cmp_mlp(xc, pe, w1, cmp_w2.astype(BF16), k_norm[0])
    oc, selb = cmp_attn_select(qn, kc, vct)
    ow = win_attn(qn, kw, vw)
    gl = zg[:, :3 * NSA_HEADS].reshape(b, s, NSA_HEADS, 3).transpose(0, 2, 1, 3)
    return sel_attn(qn, selb, ks, vs, oc, ow, gl)


def kernel(x, mem, mix_norm, w_in, nsa_q_norm, nsa_k_norm, cmp_pe, cmp_w1, cmp_w2, sgu_ln_g, sgu_ln_b, sgu_w, sgu_b, pool_w, pool_scale, lift_a, lift_b, lift_c, w_out, xa_norm, mem_norm, xa_wq, xa_wk, xa_wv, xa_q_norm, xa_k_norm, xa_wo, ffn_norm, peer_wq, peer_keys1, peer_keys2, peer_u, peer_v):
    b, s, d = x.shape
    t = b * s
    depth = w_in.shape[0]
    splits, src, n_pad = _in_proj_layout()
    zeros_half = jnp.zeros((PEER_KEYS, PEER_HALF), BF16)
    for l in range(depth):
        w_parts = []
        for (a0, a1), (d0, d1) in zip(src, splits):
            w_parts.append(jnp.pad(w_in[l][:, a0:a1], ((0, 0), (0, (d1 - d0) - (a1 - a0)))))
        w_pad = jnp.concatenate(w_parts, axis=1).astype(BF16)
        zq, zkv, zg, zs, zp, zm = norm_matmul(x.reshape(t, d), mix_norm[l], w_pad, splits,
                                              (F32, F32, F32, F32, F32, BF16), 256, "in_proj")

        oa = nsa_layer(zq, zkv, zg, nsa_q_norm[l], nsa_k_norm[l], cmp_pe[l], cmp_w1[l], cmp_w2[l], b, s)
        bias_full = jnp.repeat(sgu_b[l].T, SGU_WIDTH // SGU_GROUPS, axis=1)
        ob = sgu(zs, sgu_ln_g[l], sgu_ln_b[l], sgu_w[l], bias_full)
        cg = POOL_WIDTH // len(POOL_WINDOWS)
        w_bd = jnp.zeros((POOL_WIDTH, POOL_WIDTH), F32)
        for gi in range(len(POOL_WINDOWS)):
            w_bd = w_bd.at[gi * cg:(gi + 1) * cg, gi * cg:(gi + 1) * cg].set(pool_w[l, gi])
        oc = pool(zp.reshape(b, s, POOL_WIDTH), w_bd.astype(BF16), pool_scale[l])
        x = merge(x, oa, ob.reshape(b, s, SGU_WIDTH), oc, zm.reshape(b, s, 3 * d),
                  lift_a[l].astype(BF16), lift_b[l].astype(BF16), lift_c[l].astype(BF16),
                  w_out[l].astype(BF16))

        mk, mv = mem_kv(mem, mem_norm[l], xa_wk[l].astype(BF16), xa_wv[l].astype(BF16), xa_k_norm[l])
        x = xattn(x, xa_norm[l], xa_wq[l].astype(BF16), xa_q_norm[l], mk, mv, xa_wo[l].astype(BF16))

        k1p = jnp.concatenate([peer_keys1[l].astype(BF16), zeros_half], axis=1)
        k2p = jnp.concatenate([zeros_half, peer_keys2[l].astype(BF16)], axis=1)
        x = peer(x.reshape(t, d), ffn_norm[l], peer_wq[l].astype(BF16), k1p, k2p,
                 peer_u[l].astype(BF16), _chunked_transpose(peer_v[l])).reshape(b, s, d)
    return x
```
